```python
import jax
import jax.numpy as jnp
from jax import lax
import numpy as np

D_MODEL = 2048
BATCH = 8
SEQ = 8192
DEPTH = 2

CHUNK = 64
LN_EPS = 1e-5
ALPHA = (2 * DEPTH) ** 0.25
BETA = (8 * DEPTH) ** -0.25
N_EVEN = (DEPTH + 1) // 2
N_ODD = DEPTH // 2

CONV_CH = D_MODEL // 2
CONV_WIDTH = 31
HGRN_HEADS = 8
HGRN_DK = 128
HGRN_DV = (D_MODEL // 2) // HGRN_HEADS
HGRN_KDIM = HGRN_HEADS * HGRN_DK
HGRN_VDIM = HGRN_HEADS * HGRN_DV
FOX_HEADS = 8
FOX_DH = 128
FOX_WIDTH = FOX_HEADS * FOX_DH
FOX_QBLOCK = 128
CA_HEADS = 8
CA_DH = 128
CA_WIDTH = CA_HEADS * CA_DH
CA_LEFT_CHUNKS = 8
CA_BAND = (CA_LEFT_CHUNKS + 1) * CHUNK
REL_CLIP = 256
REL_TABLE = (CHUNK - 1) + REL_CLIP + 1
D_FF = 4 * D_MODEL

EVEN_IN = 2 * CONV_CH + 2 * HGRN_KDIM + 2 * HGRN_VDIM
ODD_IN = 3 * FOX_WIDTH + FOX_HEADS + 3 * CA_WIDTH

kernel_name = 'hybrid_conv_hgrn2_fox_chunkattn_trunk'


def layer_norm(x, g, b):
    xf = x.astype(jnp.float32)
    mu = jnp.mean(xf, axis=-1, keepdims=True)
    var = jnp.mean(jnp.square(xf - mu), axis=-1, keepdims=True)
    return ((xf - mu) * lax.rsqrt(var + LN_EPS) * g + b).astype(x.dtype)


def conformer_conv(u, conv_w, conv_b, ln_g, ln_b):
    a, gate = jnp.split(u, 2, axis=-1)
    h = a * jax.nn.sigmoid(gate)
    h = lax.conv_general_dilated(
        h, conv_w[:, None, :].astype(h.dtype), window_strides=(1,),
        padding=[(CONV_WIDTH - 1, 0)],
        dimension_numbers=('NWC', 'WIO', 'NWC'),
        feature_group_count=CONV_CH) + conv_b
    return jax.nn.silu(layer_norm(h, ln_g, ln_b))


def hgrn2(q, f_logit, i, g, lb, gnorm_g):
    B, S, _ = q.shape
    nc = S // CHUNK
    f32 = jnp.float32
    f = lb + (1.0 - lb) * jax.nn.sigmoid(f_logit.astype(f32))
    log_f = jnp.log(f)
    k = 1.0 - f

    def to_chunks(t, d):
        return t.astype(f32).reshape(B, nc, CHUNK, HGRN_HEADS, d).transpose(1, 0, 3, 2, 4)

    qc = to_chunks(jax.nn.silu(q), HGRN_DK)
    kc = to_chunks(k, HGRN_DK)
    vc = to_chunks(i, HGRN_DV)
    lc = to_chunks(log_f, HGRN_DK)
    causal = jnp.tril(jnp.ones((CHUNK, CHUNK), dtype=bool))[:, :, None]

    def step(state, inp):
        qb, kb, vb, lfb = inp
        L = jnp.cumsum(lfb, axis=2)
        diff = L[:, :, :, None, :] - L[:, :, None, :, :]
        decay = jnp.exp(jnp.where(causal, diff, -jnp.inf))
        scores = jnp.einsum('bhtk,bhsk,bhtsk->bhts', qb, kb, decay)
        o = (jnp.einsum('bhts,bhsv->bhtv', scores, vb)
             + jnp.einsum('bhtk,bhkv->bhtv', qb * jnp.exp(L), state))
        L_end = L[:, :, -1, :]
        state = (jnp.exp(L_end)[..., None] * state
                 + jnp.einsum('bhsk,bhsv->bhkv', kb * jnp.exp(L_end[:, :, None, :] - L), vb))
        return state, o

    s0 = jnp.zeros((B, HGRN_HEADS, HGRN_DK, HGRN_DV), f32)
    _, o = lax.scan(step, s0, (qc, kc, vc, lc))
    o = o.transpose(1, 0, 3, 2, 4).reshape(B, S, HGRN_HEADS, HGRN_DV)
    o = o * lax.rsqrt(jnp.mean(jnp.square(o), axis=-1, keepdims=True) + LN_EPS)
    o = o.reshape(B, S, HGRN_VDIM) * gnorm_g * jax.nn.silu(g.astype(f32))
    return o.astype(q.dtype)


def forgetting_attention(q, k, v, f_logit):
    B, S, _ = q.shape
    nb = S // FOX_QBLOCK
    heads = lambda t: t.reshape(B, S, FOX_HEADS, FOX_DH).transpose(0, 2, 1, 3)
    qh, kh, vh = heads(q), heads(k), heads(v)
    F = jnp.cumsum(jax.nn.log_sigmoid(f_logit.astype(jnp.float32)), axis=1).transpose(0, 2, 1)
    q_blocks = qh.reshape(B, FOX_HEADS, nb, FOX_QBLOCK, FOX_DH).transpose(2, 0, 1, 3, 4)
    F_blocks = F.reshape(B, FOX_HEADS, nb, FOX_QBLOCK).transpose(2, 0, 1, 3)
    kpos = jnp.arange(S)
    scale = FOX_DH ** -0.5

    def block(args):
        qb, Fb, bidx = args
        s = jnp.einsum('bhqd,bhkd->bhqk', qb, kh).astype(jnp.float32) * scale
        s = s + Fb[..., None] - F[:, :, None, :]
        qpos = bidx * FOX_QBLOCK + jnp.arange(FOX_QBLOCK)
        s = jnp.where(kpos[None, :] <= qpos[:, None], s, -jnp.inf)
        p = jax.nn.softmax(s, axis=-1).astype(vh.dtype)
        return jnp.einsum('bhqk,bhkd->bhqd', p, vh)

    out = lax.map(block, (q_blocks, F_blocks, jnp.arange(nb)))
    return out.transpose(1, 0, 3, 2, 4).reshape(B, S, FOX_WIDTH)


def chunked_relpos_attention(q, k, v, rel_bias):
    B, S, _ = q.shape
    nc = S // CHUNK
    left = CA_LEFT_CHUNKS * CHUNK
    qc = q.reshape(B, nc, CHUNK, CA_HEADS, CA_DH).transpose(1, 0, 3, 2, 4)
    pad = lambda t: jnp.pad(t.reshape(B, S, CA_HEADS, CA_DH),
                            ((0, 0), (left, 0), (0, 0), (0, 0))).transpose(0, 2, 1, 3)
    kp, vp = pad(k), pad(v)
    qi = jnp.arange(CHUNK)
    km = jnp.arange(CA_BAND)
    rel = (left + qi[:, None]) - km[None, :]
    bias = rel_bias[:, jnp.minimum(rel, REL_CLIP) + (CHUNK - 1)].astype(jnp.float32)
    scale = CA_DH ** -0.5

    def chunk(args):
        qb, c = args
        start = c * CHUNK
        kb = lax.dynamic_slice_in_dim(kp, start, CA_BAND, axis=2)
        vb = lax.dynamic_slice_in_dim(vp, start, CA_BAND, axis=2)
        s = jnp.einsum('bhqd,bhkd->bhqk', qb, kb).astype(jnp.float32) * scale + bias
        s = jnp.where(start + km >= left, s, -jnp.inf)
        p = jax.nn.softmax(s, axis=-1).astype(vb.dtype)
        return jnp.einsum('bhqk,bhkd->bhqd', p, vb)

    out = lax.map(chunk, (qc, jnp.arange(nc)))
    return out.transpose(1, 0, 3, 2, 4).reshape(B, S, CA_WIDTH)


def _fwd_setup_inputs(seed: int = 0) -> dict:
    key = jax.random.key(seed)
    ks = jax.random.split(key, 19)
    f32 = jnp.float32

    def nrm(k, shape, scale):
        return scale * jax.random.normal(k, shape, f32)

    return {
        'x': nrm(ks[0], (BATCH, SEQ, D_MODEL), 1.0),
        'ev_w_in': nrm(ks[1], (N_EVEN, D_MODEL, EVEN_IN), D_MODEL ** -0.5),
        'ev_conv_w': nrm(ks[2], (N_EVEN, CONV_WIDTH, CONV_CH), CONV_WIDTH ** -0.5),
        'ev_conv_b': nrm(ks[3], (N_EVEN, CONV_CH), 0.02),
        'ev_conv_ln_g': 1.0 + nrm(ks[4], (N_EVEN, CONV_CH), 0.1),
        'ev_conv_ln_b': nrm(ks[5], (N_EVEN, CONV_CH), 0.02),
        'hgrn_lb_logits': nrm(ks[6], (N_EVEN + 1, HGRN_KDIM), 0.1),
        'ev_gnorm_g': 1.0 + nrm(ks[7], (N_EVEN, HGRN_VDIM), 0.1),
        'ev_w_out': nrm(ks[8], (N_EVEN, CONV_CH + HGRN_VDIM, D_MODEL), (CONV_CH + HGRN_VDIM) ** -0.5 * BETA),
        'od_w_in': nrm(ks[9], (N_ODD, D_MODEL, ODD_IN), D_MODEL ** -0.5),
        'fox_b_f': 3.0 + nrm(ks[10], (N_ODD, FOX_HEADS), 0.5),
        'rel_bias': nrm(ks[11], (N_ODD, CA_HEADS, REL_TABLE), 0.5),
        'od_w_out': nrm(ks[12], (N_ODD, FOX_WIDTH + CA_WIDTH, D_MODEL), (FOX_WIDTH + CA_WIDTH) ** -0.5 * BETA),
        'ln_mix_g': 1.0 + nrm(ks[13], (DEPTH, D_MODEL), 0.1),
        'ln_mix_b': nrm(ks[14], (DEPTH, D_MODEL), 0.02),
        'mlp_w1': nrm(ks[15], (DEPTH, D_MODEL, D_FF), D_MODEL ** -0.5),
        'mlp_w2': nrm(ks[16], (DEPTH, D_FF, D_MODEL), D_FF ** -0.5 * BETA),
        'ln_mlp_g': 1.0 + nrm(ks[17], (DEPTH, D_MODEL), 0.1),
        'ln_mlp_b': nrm(ks[18], (DEPTH, D_MODEL), 0.02),
    }


def _fwd_reference(x, ev_w_in, ev_conv_w, ev_conv_b, ev_conv_ln_g, ev_conv_ln_b, hgrn_lb_logits,
              ev_gnorm_g, ev_w_out, od_w_in, fox_b_f, rel_bias, od_w_out, ln_mix_g, ln_mix_b,
              mlp_w1, mlp_w2, ln_mlp_g, ln_mlp_b):
    lower_bounds = jnp.cumsum(jax.nn.softmax(hgrn_lb_logits.astype(jnp.float32), axis=0), axis=0)
    for l in range(DEPTH):
        j = l // 2
        if l % 2 == 0:
            u = x @ ev_w_in[j]
            conv_in, hq, hf, hi, hg = jnp.split(
                u, [2 * CONV_CH, 2 * CONV_CH + HGRN_KDIM, 2 * CONV_CH + 2 * HGRN_KDIM,
                    2 * CONV_CH + 2 * HGRN_KDIM + HGRN_VDIM], axis=-1)
            a_out = conformer_conv(conv_in, ev_conv_w[j], ev_conv_b[j], ev_conv_ln_g[j], ev_conv_ln_b[j])
            b_out = hgrn2(hq, hf, hi, hg, lower_bounds[j], ev_gnorm_g[j])
            mix = jnp.concatenate([a_out, b_out], axis=-1) @ ev_w_out[j]
        else:
            u = x @ od_w_in[j]
            c_q, c_k, c_v, c_f, d_q, d_k, d_v = jnp.split(
                u, [FOX_WIDTH, 2 * FOX_WIDTH, 3 * FOX_WIDTH, 3 * FOX_WIDTH + FOX_HEADS,
                    3 * FOX_WIDTH + FOX_HEADS + CA_WIDTH, 3 * FOX_WIDTH + FOX_HEADS + 2 * CA_WIDTH], axis=-1)
            c_out = forgetting_attention(c_q, c_k, c_v, c_f + fox_b_f[j])
            d_out = chunked_relpos_attention(d_q, d_k, d_v, rel_bias[j])
            mix = jnp.concatenate([c_out, d_out], axis=-1) @ od_w_out[j]
        x = layer_norm(ALPHA * x + mix, ln_mix_g[l], ln_mix_b[l])
        h = jnp.square(jax.nn.relu(x @ mlp_w1[l])) @ mlp_w2[l]
        x = layer_norm(ALPHA * x + h, ln_mlp_g[l], ln_mlp_b[l])
    return x


import jax as _jax
import jax.numpy as _jnp

TWIN_FORMAT = 'train_step'
FWD_PARAMS = ['x', 'ev_w_in', 'ev_conv_w', 'ev_conv_b', 'ev_conv_ln_g', 'ev_conv_ln_b', 'hgrn_lb_logits', 'ev_gnorm_g', 'ev_w_out', 'od_w_in', 'fox_b_f', 'rel_bias', 'od_w_out', 'ln_mix_g', 'ln_mix_b', 'mlp_w1', 'mlp_w2', 'ln_mlp_g', 'ln_mlp_b']
TWIN_WEIGHTS = ['ev_w_in', 'ev_conv_w', 'ev_conv_b', 'ev_conv_ln_g', 'ev_conv_ln_b', 'hgrn_lb_logits', 'ev_gnorm_g', 'ev_w_out', 'od_w_in', 'fox_b_f', 'rel_bias', 'od_w_out', 'ln_mix_g', 'ln_mix_b', 'mlp_w1', 'mlp_w2', 'ln_mlp_g', 'ln_mlp_b']
TWIN_DIFF_INPUT = 'x'
TWIN_INPUTS = ['x', 'ev_w_in', 'ev_conv_w', 'ev_conv_b', 'ev_conv_ln_g', 'ev_conv_ln_b', 'hgrn_lb_logits', 'ev_gnorm_g', 'ev_w_out', 'od_w_in', 'fox_b_f', 'rel_bias', 'od_w_out', 'ln_mix_g', 'ln_mix_b', 'mlp_w1', 'mlp_w2', 'ln_mlp_g', 'ln_mlp_b', 'loss_target', 'm_ev_w_in', 'm_ev_conv_w', 'm_ev_conv_b', 'm_ev_conv_ln_g', 'm_ev_conv_ln_b', 'm_hgrn_lb_logits', 'm_ev_gnorm_g', 'm_ev_w_out', 'm_od_w_in', 'm_fox_b_f', 'm_rel_bias', 'm_od_w_out', 'm_ln_mix_g', 'm_ln_mix_b', 'm_mlp_w1', 'm_mlp_w2', 'm_ln_mlp_g', 'm_ln_mlp_b', 'v_ev_w_in', 'v_ev_conv_w', 'v_ev_conv_b', 'v_ev_conv_ln_g', 'v_ev_conv_ln_b', 'v_hgrn_lb_logits', 'v_ev_gnorm_g', 'v_ev_w_out', 'v_od_w_in', 'v_fox_b_f', 'v_rel_bias', 'v_od_w_out', 'v_ln_mix_g', 'v_ln_mix_b', 'v_mlp_w1', 'v_mlp_w2', 'v_ln_mlp_g', 'v_ln_mlp_b']
TWIN_OUTPUTS = ['loss', 'grad_x', 'grad_ev_w_in', 'grad_ev_conv_w', 'grad_ev_conv_b', 'grad_ev_conv_ln_g', 'grad_ev_conv_ln_b', 'grad_hgrn_lb_logits', 'grad_ev_gnorm_g', 'grad_ev_w_out', 'grad_od_w_in', 'grad_fox_b_f', 'grad_rel_bias', 'grad_od_w_out', 'grad_ln_mix_g', 'grad_ln_mix_b', 'grad_mlp_w1', 'grad_mlp_w2', 'grad_ln_mlp_g', 'grad_ln_mlp_b', 'delta_ev_w_in', 'delta_ev_conv_w', 'delta_ev_conv_b', 'delta_ev_conv_ln_g', 'delta_ev_conv_ln_b', 'delta_hgrn_lb_logits', 'delta_ev_gnorm_g', 'delta_ev_w_out', 'delta_od_w_in', 'delta_fox_b_f', 'delta_rel_bias', 'delta_od_w_out', 'delta_ln_mix_g', 'delta_ln_mix_b', 'delta_mlp_w1', 'delta_mlp_w2', 'delta_ln_mlp_g', 'delta_ln_mlp_b', 'new_m_ev_w_in', 'new_m_ev_conv_w', 'new_m_ev_conv_b', 'new_m_ev_conv_ln_g', 'new_m_ev_conv_ln_b', 'new_m_hgrn_lb_logits', 'new_m_ev_gnorm_g', 'new_m_ev_w_out', 'new_m_od_w_in', 'new_m_fox_b_f', 'new_m_rel_bias', 'new_m_od_w_out', 'new_m_ln_mix_g', 'new_m_ln_mix_b', 'new_m_mlp_w1', 'new_m_mlp_w2', 'new_m_ln_mlp_g', 'new_m_ln_mlp_b', 'new_v_ev_w_in', 'new_v_ev_conv_w', 'new_v_ev_conv_b', 'new_v_ev_conv_ln_g', 'new_v_ev_conv_ln_b', 'new_v_hgrn_lb_logits', 'new_v_ev_gnorm_g', 'new_v_ev_w_out', 'new_v_od_w_in', 'new_v_fox_b_f', 'new_v_rel_bias', 'new_v_od_w_out', 'new_v_ln_mix_g', 'new_v_ln_mix_b', 'new_v_mlp_w1', 'new_v_mlp_w2', 'new_v_ln_mlp_g', 'new_v_ln_mlp_b']
TWIN_LEAF_KINDS = {'loss': 'loss', 'grad_x': 'grad_x', 'grad_ev_w_in': 'grad_w', 'grad_ev_conv_w': 'grad_w', 'grad_ev_conv_b': 'grad_w', 'grad_ev_conv_ln_g': 'grad_w', 'grad_ev_conv_ln_b': 'grad_w', 'grad_hgrn_lb_logits': 'grad_w', 'grad_ev_gnorm_g': 'grad_w', 'grad_ev_w_out': 'grad_w', 'grad_od_w_in': 'grad_w', 'grad_fox_b_f': 'grad_w', 'grad_rel_bias': 'grad_w', 'grad_od_w_out': 'grad_w', 'grad_ln_mix_g': 'grad_w', 'grad_ln_mix_b': 'grad_w', 'grad_mlp_w1': 'grad_w', 'grad_mlp_w2': 'grad_w', 'grad_ln_mlp_g': 'grad_w', 'grad_ln_mlp_b': 'grad_w', 'delta_ev_w_in': 'delta_w', 'delta_ev_conv_w': 'delta_w', 'delta_ev_conv_b': 'delta_w', 'delta_ev_conv_ln_g': 'delta_w', 'delta_ev_conv_ln_b': 'delta_w', 'delta_hgrn_lb_logits': 'delta_w', 'delta_ev_gnorm_g': 'delta_w', 'delta_ev_w_out': 'delta_w', 'delta_od_w_in': 'delta_w', 'delta_fox_b_f': 'delta_w', 'delta_rel_bias': 'delta_w', 'delta_od_w_out': 'delta_w', 'delta_ln_mix_g': 'delta_w', 'delta_ln_mix_b': 'delta_w', 'delta_mlp_w1': 'delta_w', 'delta_mlp_w2': 'delta_w', 'delta_ln_mlp_g': 'delta_w', 'delta_ln_mlp_b': 'delta_w', 'new_m_ev_w_in': 'new_m', 'new_m_ev_conv_w': 'new_m', 'new_m_ev_conv_b': 'new_m', 'new_m_ev_conv_ln_g': 'new_m', 'new_m_ev_conv_ln_b': 'new_m', 'new_m_hgrn_lb_logits': 'new_m', 'new_m_ev_gnorm_g': 'new_m', 'new_m_ev_w_out': 'new_m', 'new_m_od_w_in': 'new_m', 'new_m_fox_b_f': 'new_m', 'new_m_rel_bias': 'new_m', 'new_m_od_w_out': 'new_m', 'new_m_ln_mix_g': 'new_m', 'new_m_ln_mix_b': 'new_m', 'new_m_mlp_w1': 'new_m', 'new_m_mlp_w2': 'new_m', 'new_m_ln_mlp_g': 'new_m', 'new_m_ln_mlp_b': 'new_m', 'new_v_ev_w_in': 'new_v', 'new_v_ev_conv_w': 'new_v', 'new_v_ev_conv_b': 'new_v', 'new_v_ev_conv_ln_g': 'new_v', 'new_v_ev_conv_ln_b': 'new_v', 'new_v_hgrn_lb_logits': 'new_v', 'new_v_ev_gnorm_g': 'new_v', 'new_v_ev_w_out': 'new_v', 'new_v_od_w_in': 'new_v', 'new_v_fox_b_f': 'new_v', 'new_v_rel_bias': 'new_v', 'new_v_od_w_out': 'new_v', 'new_v_ln_mix_g': 'new_v', 'new_v_ln_mix_b': 'new_v', 'new_v_mlp_w1': 'new_v', 'new_v_mlp_w2': 'new_v', 'new_v_ln_mlp_g': 'new_v', 'new_v_ln_mlp_b': 'new_v'}


def _forward(args):
    return _fwd_reference(*[args[k] for k in FWD_PARAMS])


def _output_shape():
    def fwd():
        inp = _fwd_setup_inputs(0)
        return _fwd_reference(*[inp[k] for k in FWD_PARAMS])
    out = _jax.eval_shape(fwd)
    return out.shape, out.dtype

N_MICROBATCH = 1
ADAM_LR = 0.001
ADAM_B1 = 0.9
ADAM_B2 = 0.999
ADAM_EPS = 1e-08
ADAM_WD = 0.01
ADAM_STEP = 10
PER_EXAMPLE_BATCH_AXIS = {'x': 0, 'loss_target': 0}
SHARED_INPUTS = []
_WEIGHT_DTYPES = {'ev_w_in': _jnp.float32, 'ev_conv_w': _jnp.float32, 'ev_conv_b': _jnp.float32, 'ev_conv_ln_g': _jnp.float32, 'ev_conv_ln_b': _jnp.float32, 'hgrn_lb_logits': _jnp.float32, 'ev_gnorm_g': _jnp.float32, 'ev_w_out': _jnp.float32, 'od_w_in': _jnp.float32, 'fox_b_f': _jnp.float32, 'rel_bias': _jnp.float32, 'od_w_out': _jnp.float32, 'ln_mix_g': _jnp.float32, 'ln_mix_b': _jnp.float32, 'mlp_w1': _jnp.float32, 'mlp_w2': _jnp.float32, 'ln_mlp_g': _jnp.float32, 'ln_mlp_b': _jnp.float32}
MOMENT_SCALE = {'ev_w_in': 2.098378e-02, 'ev_conv_w': 3.343800e-02, 'ev_conv_b': 4.007690e-01, 'ev_conv_ln_g': 1.473304e-01, 'ev_conv_ln_b': 2.401752e-01, 'hgrn_lb_logits': 2.745563e-03, 'ev_gnorm_g': 2.953193e-02, 'ev_w_out': 1.144622e-01, 'od_w_in': 3.783262e-02, 'fox_b_f': 1.081166e-01, 'rel_bias': 4.473155e-03, 'od_w_out': 1.476012e-01, 'ln_mix_g': 5.009182e+00, 'ln_mix_b': 1.402074e+00, 'mlp_w1': 3.541949e-02, 'mlp_w2': 4.287465e-01, 'ln_mlp_g': 2.382711e+01, 'ln_mlp_b': 5.462546e+00}


def _to_microbatches(a, axis):
    t = _jnp.moveaxis(a, axis, 0)
    t = t.reshape((N_MICROBATCH, t.shape[0] // N_MICROBATCH) + t.shape[1:])
    return _jnp.moveaxis(t, 1, axis + 1)


def setup_inputs(seed: int = 0) -> dict:
    inp = _fwd_setup_inputs(seed)
    key = _jax.random.fold_in(_jax.random.key(seed), 7919)
    shape, _ = _output_shape()
    out = dict(inp)
    out["loss_target"] = _jax.random.normal(_jax.random.fold_in(key, 0), shape, _jnp.float32)
    for i, name in enumerate(TWIN_WEIGHTS):
        w = inp[name].astype(_jnp.float32)
        if MOMENT_SCALE is None:
            s = _jnp.sqrt(_jnp.mean(_jnp.square(w)) + 1e-30)
        else:
            s = MOMENT_SCALE[name]
        km, kv = _jax.random.split(_jax.random.fold_in(key, i + 1))
        out[name] = w
        out["m_" + name] = s * _jax.random.normal(km, w.shape, _jnp.float32)
        out["v_" + name] = (s * s) * _jax.random.uniform(kv, w.shape, _jnp.float32, 0.5, 1.5)
    if N_MICROBATCH > 1:
        for name, axis in PER_EXAMPLE_BATCH_AXIS.items():
            out[name] = _to_microbatches(out[name], axis)
    return {'x': out['x'], 'ev_w_in': out['ev_w_in'], 'ev_conv_w': out['ev_conv_w'], 'ev_conv_b': out['ev_conv_b'], 'ev_conv_ln_g': out['ev_conv_ln_g'], 'ev_conv_ln_b': out['ev_conv_ln_b'], 'hgrn_lb_logits': out['hgrn_lb_logits'], 'ev_gnorm_g': out['ev_gnorm_g'], 'ev_w_out': out['ev_w_out'], 'od_w_in': out['od_w_in'], 'fox_b_f': out['fox_b_f'], 'rel_bias': out['rel_bias'], 'od_w_out': out['od_w_out'], 'ln_mix_g': out['ln_mix_g'], 'ln_mix_b': out['ln_mix_b'], 'mlp_w1': out['mlp_w1'], 'mlp_w2': out['mlp_w2'], 'ln_mlp_g': out['ln_mlp_g'], 'ln_mlp_b': out['ln_mlp_b'], 'loss_target': out['loss_target'], 'm_ev_w_in': out['m_ev_w_in'], 'm_ev_conv_w': out['m_ev_conv_w'], 'm_ev_conv_b': out['m_ev_conv_b'], 'm_ev_conv_ln_g': out['m_ev_conv_ln_g'], 'm_ev_conv_ln_b': out['m_ev_conv_ln_b'], 'm_hgrn_lb_logits': out['m_hgrn_lb_logits'], 'm_ev_gnorm_g': out['m_ev_gnorm_g'], 'm_ev_w_out': out['m_ev_w_out'], 'm_od_w_in': out['m_od_w_in'], 'm_fox_b_f': out['m_fox_b_f'], 'm_rel_bias': out['m_rel_bias'], 'm_od_w_out': out['m_od_w_out'], 'm_ln_mix_g': out['m_ln_mix_g'], 'm_ln_mix_b': out['m_ln_mix_b'], 'm_mlp_w1': out['m_mlp_w1'], 'm_mlp_w2': out['m_mlp_w2'], 'm_ln_mlp_g': out['m_ln_mlp_g'], 'm_ln_mlp_b': out['m_ln_mlp_b'], 'v_ev_w_in': out['v_ev_w_in'], 'v_ev_conv_w': out['v_ev_conv_w'], 'v_ev_conv_b': out['v_ev_conv_b'], 'v_ev_conv_ln_g': out['v_ev_conv_ln_g'], 'v_ev_conv_ln_b': out['v_ev_conv_ln_b'], 'v_hgrn_lb_logits': out['v_hgrn_lb_logits'], 'v_ev_gnorm_g': out['v_ev_gnorm_g'], 'v_ev_w_out': out['v_ev_w_out'], 'v_od_w_in': out['v_od_w_in'], 'v_fox_b_f': out['v_fox_b_f'], 'v_rel_bias': out['v_rel_bias'], 'v_od_w_out': out['v_od_w_out'], 'v_ln_mix_g': out['v_ln_mix_g'], 'v_ln_mix_b': out['v_ln_mix_b'], 'v_mlp_w1': out['v_mlp_w1'], 'v_mlp_w2': out['v_mlp_w2'], 'v_ln_mlp_g': out['v_ln_mlp_g'], 'v_ln_mlp_b': out['v_ln_mlp_b']}


def _loss(weights, diff, rest, loss_target):
    with _jax.named_scope("forward"):
        args = {**rest, TWIN_DIFF_INPUT: diff, **{k: w.astype(_WEIGHT_DTYPES[k]) for k, w in weights.items()}}
        y = _forward(args)
    with _jax.named_scope("loss_head"):
        err = _jnp.square(y.astype(_jnp.float32) - loss_target)
        return 0.5 * _jnp.sum(_jnp.mean(err, axis=-1)) if err.ndim else 0.5 * err


def _adamw(w, g, m, v):
    m = ADAM_B1 * m + (1.0 - ADAM_B1) * g
    v = ADAM_B2 * v + (1.0 - ADAM_B2) * _jnp.square(g)
    m_hat = m / (1.0 - ADAM_B1 ** ADAM_STEP)
    v_hat = v / (1.0 - ADAM_B2 ** ADAM_STEP)
    delta = -ADAM_LR * (m_hat / (_jnp.sqrt(v_hat) + ADAM_EPS) + ADAM_WD * w)
    return delta, m, v


def reference(x, ev_w_in, ev_conv_w, ev_conv_b, ev_conv_ln_g, ev_conv_ln_b, hgrn_lb_logits, ev_gnorm_g, ev_w_out, od_w_in, fox_b_f, rel_bias, od_w_out, ln_mix_g, ln_mix_b, mlp_w1, mlp_w2, ln_mlp_g, ln_mlp_b, loss_target, m_ev_w_in, m_ev_conv_w, m_ev_conv_b, m_ev_conv_ln_g, m_ev_conv_ln_b, m_hgrn_lb_logits, m_ev_gnorm_g, m_ev_w_out, m_od_w_in, m_fox_b_f, m_rel_bias, m_od_w_out, m_ln_mix_g, m_ln_mix_b, m_mlp_w1, m_mlp_w2, m_ln_mlp_g, m_ln_mlp_b, v_ev_w_in, v_ev_conv_w, v_ev_conv_b, v_ev_conv_ln_g, v_ev_conv_ln_b, v_hgrn_lb_logits, v_ev_gnorm_g, v_ev_w_out, v_od_w_in, v_fox_b_f, v_rel_bias, v_od_w_out, v_ln_mix_g, v_ln_mix_b, v_mlp_w1, v_mlp_w2, v_ln_mlp_g, v_ln_mlp_b):
    given = dict(x=x, ev_w_in=ev_w_in, ev_conv_w=ev_conv_w, ev_conv_b=ev_conv_b, ev_conv_ln_g=ev_conv_ln_g, ev_conv_ln_b=ev_conv_ln_b, hgrn_lb_logits=hgrn_lb_logits, ev_gnorm_g=ev_gnorm_g, ev_w_out=ev_w_out, od_w_in=od_w_in, fox_b_f=fox_b_f, rel_bias=rel_bias, od_w_out=od_w_out, ln_mix_g=ln_mix_g, ln_mix_b=ln_mix_b, mlp_w1=mlp_w1, mlp_w2=mlp_w2, ln_mlp_g=ln_mlp_g, ln_mlp_b=ln_mlp_b, loss_target=loss_target, m_ev_w_in=m_ev_w_in, m_ev_conv_w=m_ev_conv_w, m_ev_conv_b=m_ev_conv_b, m_ev_conv_ln_g=m_ev_conv_ln_g, m_ev_conv_ln_b=m_ev_conv_ln_b, m_hgrn_lb_logits=m_hgrn_lb_logits, m_ev_gnorm_g=m_ev_gnorm_g, m_ev_w_out=m_ev_w_out, m_od_w_in=m_od_w_in, m_fox_b_f=m_fox_b_f, m_rel_bias=m_rel_bias, m_od_w_out=m_od_w_out, m_ln_mix_g=m_ln_mix_g, m_ln_mix_b=m_ln_mix_b, m_mlp_w1=m_mlp_w1, m_mlp_w2=m_mlp_w2, m_ln_mlp_g=m_ln_mlp_g, m_ln_mlp_b=m_ln_mlp_b, v_ev_w_in=v_ev_w_in, v_ev_conv_w=v_ev_conv_w, v_ev_conv_b=v_ev_conv_b, v_ev_conv_ln_g=v_ev_conv_ln_g, v_ev_conv_ln_b=v_ev_conv_ln_b, v_hgrn_lb_logits=v_hgrn_lb_logits, v_ev_gnorm_g=v_ev_gnorm_g, v_ev_w_out=v_ev_w_out, v_od_w_in=v_od_w_in, v_fox_b_f=v_fox_b_f, v_rel_bias=v_rel_bias, v_od_w_out=v_od_w_out, v_ln_mix_g=v_ln_mix_g, v_ln_mix_b=v_ln_mix_b, v_mlp_w1=v_mlp_w1, v_mlp_w2=v_mlp_w2, v_ln_mlp_g=v_ln_mlp_g, v_ln_mlp_b=v_ln_mlp_b)
    weights = {n: given[n] for n in TWIN_WEIGHTS}
    shared = {n: given[n] for n in SHARED_INPUTS}
    per_example = {n: given[n] for n in ['x']}
    grad_fn = _jax.value_and_grad(_loss, argnums=(0, 1))

    def one_microbatch(ex, loss_target):
        ex = dict(ex)
        diff = ex.pop(TWIN_DIFF_INPUT)
        return grad_fn(weights, diff, {**shared, **ex}, loss_target)

    if N_MICROBATCH == 1:
        loss, (grad_w, grad_x) = one_microbatch(per_example, given["loss_target"])
    else:
        def body(carry, xs):
            loss_sum, grad_sum = carry
            l_k, (gw_k, gx_k) = one_microbatch(xs[0], xs[1])
            with _jax.named_scope("update"):
                return (loss_sum + l_k, _jax.tree.map(_jnp.add, grad_sum, gw_k)), gx_k

        init = (_jnp.zeros((), _jnp.float32), _jax.tree.map(_jnp.zeros_like, weights))
        (loss, grad_w), grad_x = _jax.lax.scan(body, init, (per_example, given["loss_target"]))
    with _jax.named_scope("update"):
        delta_w, new_m, new_v = {}, {}, {}
        for n in TWIN_WEIGHTS:
            delta_w[n], new_m[n], new_v[n] = _adamw(weights[n], grad_w[n], given["m_" + n], given["v_" + n])
    return (loss, grad_x, *[grad_w[n] for n in TWIN_WEIGHTS], *[delta_w[n] for n in TWIN_WEIGHTS],
            *[new_m[n] for n in TWIN_WEIGHTS], *[new_v[n] for n in TWIN_WEIGHTS])
```

```python
import functools
import math

import jax
import jax.numpy as jnp
from jax import lax
from jax.experimental import pallas as pl
from jax.experimental.pallas import tpu as pltpu

F32 = jnp.float32
BF16 = jnp.bfloat16
HI = lax.Precision.HIGHEST
MESH_ID = pl.DeviceIdType.MESH

N_DEV = 8
LN_EPS = 1e-5
CHUNK = 64
HEAD_DIM = 128
CONV_WIDTH = 31
CONV_HALO = 32
CA_LEFT_CHUNKS = 8
CA_TILE = 256
CA_WIN = CA_TILE + CA_LEFT_CHUNKS * CHUNK
CA_SKEW = 1024
REL_CLIP = 256
REL_TABLE = (CHUNK - 1) + REL_CLIP + 1
NEG = -1e30
ADAM_LR = 0.001
ADAM_B1 = 0.9
ADAM_B2 = 0.999
ADAM_EPS = 1e-08
ADAM_WD = 0.01
ADAM_STEP = 10
VMEM_LIMIT_V7X = 56 * 1024 * 1024


def _cparams(*sem):
    return pltpu.CompilerParams(dimension_semantics=sem, vmem_limit_bytes=VMEM_LIMIT_V7X)


def _tile(n, t):
    if n <= t:
        return n
    for c in range(t - t % 128, 0, -128):
        if n % c == 0:
            return c
    return n


def _sigmoid(x):
    return 1.0 / (1.0 + jnp.exp(-x))


def _dot(a, b, dims, precision=None):
    return lax.dot_general(a, b, (dims, ((), ())), preferred_element_type=F32, precision=precision)


def _round_bf16(x):
    return x.astype(BF16).astype(F32)


NN = ((1,), (0,))
NT = ((1,), (1,))
TN = ((0,), (0,))


def _mm(a, b, *, mode, name, out_dtypes=(F32,), extras=(), epi=None, exact_products=False, tm=512, tn=1024, tk=2048):
    dims = {"nn": NN, "nt": NT, "tn": TN}[mode]
    if mode == "tn":
        K, M = a.shape
    else:
        M, K = a.shape
    N = b.shape[0] if mode == "nt" else b.shape[1]
    tm, tn, tk = _tile(M, tm), _tile(N, tn), _tile(K, tk)
    nk = K // tk
    n_ex, n_out = len(extras), len(out_dtypes)

    def body(*refs):
        a_ref, b_ref = refs[0], refs[1]
        ex_refs = refs[2:2 + n_ex]
        o_refs = refs[2 + n_ex:2 + n_ex + n_out]
        if exact_products:
            d = _dot(_round_bf16(a_ref[...]), _round_bf16(b_ref[...]), dims, HI)
        else:
            d = _dot(a_ref[...].astype(BF16), b_ref[...].astype(BF16), dims)

        def finish(acc):
            outs = (acc,) if epi is None else epi(acc, *[r[...] for r in ex_refs])
            for o_ref, o in zip(o_refs, outs):
                o_ref[...] = o.astype(o_ref.dtype)

        if nk == 1:
            finish(d)
        else:
            acc_ref = refs[-1]
            k = pl.program_id(2)

            @pl.when(k == 0)
            def _():
                acc_ref[...] = d

            @pl.when(k > 0)
            def _():
                acc_ref[...] += d

            @pl.when(k == nk - 1)
            def _():
                finish(acc_ref[...])

    a_spec = pl.BlockSpec((tk, tm), lambda i, j, k: (k, i)) if mode == "tn" else pl.BlockSpec((tm, tk), lambda i, j, k: (i, k))
    b_spec = pl.BlockSpec((tn, tk), lambda i, j, k: (j, k)) if mode == "nt" else pl.BlockSpec((tk, tn), lambda i, j, k: (k, j))
    mn_spec = pl.BlockSpec((tm, tn), lambda i, j, k: (i, j))
    outs = pl.pallas_call(
        body, name=name, grid=(M // tm, N // tn, nk),
        in_specs=[a_spec, b_spec] + [mn_spec] * n_ex,
        out_specs=[mn_spec] * n_out,
        out_shape=[jax.ShapeDtypeStruct((M, N), dt) for dt in out_dtypes],
        scratch_shapes=[pltpu.VMEM((tm, tn), F32)] if nk > 1 else [],
        compiler_params=_cparams("parallel", "parallel", "arbitrary"),
    )(a, b, *extras)
    return outs[0] if n_out == 1 else outs


def _ln_fwd(x, r, g, b, alpha, name, tr=256):
    S, D = x.shape
    tr = _tile(S, tr)

    def body(x_ref, r_ref, g_ref, b_ref, y_ref, xh_ref, rs_ref):
        z = alpha * x_ref[...] + r_ref[...]
        zc = z - jnp.mean(z, axis=-1, keepdims=True)
        rs = lax.rsqrt(jnp.mean(zc * zc, axis=-1, keepdims=True) + LN_EPS)
        xh = zc * rs
        xh_ref[...] = xh
        rs_ref[...] = rs
        y_ref[...] = xh * g_ref[...] + b_ref[...]

    row = pl.BlockSpec((tr, D), lambda i: (i, 0))
    par = pl.BlockSpec((1, D), lambda i: (0, 0))
    return pl.pallas_call(
        body, name=name, grid=(S // tr,),
        in_specs=[row, row, par, par],
        out_specs=[row, row, pl.BlockSpec((tr, 1), lambda i: (i, 0))],
        out_shape=[jax.ShapeDtypeStruct((S, D), F32), jax.ShapeDtypeStruct((S, D), F32), jax.ShapeDtypeStruct((S, 1), F32)],
        compiler_params=_cparams("parallel"),
    )(x, r, g, b)


def _ln_bwd(dy, dy2, scale2, xh, rs, g, name, tr=256):
    S, D = dy.shape
    tr = _tile(S, tr)
    two = dy2 is not None

    def body(*refs):
        if two:
            dy_ref, dy2_ref, xh_ref, rs_ref, g_ref, dz_ref, dg_ref, db_ref = refs
            dyt = dy_ref[...] + scale2 * dy2_ref[...]
        else:
            dy_ref, xh_ref, rs_ref, g_ref, dz_ref, dg_ref, db_ref = refs
            dyt = dy_ref[...]
        xh = xh_ref[...]
        dxh = dyt * g_ref[...]
        m1 = jnp.mean(dxh, axis=-1, keepdims=True)
        m2 = jnp.mean(dxh * xh, axis=-1, keepdims=True)
        dz_ref[...] = rs_ref[...] * (dxh - m1 - xh * m2)

        @pl.when(pl.program_id(0) == 0)
        def _():
            dg_ref[...] = jnp.zeros_like(dg_ref)
            db_ref[...] = jnp.zeros_like(db_ref)

        dg_ref[...] += jnp.sum(dyt * xh, axis=0, keepdims=True)
        db_ref[...] += jnp.sum(dyt, axis=0, keepdims=True)

    row = pl.BlockSpec((tr, D), lambda i: (i, 0))
    par = pl.BlockSpec((1, D), lambda i: (0, 0))
    ins = [dy] + ([dy2] if two else []) + [xh, rs, g]
    return pl.pallas_call(
        body, name=name, grid=(S // tr,),
        in_specs=[row] * (2 if two else 1) + [row, pl.BlockSpec((tr, 1), lambda i: (i, 0)), par],
        out_specs=[row, par, par],
        out_shape=[jax.ShapeDtypeStruct((S, D), F32), jax.ShapeDtypeStruct((1, D), F32), jax.ShapeDtypeStruct((1, D), F32)],
        compiler_params=_cparams("arbitrary"),
    )(*ins)


def _loss_head(y, target, name, tr=256):
    S, D = y.shape
    tr = _tile(S, tr)

    def body(y_ref, t_ref, dy_ref, loss_ref):
        e = y_ref[...] - t_ref[...]
        dy_ref[...] = e * (1.0 / D)

        @pl.when(pl.program_id(0) == 0)
        def _():
            loss_ref[...] = jnp.zeros_like(loss_ref)

        loss_ref[...] += jnp.sum(jnp.sum(e * e, axis=-1, keepdims=True), axis=0, keepdims=True) * (0.5 / D)

    row = pl.BlockSpec((tr, D), lambda i: (i, 0))
    return pl.pallas_call(
        body, name=name, grid=(S // tr,),
        in_specs=[row, row],
        out_specs=[row, pl.BlockSpec((1, 1), lambda i: (0, 0))],
        out_shape=[jax.ShapeDtypeStruct((S, D), F32), jax.ShapeDtypeStruct((1, 1), F32)],
        compiler_params=_cparams("arbitrary"),
    )(y, target)


def _conv_fwd(u, w, cb, lg, lb, name, tt=512):
    S = u.shape[0]
    C = w.shape[1]
    tt = _tile(S, tt)
    hpt = tt // CONV_HALO

    def body(a_ref, g_ref, ap_ref, gp_ref, w_ref, cb_ref, lg_ref, lb_ref, out_ref, cv_ref, hext):
        i = pl.program_id(0)
        hext[pl.ds(CONV_HALO, tt), :] = a_ref[...] * _sigmoid(g_ref[...])
        hp = ap_ref[...] * _sigmoid(gp_ref[...])
        hext[pl.ds(0, CONV_HALO), :] = jnp.where(i > 0, hp, 0.0)
        acc = jnp.zeros((tt, C), F32)
        for k in range(CONV_WIDTH):
            acc = acc + w_ref[pl.ds(k, 1), :] * hext[pl.ds(CONV_HALO - (CONV_WIDTH - 1) + k, tt), :]
        cv = acc + cb_ref[...]
        cv_ref[...] = cv
        zc = cv - jnp.mean(cv, axis=-1, keepdims=True)
        n = zc * lax.rsqrt(jnp.mean(zc * zc, axis=-1, keepdims=True) + LN_EPS) * lg_ref[...] + lb_ref[...]
        out_ref[...] = n * _sigmoid(n)

    cur = lambda cb_: pl.BlockSpec((tt, C), lambda i: (i, cb_))
    prev = lambda cb_: pl.BlockSpec((CONV_HALO, C), lambda i: (jnp.maximum(i * hpt - 1, 0), cb_))
    par = pl.BlockSpec((1, C), lambda i: (0, 0))
    row = pl.BlockSpec((tt, C), lambda i: (i, 0))
    return pl.pallas_call(
        body, name=name, grid=(S // tt,),
        in_specs=[cur(0), cur(1), prev(0), prev(1), pl.BlockSpec((CONV_WIDTH, C), lambda i: (0, 0)), par, par, par],
        out_specs=[row, row],
        out_shape=[jax.ShapeDtypeStruct((S, C), F32)] * 2,
        scratch_shapes=[pltpu.VMEM((tt + CONV_HALO, C), F32)],
        compiler_params=_cparams("parallel"),
    )(u, u, u, u, w, cb, lg, lb)


def _conv_bwd_norm(da, cv, lg, lb, name, tt=512):
    S, C = cv.shape
    tt = _tile(S, tt)

    def body(da_ref, cv_ref, lg_ref, lb_ref, dcv_ref, dlg_ref, dlb_ref, dcb_ref):
        cv = cv_ref[...]
        zc = cv - jnp.mean(cv, axis=-1, keepdims=True)
        rs = lax.rsqrt(jnp.mean(zc * zc, axis=-1, keepdims=True) + LN_EPS)
        xh = zc * rs
        n = xh * lg_ref[...] + lb_ref[...]
        sg = _sigmoid(n)
        dn = da_ref[...] * sg * (1.0 + n * (1.0 - sg))
        dxh = dn * lg_ref[...]
        m1 = jnp.mean(dxh, axis=-1, keepdims=True)
        m2 = jnp.mean(dxh * xh, axis=-1, keepdims=True)
        dcv = rs * (dxh - m1 - xh * m2)
        dcv_ref[...] = dcv

        @pl.when(pl.program_id(0) == 0)
        def _():
            dlg_ref[...] = jnp.zeros_like(dlg_ref)
            dlb_ref[...] = jnp.zeros_like(dlb_ref)
            dcb_ref[...] = jnp.zeros_like(dcb_ref)

        dlg_ref[...] += jnp.sum(dn * xh, axis=0, keepdims=True)
        dlb_ref[...] += jnp.sum(dn, axis=0, keepdims=True)
        dcb_ref[...] += jnp.sum(dcv, axis=0, keepdims=True)

    row = pl.BlockSpec((tt, C), lambda i: (i, 0))
    par = pl.BlockSpec((1, C), lambda i: (0, 0))
    return pl.pallas_call(
        body, name=name, grid=(S // tt,),
        in_specs=[row, row, par, par],
        out_specs=[row, par, par, par],
        out_shape=[jax.ShapeDtypeStruct((S, C), F32)] + [jax.ShapeDtypeStruct((1, C), F32)] * 3,
        compiler_params=_cparams("arbitrary"),
    )(da, cv, lg, lb)


def _conv_bwd_taps(dcv, u, w, name, tt=512):
    S, C = dcv.shape
    tt = _tile(S, tt)
    hpt = tt // CONV_HALO
    nt = S // tt
    WPAD = 32

    def body(dc_ref, dn_ref, a_ref, g_ref, ap_ref, gp_ref, w_ref, da_ref, dg_ref, dw_ref, hext, dext):
        i = pl.program_id(0)
        a = a_ref[...]
        sg = _sigmoid(g_ref[...])
        hext[pl.ds(CONV_HALO, tt), :] = a * sg
        hp = ap_ref[...] * _sigmoid(gp_ref[...])
        hext[pl.ds(0, CONV_HALO), :] = jnp.where(i > 0, hp, 0.0)
        dc = dc_ref[...]
        dext[pl.ds(0, tt), :] = dc
        dext[pl.ds(tt, CONV_HALO), :] = jnp.where(i < nt - 1, dn_ref[...], 0.0)

        @pl.when(i == 0)
        def _():
            dw_ref[...] = jnp.zeros_like(dw_ref)

        dh = jnp.zeros((tt, C), F32)
        for k in range(CONV_WIDTH):
            dh = dh + w_ref[pl.ds(k, 1), :] * dext[pl.ds(CONV_WIDTH - 1 - k, tt), :]
            hk = hext[pl.ds(CONV_HALO - (CONV_WIDTH - 1) + k, tt), :]
            dw_ref[pl.ds(k, 1), :] += jnp.sum(dc * hk, axis=0, keepdims=True)
        da_ref[...] = dh * sg
        dg_ref[...] = dh * a * sg * (1.0 - sg)

    row = pl.BlockSpec((tt, C), lambda i: (i, 0))
    nxt = pl.BlockSpec((CONV_HALO, C), lambda i: (jnp.minimum((i + 1) * hpt, S // CONV_HALO - 1), 0))
    cur = lambda cb_: pl.BlockSpec((tt, C), lambda i: (i, cb_))
    prev = lambda cb_: pl.BlockSpec((CONV_HALO, C), lambda i: (jnp.maximum(i * hpt - 1, 0), cb_))
    da, dg, dw = pl.pallas_call(
        body, name=name, grid=(nt,),
        in_specs=[row, nxt, cur(0), cur(1), prev(0), prev(1), pl.BlockSpec((CONV_WIDTH, C), lambda i: (0, 0))],
        out_specs=[row, row, pl.BlockSpec((WPAD, C), lambda i: (0, 0))],
        out_shape=[jax.ShapeDtypeStruct((S, C), F32)] * 2 + [jax.ShapeDtypeStruct((WPAD, C), F32)],
        scratch_shapes=[pltpu.VMEM((tt + CONV_HALO, C), F32)] * 2,
        compiler_params=_cparams("arbitrary"),
    )(dcv, dcv, u, u, u, u, w)
    return da, dg, dw[:CONV_WIDTH]


def _lower_bound(logits):
    e = jnp.exp(logits - jnp.max(logits, axis=0, keepdims=True))
    p = e / jnp.sum(e, axis=0, keepdims=True)
    return p[0:1, :], p


def _tri(n, lower):
    r = lax.broadcasted_iota(jnp.int32, (n, n), 0)
    c = lax.broadcasted_iota(jnp.int32, (n, n), 1)
    return ((c <= r) if lower else (c >= r)).astype(F32)


def _hgrn_gates(xq, xf, lb):
    sq = _sigmoid(xq)
    q = xq * sq
    sf = _sigmoid(xf)
    f = lb + (1.0 - lb) * sf
    logf = jnp.log(f)
    L = _dot(_tri(CHUNK, True), logf, NN, HI)
    Lend = L[CHUNK - 1:CHUNK, :]
    eL = jnp.exp(L)
    enL = jnp.exp(-L)
    eLe = jnp.exp(Lend - L)
    kk = 1.0 - f
    return dict(sq=sq, q=q, sf=sf, f=f, L=L, Lend=Lend, eL=eL, enL=enL, eLe=eLe, kk=kk,
                qe=q * eL, ke=kk * enL, kd=kk * eLe)


def _hgrn_fwd(u, lbl, gn, name, tb=256):
    S = u.shape[0]
    W = gn.shape[1]
    H = W // HEAD_DIM
    tb = _tile(S, tb)
    cpb = tb // CHUNK
    nc = S // CHUNK
    R = lbl.shape[0]

    def body(q_ref, f_ref, i_ref, g_ref, lbl_ref, gn_ref, out_ref, raw_ref, st_ref, state):
        @pl.when(pl.program_id(0) == 0)
        def _():
            state[...] = jnp.zeros_like(state)

        lb, _ = _lower_bound(lbl_ref[...])
        tril = _tri(CHUNK, True) > 0.5

        def chunk(c, carry):
            rows = pl.ds(pl.multiple_of(c * CHUNK, CHUNK), CHUNK)
            G = _hgrn_gates(q_ref[rows, :], f_ref[rows, :], lb)
            v = i_ref[rows, :]
            xg = g_ref[rows, :]
            outs = []
            for h in range(H):
                ln = slice(h * HEAD_DIM, (h + 1) * HEAD_DIM)
                qe, ke, kd, vh = G["qe"][:, ln].astype(BF16), G["ke"][:, ln].astype(BF16), G["kd"][:, ln].astype(BF16), v[:, ln].astype(BF16)
                st = state[h]
                st_ref[c, h] = st
                A = jnp.where(tril, _dot(qe, ke, NT), 0.0)
                o = _dot(A.astype(BF16), vh, NN) + _dot(qe, st.astype(BF16), NT)
                state[h] = jnp.exp(G["Lend"][:, ln]) * st + _dot(vh, kd, TN)
                outs.append(o)
            o = jnp.concatenate(outs, axis=1)
            raw_ref[rows, :] = o
            ns = []
            for h in range(H):
                oh = outs[h]
                ns.append(oh * lax.rsqrt(jnp.mean(oh * oh, axis=-1, keepdims=True) + LN_EPS))
            n = jnp.concatenate(ns, axis=1)
            out_ref[rows, :] = n * gn_ref[...] * (xg * _sigmoid(xg))
            return carry

        lax.fori_loop(0, cpb, chunk, 0)

    col = lambda cb_: pl.BlockSpec((tb, W), lambda i: (i, cb_))
    row = pl.BlockSpec((tb, W), lambda i: (i, 0))
    return pl.pallas_call(
        body, name=name, grid=(S // tb,),
        in_specs=[col(2), col(3), col(4), col(5), pl.BlockSpec((R, W), lambda i: (0, 0)), pl.BlockSpec((1, W), lambda i: (0, 0))],
        out_specs=[row, row, pl.BlockSpec((cpb, H, HEAD_DIM, HEAD_DIM), lambda i: (i, 0, 0, 0))],
        out_shape=[jax.ShapeDtypeStruct((S, W), F32), jax.ShapeDtypeStruct((S, W), F32),
                   jax.ShapeDtypeStruct((nc, H, HEAD_DIM, HEAD_DIM), F32)],
        scratch_shapes=[pltpu.VMEM((H, HEAD_DIM, HEAD_DIM), F32)],
        compiler_params=_cparams("arbitrary"),
    )(u, u, u, u, lbl, gn)


def _hgrn_bwd(dout, u, raw, states, lbl, gn, name, tb=256):
    S = u.shape[0]
    W = gn.shape[1]
    H = W // HEAD_DIM
    tb = _tile(S, tb)
    cpb = tb // CHUNK
    nb = S // tb
    R = lbl.shape[0]

    def body(do_ref, q_ref, f_ref, i_ref, g_ref, raw_ref, st_ref, lbl_ref, gn_ref,
             dq_ref, df_ref, di_ref, dg_ref, dgn_ref, dlbl_ref, dstate, dlb_acc):
        @pl.when(pl.program_id(0) == 0)
        def _():
            dstate[...] = jnp.zeros_like(dstate)
            dlb_acc[...] = jnp.zeros_like(dlb_acc)
            dgn_ref[...] = jnp.zeros_like(dgn_ref)

        lb, p = _lower_bound(lbl_ref[...])
        tril = _tri(CHUNK, True) > 0.5
        triu = _tri(CHUNK, False)
        gn_row = gn_ref[...]

        def chunk(cc, carry):
            c = cpb - 1 - cc
            rows = pl.ds(pl.multiple_of(c * CHUNK, CHUNK), CHUNK)
            xq, xf = q_ref[rows, :], f_ref[rows, :]
            G = _hgrn_gates(xq, xf, lb)
            v = i_ref[rows, :]
            xg = g_ref[rows, :]
            dy = do_ref[rows, :]
            o = raw_ref[rows, :]
            sgg = _sigmoid(xg)
            silu_g = xg * sgg
            do_parts, n_parts = [], []
            for h in range(H):
                ln = slice(h * HEAD_DIM, (h + 1) * HEAD_DIM)
                oh = o[:, ln]
                r = lax.rsqrt(jnp.mean(oh * oh, axis=-1, keepdims=True) + LN_EPS)
                nh = oh * r
                dn = dy[:, ln] * gn_row[:, ln] * silu_g[:, ln]
                do_parts.append(r * (dn - nh * jnp.mean(dn * nh, axis=-1, keepdims=True)))
                n_parts.append(nh)
            n = jnp.concatenate(n_parts, axis=1)
            dgn_ref[...] += jnp.sum(dy * n * silu_g, axis=0, keepdims=True)
            dg_ref[rows, :] = dy * n * gn_row * sgg * (1.0 + xg * (1.0 - sgg))
            dqe_p, dke_p, dkd_p, dv_p, dle_p = [], [], [], [], []
            for h in range(H):
                ln = slice(h * HEAD_DIM, (h + 1) * HEAD_DIM)
                qe, ke, kd, vh = G["qe"][:, ln].astype(BF16), G["ke"][:, ln].astype(BF16), G["kd"][:, ln].astype(BF16), v[:, ln].astype(BF16)
                doh = do_parts[h].astype(BF16)
                st = st_ref[c, h]
                dst = dstate[h]
                dec = jnp.exp(G["Lend"][:, ln])
                A = jnp.where(tril, _dot(qe, ke, NT), 0.0).astype(BF16)
                dA = jnp.where(tril, _dot(doh, vh, NT), 0.0).astype(BF16)
                dv_p.append(_dot(A, doh, TN) + _dot(kd, dst.astype(BF16), NT))
                dqe_p.append(_dot(dA, ke, NN) + _dot(doh, st.astype(BF16), NN))
                dke_p.append(_dot(dA, qe, TN))
                dkd_p.append(_dot(vh, dst.astype(BF16), NN))
                dle_p.append(jnp.sum(dst * dec * st, axis=0, keepdims=True))
                dstate[h] = dec * dst + _dot(doh, qe, TN)
            dqe = jnp.concatenate(dqe_p, axis=1)
            dke = jnp.concatenate(dke_p, axis=1)
            dkd = jnp.concatenate(dkd_p, axis=1)
            qe_r, ke_r, kd_r = _round_bf16(G["qe"]), _round_bf16(G["ke"]), _round_bf16(G["kd"])
            dLend = jnp.concatenate(dle_p, axis=1) + jnp.sum(dkd * kd_r, axis=0, keepdims=True)
            dL = dqe * qe_r - dke * ke_r - dkd * kd_r
            dlogf = _dot(triu, dL, NN, HI) + dLend
            dkk = dke * G["enL"] + dkd * G["eLe"]
            dfv = dlogf / G["f"] - dkk
            sf = G["sf"]
            df_ref[rows, :] = dfv * (1.0 - lb) * sf * (1.0 - sf)
            dlb_acc[...] += jnp.sum(dfv * (1.0 - sf), axis=0, keepdims=True)
            sq = G["sq"]
            dq_ref[rows, :] = dqe * G["eL"] * sq * (1.0 + xq * (1.0 - sq))
            di_ref[rows, :] = jnp.concatenate(dv_p, axis=1)
            return carry

        lax.fori_loop(0, cpb, chunk, 0)

        onehot0 = (lax.broadcasted_iota(jnp.int32, (R, W), 0) == 0).astype(F32)
        dlbl_ref[...] = p * (onehot0 - p[0:1, :]) * dlb_acc[...]

    rev = lambda i: nb - 1 - i
    col = lambda cb_: pl.BlockSpec((tb, W), lambda i: (rev(i), cb_))
    row = pl.BlockSpec((tb, W), lambda i: (rev(i), 0))
    par = pl.BlockSpec((1, W), lambda i: (0, 0))
    parR = pl.BlockSpec((R, W), lambda i: (0, 0))
    return pl.pallas_call(
        body, name=name, grid=(nb,),
        in_specs=[row, col(2), col(3), col(4), col(5), row,
                  pl.BlockSpec((cpb, H, HEAD_DIM, HEAD_DIM), lambda i: (rev(i), 0, 0, 0)), parR, par],
        out_specs=[row, row, row, row, par, parR],
        out_shape=[jax.ShapeDtypeStruct((S, W), F32)] * 4 + [jax.ShapeDtypeStruct((1, W), F32), jax.ShapeDtypeStruct((R, W), F32)],
        scratch_shapes=[pltpu.VMEM((H, HEAD_DIM, HEAD_DIM), F32), pltpu.VMEM((1, W), F32)],
        compiler_params=_cparams("arbitrary"),
    )(dout, u, u, u, u, raw, states, lbl, gn)


def _fgate_fwd(xb, wft, bf, name, ts=512):
    S, D = xb.shape
    H = wft.shape[0]
    ts = _tile(S, ts)

    def body(x_ref, w_ref, b_ref, lg_ref, F_ref, carry):
        @pl.when(pl.program_id(0) == 0)
        def _():
            carry[...] = jnp.zeros_like(carry)

        lg = _dot(_round_bf16(w_ref[...]), _round_bf16(x_ref[...]), NT, HI) + b_ref[...]
        lg_ref[...] = lg
        ls = jnp.minimum(lg, 0.0) - jnp.log(1.0 + jnp.exp(-jnp.abs(lg)))
        F = _dot(ls, _tri(ts, False), NN, HI) + carry[...]
        F_ref[...] = F
        carry[...] = F[:, ts - 1:ts]

    return pl.pallas_call(
        body, name=name, grid=(S // ts,),
        in_specs=[pl.BlockSpec((ts, D), lambda i: (i, 0)), pl.BlockSpec((H, D), lambda i: (0, 0)), pl.BlockSpec((H, 1), lambda i: (0, 0))],
        out_specs=[pl.BlockSpec((H, ts), lambda i: (0, i))] * 2,
        out_shape=[jax.ShapeDtypeStruct((H, S), F32)] * 2,
        scratch_shapes=[pltpu.VMEM((H, 1), F32)],
        compiler_params=_cparams("arbitrary"),
    )(xb, wft, bf)


def _fgate_bwd(dF, lg, name, ts=512):
    H, S = dF.shape
    ts = _tile(S, ts)
    nb = S // ts

    def body(dF_ref, lg_ref, dl_ref, db_ref, carry):
        @pl.when(pl.program_id(0) == 0)
        def _():
            carry[...] = jnp.zeros_like(carry)
            db_ref[...] = jnp.zeros_like(db_ref)

        dls = _dot(dF_ref[...], _tri(ts, True), NN, HI) + carry[...]
        carry[...] = dls[:, 0:1]
        dl = dls * _sigmoid(-lg_ref[...])
        dl_ref[...] = dl
        db_ref[...] += jnp.sum(dl, axis=1, keepdims=True)

    blk = pl.BlockSpec((H, ts), lambda i: (0, nb - 1 - i))
    return pl.pallas_call(
        body, name=name, grid=(nb,),
        in_specs=[blk, blk],
        out_specs=[blk, pl.BlockSpec((H, 1), lambda i: (0, 0))],
        out_shape=[jax.ShapeDtypeStruct((H, S), F32), jax.ShapeDtypeStruct((H, 1), F32)],
        scratch_shapes=[pltpu.VMEM((H, 1), F32)],
        compiler_params=_cparams("arbitrary"),
    )(dF, lg)


def _causal_keep(i, j, tq, tk):
    rows = lax.broadcasted_iota(jnp.int32, (tq, tk), 0)
    cols = lax.broadcasted_iota(jnp.int32, (tq, tk), 1)
    return jnp.logical_or(j < i, cols <= rows)


def _fox_lse(u, F3, H, name, t=512):
    S = u.shape[0]
    t = _tile(S, t)
    n = S // t
    scale = HEAD_DIM ** -0.5

    def body(q_ref, k_ref, fk_ref, lse_ref, m_s, l_s):
        i, j = pl.program_id(1), pl.program_id(2)

        @pl.when(j == 0)
        def _():
            m_s[...] = jnp.full_like(m_s, NEG)
            l_s[...] = jnp.zeros_like(l_s)

        @pl.when(j <= i)
        def _():
            s = _dot(q_ref[...].astype(BF16), k_ref[...].astype(BF16), NT) * scale - fk_ref[...]
            s = jnp.where(_causal_keep(i, j, t, t), s, NEG)
            m_new = jnp.maximum(m_s[...], jnp.max(s, axis=-1, keepdims=True))
            l_s[...] = jnp.exp(m_s[...] - m_new) * l_s[...] + jnp.sum(jnp.exp(s - m_new), axis=-1, keepdims=True)
            m_s[...] = m_new

        @pl.when(j == n - 1)
        def _():
            lse_ref[...] = m_s[...] + jnp.log(l_s[...])

    return pl.pallas_call(
        body, name=name, grid=(H, n, n),
        in_specs=[pl.BlockSpec((t, HEAD_DIM), lambda h, i, j: (i, h)),
                  pl.BlockSpec((t, HEAD_DIM), lambda h, i, j: (jnp.minimum(j, i), H + h)),
                  pl.BlockSpec((None, 1, t), lambda h, i, j: (h, 0, jnp.minimum(j, i)))],
        out_specs=pl.BlockSpec((None, t, 1), lambda h, i, j: (h, i, 0)),
        out_shape=jax.ShapeDtypeStruct((H, S, 1), F32),
        scratch_shapes=[pltpu.VMEM((t, 1), F32), pltpu.VMEM((t, 1), F32)],
        compiler_params=_cparams("parallel", "parallel", "arbitrary"),
    )(u, u, F3)


def _fox_probs(q_ref, k_ref, fk_ref, lse_ref, i, j, t, scale):
    s = _dot(q_ref[...].astype(BF16), k_ref[...].astype(BF16), NT) * scale - fk_ref[...]
    p = jnp.where(_causal_keep(i, j, t, t), jnp.exp(s - lse_ref[...]), 0.0)
    return p, p.astype(BF16)


def _softmax_bwd(p, pb, dp, delta):
    return pb.astype(F32) * dp - p * delta


def _fox_fwd(u, F3, lse, H, name, t=512):
    S = u.shape[0]
    W = H * HEAD_DIM
    t = _tile(S, t)
    n = S // t
    scale = HEAD_DIM ** -0.5

    def body(q_ref, k_ref, v_ref, fk_ref, lse_ref, o_ref, acc_s):
        i, j = pl.program_id(1), pl.program_id(2)

        @pl.when(j == 0)
        def _():
            acc_s[...] = jnp.zeros_like(acc_s)

        @pl.when(j <= i)
        def _():
            _, pb = _fox_probs(q_ref, k_ref, fk_ref, lse_ref, i, j, t, scale)
            acc_s[...] += _dot(pb, v_ref[...].astype(BF16), NN)

        @pl.when(j == n - 1)
        def _():
            o_ref[...] = acc_s[...]

    kv = lambda off: pl.BlockSpec((t, HEAD_DIM), lambda h, i, j: (jnp.minimum(j, i), off * H + h))
    return pl.pallas_call(
        body, name=name, grid=(H, n, n),
        in_specs=[pl.BlockSpec((t, HEAD_DIM), lambda h, i, j: (i, h)), kv(1), kv(2),
                  pl.BlockSpec((None, 1, t), lambda h, i, j: (h, 0, jnp.minimum(j, i))),
                  pl.BlockSpec((None, t, 1), lambda h, i, j: (h, i, 0))],
        out_specs=pl.BlockSpec((t, HEAD_DIM), lambda h, i, j: (i, h)),
        out_shape=jax.ShapeDtypeStruct((S, W), F32),
        scratch_shapes=[pltpu.VMEM((t, HEAD_DIM), F32)],
        compiler_params=_cparams("parallel", "parallel", "arbitrary"),
    )(u, u, u, F3, lse)


def _fox_bwd_dq(u, F3, o, do, lse, H, name, t=512):
    S = u.shape[0]
    W = H * HEAD_DIM
    t = _tile(S, t)
    n = S // t
    scale = HEAD_DIM ** -0.5

    def body(q_ref, k_ref, v_ref, fk_ref, o_ref, do_ref, lse_ref, dq_ref, acc_s, dl_s):
        i, j = pl.program_id(1), pl.program_id(2)

        @pl.when(j == 0)
        def _():
            acc_s[...] = jnp.zeros_like(acc_s)
            dl_s[...] = jnp.sum(_round_bf16(do_ref[...]) * o_ref[...], axis=-1, keepdims=True)

        @pl.when(j <= i)
        def _():
            p, pb = _fox_probs(q_ref, k_ref, fk_ref, lse_ref, i, j, t, scale)
            dp = _dot(do_ref[...].astype(BF16), v_ref[...].astype(BF16), NT)
            ds = _softmax_bwd(p, pb, dp, dl_s[...])
            acc_s[...] += _dot(ds.astype(BF16), k_ref[...].astype(BF16), NN)

        @pl.when(j == n - 1)
        def _():
            dq_ref[...] = acc_s[...] * scale

    qblk = pl.BlockSpec((t, HEAD_DIM), lambda h, i, j: (i, h))
    kv = lambda off: pl.BlockSpec((t, HEAD_DIM), lambda h, i, j: (jnp.minimum(j, i), off * H + h))
    return pl.pallas_call(
        body, name=name, grid=(H, n, n),
        in_specs=[qblk, kv(1), kv(2), pl.BlockSpec((None, 1, t), lambda h, i, j: (h, 0, jnp.minimum(j, i))),
                  qblk, qblk, pl.BlockSpec((None, t, 1), lambda h, i, j: (h, i, 0))],
        out_specs=qblk,
        out_shape=jax.ShapeDtypeStruct((S, W), F32),
        scratch_shapes=[pltpu.VMEM((t, HEAD_DIM), F32), pltpu.VMEM((t, 1), F32)],
        compiler_params=_cparams("parallel", "parallel", "arbitrary"),
    )(u, u, u, F3, o, do, lse)


def _fox_bwd_dkv(u, F3, o, do, lse, H, name, t=512):
    S = u.shape[0]
    W = H * HEAD_DIM
    t = _tile(S, t)
    n = S // t
    scale = HEAD_DIM ** -0.5

    def body(q_ref, k_ref, v_ref, fk_ref, o_ref, do_ref, lse_ref, dk_ref, dv_ref, dF_ref, dk_s, dv_s, dF_s):
        j, i = pl.program_id(1), pl.program_id(2)

        @pl.when(i == 0)
        def _():
            dk_s[...] = jnp.zeros_like(dk_s)
            dv_s[...] = jnp.zeros_like(dv_s)
            dF_s[...] = jnp.zeros_like(dF_s)

        @pl.when(i >= j)
        def _():
            q = q_ref[...].astype(BF16)
            do = do_ref[...]
            dob = do.astype(BF16)
            p, pb = _fox_probs(q_ref, k_ref, fk_ref, lse_ref, i, j, t, scale)
            dp = _dot(dob, v_ref[...].astype(BF16), NT)
            delta = jnp.sum(dob.astype(F32) * o_ref[...], axis=-1, keepdims=True)
            ds = _softmax_bwd(p, pb, dp, delta)
            dv_s[...] += _dot(pb, dob, TN)
            dk_s[...] += _dot(ds.astype(BF16), q, TN)
            dF_s[...] -= jnp.sum(ds, axis=0, keepdims=True)

        @pl.when(i == n - 1)
        def _():
            dk_ref[...] = dk_s[...] * scale
            dv_ref[...] = dv_s[...]
            dF_ref[...] = dF_s[...]

    qblk = pl.BlockSpec((t, HEAD_DIM), lambda h, j, i: (jnp.maximum(i, j), h))
    kv = lambda off: pl.BlockSpec((t, HEAD_DIM), lambda h, j, i: (j, off * H + h))
    oblk = pl.BlockSpec((t, HEAD_DIM), lambda h, j, i: (j, h))
    fblk = pl.BlockSpec((None, 1, t), lambda h, j, i: (h, 0, j))
    return pl.pallas_call(
        body, name=name, grid=(H, n, n),
        in_specs=[qblk, kv(1), kv(2), fblk, qblk, qblk, pl.BlockSpec((None, t, 1), lambda h, j, i: (h, jnp.maximum(i, j), 0))],
        out_specs=[oblk, oblk, fblk],
        out_shape=[jax.ShapeDtypeStruct((S, W), F32), jax.ShapeDtypeStruct((S, W), F32), jax.ShapeDtypeStruct((H, 1, S), F32)],
        scratch_shapes=[pltpu.VMEM((t, HEAD_DIM), F32), pltpu.VMEM((t, HEAD_DIM), F32), pltpu.VMEM((1, t), F32)],
        compiler_params=_cparams("parallel", "parallel", "arbitrary"),
    )(u, u, u, F3, o, do, lse)


def _rel_index_matrix():
    a = lax.broadcasted_iota(jnp.int32, (REL_TABLE, CA_SKEW), 0)
    j = lax.broadcasted_iota(jnp.int32, (REL_TABLE, CA_SKEW), 1)
    rel = jnp.where(j < CA_WIN, CA_LEFT_CHUNKS * CHUNK - j, REL_CLIP)
    idx = jnp.clip(jnp.minimum(rel, REL_CLIP) + (CHUNK - 1), 0, REL_TABLE - 1)
    return (a == idx).astype(F32)


def _skew(x, sign):
    r = lax.broadcasted_iota(jnp.int32, x.shape, 0)
    for b in range(int(math.log2(CA_TILE))):
        sh = (1 << b) if sign > 0 else CA_SKEW - (1 << b)
        x = jnp.where((r >> b) & 1 == 1, pltpu.roll(x, sh, 1), x)
    return x


def _band_valid():
    shift = int(math.log2(CHUNK))
    r = lax.broadcasted_iota(jnp.int32, (CA_TILE, CA_WIN), 0) >> shift
    m = lax.broadcasted_iota(jnp.int32, (CA_TILE, CA_WIN), 1) >> shift
    return jnp.logical_and(m >= r, m <= r + CA_LEFT_CHUNKS)


def _relbias_fwd(table, name):
    H = table.shape[0]

    def body(t_ref, b_ref):
        rowv = _dot(t_ref[...], _rel_index_matrix(), NN, HI)
        valid = _band_valid()
        for h in range(H):
            x = _skew(jnp.broadcast_to(rowv[h:h + 1, :], (CA_TILE, CA_SKEW)), +1)
            b_ref[h] = jnp.where(valid, x[:, :CA_WIN], NEG)

    return pl.pallas_call(
        body, name=name,
        out_shape=jax.ShapeDtypeStruct((H, CA_TILE, CA_WIN), F32),
        compiler_params=pltpu.CompilerParams(vmem_limit_bytes=VMEM_LIMIT_V7X),
    )(table)


def _relbias_bwd(dB, name):
    H = dB.shape[0]
    HP = -(-H // 8) * 8

    def body(d_ref, dt_ref, rows):
        rows[...] = jnp.zeros_like(rows)
        for h in range(H):
            x = jnp.concatenate([d_ref[h], jnp.zeros((CA_TILE, CA_SKEW - CA_WIN), F32)], axis=1)
            rows[pl.ds(h, 1), :] = jnp.sum(_skew(x, -1), axis=0, keepdims=True)
        dt_ref[...] = _dot(rows[...], _rel_index_matrix(), NT, HI)[:H]

    return pl.pallas_call(
        body, name=name,
        out_shape=jax.ShapeDtypeStruct((H, REL_TABLE), F32),
        scratch_shapes=[pltpu.VMEM((HP, CA_SKEW), F32)],
        compiler_params=pltpu.CompilerParams(vmem_limit_bytes=VMEM_LIMIT_V7X),
    )(dB)


CA_PIECES = CA_WIN // CA_TILE


def _ca_fwd(u, bias, H, name):
    S = u.shape[0]
    W = H * HEAD_DIM
    T = CA_TILE
    n = S // T
    scale = HEAD_DIM ** -0.5

    def body(q_ref, k0, k1, k2, v0, v1, v2, b_ref, o_ref, lse_ref):
        i = pl.program_id(1)
        q = q_ref[...].astype(BF16)
        ss = []
        for pce, k_ref in enumerate((k0, k1, k2)):
            s = _dot(q, k_ref[...].astype(BF16), NT) * scale + b_ref[:, pce * T:(pce + 1) * T]
            ss.append(jnp.where(i + pce >= CA_PIECES - 1, s, NEG))
        m = jnp.maximum(jnp.maximum(jnp.max(ss[0], -1, keepdims=True), jnp.max(ss[1], -1, keepdims=True)), jnp.max(ss[2], -1, keepdims=True))
        ps = [jnp.exp(s - m) for s in ss]
        l = sum(jnp.sum(p, -1, keepdims=True) for p in ps)
        inv = 1.0 / l
        o_ref[...] = sum(_dot((p * inv).astype(BF16), v_ref[...].astype(BF16), NN) for p, v_ref in zip(ps, (v0, v1, v2)))
        lse_ref[...] = m + jnp.log(l)

    qblk = pl.BlockSpec((T, HEAD_DIM), lambda h, i: (i, h))
    kv = lambda off, back: pl.BlockSpec((T, HEAD_DIM), lambda h, i: (jnp.maximum(i - back, 0), off * H + h))
    return pl.pallas_call(
        body, name=name, grid=(H, n),
        in_specs=[qblk, kv(1, 2), kv(1, 1), kv(1, 0), kv(2, 2), kv(2, 1), kv(2, 0),
                  pl.BlockSpec((None, T, CA_WIN), lambda h, i: (h, 0, 0))],
        out_specs=[qblk, pl.BlockSpec((None, T, 1), lambda h, i: (h, i, 0))],
        out_shape=[jax.ShapeDtypeStruct((S, W), F32), jax.ShapeDtypeStruct((H, S, 1), F32)],
        compiler_params=_cparams("parallel", "arbitrary"),
    )(u, u, u, u, u, u, u, bias)


def _ca_bwd_dq(u, bias, o, do, lse, H, name):
    S = u.shape[0]
    W = H * HEAD_DIM
    T = CA_TILE
    n = S // T
    scale = HEAD_DIM ** -0.5

    def body(q_ref, k0, k1, k2, v0, v1, v2, b_ref, o_ref, do_ref, lse_ref, dq_ref, db_ref):
        i = pl.program_id(1)

        @pl.when(i == 0)
        def _():
            db_ref[...] = jnp.zeros_like(db_ref)

        q = q_ref[...].astype(BF16)
        do = do_ref[...]
        dob = do.astype(BF16)
        delta = jnp.sum(dob.astype(F32) * o_ref[...], axis=-1, keepdims=True)
        dq = jnp.zeros((T, HEAD_DIM), F32)
        for pce, (k_ref, v_ref) in enumerate(((k0, v0), (k1, v1), (k2, v2))):
            k = k_ref[...].astype(BF16)
            s = _dot(q, k, NT) * scale + b_ref[:, pce * T:(pce + 1) * T]
            p = jnp.where(i + pce >= CA_PIECES - 1, jnp.exp(s - lse_ref[...]), 0.0)
            ds = _softmax_bwd(p, p.astype(BF16), _dot(dob, v_ref[...].astype(BF16), NT), delta)
            db_ref[:, pce * T:(pce + 1) * T] += ds
            dq = dq + _dot(ds.astype(BF16), k, NN)
        dq_ref[...] = dq * scale

    qblk = pl.BlockSpec((T, HEAD_DIM), lambda h, i: (i, h))
    kv = lambda off, back: pl.BlockSpec((T, HEAD_DIM), lambda h, i: (jnp.maximum(i - back, 0), off * H + h))
    bblk = pl.BlockSpec((None, T, CA_WIN), lambda h, i: (h, 0, 0))
    return pl.pallas_call(
        body, name=name, grid=(H, n),
        in_specs=[qblk, kv(1, 2), kv(1, 1), kv(1, 0), kv(2, 2), kv(2, 1), kv(2, 0), bblk,
                  qblk, qblk, pl.BlockSpec((None, T, 1), lambda h, i: (h, i, 0))],
        out_specs=[qblk, bblk],
        out_shape=[jax.ShapeDtypeStruct((S, W), F32), jax.ShapeDtypeStruct((H, T, CA_WIN), F32)],
        compiler_params=_cparams("parallel", "arbitrary"),
    )(u, u, u, u, u, u, u, bias, o, do, lse)


def _ca_bwd_dkv(u, bias, o, do, lse, H, name):
    S = u.shape[0]
    W = H * HEAD_DIM
    T = CA_TILE
    n = S // T
    scale = HEAD_DIM ** -0.5

    def body(k_ref, v_ref, q0, q1, q2, o0, o1, o2, d0, d1, d2, l0, l1, l2, b_ref, dk_ref, dv_ref):
        i = pl.program_id(1)
        k = k_ref[...].astype(BF16)
        v = v_ref[...].astype(BF16)
        dk = jnp.zeros((T, HEAD_DIM), F32)
        dv = jnp.zeros((T, HEAD_DIM), F32)
        for ahead, (q_ref, o_ref, do_ref, lse_ref) in enumerate(((q0, o0, d0, l0), (q1, o1, d1, l1), (q2, o2, d2, l2))):
            pce = CA_PIECES - 1 - ahead
            q = q_ref[...].astype(BF16)
            do = do_ref[...]
            dob = do.astype(BF16)
            s = _dot(q, k, NT) * scale + b_ref[:, pce * T:(pce + 1) * T]
            p = jnp.where(i + ahead <= n - 1, jnp.exp(s - lse_ref[...]), 0.0)
            delta = jnp.sum(dob.astype(F32) * o_ref[...], axis=-1, keepdims=True)
            ds = _softmax_bwd(p, p.astype(BF16), _dot(dob, v, NT), delta)
            dv = dv + _dot(p.astype(BF16), dob, TN)
            dk = dk + _dot(ds.astype(BF16), q, TN)
        dk_ref[...] = dk * scale
        dv_ref[...] = dv

    kv = lambda off: pl.BlockSpec((T, HEAD_DIM), lambda h, i: (i, off * H + h))
    qa = lambda ahead: pl.BlockSpec((T, HEAD_DIM), lambda h, i: (jnp.minimum(i + ahead, n - 1), h))
    la = lambda ahead: pl.BlockSpec((None, T, 1), lambda h, i: (h, jnp.minimum(i + ahead, n - 1), 0))
    oblk = pl.BlockSpec((T, HEAD_DIM), lambda h, i: (i, h))
    return pl.pallas_call(
        body, name=name, grid=(H, n),
        in_specs=[kv(1), kv(2), qa(0), qa(1), qa(2), qa(0), qa(1), qa(2), qa(0), qa(1), qa(2), la(0), la(1), la(2),
                  pl.BlockSpec((None, T, CA_WIN), lambda h, i: (h, 0, 0))],
        out_specs=[oblk, oblk],
        out_shape=[jax.ShapeDtypeStruct((S, W), F32)] * 2,
        compiler_params=_cparams("parallel", "parallel"),
    )(u, u, u, u, u, o, o, o, do, do, do, lse, lse, lse, bias)


def _sum_parts(parts, name):
    _, R, C = parts.shape

    def body(p_ref, o_ref):
        acc = p_ref[0].astype(F32)
        for d in range(1, N_DEV):
            acc = acc + p_ref[d].astype(F32)
        o_ref[...] = acc

    return pl.pallas_call(
        body, name=name, out_shape=jax.ShapeDtypeStruct((R, C), F32),
        compiler_params=pltpu.CompilerParams(vmem_limit_bytes=VMEM_LIMIT_V7X),
    )(parts)


def _adamw(parts, w, m, v, name, tr=256):
    P, R, C = parts.shape
    tr = _tile(R, tr)
    c1 = 1.0 / (1.0 - ADAM_B1 ** ADAM_STEP)
    c2 = 1.0 / (1.0 - ADAM_B2 ** ADAM_STEP)

    def body(p_ref, w_ref, m_ref, v_ref, g_ref, d_ref, nm_ref, nv_ref):
        g = p_ref[0].astype(F32)
        for d in range(1, P):
            g = g + p_ref[d].astype(F32)
        nm = ADAM_B1 * m_ref[...] + (1.0 - ADAM_B1) * g
        nv = ADAM_B2 * v_ref[...] + (1.0 - ADAM_B2) * (g * g)
        g_ref[...] = g
        nm_ref[...] = nm
        nv_ref[...] = nv
        d_ref[...] = -ADAM_LR * ((nm * c1) / (jnp.sqrt(nv * c2) + ADAM_EPS) + ADAM_WD * w_ref[...])

    blk = pl.BlockSpec((tr, C), lambda i: (i, 0))
    return pl.pallas_call(
        body, name=name, grid=(R // tr,),
        in_specs=[pl.BlockSpec((P, tr, C), lambda i: (0, i, 0)), blk, blk, blk],
        out_specs=[blk] * 4,
        out_shape=[jax.ShapeDtypeStruct((R, C), F32)] * 4,
        compiler_params=_cparams("parallel"),
    )(parts, w, m, v)


def _peer(d):
    x, y, c = lax.axis_index("x"), lax.axis_index("y"), lax.axis_index("c")
    px = (1 - x) if d & 4 else x
    py = (1 - y) if d & 2 else y
    pc = (1 - c) if d & 1 else c
    return (px, py, pc), 4 * px + 2 * py + pc


def _exchange(arrays, scatter, name):
    n = len(arrays)
    npeer = N_DEV - 1

    def body(*refs):
        ins, outs = refs[:n], refs[n:2 * n]
        send_sems, recv_sems, local_sems = refs[2 * n:]
        me = 4 * lax.axis_index("x") + 2 * lax.axis_index("y") + lax.axis_index("c")
        local = []
        for t in range(n):
            src = ins[t].at[me] if scatter else ins[t]
            cp = pltpu.make_async_copy(src, outs[t].at[me], local_sems.at[t])
            cp.start()
            local.append(cp)
            for d in range(1, N_DEV):
                peer, pidx = _peer(d)
                src = ins[t].at[pidx] if scatter else ins[t]
                pltpu.make_async_remote_copy(
                    src_ref=src, dst_ref=outs[t].at[me], send_sem=send_sems.at[t * npeer + d - 1],
                    recv_sem=recv_sems.at[t * npeer + d - 1], device_id=peer, device_id_type=MESH_ID).start()
        for t in range(n):
            local[t].wait()
            for d in range(1, N_DEV):
                peer, pidx = _peer(d)
                src = ins[t].at[pidx] if scatter else ins[t]
                pltpu.make_async_remote_copy(
                    src_ref=src, dst_ref=outs[t].at[pidx], send_sem=send_sems.at[t * npeer + d - 1],
                    recv_sem=recv_sems.at[t * npeer + d - 1], device_id=peer, device_id_type=MESH_ID).wait()

    hbm = pl.BlockSpec(memory_space=pltpu.HBM)
    shapes = [a.shape if scatter else (N_DEV,) + a.shape for a in arrays]
    return pl.pallas_call(
        body, name=name,
        in_specs=[hbm] * n, out_specs=[hbm] * n,
        out_shape=[jax.ShapeDtypeStruct(s, a.dtype) for s, a in zip(shapes, arrays)],
        scratch_shapes=[pltpu.SemaphoreType.DMA((n * npeer,)), pltpu.SemaphoreType.DMA((n * npeer,)), pltpu.SemaphoreType.DMA((n,))],
        compiler_params=pltpu.CompilerParams(has_side_effects=True),
    )(*arrays)


def _relu2_fwd(acc):
    r = jnp.maximum(acc, 0.0)
    return acc, r * r


def _relu2_bwd(acc, z):
    return (acc * 2.0 * jnp.maximum(z.astype(F32), 0.0),)


def _add_scaled(scale):
    def epi(acc, t):
        return (acc + scale * t,)
    return epi


def _local_step(x, target, P):
    S, D = x.shape
    H = (D // 2) // HEAD_DIM
    W = H * HEAD_DIM
    alpha = (2 * 2) ** 0.25
    g = {}

    u0 = _mm(x, P["ev_w_in"], mode="nn", name="ev_in")
    a_out, cv = _conv_fwd(u0, P["ev_conv_w"], P["ev_conv_b"], P["ev_conv_ln_g"], P["ev_conv_ln_b"], "conv_fwd")
    b_out, h_raw, h_states = _hgrn_fwd(u0, P["hgrn_lb_logits"], P["ev_gnorm_g"], "hgrn_fwd")
    cat0 = jnp.concatenate([a_out.astype(BF16), b_out.astype(BF16)], axis=1)
    mix0 = _mm(cat0, P["ev_w_out"], mode="nn", name="ev_out")
    x1, xh1, rs1 = _ln_fwd(x, mix0, P["ln_mix_g"][0:1], P["ln_mix_b"][0:1], alpha, "ln_mix0")
    z0, hh0 = _mm(x1, P["mlp_w1"][0], mode="nn", name="mlp_up0", out_dtypes=(BF16, BF16), epi=_relu2_fwd)
    m0 = _mm(hh0, P["mlp_w2"][0], mode="nn", name="mlp_down0")
    x2, xh2, rs2 = _ln_fwd(x1, m0, P["ln_mlp_g"][0:1], P["ln_mlp_b"][0:1], alpha, "ln_mlp0")

    uc = _mm(x2, P["od_w_c"], mode="nn", name="od_in_c")
    ud = _mm(x2, P["od_w_d"], mode="nn", name="od_in_d")
    f_logit, F = _fgate_fwd(x2, P["od_w_f_t"], P["fox_b_f"].reshape(H, 1), "fgate_fwd")
    F3 = F.reshape(H, 1, S)
    c_lse = _fox_lse(uc, F3, H, "fox_lse")
    c_out = _fox_fwd(uc, F3, c_lse, H, "fox_fwd")
    bias = _relbias_fwd(P["rel_bias"], "relbias_fwd")
    d_out, d_lse = _ca_fwd(ud, bias, H, "ca_fwd")
    cat1 = jnp.concatenate([c_out.astype(BF16), d_out.astype(BF16)], axis=1)
    mix1 = _mm(cat1, P["od_w_out"], mode="nn", name="od_out")
    x3, xh3, rs3 = _ln_fwd(x2, mix1, P["ln_mix_g"][1:2], P["ln_mix_b"][1:2], alpha, "ln_mix1")
    z1, hh1 = _mm(x3, P["mlp_w1"][1], mode="nn", name="mlp_up1", out_dtypes=(BF16, BF16), epi=_relu2_fwd)
    m1 = _mm(hh1, P["mlp_w2"][1], mode="nn", name="mlp_down1")
    x4, xh4, rs4 = _ln_fwd(x3, m1, P["ln_mlp_g"][1:2], P["ln_mlp_b"][1:2], alpha, "ln_mlp1")

    dy, loss = _loss_head(x4, target, "loss_head")

    dzm1, dg_, db_ = _ln_bwd(dy, None, 0.0, xh4, rs4, P["ln_mlp_g"][1:2], "ln_mlp1_bwd")
    g["ln_mlp_g1"], g["ln_mlp_b1"] = dg_, db_
    g["mlp_w2_1"] = _mm(hh1, dzm1, mode="tn", name="mlp_down1_dw", out_dtypes=(BF16,))
    dz1 = _mm(dzm1, P["mlp_w2"][1], mode="nt", name="mlp_down1_dx", out_dtypes=(BF16,), extras=(z1,), epi=_relu2_bwd)
    g["mlp_w1_1"] = _mm(x3, dz1, mode="tn", name="mlp_up1_dw", out_dtypes=(BF16,))
    dx3 = _mm(dz1, P["mlp_w1"][1], mode="nt", name="mlp_up1_dx", extras=(dzm1,), epi=_add_scaled(alpha))
    dzx1, dg_, db_ = _ln_bwd(dx3, None, 0.0, xh3, rs3, P["ln_mix_g"][1:2], "ln_mix1_bwd")
    g["ln_mix_g1"], g["ln_mix_b1"] = dg_, db_
    g["od_w_out"] = _mm(cat1, dzx1, mode="tn", name="od_out_dw", out_dtypes=(BF16,))
    dcat1 = _mm(dzx1, P["od_w_out"], mode="nt", name="od_out_dx")
    dc_out, dd_out = dcat1[:, :W], dcat1[:, W:]
    dq_d, dbias = _ca_bwd_dq(ud, bias, d_out, dd_out, d_lse, H, "ca_bwd_dq")
    dk_d, dv_d = _ca_bwd_dkv(ud, bias, d_out, dd_out, d_lse, H, "ca_bwd_dkv")
    g["rel_bias"] = _relbias_bwd(dbias, "relbias_bwd")
    dud = jnp.concatenate([dq_d, dk_d, dv_d], axis=1)
    dq_c = _fox_bwd_dq(uc, F3, c_out, dc_out, c_lse, H, "fox_bwd_dq")
    dk_c, dv_c, dF3 = _fox_bwd_dkv(uc, F3, c_out, dc_out, c_lse, H, "fox_bwd_dkv")
    duc = jnp.concatenate([dq_c, dk_c, dv_c], axis=1)
    dfl, dbf = _fgate_bwd(dF3.reshape(H, S), f_logit, "fgate_bwd")
    g["fox_b_f"] = dbf.reshape(1, H)
    g["od_w_c"] = _mm(x2, duc, mode="tn", name="od_in_c_dw", out_dtypes=(BF16,))
    g["od_w_d"] = _mm(x2, dud, mode="tn", name="od_in_d_dw", out_dtypes=(BF16,))
    g["od_w_f_t"] = _mm(dfl, x2, mode="nn", name="od_in_f_dw", out_dtypes=(BF16,), exact_products=True)
    dx2 = _mm(duc, P["od_w_c"], mode="nt", name="od_in_c_dx", extras=(dzx1,), epi=_add_scaled(alpha))
    dx2 = _mm(dud, P["od_w_d"], mode="nt", name="od_in_d_dx", extras=(dx2,), epi=_add_scaled(1.0))
    dx2 = _mm(dfl, P["od_w_f_t"], mode="tn", name="od_in_f_dx", extras=(dx2,), epi=_add_scaled(1.0), exact_products=True)

    dzm0, dg_, db_ = _ln_bwd(dx2, None, 0.0, xh2, rs2, P["ln_mlp_g"][0:1], "ln_mlp0_bwd")
    g["ln_mlp_g0"], g["ln_mlp_b0"] = dg_, db_
    g["mlp_w2_0"] = _mm(hh0, dzm0, mode="tn", name="mlp_down0_dw", out_dtypes=(BF16,))
    dz0 = _mm(dzm0, P["mlp_w2"][0], mode="nt", name="mlp_down0_dx", out_dtypes=(BF16,), extras=(z0,), epi=_relu2_bwd)
    g["mlp_w1_0"] = _mm(x1, dz0, mode="tn", name="mlp_up0_dw", out_dtypes=(BF16,))
    dx1 = _mm(dz0, P["mlp_w1"][0], mode="nt", name="mlp_up0_dx", extras=(dzm0,), epi=_add_scaled(alpha))
    dzx0, dg_, db_ = _ln_bwd(dx1, None, 0.0, xh1, rs1, P["ln_mix_g"][0:1], "ln_mix0_bwd")
    g["ln_mix_g0"], g["ln_mix_b0"] = dg_, db_
    g["ev_w_out"] = _mm(cat0, dzx0, mode="tn", name="ev_out_dw", out_dtypes=(BF16,))
    dcat0 = _mm(dzx0, P["ev_w_out"], mode="nt", name="ev_out_dx")
    da_out, db_out = dcat0[:, :W], dcat0[:, W:]
    dcv, g["ev_conv_ln_g"], g["ev_conv_ln_b"], g["ev_conv_b"] = _conv_bwd_norm(da_out, cv, P["ev_conv_ln_g"], P["ev_conv_ln_b"], "conv_bwd_norm")
    du_a, du_g, g["ev_conv_w"] = _conv_bwd_taps(dcv, u0, P["ev_conv_w"], "conv_bwd_taps")
    dhq, dhf, dhi, dhg, g["ev_gnorm_g"], g["hgrn_lb_logits"] = _hgrn_bwd(db_out, u0, h_raw, h_states, P["hgrn_lb_logits"], P["ev_gnorm_g"], "hgrn_bwd")
    du0 = jnp.concatenate([du_a, du_g, dhq, dhf, dhi, dhg], axis=1)
    g["ev_w_in"] = _mm(x, du0, mode="tn", name="ev_in_dw", out_dtypes=(BF16,))
    dx0 = _mm(du0, P["ev_w_in"], mode="nt", name="ev_in_dx", extras=(dzx0,), epi=_add_scaled(alpha))
    return loss, dx0, g


_NAMES = ['ev_w_in', 'ev_conv_w', 'ev_conv_b', 'ev_conv_ln_g', 'ev_conv_ln_b', 'hgrn_lb_logits', 'ev_gnorm_g', 'ev_w_out',
          'od_w_in', 'fox_b_f', 'rel_bias', 'od_w_out', 'ln_mix_g', 'ln_mix_b', 'mlp_w1', 'mlp_w2', 'ln_mlp_g', 'ln_mlp_b']
_SMALL = ['ev_conv_b', 'ev_conv_ln_g', 'ev_conv_ln_b', 'hgrn_lb_logits', 'ev_gnorm_g', 'fox_b_f', 'ln_mix_g', 'ln_mix_b',
          'ln_mlp_g', 'ln_mlp_b', 'ev_conv_w', 'rel_bias']
_PACK_COLS = 2048


def _cols_to_full(gathered):
    nd, K, n = gathered.shape
    return jnp.transpose(gathered, (1, 0, 2)).reshape(K, nd * n)


def _full_to_cols(full):
    K, N = full.shape
    return jnp.transpose(full.reshape(K, N_DEV, N // N_DEV), (1, 0, 2))


def _pack(parts):
    flat = jnp.concatenate([p.reshape(-1).astype(F32) for p in parts])
    rows = -(-flat.shape[0] // (_PACK_COLS * 8)) * 8
    return jnp.pad(flat, (0, rows * _PACK_COLS - flat.shape[0])).reshape(rows, _PACK_COLS)


def _unpack(packed, shapes):
    flat = packed.reshape(-1)
    out, off = [], 0
    for s in shapes:
        n = math.prod(s)
        out.append(flat[off:off + n].reshape(s))
        off += n
    return out


def kernel(x, ev_w_in, ev_conv_w, ev_conv_b, ev_conv_ln_g, ev_conv_ln_b, hgrn_lb_logits, ev_gnorm_g, ev_w_out, od_w_in, fox_b_f, rel_bias, od_w_out, ln_mix_g, ln_mix_b, mlp_w1, mlp_w2, ln_mlp_g, ln_mlp_b, loss_target, m_ev_w_in, m_ev_conv_w, m_ev_conv_b, m_ev_conv_ln_g, m_ev_conv_ln_b, m_hgrn_lb_logits, m_ev_gnorm_g, m_ev_w_out, m_od_w_in, m_fox_b_f, m_rel_bias, m_od_w_out, m_ln_mix_g, m_ln_mix_b, m_mlp_w1, m_mlp_w2, m_ln_mlp_g, m_ln_mlp_b, v_ev_w_in, v_ev_conv_w, v_ev_conv_b, v_ev_conv_ln_g, v_ev_conv_ln_b, v_hgrn_lb_logits, v_ev_gnorm_g, v_ev_w_out, v_od_w_in, v_fox_b_f, v_rel_bias, v_od_w_out, v_ln_mix_g, v_ln_mix_b, v_mlp_w1, v_mlp_w2, v_ln_mlp_g, v_ln_mlp_b):
    args = locals()
    w = {n: args[n] for n in _NAMES}
    m = {n: args["m_" + n] for n in _NAMES}
    v = {n: args["v_" + n] for n in _NAMES}
    me = 4 * lax.axis_index("x") + 2 * lax.axis_index("y") + lax.axis_index("c")
    S, D = x.shape[1], x.shape[2]
    H = (D // 2) // HEAD_DIM
    W = H * HEAD_DIM
    n_layers = mlp_w1.shape[0]
    assert n_layers == 2 and ev_w_in.shape[0] == 1 and od_w_in.shape[0] == 1

    shards = [ev_w_in[0].astype(BF16), ev_w_out[0].astype(BF16), od_w_in[0].astype(BF16), od_w_out[0].astype(BF16)]
    shards += [mlp_w1[l].astype(BF16) for l in range(n_layers)] + [mlp_w2[l].astype(BF16) for l in range(n_layers)]
    shards += [ev_conv_w[0], rel_bias[0]]
    G = _exchange(shards, False, "gather_weights")
    od_full = _cols_to_full(G[2])
    P = {
        "ev_w_in": _cols_to_full(G[0]),
        "ev_w_out": G[1].reshape(D, D),
        "od_w_c": od_full[:, :3 * W],
        "od_w_f_t": jnp.transpose(od_full[:, 3 * W:3 * W + H]),
        "od_w_d": od_full[:, 3 * W + H:],
        "od_w_out": G[3].reshape(D, D),
        "mlp_w1": [_cols_to_full(G[4 + l]) for l in range(n_layers)],
        "mlp_w2": [G[4 + n_layers + l].reshape(-1, D) for l in range(n_layers)],
        "ev_conv_w": _cols_to_full(G[4 + 2 * n_layers]),
        "rel_bias": _cols_to_full(G[5 + 2 * n_layers]),
        "ev_conv_b": ev_conv_b, "ev_conv_ln_g": ev_conv_ln_g, "ev_conv_ln_b": ev_conv_ln_b,
        "hgrn_lb_logits": hgrn_lb_logits, "ev_gnorm_g": ev_gnorm_g, "fox_b_f": fox_b_f,
        "ln_mix_g": ln_mix_g, "ln_mix_b": ln_mix_b, "ln_mlp_g": ln_mlp_g, "ln_mlp_b": ln_mlp_b,
    }

    loss, grad_x, g = _local_step(x[0], loss_target[0], P)

    g_od = jnp.concatenate([g["od_w_c"], jnp.transpose(g["od_w_f_t"]), g["od_w_d"]], axis=1)
    blocks = [_full_to_cols(g["ev_w_in"]), g["ev_w_out"].reshape(N_DEV, -1, D), _full_to_cols(g_od), g["od_w_out"].reshape(N_DEV, -1, D)]
    blocks += [_full_to_cols(g["mlp_w1_%d" % l]) for l in range(n_layers)]
    blocks += [g["mlp_w2_%d" % l].reshape(N_DEV, -1, D) for l in range(n_layers)]
    small = {
        "ev_conv_b": g["ev_conv_b"], "ev_conv_ln_g": g["ev_conv_ln_g"], "ev_conv_ln_b": g["ev_conv_ln_b"],
        "hgrn_lb_logits": g["hgrn_lb_logits"], "ev_gnorm_g": g["ev_gnorm_g"], "fox_b_f": g["fox_b_f"],
        "ln_mix_g": jnp.concatenate([g["ln_mix_g0"], g["ln_mix_g1"]]), "ln_mix_b": jnp.concatenate([g["ln_mix_b0"], g["ln_mix_b1"]]),
        "ln_mlp_g": jnp.concatenate([g["ln_mlp_g0"], g["ln_mlp_g1"]]), "ln_mlp_b": jnp.concatenate([g["ln_mlp_b0"], g["ln_mlp_b1"]]),
        "ev_conv_w": g["ev_conv_w"], "rel_bias": g["rel_bias"],
    }
    full_shapes = [small[n].shape for n in _SMALL]
    recv = _exchange(blocks, True, "scatter_grads")
    small_all = _exchange([_pack([small[n] for n in _SMALL])], False, "gather_small_grads")[0]
    small_sum = _unpack(_sum_parts(small_all, "sum_small_grads"), full_shapes)
    small_g = dict(zip(_SMALL, small_sum))
    cw = small_g["ev_conv_w"]
    small_g["ev_conv_w"] = lax.dynamic_slice_in_dim(cw, me * (cw.shape[1] // N_DEV), cw.shape[1] // N_DEV, axis=1)
    rb = small_g["rel_bias"]
    small_g["rel_bias"] = lax.dynamic_slice_in_dim(rb, me * (rb.shape[1] // N_DEV), rb.shape[1] // N_DEV, axis=1)

    out_g, out_d, out_m, out_v = {}, {}, {}, {}

    def put(name, res, shape):
        out_g[name], out_d[name], out_m[name], out_v[name] = [r.reshape(shape) for r in res]

    big = [("ev_w_in", recv[0], None), ("ev_w_out", recv[1], None), ("od_w_in", recv[2], None), ("od_w_out", recv[3], None)]
    for name, parts, _ in big:
        shp = w[name].shape
        put(name, _adamw(parts, w[name][0], m[name][0], v[name][0], "adamw_" + name), shp)
    for name, base in (("mlp_w1", 4), ("mlp_w2", 4 + n_layers)):
        res = [_adamw(recv[base + l], w[name][l], m[name][l], v[name][l], "adamw_%s_%d" % (name, l)) for l in range(n_layers)]
        put(name, [jnp.stack([res[l][k] for l in range(n_layers)]) for k in range(4)], w[name].shape)
    shapes = [w[n].shape for n in _SMALL]
    packed = _adamw(_pack([small_g[n] for n in _SMALL])[None], _pack([w[n] for n in _SMALL]), _pack([m[n] for n in _SMALL]),
                    _pack([v[n] for n in _SMALL]), "adamw_small")
    for k, dst in enumerate((out_g, out_d, out_m, out_v)):
        for n, a in zip(_SMALL, _unpack(packed[k], shapes)):
            dst[n] = a

    loss = lax.psum(loss[0, 0], ("x", "y", "c"))
    return (loss, grad_x[None], *[out_g[n] for n in _NAMES], *[out_d[n] for n in _NAMES],
            *[out_m[n] for n in _NAMES], *[out_v[n] for n in _NAMES])
```

```python
import functools
import math

import jax
import jax.numpy as jnp
from jax import lax
from jax.experimental import pallas as pl
from jax.experimental.pallas import tpu as pltpu

F32 = jnp.float32
BF16 = jnp.bfloat16
HI = lax.Precision.HIGHEST
MESH_ID = pl.DeviceIdType.MESH

N_DEV = 8
LN_EPS = 1e-5
CHUNK = 64
HEAD_DIM = 128
CONV_WIDTH = 31
CONV_HALO = 32
CA_LEFT_CHUNKS = 8
CA_TILE = 256
CA_WIN = CA_TILE + CA_LEFT_CHUNKS * CHUNK
CA_SKEW = 1024
REL_CLIP = 256
REL_TABLE = (CHUNK - 1) + REL_CLIP + 1
NEG = -1e30
ADAM_LR = 0.001
ADAM_B1 = 0.9
ADAM_B2 = 0.999
ADAM_EPS = 1e-08
ADAM_WD = 0.01
ADAM_STEP = 10
VMEM_LIMIT_V7X = 56 * 1024 * 1024


def _cparams(*sem):
    return pltpu.CompilerParams(dimension_semantics=sem, vmem_limit_bytes=VMEM_LIMIT_V7X)


def _tile(n, t):
    if n <= t:
        return n
    for c in range(t - t % 128, 0, -128):
        if n % c == 0:
            return c
    return n


def _sigmoid(x):
    return 1.0 / (1.0 + jnp.exp(-x))


def _dot(a, b, dims, precision=None):
    return lax.dot_general(a, b, (dims, ((), ())), preferred_element_type=F32, precision=precision)


def _round_bf16(x):
    return x.astype(BF16).astype(F32)


NN = ((1,), (0,))
NT = ((1,), (1,))
TN = ((0,), (0,))


def _mm(a, b, *, mode, name, out_dtypes=(F32,), extras=(), epi=None, exact_products=False, carry=None, tm=512, tn=1024, tk=2048):
    dims = {"nn": NN, "nt": NT, "tn": TN}[mode]
    if mode == "tn":
        K, M = a.shape
    else:
        M, K = a.shape
    N = b.shape[0] if mode == "nt" else b.shape[1]
    tm, tn, tk = _tile(M, tm), _tile(N, tn), _tile(K, tk)
    nk = K // tk
    n_ex, n_out = len(extras), len(out_dtypes)

    def body(*refs):
        a_ref, b_ref = refs[0], refs[1]
        ex_refs = refs[2:2 + n_ex]
        o_refs = refs[2 + n_ex:2 + n_ex + n_out]
        if exact_products:
            d = _dot(_round_bf16(a_ref[...]), _round_bf16(b_ref[...]), dims, HI)
        else:
            d = _dot(a_ref[...].astype(BF16), b_ref[...].astype(BF16), dims)

        def finish(acc):
            outs = (acc,) if epi is None else epi(acc, *[r[...] for r in ex_refs])
            for o_ref, o in zip(o_refs, outs):
                o_ref[...] = o.astype(o_ref.dtype)

        if nk == 1:
            finish(d)
        else:
            acc_ref = refs[-1]
            k = pl.program_id(2)

            @pl.when(k == 0)
            def _():
                acc_ref[...] = d

            @pl.when(k > 0)
            def _():
                acc_ref[...] += d

            @pl.when(k == nk - 1)
            def _():
                finish(acc_ref[...])

    a_spec = pl.BlockSpec((tk, tm), lambda i, j, k: (k, i)) if mode == "tn" else pl.BlockSpec((tm, tk), lambda i, j, k: (i, k))
    b_spec = pl.BlockSpec((tn, tk), lambda i, j, k: (j, k)) if mode == "nt" else pl.BlockSpec((tk, tn), lambda i, j, k: (k, j))
    mn_spec = pl.BlockSpec((tm, tn), lambda i, j, k: (i, j))
    outs, carried = _call(
        body, name=name, grid=(M // tm, N // tn, nk),
        in_specs=[a_spec, b_spec] + [mn_spec] * n_ex,
        out_specs=[mn_spec] * n_out,
        out_shape=[jax.ShapeDtypeStruct((M, N), dt) for dt in out_dtypes],
        scratch_shapes=[pltpu.VMEM((tm, tn), F32)] if nk > 1 else [],
        sem=("parallel", "parallel", "arbitrary"), args=(a, b, *extras), carry=carry)
    return (outs[0] if n_out == 1 else outs), carried


def _ln_fwd(x, r, g, b, alpha, name, tr=256):
    S, D = x.shape
    tr = _tile(S, tr)

    def body(x_ref, r_ref, g_ref, b_ref, y_ref, xh_ref, rs_ref):
        z = alpha * x_ref[...] + r_ref[...]
        zc = z - jnp.mean(z, axis=-1, keepdims=True)
        rs = lax.rsqrt(jnp.mean(zc * zc, axis=-1, keepdims=True) + LN_EPS)
        xh = zc * rs
        xh_ref[...] = xh
        rs_ref[...] = rs
        y_ref[...] = xh * g_ref[...] + b_ref[...]

    row = pl.BlockSpec((tr, D), lambda i: (i, 0))
    par = pl.BlockSpec((1, D), lambda i: (0, 0))
    return pl.pallas_call(
        body, name=name, grid=(S // tr,),
        in_specs=[row, row, par, par],
        out_specs=[row, row, pl.BlockSpec((tr, 1), lambda i: (i, 0))],
        out_shape=[jax.ShapeDtypeStruct((S, D), F32), jax.ShapeDtypeStruct((S, D), F32), jax.ShapeDtypeStruct((S, 1), F32)],
        compiler_params=_cparams("parallel"),
    )(x, r, g, b)


def _ln_bwd(dy, dy2, scale2, xh, rs, g, name, tr=256):
    S, D = dy.shape
    tr = _tile(S, tr)
    two = dy2 is not None

    def body(*refs):
        if two:
            dy_ref, dy2_ref, xh_ref, rs_ref, g_ref, dz_ref, dg_ref, db_ref = refs
            dyt = dy_ref[...] + scale2 * dy2_ref[...]
        else:
            dy_ref, xh_ref, rs_ref, g_ref, dz_ref, dg_ref, db_ref = refs
            dyt = dy_ref[...]
        xh = xh_ref[...]
        dxh = dyt * g_ref[...]
        m1 = jnp.mean(dxh, axis=-1, keepdims=True)
        m2 = jnp.mean(dxh * xh, axis=-1, keepdims=True)
        dz_ref[...] = rs_ref[...] * (dxh - m1 - xh * m2)

        @pl.when(pl.program_id(0) == 0)
        def _():
            dg_ref[...] = jnp.zeros_like(dg_ref)
            db_ref[...] = jnp.zeros_like(db_ref)

        dg_ref[...] += jnp.sum(dyt * xh, axis=0, keepdims=True)
        db_ref[...] += jnp.sum(dyt, axis=0, keepdims=True)

    row = pl.BlockSpec((tr, D), lambda i: (i, 0))
    par = pl.BlockSpec((1, D), lambda i: (0, 0))
    ins = [dy] + ([dy2] if two else []) + [xh, rs, g]
    return pl.pallas_call(
        body, name=name, grid=(S // tr,),
        in_specs=[row] * (2 if two else 1) + [row, pl.BlockSpec((tr, 1), lambda i: (i, 0)), par],
        out_specs=[row, par, par],
        out_shape=[jax.ShapeDtypeStruct((S, D), F32), jax.ShapeDtypeStruct((1, D), F32), jax.ShapeDtypeStruct((1, D), F32)],
        compiler_params=_cparams("arbitrary"),
    )(*ins)


def _loss_head(y, target, name, tr=256):
    S, D = y.shape
    tr = _tile(S, tr)

    def body(y_ref, t_ref, dy_ref, loss_ref):
        e = y_ref[...] - t_ref[...]
        dy_ref[...] = e * (1.0 / D)

        @pl.when(pl.program_id(0) == 0)
        def _():
            loss_ref[...] = jnp.zeros_like(loss_ref)

        loss_ref[...] += jnp.sum(jnp.sum(e * e, axis=-1, keepdims=True), axis=0, keepdims=True) * (0.5 / D)

    row = pl.BlockSpec((tr, D), lambda i: (i, 0))
    return pl.pallas_call(
        body, name=name, grid=(S // tr,),
        in_specs=[row, row],
        out_specs=[row, pl.BlockSpec((1, 1), lambda i: (0, 0))],
        out_shape=[jax.ShapeDtypeStruct((S, D), F32), jax.ShapeDtypeStruct((1, 1), F32)],
        compiler_params=_cparams("arbitrary"),
    )(y, target)


def _conv_fwd(u, w, cb, lg, lb, name, tt=512):
    S = u.shape[0]
    C = w.shape[1]
    tt = _tile(S, tt)
    hpt = tt // CONV_HALO

    def body(a_ref, g_ref, ap_ref, gp_ref, w_ref, cb_ref, lg_ref, lb_ref, out_ref, cv_ref, hext):
        i = pl.program_id(0)
        hext[pl.ds(CONV_HALO, tt), :] = a_ref[...] * _sigmoid(g_ref[...])
        hp = ap_ref[...] * _sigmoid(gp_ref[...])
        hext[pl.ds(0, CONV_HALO), :] = jnp.where(i > 0, hp, 0.0)
        acc = jnp.zeros((tt, C), F32)
        for k in range(CONV_WIDTH):
            acc = acc + w_ref[pl.ds(k, 1), :] * hext[pl.ds(CONV_HALO - (CONV_WIDTH - 1) + k, tt), :]
        cv = acc + cb_ref[...]
        cv_ref[...] = cv
        zc = cv - jnp.mean(cv, axis=-1, keepdims=True)
        n = zc * lax.rsqrt(jnp.mean(zc * zc, axis=-1, keepdims=True) + LN_EPS) * lg_ref[...] + lb_ref[...]
        out_ref[...] = n * _sigmoid(n)

    cur = lambda cb_: pl.BlockSpec((tt, C), lambda i: (i, cb_))
    prev = lambda cb_: pl.BlockSpec((CONV_HALO, C), lambda i: (jnp.maximum(i * hpt - 1, 0), cb_))
    par = pl.BlockSpec((1, C), lambda i: (0, 0))
    row = pl.BlockSpec((tt, C), lambda i: (i, 0))
    return pl.pallas_call(
        body, name=name, grid=(S // tt,),
        in_specs=[cur(0), cur(1), prev(0), prev(1), pl.BlockSpec((CONV_WIDTH, C), lambda i: (0, 0)), par, par, par],
        out_specs=[row, row],
        out_shape=[jax.ShapeDtypeStruct((S, C), F32)] * 2,
        scratch_shapes=[pltpu.VMEM((tt + CONV_HALO, C), F32)],
        compiler_params=_cparams("parallel"),
    )(u, u, u, u, w, cb, lg, lb)


def _conv_bwd_norm(da, cv, lg, lb, name, tt=512):
    S, C = cv.shape
    tt = _tile(S, tt)

    def body(da_ref, cv_ref, lg_ref, lb_ref, dcv_ref, dlg_ref, dlb_ref, dcb_ref):
        cv = cv_ref[...]
        zc = cv - jnp.mean(cv, axis=-1, keepdims=True)
        rs = lax.rsqrt(jnp.mean(zc * zc, axis=-1, keepdims=True) + LN_EPS)
        xh = zc * rs
        n = xh * lg_ref[...] + lb_ref[...]
        sg = _sigmoid(n)
        dn = da_ref[...] * sg * (1.0 + n * (1.0 - sg))
        dxh = dn * lg_ref[...]
        m1 = jnp.mean(dxh, axis=-1, keepdims=True)
        m2 = jnp.mean(dxh * xh, axis=-1, keepdims=True)
        dcv = rs * (dxh - m1 - xh * m2)
        dcv_ref[...] = dcv

        @pl.when(pl.program_id(0) == 0)
        def _():
            dlg_ref[...] = jnp.zeros_like(dlg_ref)
            dlb_ref[...] = jnp.zeros_like(dlb_ref)
            dcb_ref[...] = jnp.zeros_like(dcb_ref)

        dlg_ref[...] += jnp.sum(dn * xh, axis=0, keepdims=True)
        dlb_ref[...] += jnp.sum(dn, axis=0, keepdims=True)
        dcb_ref[...] += jnp.sum(dcv, axis=0, keepdims=True)

    row = pl.BlockSpec((tt, C), lambda i: (i, 0))
    par = pl.BlockSpec((1, C), lambda i: (0, 0))
    return pl.pallas_call(
        body, name=name, grid=(S // tt,),
        in_specs=[row, row, par, par],
        out_specs=[row, par, par, par],
        out_shape=[jax.ShapeDtypeStruct((S, C), F32)] + [jax.ShapeDtypeStruct((1, C), F32)] * 3,
        compiler_params=_cparams("arbitrary"),
    )(da, cv, lg, lb)


def _conv_bwd_taps(dcv, u, w, name, tt=512):
    S, C = dcv.shape
    tt = _tile(S, tt)
    hpt = tt // CONV_HALO
    nt = S // tt
    WPAD = 32

    def body(dc_ref, dn_ref, a_ref, g_ref, ap_ref, gp_ref, w_ref, da_ref, dg_ref, dw_ref, hext, dext):
        i = pl.program_id(0)
        a = a_ref[...]
        sg = _sigmoid(g_ref[...])
        hext[pl.ds(CONV_HALO, tt), :] = a * sg
        hp = ap_ref[...] * _sigmoid(gp_ref[...])
        hext[pl.ds(0, CONV_HALO), :] = jnp.where(i > 0, hp, 0.0)
        dc = dc_ref[...]
        dext[pl.ds(0, tt), :] = dc
        dext[pl.ds(tt, CONV_HALO), :] = jnp.where(i < nt - 1, dn_ref[...], 0.0)

        @pl.when(i == 0)
        def _():
            dw_ref[...] = jnp.zeros_like(dw_ref)

        dh = jnp.zeros((tt, C), F32)
        for k in range(CONV_WIDTH):
            dh = dh + w_ref[pl.ds(k, 1), :] * dext[pl.ds(CONV_WIDTH - 1 - k, tt), :]
            hk = hext[pl.ds(CONV_HALO - (CONV_WIDTH - 1) + k, tt), :]
            dw_ref[pl.ds(k, 1), :] += jnp.sum(dc * hk, axis=0, keepdims=True)
        da_ref[...] = dh * sg
        dg_ref[...] = dh * a * sg * (1.0 - sg)

    row = pl.BlockSpec((tt, C), lambda i: (i, 0))
    nxt = pl.BlockSpec((CONV_HALO, C), lambda i: (jnp.minimum((i + 1) * hpt, S // CONV_HALO - 1), 0))
    cur = lambda cb_: pl.BlockSpec((tt, C), lambda i: (i, cb_))
    prev = lambda cb_: pl.BlockSpec((CONV_HALO, C), lambda i: (jnp.maximum(i * hpt - 1, 0), cb_))
    da, dg, dw = pl.pallas_call(
        body, name=name, grid=(nt,),
        in_specs=[row, nxt, cur(0), cur(1), prev(0), prev(1), pl.BlockSpec((CONV_WIDTH, C), lambda i: (0, 0))],
        out_specs=[row, row, pl.BlockSpec((WPAD, C), lambda i: (0, 0))],
        out_shape=[jax.ShapeDtypeStruct((S, C), F32)] * 2 + [jax.ShapeDtypeStruct((WPAD, C), F32)],
        scratch_shapes=[pltpu.VMEM((tt + CONV_HALO, C), F32)] * 2,
        compiler_params=_cparams("arbitrary"),
    )(dcv, dcv, u, u, u, u, w)
    return da, dg, dw[:CONV_WIDTH]


def _lower_bound(logits):
    e = jnp.exp(logits - jnp.max(logits, axis=0, keepdims=True))
    p = e / jnp.sum(e, axis=0, keepdims=True)
    return p[0:1, :], p


def _tri(n, lower):
    r = lax.broadcasted_iota(jnp.int32, (n, n), 0)
    c = lax.broadcasted_iota(jnp.int32, (n, n), 1)
    return ((c <= r) if lower else (c >= r)).astype(F32)


def _hgrn_gates(xq, xf, lb):
    sq = _sigmoid(xq)
    q = xq * sq
    sf = _sigmoid(xf)
    f = lb + (1.0 - lb) * sf
    logf = jnp.log(f)
    L = _dot(_tri(CHUNK, True), logf, NN, HI)
    Lend = L[CHUNK - 1:CHUNK, :]
    eL = jnp.exp(L)
    enL = jnp.exp(-L)
    eLe = jnp.exp(Lend - L)
    kk = 1.0 - f
    return dict(sq=sq, q=q, sf=sf, f=f, L=L, Lend=Lend, eL=eL, enL=enL, eLe=eLe, kk=kk,
                qe=q * eL, ke=kk * enL, kd=kk * eLe)


def _hgrn_fwd(u, lbl, gn, name, tb=256):
    S = u.shape[0]
    W = gn.shape[1]
    H = W // HEAD_DIM
    tb = _tile(S, tb)
    cpb = tb // CHUNK
    nc = S // CHUNK
    R = lbl.shape[0]

    def body(q_ref, f_ref, i_ref, g_ref, lbl_ref, gn_ref, out_ref, raw_ref, st_ref, state):
        @pl.when(pl.program_id(0) == 0)
        def _():
            state[...] = jnp.zeros_like(state)

        lb, _ = _lower_bound(lbl_ref[...])
        tril = _tri(CHUNK, True) > 0.5

        def chunk(c, carry):
            rows = pl.ds(pl.multiple_of(c * CHUNK, CHUNK), CHUNK)
            G = _hgrn_gates(q_ref[rows, :], f_ref[rows, :], lb)
            v = i_ref[rows, :]
            xg = g_ref[rows, :]
            outs = []
            for h in range(H):
                ln = slice(h * HEAD_DIM, (h + 1) * HEAD_DIM)
                qe, ke, kd, vh = G["qe"][:, ln].astype(BF16), G["ke"][:, ln].astype(BF16), G["kd"][:, ln].astype(BF16), v[:, ln].astype(BF16)
                st = state[h]
                st_ref[c, h] = st
                A = jnp.where(tril, _dot(qe, ke, NT), 0.0)
                o = _dot(A.astype(BF16), vh, NN) + _dot(qe, st.astype(BF16), NT)
                state[h] = jnp.exp(G["Lend"][:, ln]) * st + _dot(vh, kd, TN)
                outs.append(o)
            o = jnp.concatenate(outs, axis=1)
            raw_ref[rows, :] = o
            ns = []
            for h in range(H):
                oh = outs[h]
                ns.append(oh * lax.rsqrt(jnp.mean(oh * oh, axis=-1, keepdims=True) + LN_EPS))
            n = jnp.concatenate(ns, axis=1)
            out_ref[rows, :] = n * gn_ref[...] * (xg * _sigmoid(xg))
            return carry

        lax.fori_loop(0, cpb, chunk, 0)

    col = lambda cb_: pl.BlockSpec((tb, W), lambda i: (i, cb_))
    row = pl.BlockSpec((tb, W), lambda i: (i, 0))
    return pl.pallas_call(
        body, name=name, grid=(S // tb,),
        in_specs=[col(2), col(3), col(4), col(5), pl.BlockSpec((R, W), lambda i: (0, 0)), pl.BlockSpec((1, W), lambda i: (0, 0))],
        out_specs=[row, row, pl.BlockSpec((cpb, H, HEAD_DIM, HEAD_DIM), lambda i: (i, 0, 0, 0))],
        out_shape=[jax.ShapeDtypeStruct((S, W), F32), jax.ShapeDtypeStruct((S, W), F32),
                   jax.ShapeDtypeStruct((nc, H, HEAD_DIM, HEAD_DIM), F32)],
        scratch_shapes=[pltpu.VMEM((H, HEAD_DIM, HEAD_DIM), F32)],
        compiler_params=_cparams("arbitrary"),
    )(u, u, u, u, lbl, gn)


def _hgrn_bwd(dout, u, raw, states, lbl, gn, name, tb=256):
    S = u.shape[0]
    W = gn.shape[1]
    H = W // HEAD_DIM
    tb = _tile(S, tb)
    cpb = tb // CHUNK
    nb = S // tb
    R = lbl.shape[0]

    def body(do_ref, q_ref, f_ref, i_ref, g_ref, raw_ref, st_ref, lbl_ref, gn_ref,
             dq_ref, df_ref, di_ref, dg_ref, dgn_ref, dlbl_ref, dstate, dlb_acc):
        @pl.when(pl.program_id(0) == 0)
        def _():
            dstate[...] = jnp.zeros_like(dstate)
            dlb_acc[...] = jnp.zeros_like(dlb_acc)
            dgn_ref[...] = jnp.zeros_like(dgn_ref)

        lb, p = _lower_bound(lbl_ref[...])
        tril = _tri(CHUNK, True) > 0.5
        triu = _tri(CHUNK, False)
        gn_row = gn_ref[...]

        def chunk(cc, carry):
            c = cpb - 1 - cc
            rows = pl.ds(pl.multiple_of(c * CHUNK, CHUNK), CHUNK)
            xq, xf = q_ref[rows, :], f_ref[rows, :]
            G = _hgrn_gates(xq, xf, lb)
            v = i_ref[rows, :]
            xg = g_ref[rows, :]
            dy = do_ref[rows, :]
            o = raw_ref[rows, :]
            sgg = _sigmoid(xg)
            silu_g = xg * sgg
            do_parts, n_parts = [], []
            for h in range(H):
                ln = slice(h * HEAD_DIM, (h + 1) * HEAD_DIM)
                oh = o[:, ln]
                r = lax.rsqrt(jnp.mean(oh * oh, axis=-1, keepdims=True) + LN_EPS)
                nh = oh * r
                dn = dy[:, ln] * gn_row[:, ln] * silu_g[:, ln]
                do_parts.append(r * (dn - nh * jnp.mean(dn * nh, axis=-1, keepdims=True)))
                n_parts.append(nh)
            n = jnp.concatenate(n_parts, axis=1)
            dgn_ref[...] += jnp.sum(dy * n * silu_g, axis=0, keepdims=True)
            dg_ref[rows, :] = dy * n * gn_row * sgg * (1.0 + xg * (1.0 - sgg))
            dqe_p, dke_p, dkd_p, dv_p, dle_p = [], [], [], [], []
            for h in range(H):
                ln = slice(h * HEAD_DIM, (h + 1) * HEAD_DIM)
                qe, ke, kd, vh = G["qe"][:, ln].astype(BF16), G["ke"][:, ln].astype(BF16), G["kd"][:, ln].astype(BF16), v[:, ln].astype(BF16)
                doh = do_parts[h].astype(BF16)
                st = st_ref[c, h]
                dst = dstate[h]
                dec = jnp.exp(G["Lend"][:, ln])
                A = jnp.where(tril, _dot(qe, ke, NT), 0.0).astype(BF16)
                dA = jnp.where(tril, _dot(doh, vh, NT), 0.0).astype(BF16)
                dv_p.append(_dot(A, doh, TN) + _dot(kd, dst.astype(BF16), NT))
                dqe_p.append(_dot(dA, ke, NN) + _dot(doh, st.astype(BF16), NN))
                dke_p.append(_dot(dA, qe, TN))
                dkd_p.append(_dot(vh, dst.astype(BF16), NN))
                dle_p.append(jnp.sum(dst * dec * st, axis=0, keepdims=True))
                dstate[h] = dec * dst + _dot(doh, qe, TN)
            dqe = jnp.concatenate(dqe_p, axis=1)
            dke = jnp.concatenate(dke_p, axis=1)
            dkd = jnp.concatenate(dkd_p, axis=1)
            qe_r, ke_r, kd_r = _round_bf16(G["qe"]), _round_bf16(G["ke"]), _round_bf16(G["kd"])
            dLend = jnp.concatenate(dle_p, axis=1) + jnp.sum(dkd * kd_r, axis=0, keepdims=True)
            dL = dqe * qe_r - dke * ke_r - dkd * kd_r
            dlogf = _dot(triu, dL, NN, HI) + dLend
            dkk = dke * G["enL"] + dkd * G["eLe"]
            dfv = dlogf / G["f"] - dkk
            sf = G["sf"]
            df_ref[rows, :] = dfv * (1.0 - lb) * sf * (1.0 - sf)
            dlb_acc[...] += jnp.sum(dfv * (1.0 - sf), axis=0, keepdims=True)
            sq = G["sq"]
            dq_ref[rows, :] = dqe * G["eL"] * sq * (1.0 + xq * (1.0 - sq))
            di_ref[rows, :] = jnp.concatenate(dv_p, axis=1)
            return carry

        lax.fori_loop(0, cpb, chunk, 0)

        onehot0 = (lax.broadcasted_iota(jnp.int32, (R, W), 0) == 0).astype(F32)
        dlbl_ref[...] = p * (onehot0 - p[0:1, :]) * dlb_acc[...]

    rev = lambda i: nb - 1 - i
    col = lambda cb_: pl.BlockSpec((tb, W), lambda i: (rev(i), cb_))
    row = pl.BlockSpec((tb, W), lambda i: (rev(i), 0))
    par = pl.BlockSpec((1, W), lambda i: (0, 0))
    parR = pl.BlockSpec((R, W), lambda i: (0, 0))
    return pl.pallas_call(
        body, name=name, grid=(nb,),
        in_specs=[row, col(2), col(3), col(4), col(5), row,
                  pl.BlockSpec((cpb, H, HEAD_DIM, HEAD_DIM), lambda i: (rev(i), 0, 0, 0)), parR, par],
        out_specs=[row, row, row, row, par, parR],
        out_shape=[jax.ShapeDtypeStruct((S, W), F32)] * 4 + [jax.ShapeDtypeStruct((1, W), F32), jax.ShapeDtypeStruct((R, W), F32)],
        scratch_shapes=[pltpu.VMEM((H, HEAD_DIM, HEAD_DIM), F32), pltpu.VMEM((1, W), F32)],
        compiler_params=_cparams("arbitrary"),
    )(dout, u, u, u, u, raw, states, lbl, gn)


def _fgate_fwd(xb, wft, bf, name, ts=512):
    S, D = xb.shape
    H = wft.shape[0]
    ts = _tile(S, ts)

    def body(x_ref, w_ref, b_ref, lg_ref, F_ref, carry):
        @pl.when(pl.program_id(0) == 0)
        def _():
            carry[...] = jnp.zeros_like(carry)

        lg = _dot(_round_bf16(w_ref[...]), _round_bf16(x_ref[...]), NT, HI) + b_ref[...]
        lg_ref[...] = lg
        ls = jnp.minimum(lg, 0.0) - jnp.log(1.0 + jnp.exp(-jnp.abs(lg)))
        F = _dot(ls, _tri(ts, False), NN, HI) + carry[...]
        F_ref[...] = F
        carry[...] = F[:, ts - 1:ts]

    return pl.pallas_call(
        body, name=name, grid=(S // ts,),
        in_specs=[pl.BlockSpec((ts, D), lambda i: (i, 0)), pl.BlockSpec((H, D), lambda i: (0, 0)), pl.BlockSpec((H, 1), lambda i: (0, 0))],
        out_specs=[pl.BlockSpec((H, ts), lambda i: (0, i))] * 2,
        out_shape=[jax.ShapeDtypeStruct((H, S), F32)] * 2,
        scratch_shapes=[pltpu.VMEM((H, 1), F32)],
        compiler_params=_cparams("arbitrary"),
    )(xb, wft, bf)


def _fgate_bwd(dF, lg, name, ts=512):
    H, S = dF.shape
    ts = _tile(S, ts)
    nb = S // ts

    def body(dF_ref, lg_ref, dl_ref, db_ref, carry):
        @pl.when(pl.program_id(0) == 0)
        def _():
            carry[...] = jnp.zeros_like(carry)
            db_ref[...] = jnp.zeros_like(db_ref)

        dls = _dot(dF_ref[...], _tri(ts, True), NN, HI) + carry[...]
        carry[...] = dls[:, 0:1]
        dl = dls * _sigmoid(-lg_ref[...])
        dl_ref[...] = dl
        db_ref[...] += jnp.sum(dl, axis=1, keepdims=True)

    blk = pl.BlockSpec((H, ts), lambda i: (0, nb - 1 - i))
    return pl.pallas_call(
        body, name=name, grid=(nb,),
        in_specs=[blk, blk],
        out_specs=[blk, pl.BlockSpec((H, 1), lambda i: (0, 0))],
        out_shape=[jax.ShapeDtypeStruct((H, S), F32), jax.ShapeDtypeStruct((H, 1), F32)],
        scratch_shapes=[pltpu.VMEM((H, 1), F32)],
        compiler_params=_cparams("arbitrary"),
    )(dF, lg)


def _causal_keep(i, j, tq, tk):
    rows = lax.broadcasted_iota(jnp.int32, (tq, tk), 0)
    cols = lax.broadcasted_iota(jnp.int32, (tq, tk), 1)
    return jnp.logical_or(j < i, cols <= rows)


def _fox_lse(u, F3, H, name, carry=None, t=512):
    S = u.shape[0]
    t = _tile(S, t)
    n = S // t
    scale = HEAD_DIM ** -0.5

    def body(q_ref, k_ref, fk_ref, lse_ref, m_s, l_s):
        i, j = pl.program_id(1), pl.program_id(2)

        @pl.when(j == 0)
        def _():
            m_s[...] = jnp.full_like(m_s, NEG)
            l_s[...] = jnp.zeros_like(l_s)

        @pl.when(j <= i)
        def _():
            s = _dot(q_ref[...].astype(BF16), k_ref[...].astype(BF16), NT) * scale - fk_ref[...]
            s = jnp.where(_causal_keep(i, j, t, t), s, NEG)
            m_new = jnp.maximum(m_s[...], jnp.max(s, axis=-1, keepdims=True))
            l_s[...] = jnp.exp(m_s[...] - m_new) * l_s[...] + jnp.sum(jnp.exp(s - m_new), axis=-1, keepdims=True)
            m_s[...] = m_new

        @pl.when(j == n - 1)
        def _():
            lse_ref[...] = m_s[...] + jnp.log(l_s[...])

    outs, carried = _call(
        body, name=name, grid=(H, n, n),
        in_specs=[pl.BlockSpec((t, HEAD_DIM), lambda h, i, j: (i, h)),
                  pl.BlockSpec((t, HEAD_DIM), lambda h, i, j: (jnp.minimum(j, i), H + h)),
                  pl.BlockSpec((None, 1, t), lambda h, i, j: (h, 0, jnp.minimum(j, i)))],
        out_specs=[pl.BlockSpec((None, t, 1), lambda h, i, j: (h, i, 0))],
        out_shape=[jax.ShapeDtypeStruct((H, S, 1), F32)],
        scratch_shapes=[pltpu.VMEM((t, 1), F32), pltpu.VMEM((t, 1), F32)],
        sem=("parallel", "parallel", "arbitrary"), args=(u, u, F3), carry=carry)
    return outs[0], carried


def _fox_probs(q_ref, k_ref, fk_ref, lse_ref, i, j, t, scale):
    s = _dot(q_ref[...].astype(BF16), k_ref[...].astype(BF16), NT) * scale - fk_ref[...]
    p = jnp.where(_causal_keep(i, j, t, t), jnp.exp(s - lse_ref[...]), 0.0)
    return p, p.astype(BF16)


def _softmax_bwd(p, pb, dp, delta):
    return pb.astype(F32) * dp - p * delta


def _fox_fwd(u, F3, lse, H, name, t=512):
    S = u.shape[0]
    W = H * HEAD_DIM
    t = _tile(S, t)
    n = S // t
    scale = HEAD_DIM ** -0.5

    def body(q_ref, k_ref, v_ref, fk_ref, lse_ref, o_ref, acc_s):
        i, j = pl.program_id(1), pl.program_id(2)

        @pl.when(j == 0)
        def _():
            acc_s[...] = jnp.zeros_like(acc_s)

        @pl.when(j <= i)
        def _():
            _, pb = _fox_probs(q_ref, k_ref, fk_ref, lse_ref, i, j, t, scale)
            acc_s[...] += _dot(pb, v_ref[...].astype(BF16), NN)

        @pl.when(j == n - 1)
        def _():
            o_ref[...] = acc_s[...]

    kv = lambda off: pl.BlockSpec((t, HEAD_DIM), lambda h, i, j: (jnp.minimum(j, i), off * H + h))
    return pl.pallas_call(
        body, name=name, grid=(H, n, n),
        in_specs=[pl.BlockSpec((t, HEAD_DIM), lambda h, i, j: (i, h)), kv(1), kv(2),
                  pl.BlockSpec((None, 1, t), lambda h, i, j: (h, 0, jnp.minimum(j, i))),
                  pl.BlockSpec((None, t, 1), lambda h, i, j: (h, i, 0))],
        out_specs=pl.BlockSpec((t, HEAD_DIM), lambda h, i, j: (i, h)),
        out_shape=jax.ShapeDtypeStruct((S, W), F32),
        scratch_shapes=[pltpu.VMEM((t, HEAD_DIM), F32)],
        compiler_params=_cparams("parallel", "parallel", "arbitrary"),
    )(u, u, u, F3, lse)


def _fox_bwd_dq(u, F3, o, do, lse, H, name, t=512):
    S = u.shape[0]
    W = H * HEAD_DIM
    t = _tile(S, t)
    n = S // t
    scale = HEAD_DIM ** -0.5

    def body(q_ref, k_ref, v_ref, fk_ref, o_ref, do_ref, lse_ref, dq_ref, acc_s, dl_s):
        i, j = pl.program_id(1), pl.program_id(2)

        @pl.when(j == 0)
        def _():
            acc_s[...] = jnp.zeros_like(acc_s)
            dl_s[...] = jnp.sum(_round_bf16(do_ref[...]) * o_ref[...], axis=-1, keepdims=True)

        @pl.when(j <= i)
        def _():
            p, pb = _fox_probs(q_ref, k_ref, fk_ref, lse_ref, i, j, t, scale)
            dp = _dot(do_ref[...].astype(BF16), v_ref[...].astype(BF16), NT)
            ds = _softmax_bwd(p, pb, dp, dl_s[...])
            acc_s[...] += _dot(ds.astype(BF16), k_ref[...].astype(BF16), NN)

        @pl.when(j == n - 1)
        def _():
            dq_ref[...] = acc_s[...] * scale

    qblk = pl.BlockSpec((t, HEAD_DIM), lambda h, i, j: (i, h))
    kv = lambda off: pl.BlockSpec((t, HEAD_DIM), lambda h, i, j: (jnp.minimum(j, i), off * H + h))
    return pl.pallas_call(
        body, name=name, grid=(H, n, n),
        in_specs=[qblk, kv(1), kv(2), pl.BlockSpec((None, 1, t), lambda h, i, j: (h, 0, jnp.minimum(j, i))),
                  qblk, qblk, pl.BlockSpec((None, t, 1), lambda h, i, j: (h, i, 0))],
        out_specs=qblk,
        out_shape=jax.ShapeDtypeStruct((S, W), F32),
        scratch_shapes=[pltpu.VMEM((t, HEAD_DIM), F32), pltpu.VMEM((t, 1), F32)],
        compiler_params=_cparams("parallel", "parallel", "arbitrary"),
    )(u, u, u, F3, o, do, lse)


def _fox_bwd_dkv(u, F3, o, do, lse, H, name, t=512):
    S = u.shape[0]
    W = H * HEAD_DIM
    t = _tile(S, t)
    n = S // t
    scale = HEAD_DIM ** -0.5

    def body(q_ref, k_ref, v_ref, fk_ref, o_ref, do_ref, lse_ref, dk_ref, dv_ref, dF_ref, dk_s, dv_s, dF_s):
        j, i = pl.program_id(1), pl.program_id(2)

        @pl.when(i == 0)
        def _():
            dk_s[...] = jnp.zeros_like(dk_s)
            dv_s[...] = jnp.zeros_like(dv_s)
            dF_s[...] = jnp.zeros_like(dF_s)

        @pl.when(i >= j)
        def _():
            q = q_ref[...].astype(BF16)
            do = do_ref[...]
            dob = do.astype(BF16)
            p, pb = _fox_probs(q_ref, k_ref, fk_ref, lse_ref, i, j, t, scale)
            dp = _dot(dob, v_ref[...].astype(BF16), NT)
            delta = jnp.sum(dob.astype(F32) * o_ref[...], axis=-1, keepdims=True)
            ds = _softmax_bwd(p, pb, dp, delta)
            dv_s[...] += _dot(pb, dob, TN)
            dk_s[...] += _dot(ds.astype(BF16), q, TN)
            dF_s[...] -= jnp.sum(ds, axis=0, keepdims=True)

        @pl.when(i == n - 1)
        def _():
            dk_ref[...] = dk_s[...] * scale
            dv_ref[...] = dv_s[...]
            dF_ref[...] = dF_s[...]

    qblk = pl.BlockSpec((t, HEAD_DIM), lambda h, j, i: (jnp.maximum(i, j), h))
    kv = lambda off: pl.BlockSpec((t, HEAD_DIM), lambda h, j, i: (j, off * H + h))
    oblk = pl.BlockSpec((t, HEAD_DIM), lambda h, j, i: (j, h))
    fblk = pl.BlockSpec((None, 1, t), lambda h, j, i: (h, 0, j))
    return pl.pallas_call(
        body, name=name, grid=(H, n, n),
        in_specs=[qblk, kv(1), kv(2), fblk, qblk, qblk, pl.BlockSpec((None, t, 1), lambda h, j, i: (h, jnp.maximum(i, j), 0))],
        out_specs=[oblk, oblk, fblk],
        out_shape=[jax.ShapeDtypeStruct((S, W), F32), jax.ShapeDtypeStruct((S, W), F32), jax.ShapeDtypeStruct((H, 1, S), F32)],
        scratch_shapes=[pltpu.VMEM((t, HEAD_DIM), F32), pltpu.VMEM((t, HEAD_DIM), F32), pltpu.VMEM((1, t), F32)],
        compiler_params=_cparams("parallel", "parallel", "arbitrary"),
    )(u, u, u, F3, o, do, lse)


def _rel_index_matrix():
    a = lax.broadcasted_iota(jnp.int32, (REL_TABLE, CA_SKEW), 0)
    j = lax.broadcasted_iota(jnp.int32, (REL_TABLE, CA_SKEW), 1)
    rel = jnp.where(j < CA_WIN, CA_LEFT_CHUNKS * CHUNK - j, REL_CLIP)
    idx = jnp.clip(jnp.minimum(rel, REL_CLIP) + (CHUNK - 1), 0, REL_TABLE - 1)
    return (a == idx).astype(F32)


def _skew(x, sign):
    r = lax.broadcasted_iota(jnp.int32, x.shape, 0)
    for b in range(int(math.log2(CA_TILE))):
        sh = (1 << b) if sign > 0 else CA_SKEW - (1 << b)
        x = jnp.where((r >> b) & 1 == 1, pltpu.roll(x, sh, 1), x)
    return x


def _band_valid():
    shift = int(math.log2(CHUNK))
    r = lax.broadcasted_iota(jnp.int32, (CA_TILE, CA_WIN), 0) >> shift
    m = lax.broadcasted_iota(jnp.int32, (CA_TILE, CA_WIN), 1) >> shift
    return jnp.logical_and(m >= r, m <= r + CA_LEFT_CHUNKS)


def _relbias_fwd(table, name):
    H = table.shape[0]

    def body(t_ref, b_ref):
        rowv = _dot(t_ref[...], _rel_index_matrix(), NN, HI)
        valid = _band_valid()
        for h in range(H):
            x = _skew(jnp.broadcast_to(rowv[h:h + 1, :], (CA_TILE, CA_SKEW)), +1)
            b_ref[h] = jnp.where(valid, x[:, :CA_WIN], NEG)

    return pl.pallas_call(
        body, name=name,
        out_shape=jax.ShapeDtypeStruct((H, CA_TILE, CA_WIN), F32),
        compiler_params=pltpu.CompilerParams(vmem_limit_bytes=VMEM_LIMIT_V7X),
    )(table)


def _relbias_bwd(dB, name):
    H = dB.shape[0]
    HP = -(-H // 8) * 8

    def body(d_ref, dt_ref, rows):
        rows[...] = jnp.zeros_like(rows)
        for h in range(H):
            x = jnp.concatenate([d_ref[h], jnp.zeros((CA_TILE, CA_SKEW - CA_WIN), F32)], axis=1)
            rows[pl.ds(h, 1), :] = jnp.sum(_skew(x, -1), axis=0, keepdims=True)
        dt_ref[...] = _dot(rows[...], _rel_index_matrix(), NT, HI)[:H]

    return pl.pallas_call(
        body, name=name,
        out_shape=jax.ShapeDtypeStruct((H, REL_TABLE), F32),
        scratch_shapes=[pltpu.VMEM((HP, CA_SKEW), F32)],
        compiler_params=pltpu.CompilerParams(vmem_limit_bytes=VMEM_LIMIT_V7X),
    )(dB)


CA_PIECES = CA_WIN // CA_TILE


def _ca_fwd(u, bias, H, name):
    S = u.shape[0]
    W = H * HEAD_DIM
    T = CA_TILE
    n = S // T
    scale = HEAD_DIM ** -0.5

    def body(q_ref, k0, k1, k2, v0, v1, v2, b_ref, o_ref, lse_ref):
        i = pl.program_id(1)
        q = q_ref[...].astype(BF16)
        ss = []
        for pce, k_ref in enumerate((k0, k1, k2)):
            s = _dot(q, k_ref[...].astype(BF16), NT) * scale + b_ref[:, pce * T:(pce + 1) * T]
            ss.append(jnp.where(i + pce >= CA_PIECES - 1, s, NEG))
        m = jnp.maximum(jnp.maximum(jnp.max(ss[0], -1, keepdims=True), jnp.max(ss[1], -1, keepdims=True)), jnp.max(ss[2], -1, keepdims=True))
        ps = [jnp.exp(s - m) for s in ss]
        l = sum(jnp.sum(p, -1, keepdims=True) for p in ps)
        inv = 1.0 / l
        o_ref[...] = sum(_dot((p * inv).astype(BF16), v_ref[...].astype(BF16), NN) for p, v_ref in zip(ps, (v0, v1, v2)))
        lse_ref[...] = m + jnp.log(l)

    qblk = pl.BlockSpec((T, HEAD_DIM), lambda h, i: (i, h))
    kv = lambda off, back: pl.BlockSpec((T, HEAD_DIM), lambda h, i: (jnp.maximum(i - back, 0), off * H + h))
    return pl.pallas_call(
        body, name=name, grid=(H, n),
        in_specs=[qblk, kv(1, 2), kv(1, 1), kv(1, 0), kv(2, 2), kv(2, 1), kv(2, 0),
                  pl.BlockSpec((None, T, CA_WIN), lambda h, i: (h, 0, 0))],
        out_specs=[qblk, pl.BlockSpec((None, T, 1), lambda h, i: (h, i, 0))],
        out_shape=[jax.ShapeDtypeStruct((S, W), F32), jax.ShapeDtypeStruct((H, S, 1), F32)],
        compiler_params=_cparams("parallel", "arbitrary"),
    )(u, u, u, u, u, u, u, bias)


def _ca_bwd_dq(u, bias, o, do, lse, H, name):
    S = u.shape[0]
    W = H * HEAD_DIM
    T = CA_TILE
    n = S // T
    scale = HEAD_DIM ** -0.5

    def body(q_ref, k0, k1, k2, v0, v1, v2, b_ref, o_ref, do_ref, lse_ref, dq_ref, db_ref):
        i = pl.program_id(1)

        @pl.when(i == 0)
        def _():
            db_ref[...] = jnp.zeros_like(db_ref)

        q = q_ref[...].astype(BF16)
        do = do_ref[...]
        dob = do.astype(BF16)
        delta = jnp.sum(dob.astype(F32) * o_ref[...], axis=-1, keepdims=True)
        dq = jnp.zeros((T, HEAD_DIM), F32)
        for pce, (k_ref, v_ref) in enumerate(((k0, v0), (k1, v1), (k2, v2))):
            k = k_ref[...].astype(BF16)
            s = _dot(q, k, NT) * scale + b_ref[:, pce * T:(pce + 1) * T]
            p = jnp.where(i + pce >= CA_PIECES - 1, jnp.exp(s - lse_ref[...]), 0.0)
            ds = _softmax_bwd(p, p.astype(BF16), _dot(dob, v_ref[...].astype(BF16), NT), delta)
            db_ref[:, pce * T:(pce + 1) * T] += ds
            dq = dq + _dot(ds.astype(BF16), k, NN)
        dq_ref[...] = dq * scale

    qblk = pl.BlockSpec((T, HEAD_DIM), lambda h, i: (i, h))
    kv = lambda off, back: pl.BlockSpec((T, HEAD_DIM), lambda h, i: (jnp.maximum(i - back, 0), off * H + h))
    bblk = pl.BlockSpec((None, T, CA_WIN), lambda h, i: (h, 0, 0))
    return pl.pallas_call(
        body, name=name, grid=(H, n),
        in_specs=[qblk, kv(1, 2), kv(1, 1), kv(1, 0), kv(2, 2), kv(2, 1), kv(2, 0), bblk,
                  qblk, qblk, pl.BlockSpec((None, T, 1), lambda h, i: (h, i, 0))],
        out_specs=[qblk, bblk],
        out_shape=[jax.ShapeDtypeStruct((S, W), F32), jax.ShapeDtypeStruct((H, T, CA_WIN), F32)],
        compiler_params=_cparams("parallel", "arbitrary"),
    )(u, u, u, u, u, u, u, bias, o, do, lse)


def _ca_bwd_dkv(u, bias, o, do, lse, H, name):
    S = u.shape[0]
    W = H * HEAD_DIM
    T = CA_TILE
    n = S // T
    scale = HEAD_DIM ** -0.5

    def body(k_ref, v_ref, q0, q1, q2, o0, o1, o2, d0, d1, d2, l0, l1, l2, b_ref, dk_ref, dv_ref):
        i = pl.program_id(1)
        k = k_ref[...].astype(BF16)
        v = v_ref[...].astype(BF16)
        dk = jnp.zeros((T, HEAD_DIM), F32)
        dv = jnp.zeros((T, HEAD_DIM), F32)
        for ahead, (q_ref, o_ref, do_ref, lse_ref) in enumerate(((q0, o0, d0, l0), (q1, o1, d1, l1), (q2, o2, d2, l2))):
            pce = CA_PIECES - 1 - ahead
            q = q_ref[...].astype(BF16)
            do = do_ref[...]
            dob = do.astype(BF16)
            s = _dot(q, k, NT) * scale + b_ref[:, pce * T:(pce + 1) * T]
            p = jnp.where(i + ahead <= n - 1, jnp.exp(s - lse_ref[...]), 0.0)
            delta = jnp.sum(dob.astype(F32) * o_ref[...], axis=-1, keepdims=True)
            ds = _softmax_bwd(p, p.astype(BF16), _dot(dob, v, NT), delta)
            dv = dv + _dot(p.astype(BF16), dob, TN)
            dk = dk + _dot(ds.astype(BF16), q, TN)
        dk_ref[...] = dk * scale
        dv_ref[...] = dv

    kv = lambda off: pl.BlockSpec((T, HEAD_DIM), lambda h, i: (i, off * H + h))
    qa = lambda ahead: pl.BlockSpec((T, HEAD_DIM), lambda h, i: (jnp.minimum(i + ahead, n - 1), h))
    la = lambda ahead: pl.BlockSpec((None, T, 1), lambda h, i: (h, jnp.minimum(i + ahead, n - 1), 0))
    oblk = pl.BlockSpec((T, HEAD_DIM), lambda h, i: (i, h))
    return pl.pallas_call(
        body, name=name, grid=(H, n),
        in_specs=[kv(1), kv(2), qa(0), qa(1), qa(2), qa(0), qa(1), qa(2), qa(0), qa(1), qa(2), la(0), la(1), la(2),
                  pl.BlockSpec((None, T, CA_WIN), lambda h, i: (h, 0, 0))],
        out_specs=[oblk, oblk],
        out_shape=[jax.ShapeDtypeStruct((S, W), F32)] * 2,
        compiler_params=_cparams("parallel", "parallel"),
    )(u, u, u, u, u, o, o, o, do, do, do, lse, lse, lse, bias)


def _sum_parts(parts, name):
    _, R, C = parts.shape

    def body(p_ref, o_ref):
        acc = p_ref[0].astype(F32)
        for d in range(1, N_DEV):
            acc = acc + p_ref[d].astype(F32)
        o_ref[...] = acc

    return pl.pallas_call(
        body, name=name, out_shape=jax.ShapeDtypeStruct((R, C), F32),
        compiler_params=pltpu.CompilerParams(vmem_limit_bytes=VMEM_LIMIT_V7X),
    )(parts)


def _adamw(parts, w, m, v, name, tr=256):
    P, R, C = parts.shape
    tr = _tile(R, tr)
    c1 = 1.0 / (1.0 - ADAM_B1 ** ADAM_STEP)
    c2 = 1.0 / (1.0 - ADAM_B2 ** ADAM_STEP)

    def body(p_ref, w_ref, m_ref, v_ref, g_ref, d_ref, nm_ref, nv_ref):
        g = p_ref[0].astype(F32)
        for d in range(1, P):
            g = g + p_ref[d].astype(F32)
        nm = ADAM_B1 * m_ref[...] + (1.0 - ADAM_B1) * g
        nv = ADAM_B2 * v_ref[...] + (1.0 - ADAM_B2) * (g * g)
        g_ref[...] = g
        nm_ref[...] = nm
        nv_ref[...] = nv
        d_ref[...] = -ADAM_LR * ((nm * c1) / (jnp.sqrt(nv * c2) + ADAM_EPS) + ADAM_WD * w_ref[...])

    blk = pl.BlockSpec((tr, C), lambda i: (i, 0))
    return pl.pallas_call(
        body, name=name, grid=(R // tr,),
        in_specs=[pl.BlockSpec((P, tr, C), lambda i: (0, i, 0)), blk, blk, blk],
        out_specs=[blk] * 4,
        out_shape=[jax.ShapeDtypeStruct((R, C), F32)] * 4,
        compiler_params=_cparams("parallel"),
    )(parts, w, m, v)


def _peer(d):
    x, y, c = lax.axis_index("x"), lax.axis_index("y"), lax.axis_index("c")
    px = (1 - x) if d & 4 else x
    py = (1 - y) if d & 2 else y
    pc = (1 - c) if d & 1 else c
    return (px, py, pc), 4 * px + 2 * py + pc


N_PEER = N_DEV - 1


def _exchange_copies(ins, outs, sems, scatter):
    send_sems, recv_sems, local_sems = sems
    me = 4 * lax.axis_index("x") + 2 * lax.axis_index("y") + lax.axis_index("c")
    starts, waits = [], []
    for t in range(len(ins)):
        loc = pltpu.make_async_copy(ins[t].at[me] if scatter else ins[t], outs[t].at[me], local_sems.at[t])
        starts.append(loc.start)
        waits.append(loc.wait)
        for d in range(1, N_DEV):
            peer, pidx = _peer(d)
            src = ins[t].at[pidx] if scatter else ins[t]
            k = t * N_PEER + d - 1
            common = dict(src_ref=src, send_sem=send_sems.at[k], recv_sem=recv_sems.at[k], device_id=peer, device_id_type=MESH_ID)
            starts.append(pltpu.make_async_remote_copy(dst_ref=outs[t].at[me], **common).start)
            waits.append(pltpu.make_async_remote_copy(dst_ref=outs[t].at[pidx], **common).wait)
    return starts, waits


def _exchange_scratch(n):
    return [pltpu.SemaphoreType.DMA((n * N_PEER,)), pltpu.SemaphoreType.DMA((n * N_PEER,)), pltpu.SemaphoreType.DMA((n,))]


def _exchange_shapes(arrays, scatter):
    return [jax.ShapeDtypeStruct(a.shape if scatter else (N_DEV,) + a.shape, a.dtype) for a in arrays]


def _exchange(arrays, scatter, name):
    n = len(arrays)

    def body(*refs):
        starts, waits = _exchange_copies(refs[:n], refs[n:2 * n], refs[2 * n:], scatter)
        for f in starts:
            f()
        for f in waits:
            f()

    hbm = pl.BlockSpec(memory_space=pltpu.HBM)
    return pl.pallas_call(
        body, name=name,
        in_specs=[hbm] * n, out_specs=[hbm] * n,
        out_shape=_exchange_shapes(arrays, scatter),
        scratch_shapes=_exchange_scratch(n),
        compiler_params=pltpu.CompilerParams(has_side_effects=True),
    )(*arrays)


def _call(body, *, name, grid, in_specs, out_specs, out_shape, scratch_shapes, sem, args, carry=None):
    if carry is None:
        outs = pl.pallas_call(body, name=name, grid=grid, in_specs=in_specs, out_specs=out_specs, out_shape=out_shape,
                              scratch_shapes=scratch_shapes, compiler_params=_cparams(*sem))(*args)
        return list(outs), []
    arrays, scatter = carry
    nc, n_in, n_out, n_scr = len(arrays), len(in_specs), len(out_specs), len(scratch_shapes)

    def wrapped(*refs):
        ins, cins = refs[:n_in], refs[n_in:n_in + nc]
        outs, couts = refs[n_in + nc:n_in + nc + n_out], refs[n_in + nc + n_out:n_in + 2 * nc + n_out]
        scr = refs[n_in + 2 * nc + n_out:n_in + 2 * nc + n_out + n_scr]
        ids = [pl.program_id(k) for k in range(len(grid))]
        first = functools.reduce(jnp.logical_and, [i == 0 for i in ids])
        last = functools.reduce(jnp.logical_and, [i == g - 1 for i, g in zip(ids, grid)])
        starts, waits = _exchange_copies(cins, couts, refs[-3:], scatter)

        @pl.when(first)
        def _():
            for f in starts:
                f()

        body(*ins, *outs, *scr)

        @pl.when(last)
        def _():
            for f in waits:
                f()

    hbm = pl.BlockSpec(memory_space=pltpu.HBM)
    res = pl.pallas_call(
        wrapped, name=name, grid=grid,
        in_specs=list(in_specs) + [hbm] * nc, out_specs=list(out_specs) + [hbm] * nc,
        out_shape=list(out_shape) + _exchange_shapes(arrays, scatter),
        scratch_shapes=list(scratch_shapes) + _exchange_scratch(nc),
        compiler_params=_cparams(*(("arbitrary",) * len(grid))),
    )(*args, *arrays)
    return list(res[:n_out]), list(res[n_out:])


def _relu2_fwd(acc):
    r = jnp.maximum(acc, 0.0)
    return acc, r * r


def _relu2_bwd(acc, z):
    return (acc * 2.0 * jnp.maximum(z.astype(F32), 0.0),)


def _add_scaled(scale):
    def epi(acc, t):
        return (acc + scale * t,)
    return epi


def _local_step(x, target, P, plan):
    S, D = x.shape
    H = (D // 2) // HEAD_DIM
    W = H * HEAD_DIM
    alpha = (2 * 2) ** 0.25
    g = {}

    def mm(a, b, *, name, **kw):
        res, carried = _mm(a, b, name=name, carry=plan.carry(name, g), **kw)
        plan.arrived(name, carried, P, g)
        return res

    u0 = mm(x, P["ev_w_in"], mode="nn", name="ev_in")
    a_out, cv = _conv_fwd(u0, P["ev_conv_w"], P["ev_conv_b"], P["ev_conv_ln_g"], P["ev_conv_ln_b"], "conv_fwd")
    b_out, h_raw, h_states = _hgrn_fwd(u0, P["hgrn_lb_logits"], P["ev_gnorm_g"], "hgrn_fwd")
    cat0 = jnp.concatenate([a_out.astype(BF16), b_out.astype(BF16)], axis=1)
    mix0 = mm(cat0, P["ev_w_out"], mode="nn", name="ev_out")
    x1, xh1, rs1 = _ln_fwd(x, mix0, P["ln_mix_g"][0:1], P["ln_mix_b"][0:1], alpha, "ln_mix0")
    z0, hh0 = mm(x1, P["mlp_w1_0"], mode="nn", name="mlp_up0", out_dtypes=(BF16, BF16), epi=_relu2_fwd)
    m0 = mm(hh0, P["mlp_w2_0"], mode="nn", name="mlp_down0")
    x2, xh2, rs2 = _ln_fwd(x1, m0, P["ln_mlp_g"][0:1], P["ln_mlp_b"][0:1], alpha, "ln_mlp0")

    uc = mm(x2, P["od_w_c"], mode="nn", name="od_in_c")
    ud = mm(x2, P["od_w_d"], mode="nn", name="od_in_d")
    f_logit, F = _fgate_fwd(x2, P["od_w_f_t"], P["fox_b_f"].reshape(H, 1), "fgate_fwd")
    F3 = F.reshape(H, 1, S)
    c_lse, carried = _fox_lse(uc, F3, H, "fox_lse", carry=plan.carry("fox_lse", g))
    plan.arrived("fox_lse", carried, P, g)
    c_out = _fox_fwd(uc, F3, c_lse, H, "fox_fwd")
    bias = _relbias_fwd(P["rel_bias"], "relbias_fwd")
    d_out, d_lse = _ca_fwd(ud, bias, H, "ca_fwd")
    cat1 = jnp.concatenate([c_out.astype(BF16), d_out.astype(BF16)], axis=1)
    mix1 = mm(cat1, P["od_w_out"], mode="nn", name="od_out")
    x3, xh3, rs3 = _ln_fwd(x2, mix1, P["ln_mix_g"][1:2], P["ln_mix_b"][1:2], alpha, "ln_mix1")
    z1, hh1 = mm(x3, P["mlp_w1_1"], mode="nn", name="mlp_up1", out_dtypes=(BF16, BF16), epi=_relu2_fwd)
    m1 = mm(hh1, P["mlp_w2_1"], mode="nn", name="mlp_down1")
    x4, xh4, rs4 = _ln_fwd(x3, m1, P["ln_mlp_g"][1:2], P["ln_mlp_b"][1:2], alpha, "ln_mlp1")

    dy, loss = _loss_head(x4, target, "loss_head")

    dzm1, dg_, db_ = _ln_bwd(dy, None, 0.0, xh4, rs4, P["ln_mlp_g"][1:2], "ln_mlp1_bwd")
    g["ln_mlp_g1"], g["ln_mlp_b1"] = dg_, db_
    g["mlp_w2_1"] = mm(hh1, dzm1, mode="tn", name="mlp_down1_dw", out_dtypes=(BF16,))
    dz1 = mm(dzm1, P["mlp_w2_1"], mode="nt", name="mlp_down1_dx", out_dtypes=(BF16,), extras=(z1,), epi=_relu2_bwd)
    g["mlp_w1_1"] = mm(x3, dz1, mode="tn", name="mlp_up1_dw", out_dtypes=(BF16,))
    dx3 = mm(dz1, P["mlp_w1_1"], mode="nt", name="mlp_up1_dx", extras=(dzm1,), epi=_add_scaled(alpha))
    dzx1, dg_, db_ = _ln_bwd(dx3, None, 0.0, xh3, rs3, P["ln_mix_g"][1:2], "ln_mix1_bwd")
    g["ln_mix_g1"], g["ln_mix_b1"] = dg_, db_
    g["od_w_out"] = mm(cat1, dzx1, mode="tn", name="od_out_dw", out_dtypes=(BF16,))
    dcat1 = mm(dzx1, P["od_w_out"], mode="nt", name="od_out_dx")
    dc_out, dd_out = dcat1[:, :W], dcat1[:, W:]
    dq_d, dbias = _ca_bwd_dq(ud, bias, d_out, dd_out, d_lse, H, "ca_bwd_dq")
    dk_d, dv_d = _ca_bwd_dkv(ud, bias, d_out, dd_out, d_lse, H, "ca_bwd_dkv")
    g["rel_bias"] = _relbias_bwd(dbias, "relbias_bwd")
    dud = jnp.concatenate([dq_d, dk_d, dv_d], axis=1)
    dq_c = _fox_bwd_dq(uc, F3, c_out, dc_out, c_lse, H, "fox_bwd_dq")
    dk_c, dv_c, dF3 = _fox_bwd_dkv(uc, F3, c_out, dc_out, c_lse, H, "fox_bwd_dkv")
    duc = jnp.concatenate([dq_c, dk_c, dv_c], axis=1)
    dfl, dbf = _fgate_bwd(dF3.reshape(H, S), f_logit, "fgate_bwd")
    g["fox_b_f"] = dbf.reshape(1, H)
    g["od_w_c"] = mm(x2, duc, mode="tn", name="od_in_c_dw", out_dtypes=(BF16,))
    g["od_w_d"] = mm(x2, dud, mode="tn", name="od_in_d_dw", out_dtypes=(BF16,))
    g["od_w_f_t"] = mm(dfl, x2, mode="nn", name="od_in_f_dw", out_dtypes=(BF16,), exact_products=True)
    dx2 = mm(duc, P["od_w_c"], mode="nt", name="od_in_c_dx", extras=(dzx1,), epi=_add_scaled(alpha))
    dx2 = mm(dud, P["od_w_d"], mode="nt", name="od_in_d_dx", extras=(dx2,), epi=_add_scaled(1.0))
    dx2 = mm(dfl, P["od_w_f_t"], mode="tn", name="od_in_f_dx", extras=(dx2,), epi=_add_scaled(1.0), exact_products=True)

    dzm0, dg_, db_ = _ln_bwd(dx2, None, 0.0, xh2, rs2, P["ln_mlp_g"][0:1], "ln_mlp0_bwd")
    g["ln_mlp_g0"], g["ln_mlp_b0"] = dg_, db_
    g["mlp_w2_0"] = mm(hh0, dzm0, mode="tn", name="mlp_down0_dw", out_dtypes=(BF16,))
    dz0 = mm(dzm0, P["mlp_w2_0"], mode="nt", name="mlp_down0_dx", out_dtypes=(BF16,), extras=(z0,), epi=_relu2_bwd)
    g["mlp_w1_0"] = mm(x1, dz0, mode="tn", name="mlp_up0_dw", out_dtypes=(BF16,))
    dx1 = mm(dz0, P["mlp_w1_0"], mode="nt", name="mlp_up0_dx", extras=(dzm0,), epi=_add_scaled(alpha))
    dzx0, dg_, db_ = _ln_bwd(dx1, None, 0.0, xh1, rs1, P["ln_mix_g"][0:1], "ln_mix0_bwd")
    g["ln_mix_g0"], g["ln_mix_b0"] = dg_, db_
    g["ev_w_out"] = mm(cat0, dzx0, mode="tn", name="ev_out_dw", out_dtypes=(BF16,))
    dcat0 = mm(dzx0, P["ev_w_out"], mode="nt", name="ev_out_dx")
    da_out, db_out = dcat0[:, :W], dcat0[:, W:]
    dcv, g["ev_conv_ln_g"], g["ev_conv_ln_b"], g["ev_conv_b"] = _conv_bwd_norm(da_out, cv, P["ev_conv_ln_g"], P["ev_conv_ln_b"], "conv_bwd_norm")
    du_a, du_g, g["ev_conv_w"] = _conv_bwd_taps(dcv, u0, P["ev_conv_w"], "conv_bwd_taps")
    dhq, dhf, dhi, dhg, g["ev_gnorm_g"], g["hgrn_lb_logits"] = _hgrn_bwd(db_out, u0, h_raw, h_states, P["hgrn_lb_logits"], P["ev_gnorm_g"], "hgrn_bwd")
    du0 = jnp.concatenate([du_a, du_g, dhq, dhf, dhi, dhg], axis=1)
    g["ev_w_in"] = mm(x, du0, mode="tn", name="ev_in_dw", out_dtypes=(BF16,))
    dx0 = mm(du0, P["ev_w_in"], mode="nt", name="ev_in_dx", extras=(dzx0,), epi=_add_scaled(alpha))
    return loss, dx0, g


_NAMES = ['ev_w_in', 'ev_conv_w', 'ev_conv_b', 'ev_conv_ln_g', 'ev_conv_ln_b', 'hgrn_lb_logits', 'ev_gnorm_g', 'ev_w_out',
          'od_w_in', 'fox_b_f', 'rel_bias', 'od_w_out', 'ln_mix_g', 'ln_mix_b', 'mlp_w1', 'mlp_w2', 'ln_mlp_g', 'ln_mlp_b']
_SMALL = ['ev_conv_b', 'ev_conv_ln_g', 'ev_conv_ln_b', 'hgrn_lb_logits', 'ev_gnorm_g', 'fox_b_f', 'ln_mix_g', 'ln_mix_b',
          'ln_mlp_g', 'ln_mlp_b', 'ev_conv_w', 'rel_bias']
_PACK_COLS = 2048


def _cols_to_full(gathered):
    nd, K, n = gathered.shape
    return jnp.transpose(gathered, (1, 0, 2)).reshape(K, nd * n)


def _full_to_cols(full):
    K, N = full.shape
    return jnp.transpose(full.reshape(K, N_DEV, N // N_DEV), (1, 0, 2))


_ROW_SHARDED = ("ev_w_out", "od_w_out", "mlp_w2_0", "mlp_w2_1")


def _full_weights(name, gathered, heads):
    if name in _ROW_SHARDED:
        return {name: gathered.reshape(-1, gathered.shape[-1])}
    full = _cols_to_full(gathered)
    if name != "od_w_in":
        return {name: full}
    w = heads * HEAD_DIM
    return {"od_w_c": full[:, :3 * w], "od_w_f_t": jnp.transpose(full[:, 3 * w:3 * w + heads]), "od_w_d": full[:, 3 * w + heads:]}


def _grad_blocks(name, g):
    if name in _ROW_SHARDED:
        return g[name].reshape(N_DEV, -1, g[name].shape[-1])
    if name == "od_w_in":
        return _full_to_cols(jnp.concatenate([g["od_w_c"], jnp.transpose(g["od_w_f_t"]), g["od_w_d"]], axis=1))
    return _full_to_cols(g[name])


class _Plan:
    GATHER = {"ev_in": ("mlp_w1_0",), "mlp_up0": ("mlp_w2_0",), "mlp_down0": ("od_w_in",), "od_in_c": ("od_w_out",),
              "fox_lse": ("mlp_w1_1", "mlp_w2_1")}
    SCATTER = {"mlp_down1_dx": "mlp_w2_1", "mlp_up1_dx": "mlp_w1_1", "od_out_dx": "od_w_out", "od_in_c_dx": "od_w_in",
               "mlp_down0_dx": "mlp_w2_0", "mlp_up0_dx": "mlp_w1_0", "ev_out_dx": "ev_w_out", "ev_in_dx": "ev_w_in"}

    def __init__(self, shards, heads):
        self.shards, self.heads = shards, heads

    def carry(self, call, g):
        if call in self.GATHER:
            return [self.shards[n] for n in self.GATHER[call]], False
        if call in self.SCATTER:
            return [_grad_blocks(self.SCATTER[call], g)], True
        return None

    def arrived(self, call, carried, P, g):
        if call in self.GATHER:
            for n, gathered in zip(self.GATHER[call], carried):
                P.update(_full_weights(n, gathered, self.heads))
        elif call in self.SCATTER:
            g[self.SCATTER[call]] = carried[0]


def _pack(parts):
    flat = jnp.concatenate([p.reshape(-1).astype(F32) for p in parts])
    rows = -(-flat.shape[0] // (_PACK_COLS * 8)) * 8
    return jnp.pad(flat, (0, rows * _PACK_COLS - flat.shape[0])).reshape(rows, _PACK_COLS)


def _unpack(packed, shapes):
    flat = packed.reshape(-1)
    out, off = [], 0
    for s in shapes:
        n = math.prod(s)
        out.append(flat[off:off + n].reshape(s))
        off += n
    return out


def kernel(x, ev_w_in, ev_conv_w, ev_conv_b, ev_conv_ln_g, ev_conv_ln_b, hgrn_lb_logits, ev_gnorm_g, ev_w_out, od_w_in, fox_b_f, rel_bias, od_w_out, ln_mix_g, ln_mix_b, mlp_w1, mlp_w2, ln_mlp_g, ln_mlp_b, loss_target, m_ev_w_in, m_ev_conv_w, m_ev_conv_b, m_ev_conv_ln_g, m_ev_conv_ln_b, m_hgrn_lb_logits, m_ev_gnorm_g, m_ev_w_out, m_od_w_in, m_fox_b_f, m_rel_bias, m_od_w_out, m_ln_mix_g, m_ln_mix_b, m_mlp_w1, m_mlp_w2, m_ln_mlp_g, m_ln_mlp_b, v_ev_w_in, v_ev_conv_w, v_ev_conv_b, v_ev_conv_ln_g, v_ev_conv_ln_b, v_hgrn_lb_logits, v_ev_gnorm_g, v_ev_w_out, v_od_w_in, v_fox_b_f, v_rel_bias, v_od_w_out, v_ln_mix_g, v_ln_mix_b, v_mlp_w1, v_mlp_w2, v_ln_mlp_g, v_ln_mlp_b):
    args = locals()
    w = {n: args[n] for n in _NAMES}
    m = {n: args["m_" + n] for n in _NAMES}
    v = {n: args["v_" + n] for n in _NAMES}
    me = 4 * lax.axis_index("x") + 2 * lax.axis_index("y") + lax.axis_index("c")
    S, D = x.shape[1], x.shape[2]
    H = (D // 2) // HEAD_DIM
    W = H * HEAD_DIM
    n_layers = mlp_w1.shape[0]
    assert n_layers == 2 and ev_w_in.shape[0] == 1 and od_w_in.shape[0] == 1

    shards = {"ev_w_in": ev_w_in[0].astype(BF16), "ev_w_out": ev_w_out[0].astype(BF16),
              "od_w_in": od_w_in[0].astype(BF16), "od_w_out": od_w_out[0].astype(BF16)}
    for l in range(n_layers):
        shards["mlp_w1_%d" % l] = mlp_w1[l].astype(BF16)
        shards["mlp_w2_%d" % l] = mlp_w2[l].astype(BF16)
    first = ["ev_w_in", "ev_w_out", "ev_conv_w", "rel_bias"]
    G = _exchange([shards["ev_w_in"], shards["ev_w_out"], ev_conv_w[0], rel_bias[0]], False, "gather_first")
    P = {
        "ev_conv_b": ev_conv_b, "ev_conv_ln_g": ev_conv_ln_g, "ev_conv_ln_b": ev_conv_ln_b,
        "hgrn_lb_logits": hgrn_lb_logits, "ev_gnorm_g": ev_gnorm_g, "fox_b_f": fox_b_f,
        "ln_mix_g": ln_mix_g, "ln_mix_b": ln_mix_b, "ln_mlp_g": ln_mlp_g, "ln_mlp_b": ln_mlp_b,
    }
    for name, gathered in zip(first, G):
        P.update(_full_weights(name, gathered, H))

    loss, grad_x, g = _local_step(x[0], loss_target[0], P, _Plan(shards, H))
    recv = [g["ev_w_in"], g["ev_w_out"], g["od_w_in"], g["od_w_out"]]
    recv += [g["mlp_w1_%d" % l] for l in range(n_layers)] + [g["mlp_w2_%d" % l] for l in range(n_layers)]

    small = {
        "ev_conv_b": g["ev_conv_b"], "ev_conv_ln_g": g["ev_conv_ln_g"], "ev_conv_ln_b": g["ev_conv_ln_b"],
        "hgrn_lb_logits": g["hgrn_lb_logits"], "ev_gnorm_g": g["ev_gnorm_g"], "fox_b_f": g["fox_b_f"],
        "ln_mix_g": jnp.concatenate([g["ln_mix_g0"], g["ln_mix_g1"]]), "ln_mix_b": jnp.concatenate([g["ln_mix_b0"], g["ln_mix_b1"]]),
        "ln_mlp_g": jnp.concatenate([g["ln_mlp_g0"], g["ln_mlp_g1"]]), "ln_mlp_b": jnp.concatenate([g["ln_mlp_b0"], g["ln_mlp_b1"]]),
        "ev_conv_w": g["ev_conv_w"], "rel_bias": g["rel_bias"],
    }
    full_shapes = [small[n].shape for n in _SMALL]
    small_all = _exchange([_pack([small[n] for n in _SMALL])], False, "gather_small_grads")[0]
    small_sum = _unpack(_sum_parts(small_all, "sum_small_grads"), full_shapes)
    small_g = dict(zip(_SMALL, small_sum))
    cw = small_g["ev_conv_w"]
    small_g["ev_conv_w"] = lax.dynamic_slice_in_dim(cw, me * (cw.shape[1] // N_DEV), cw.shape[1] // N_DEV, axis=1)
    rb = small_g["rel_bias"]
    small_g["rel_bias"] = lax.dynamic_slice_in_dim(rb, me * (rb.shape[1] // N_DEV), rb.shape[1] // N_DEV, axis=1)

    out_g, out_d, out_m, out_v = {}, {}, {}, {}

    def put(name, res, shape):
        out_g[name], out_d[name], out_m[name], out_v[name] = [r.reshape(shape) for r in res]

    big = [("ev_w_in", recv[0], None), ("ev_w_out", recv[1], None), ("od_w_in", recv[2], None), ("od_w_out", recv[3], None)]
    for name, parts, _ in big:
        shp = w[name].shape
        put(name, _adamw(parts, w[name][0], m[name][0], v[name][0], "adamw_" + name), shp)
    for name, base in (("mlp_w1", 4), ("mlp_w2", 4 + n_layers)):
        res = [_adamw(recv[base + l], w[name][l], m[name][l], v[name][l], "adamw_%s_%d" % (name, l)) for l in range(n_layers)]
        put(name, [jnp.stack([res[l][k] for l in range(n_layers)]) for k in range(4)], w[name].shape)
    shapes = [w[n].shape for n in _SMALL]
    packed = _adamw(_pack([small_g[n] for n in _SMALL])[None], _pack([w[n] for n in _SMALL]), _pack([m[n] for n in _SMALL]),
                    _pack([v[n] for n in _SMALL]), "adamw_small")
    for k, dst in enumerate((out_g, out_d, out_m, out_v)):
        for n, a in zip(_SMALL, _unpack(packed[k], shapes)):
            dst[n] = a

    loss = lax.psum(loss[0, 0], ("x", "y", "c"))
    return (loss, grad_x[None], *[out_g[n] for n in _NAMES], *[out_d[n] for n in _NAMES],
            *[out_m[n] for n in _NAMES], *[out_v[n] for n in _NAMES])
```

```python
import functools
import math

import jax
import jax.numpy as jnp
from jax import lax
from jax.experimental import pallas as pl
from jax.experimental.pallas import tpu as pltpu

F32 = jnp.float32
BF16 = jnp.bfloat16
HI = lax.Precision.HIGHEST
MESH_ID = pl.DeviceIdType.MESH

N_DEV = 8
LN_EPS = 1e-5
CHUNK = 64
HEAD_DIM = 128
CONV_WIDTH = 31
CONV_HALO = 32
CA_LEFT_CHUNKS = 8
CA_TILE = 256
CA_WIN = CA_TILE + CA_LEFT_CHUNKS * CHUNK
CA_SKEW = 1024
REL_CLIP = 256
REL_TABLE = (CHUNK - 1) + REL_CLIP + 1
NEG = -1e30
ADAM_LR = 0.001
ADAM_B1 = 0.9
ADAM_B2 = 0.999
ADAM_EPS = 1e-08
ADAM_WD = 0.01
ADAM_STEP = 10
VMEM_LIMIT_V7X = 56 * 1024 * 1024


def _cparams(*sem):
    return pltpu.CompilerParams(dimension_semantics=sem, vmem_limit_bytes=VMEM_LIMIT_V7X)


def _tile(n, t):
    if n <= t:
        return n
    for c in range(t - t % 128, 0, -128):
        if n % c == 0:
            return c
    return n


def _sigmoid(x):
    return 1.0 / (1.0 + jnp.exp(-x))


def _dot(a, b, dims, precision=None):
    return lax.dot_general(a, b, (dims, ((), ())), preferred_element_type=F32, precision=precision)


def _round_bf16(x):
    return x.astype(BF16).astype(F32)


NN = ((1,), (0,))
NT = ((1,), (1,))
TN = ((0,), (0,))


def _mm(a, b, *, mode, name, out_dtypes=(F32,), extras=(), epi=None, exact_products=False, carry=None, tm=1024, tn=1024, tk=2048):
    dims = {"nn": NN, "nt": NT, "tn": TN}[mode]
    if mode == "tn":
        K, M = a.shape
    else:
        M, K = a.shape
    N = b.shape[0] if mode == "nt" else b.shape[1]
    tm, tn, tk = _tile(M, tm), _tile(N, tn), _tile(K, tk)
    nk = K // tk
    n_ex, n_out = len(extras), len(out_dtypes)

    def body(*refs):
        a_ref, b_ref = refs[0], refs[1]
        ex_refs = refs[2:2 + n_ex]
        o_refs = refs[2 + n_ex:2 + n_ex + n_out]
        if exact_products:
            d = _dot(_round_bf16(a_ref[...]), _round_bf16(b_ref[...]), dims, HI)
        else:
            d = _dot(a_ref[...].astype(BF16), b_ref[...].astype(BF16), dims)

        def finish(acc):
            outs = (acc,) if epi is None else epi(acc, *[r[...] for r in ex_refs])
            for o_ref, o in zip(o_refs, outs):
                o_ref[...] = o.astype(o_ref.dtype)

        if nk == 1:
            finish(d)
        else:
            acc_ref = refs[-1]
            k = pl.program_id(2)

            @pl.when(k == 0)
            def _():
                acc_ref[...] = d

            @pl.when(k > 0)
            def _():
                acc_ref[...] += d

            @pl.when(k == nk - 1)
            def _():
                finish(acc_ref[...])

    a_spec = pl.BlockSpec((tk, tm), lambda i, j, k: (k, i)) if mode == "tn" else pl.BlockSpec((tm, tk), lambda i, j, k: (i, k))
    b_spec = pl.BlockSpec((tn, tk), lambda i, j, k: (j, k)) if mode == "nt" else pl.BlockSpec((tk, tn), lambda i, j, k: (k, j))
    mn_spec = pl.BlockSpec((tm, tn), lambda i, j, k: (i, j))
    outs, carried = _call(
        body, name=name, grid=(M // tm, N // tn, nk),
        in_specs=[a_spec, b_spec] + [mn_spec] * n_ex,
        out_specs=[mn_spec] * n_out,
        out_shape=[jax.ShapeDtypeStruct((M, N), dt) for dt in out_dtypes],
        scratch_shapes=[pltpu.VMEM((tm, tn), F32)] if nk > 1 else [],
        sem=("parallel", "parallel", "arbitrary"), args=(a, b, *extras), carry=carry)
    return (outs[0] if n_out == 1 else outs), carried


def _ln_fwd(x, r, g, b, alpha, name, tr=256):
    S, D = x.shape
    tr = _tile(S, tr)

    def body(x_ref, r_ref, g_ref, b_ref, y_ref, xh_ref, rs_ref):
        z = alpha * x_ref[...] + r_ref[...]
        zc = z - jnp.mean(z, axis=-1, keepdims=True)
        rs = lax.rsqrt(jnp.mean(zc * zc, axis=-1, keepdims=True) + LN_EPS)
        xh = zc * rs
        xh_ref[...] = xh
        rs_ref[...] = rs
        y_ref[...] = xh * g_ref[...] + b_ref[...]

    row = pl.BlockSpec((tr, D), lambda i: (i, 0))
    par = pl.BlockSpec((1, D), lambda i: (0, 0))
    return pl.pallas_call(
        body, name=name, grid=(S // tr,),
        in_specs=[row, row, par, par],
        out_specs=[row, row, pl.BlockSpec((tr, 1), lambda i: (i, 0))],
        out_shape=[jax.ShapeDtypeStruct((S, D), F32), jax.ShapeDtypeStruct((S, D), F32), jax.ShapeDtypeStruct((S, 1), F32)],
        compiler_params=_cparams("parallel"),
    )(x, r, g, b)


def _ln_bwd(dy, dy2, scale2, xh, rs, g, name, tr=256):
    S, D = dy.shape
    tr = _tile(S, tr)
    two = dy2 is not None

    def body(*refs):
        if two:
            dy_ref, dy2_ref, xh_ref, rs_ref, g_ref, dz_ref, dg_ref, db_ref = refs
            dyt = dy_ref[...] + scale2 * dy2_ref[...]
        else:
            dy_ref, xh_ref, rs_ref, g_ref, dz_ref, dg_ref, db_ref = refs
            dyt = dy_ref[...]
        xh = xh_ref[...]
        dxh = dyt * g_ref[...]
        m1 = jnp.mean(dxh, axis=-1, keepdims=True)
        m2 = jnp.mean(dxh * xh, axis=-1, keepdims=True)
        dz_ref[...] = rs_ref[...] * (dxh - m1 - xh * m2)

        @pl.when(pl.program_id(0) == 0)
        def _():
            dg_ref[...] = jnp.zeros_like(dg_ref)
            db_ref[...] = jnp.zeros_like(db_ref)

        dg_ref[...] += jnp.sum(dyt * xh, axis=0, keepdims=True)
        db_ref[...] += jnp.sum(dyt, axis=0, keepdims=True)

    row = pl.BlockSpec((tr, D), lambda i: (i, 0))
    par = pl.BlockSpec((1, D), lambda i: (0, 0))
    ins = [dy] + ([dy2] if two else []) + [xh, rs, g]
    return pl.pallas_call(
        body, name=name, grid=(S // tr,),
        in_specs=[row] * (2 if two else 1) + [row, pl.BlockSpec((tr, 1), lambda i: (i, 0)), par],
        out_specs=[row, par, par],
        out_shape=[jax.ShapeDtypeStruct((S, D), F32), jax.ShapeDtypeStruct((1, D), F32), jax.ShapeDtypeStruct((1, D), F32)],
        compiler_params=_cparams("arbitrary"),
    )(*ins)


def _loss_head(y, target, name, tr=256):
    S, D = y.shape
    tr = _tile(S, tr)

    def body(y_ref, t_ref, dy_ref, loss_ref):
        e = y_ref[...] - t_ref[...]
        dy_ref[...] = e * (1.0 / D)

        @pl.when(pl.program_id(0) == 0)
        def _():
            loss_ref[...] = jnp.zeros_like(loss_ref)

        loss_ref[...] += jnp.sum(jnp.sum(e * e, axis=-1, keepdims=True), axis=0, keepdims=True) * (0.5 / D)

    row = pl.BlockSpec((tr, D), lambda i: (i, 0))
    return pl.pallas_call(
        body, name=name, grid=(S // tr,),
        in_specs=[row, row],
        out_specs=[row, pl.BlockSpec((1, 1), lambda i: (0, 0))],
        out_shape=[jax.ShapeDtypeStruct((S, D), F32), jax.ShapeDtypeStruct((1, 1), F32)],
        compiler_params=_cparams("arbitrary"),
    )(y, target)


def _conv_fwd(u, w, cb, lg, lb, name, tt=512):
    S = u.shape[0]
    C = w.shape[1]
    tt = _tile(S, tt)
    hpt = tt // CONV_HALO

    def body(a_ref, g_ref, ap_ref, gp_ref, w_ref, cb_ref, lg_ref, lb_ref, out_ref, cv_ref, hext):
        i = pl.program_id(0)
        hext[pl.ds(CONV_HALO, tt), :] = a_ref[...] * _sigmoid(g_ref[...])
        hp = ap_ref[...] * _sigmoid(gp_ref[...])
        hext[pl.ds(0, CONV_HALO), :] = jnp.where(i > 0, hp, 0.0)
        acc = jnp.zeros((tt, C), F32)
        for k in range(CONV_WIDTH):
            acc = acc + w_ref[pl.ds(k, 1), :] * hext[pl.ds(CONV_HALO - (CONV_WIDTH - 1) + k, tt), :]
        cv = acc + cb_ref[...]
        cv_ref[...] = cv
        zc = cv - jnp.mean(cv, axis=-1, keepdims=True)
        n = zc * lax.rsqrt(jnp.mean(zc * zc, axis=-1, keepdims=True) + LN_EPS) * lg_ref[...] + lb_ref[...]
        out_ref[...] = n * _sigmoid(n)

    cur = lambda cb_: pl.BlockSpec((tt, C), lambda i: (i, cb_))
    prev = lambda cb_: pl.BlockSpec((CONV_HALO, C), lambda i: (jnp.maximum(i * hpt - 1, 0), cb_))
    par = pl.BlockSpec((1, C), lambda i: (0, 0))
    row = pl.BlockSpec((tt, C), lambda i: (i, 0))
    return pl.pallas_call(
        body, name=name, grid=(S // tt,),
        in_specs=[cur(0), cur(1), prev(0), prev(1), pl.BlockSpec((CONV_WIDTH, C), lambda i: (0, 0)), par, par, par],
        out_specs=[row, row],
        out_shape=[jax.ShapeDtypeStruct((S, C), F32)] * 2,
        scratch_shapes=[pltpu.VMEM((tt + CONV_HALO, C), F32)],
        compiler_params=_cparams("parallel"),
    )(u, u, u, u, w, cb, lg, lb)


def _conv_bwd_norm(da, da_col, cv, lg, lb, name, tt=512):
    S, C = cv.shape
    tt = _tile(S, tt)

    def body(da_ref, cv_ref, lg_ref, lb_ref, dcv_ref, dlg_ref, dlb_ref, dcb_ref):
        cv = cv_ref[...]
        zc = cv - jnp.mean(cv, axis=-1, keepdims=True)
        rs = lax.rsqrt(jnp.mean(zc * zc, axis=-1, keepdims=True) + LN_EPS)
        xh = zc * rs
        n = xh * lg_ref[...] + lb_ref[...]
        sg = _sigmoid(n)
        dn = da_ref[...] * sg * (1.0 + n * (1.0 - sg))
        dxh = dn * lg_ref[...]
        m1 = jnp.mean(dxh, axis=-1, keepdims=True)
        m2 = jnp.mean(dxh * xh, axis=-1, keepdims=True)
        dcv = rs * (dxh - m1 - xh * m2)
        dcv_ref[...] = dcv

        @pl.when(pl.program_id(0) == 0)
        def _():
            dlg_ref[...] = jnp.zeros_like(dlg_ref)
            dlb_ref[...] = jnp.zeros_like(dlb_ref)
            dcb_ref[...] = jnp.zeros_like(dcb_ref)

        dlg_ref[...] += jnp.sum(dn * xh, axis=0, keepdims=True)
        dlb_ref[...] += jnp.sum(dn, axis=0, keepdims=True)
        dcb_ref[...] += jnp.sum(dcv, axis=0, keepdims=True)

    row = pl.BlockSpec((tt, C), lambda i: (i, 0))
    par = pl.BlockSpec((1, C), lambda i: (0, 0))
    return pl.pallas_call(
        body, name=name, grid=(S // tt,),
        in_specs=[pl.BlockSpec((tt, C), lambda i: (i, da_col)), row, par, par],
        out_specs=[row, par, par, par],
        out_shape=[jax.ShapeDtypeStruct((S, C), F32)] + [jax.ShapeDtypeStruct((1, C), F32)] * 3,
        compiler_params=_cparams("arbitrary"),
    )(da, cv, lg, lb)


def _conv_bwd_taps(dcv, u, w, name, tt=512):
    S, C = dcv.shape
    tt = _tile(S, tt)
    hpt = tt // CONV_HALO
    nt = S // tt
    WPAD = 32

    def body(dc_ref, dn_ref, a_ref, g_ref, ap_ref, gp_ref, w_ref, da_ref, dg_ref, dw_ref, hext, dext):
        i = pl.program_id(0)
        a = a_ref[...]
        sg = _sigmoid(g_ref[...])
        hext[pl.ds(CONV_HALO, tt), :] = a * sg
        hp = ap_ref[...] * _sigmoid(gp_ref[...])
        hext[pl.ds(0, CONV_HALO), :] = jnp.where(i > 0, hp, 0.0)
        dc = dc_ref[...]
        dext[pl.ds(0, tt), :] = dc
        dext[pl.ds(tt, CONV_HALO), :] = jnp.where(i < nt - 1, dn_ref[...], 0.0)

        @pl.when(i == 0)
        def _():
            dw_ref[...] = jnp.zeros_like(dw_ref)

        dh = jnp.zeros((tt, C), F32)
        for k in range(CONV_WIDTH):
            dh = dh + w_ref[pl.ds(k, 1), :] * dext[pl.ds(CONV_WIDTH - 1 - k, tt), :]
            hk = hext[pl.ds(CONV_HALO - (CONV_WIDTH - 1) + k, tt), :]
            dw_ref[pl.ds(k, 1), :] += jnp.sum(dc * hk, axis=0, keepdims=True)
        da_ref[...] = dh * sg
        dg_ref[...] = dh * a * sg * (1.0 - sg)

    row = pl.BlockSpec((tt, C), lambda i: (i, 0))
    nxt = pl.BlockSpec((CONV_HALO, C), lambda i: (jnp.minimum((i + 1) * hpt, S // CONV_HALO - 1), 0))
    cur = lambda cb_: pl.BlockSpec((tt, C), lambda i: (i, cb_))
    prev = lambda cb_: pl.BlockSpec((CONV_HALO, C), lambda i: (jnp.maximum(i * hpt - 1, 0), cb_))
    da, dg, dw = pl.pallas_call(
        body, name=name, grid=(nt,),
        in_specs=[row, nxt, cur(0), cur(1), prev(0), prev(1), pl.BlockSpec((CONV_WIDTH, C), lambda i: (0, 0))],
        out_specs=[row, row, pl.BlockSpec((WPAD, C), lambda i: (0, 0))],
        out_shape=[jax.ShapeDtypeStruct((S, C), F32)] * 2 + [jax.ShapeDtypeStruct((WPAD, C), F32)],
        scratch_shapes=[pltpu.VMEM((tt + CONV_HALO, C), F32)] * 2,
        compiler_params=_cparams("arbitrary"),
    )(dcv, dcv, u, u, u, u, w)
    return da, dg, dw[:CONV_WIDTH]


def _lower_bound(logits):
    e = jnp.exp(logits - jnp.max(logits, axis=0, keepdims=True))
    p = e / jnp.sum(e, axis=0, keepdims=True)
    return p[0:1, :], p


def _tri(n, lower):
    r = lax.broadcasted_iota(jnp.int32, (n, n), 0)
    c = lax.broadcasted_iota(jnp.int32, (n, n), 1)
    return ((c <= r) if lower else (c >= r)).astype(F32)


def _hgrn_gates(xq, xf, lb):
    sq = _sigmoid(xq)
    q = xq * sq
    sf = _sigmoid(xf)
    f = lb + (1.0 - lb) * sf
    logf = jnp.log(f)
    L = _dot(_tri(CHUNK, True), logf, NN, HI)
    Lend = L[CHUNK - 1:CHUNK, :]
    eL = jnp.exp(L)
    enL = jnp.exp(-L)
    eLe = jnp.exp(Lend - L)
    kk = 1.0 - f
    return dict(sq=sq, q=q, sf=sf, f=f, L=L, Lend=Lend, eL=eL, enL=enL, eLe=eLe, kk=kk,
                qe=q * eL, ke=kk * enL, kd=kk * eLe)


def _hgrn_fwd(u, lbl, gn, name, tb=256):
    S = u.shape[0]
    W = gn.shape[1]
    H = W // HEAD_DIM
    tb = _tile(S, tb)
    cpb = tb // CHUNK
    nc = S // CHUNK
    R = lbl.shape[0]

    def body(q_ref, f_ref, i_ref, g_ref, lbl_ref, gn_ref, out_ref, raw_ref, st_ref, state):
        @pl.when(pl.program_id(0) == 0)
        def _():
            state[...] = jnp.zeros_like(state)

        lb, _ = _lower_bound(lbl_ref[...])
        tril = _tri(CHUNK, True) > 0.5

        def chunk(c, carry):
            rows = pl.ds(pl.multiple_of(c * CHUNK, CHUNK), CHUNK)
            G = _hgrn_gates(q_ref[rows, :], f_ref[rows, :], lb)
            v = i_ref[rows, :]
            xg = g_ref[rows, :]
            outs = []
            for h in range(H):
                ln = slice(h * HEAD_DIM, (h + 1) * HEAD_DIM)
                qe, ke, kd, vh = G["qe"][:, ln].astype(BF16), G["ke"][:, ln].astype(BF16), G["kd"][:, ln].astype(BF16), v[:, ln].astype(BF16)
                st = state[h]
                st_ref[c, h] = st
                A = jnp.where(tril, _dot(qe, ke, NT), 0.0)
                o = _dot(A.astype(BF16), vh, NN) + _dot(qe, st.astype(BF16), NT)
                state[h] = jnp.exp(G["Lend"][:, ln]) * st + _dot(vh, kd, TN)
                outs.append(o)
            o = jnp.concatenate(outs, axis=1)
            raw_ref[rows, :] = o
            ns = []
            for h in range(H):
                oh = outs[h]
                ns.append(oh * lax.rsqrt(jnp.mean(oh * oh, axis=-1, keepdims=True) + LN_EPS))
            n = jnp.concatenate(ns, axis=1)
            out_ref[rows, :] = n * gn_ref[...] * (xg * _sigmoid(xg))
            return carry

        lax.fori_loop(0, cpb, chunk, 0)

    col = lambda cb_: pl.BlockSpec((tb, W), lambda i: (i, cb_))
    row = pl.BlockSpec((tb, W), lambda i: (i, 0))
    return pl.pallas_call(
        body, name=name, grid=(S // tb,),
        in_specs=[col(2), col(3), col(4), col(5), pl.BlockSpec((R, W), lambda i: (0, 0)), pl.BlockSpec((1, W), lambda i: (0, 0))],
        out_specs=[row, row, pl.BlockSpec((cpb, H, HEAD_DIM, HEAD_DIM), lambda i: (i, 0, 0, 0))],
        out_shape=[jax.ShapeDtypeStruct((S, W), F32), jax.ShapeDtypeStruct((S, W), F32),
                   jax.ShapeDtypeStruct((nc, H, HEAD_DIM, HEAD_DIM), F32)],
        scratch_shapes=[pltpu.VMEM((H, HEAD_DIM, HEAD_DIM), F32)],
        compiler_params=_cparams("arbitrary"),
    )(u, u, u, u, lbl, gn)


def _hgrn_bwd(dout, dout_col, u, raw, states, lbl, gn, name, tb=256):
    S = u.shape[0]
    W = gn.shape[1]
    H = W // HEAD_DIM
    tb = _tile(S, tb)
    cpb = tb // CHUNK
    nb = S // tb
    R = lbl.shape[0]

    def body(do_ref, q_ref, f_ref, i_ref, g_ref, raw_ref, st_ref, lbl_ref, gn_ref,
             dq_ref, df_ref, di_ref, dg_ref, dgn_ref, dlbl_ref, dstate, dlb_acc):
        @pl.when(pl.program_id(0) == 0)
        def _():
            dstate[...] = jnp.zeros_like(dstate)
            dlb_acc[...] = jnp.zeros_like(dlb_acc)
            dgn_ref[...] = jnp.zeros_like(dgn_ref)

        lb, p = _lower_bound(lbl_ref[...])
        tril = _tri(CHUNK, True) > 0.5
        triu = _tri(CHUNK, False)
        gn_row = gn_ref[...]

        def chunk(cc, carry):
            c = cpb - 1 - cc
            rows = pl.ds(pl.multiple_of(c * CHUNK, CHUNK), CHUNK)
            xq, xf = q_ref[rows, :], f_ref[rows, :]
            G = _hgrn_gates(xq, xf, lb)
            v = i_ref[rows, :]
            xg = g_ref[rows, :]
            dy = do_ref[rows, :]
            o = raw_ref[rows, :]
            sgg = _sigmoid(xg)
            silu_g = xg * sgg
            do_parts, n_parts = [], []
            for h in range(H):
                ln = slice(h * HEAD_DIM, (h + 1) * HEAD_DIM)
                oh = o[:, ln]
                r = lax.rsqrt(jnp.mean(oh * oh, axis=-1, keepdims=True) + LN_EPS)
                nh = oh * r
                dn = dy[:, ln] * gn_row[:, ln] * silu_g[:, ln]
                do_parts.append(r * (dn - nh * jnp.mean(dn * nh, axis=-1, keepdims=True)))
                n_parts.append(nh)
            n = jnp.concatenate(n_parts, axis=1)
            dgn_ref[...] += jnp.sum(dy * n * silu_g, axis=0, keepdims=True)
            dg_ref[rows, :] = dy * n * gn_row * sgg * (1.0 + xg * (1.0 - sgg))
            dqe_p, dke_p, dkd_p, dv_p, dle_p = [], [], [], [], []
            for h in range(H):
                ln = slice(h * HEAD_DIM, (h + 1) * HEAD_DIM)
                qe, ke, kd, vh = G["qe"][:, ln].astype(BF16), G["ke"][:, ln].astype(BF16), G["kd"][:, ln].astype(BF16), v[:, ln].astype(BF16)
                doh = do_parts[h].astype(BF16)
                st = st_ref[c, h]
                dst = dstate[h]
                dec = jnp.exp(G["Lend"][:, ln])
                A = jnp.where(tril, _dot(qe, ke, NT), 0.0).astype(BF16)
                dA = jnp.where(tril, _dot(doh, vh, NT), 0.0).astype(BF16)
                dv_p.append(_dot(A, doh, TN) + _dot(kd, dst.astype(BF16), NT))
                dqe_p.append(_dot(dA, ke, NN) + _dot(doh, st.astype(BF16), NN))
                dke_p.append(_dot(dA, qe, TN))
                dkd_p.append(_dot(vh, dst.astype(BF16), NN))
                dle_p.append(jnp.sum(dst * dec * st, axis=0, keepdims=True))
                dstate[h] = dec * dst + _dot(doh, qe, TN)
            dqe = jnp.concatenate(dqe_p, axis=1)
            dke = jnp.concatenate(dke_p, axis=1)
            dkd = jnp.concatenate(dkd_p, axis=1)
            qe_r, ke_r, kd_r = _round_bf16(G["qe"]), _round_bf16(G["ke"]), _round_bf16(G["kd"])
            dLend = jnp.concatenate(dle_p, axis=1) + jnp.sum(dkd * kd_r, axis=0, keepdims=True)
            dL = dqe * qe_r - dke * ke_r - dkd * kd_r
            dlogf = _dot(triu, dL, NN, HI) + dLend
            dkk = dke * G["enL"] + dkd * G["eLe"]
            dfv = dlogf / G["f"] - dkk
            sf = G["sf"]
            df_ref[rows, :] = dfv * (1.0 - lb) * sf * (1.0 - sf)
            dlb_acc[...] += jnp.sum(dfv * (1.0 - sf), axis=0, keepdims=True)
            sq = G["sq"]
            dq_ref[rows, :] = dqe * G["eL"] * sq * (1.0 + xq * (1.0 - sq))
            di_ref[rows, :] = jnp.concatenate(dv_p, axis=1)
            return carry

        lax.fori_loop(0, cpb, chunk, 0)

        onehot0 = (lax.broadcasted_iota(jnp.int32, (R, W), 0) == 0).astype(F32)
        dlbl_ref[...] = p * (onehot0 - p[0:1, :]) * dlb_acc[...]

    rev = lambda i: nb - 1 - i
    col = lambda cb_: pl.BlockSpec((tb, W), lambda i: (rev(i), cb_))
    row = pl.BlockSpec((tb, W), lambda i: (rev(i), 0))
    par = pl.BlockSpec((1, W), lambda i: (0, 0))
    parR = pl.BlockSpec((R, W), lambda i: (0, 0))
    return pl.pallas_call(
        body, name=name, grid=(nb,),
        in_specs=[pl.BlockSpec((tb, W), lambda i: (rev(i), dout_col)), col(2), col(3), col(4), col(5), row,
                  pl.BlockSpec((cpb, H, HEAD_DIM, HEAD_DIM), lambda i: (rev(i), 0, 0, 0)), parR, par],
        out_specs=[row, row, row, row, par, parR],
        out_shape=[jax.ShapeDtypeStruct((S, W), F32)] * 4 + [jax.ShapeDtypeStruct((1, W), F32), jax.ShapeDtypeStruct((R, W), F32)],
        scratch_shapes=[pltpu.VMEM((H, HEAD_DIM, HEAD_DIM), F32), pltpu.VMEM((1, W), F32)],
        compiler_params=_cparams("arbitrary"),
    )(dout, u, u, u, u, raw, states, lbl, gn)


def _fgate_fwd(xb, wft, bf, name, ts=512):
    S, D = xb.shape
    H = wft.shape[0]
    ts = _tile(S, ts)

    def body(x_ref, w_ref, b_ref, lg_ref, F_ref, carry):
        @pl.when(pl.program_id(0) == 0)
        def _():
            carry[...] = jnp.zeros_like(carry)

        lg = _dot(_round_bf16(w_ref[...]), _round_bf16(x_ref[...]), NT, HI) + b_ref[...]
        lg_ref[...] = lg
        ls = jnp.minimum(lg, 0.0) - jnp.log(1.0 + jnp.exp(-jnp.abs(lg)))
        F = _dot(ls, _tri(ts, False), NN, HI) + carry[...]
        F_ref[...] = F
        carry[...] = F[:, ts - 1:ts]

    return pl.pallas_call(
        body, name=name, grid=(S // ts,),
        in_specs=[pl.BlockSpec((ts, D), lambda i: (i, 0)), pl.BlockSpec((H, D), lambda i: (0, 0)), pl.BlockSpec((H, 1), lambda i: (0, 0))],
        out_specs=[pl.BlockSpec((H, ts), lambda i: (0, i))] * 2,
        out_shape=[jax.ShapeDtypeStruct((H, S), F32)] * 2,
        scratch_shapes=[pltpu.VMEM((H, 1), F32)],
        compiler_params=_cparams("arbitrary"),
    )(xb, wft, bf)


def _fgate_bwd(dF, lg, name, ts=512):
    H, S = dF.shape
    ts = _tile(S, ts)
    nb = S // ts

    def body(dF_ref, lg_ref, dl_ref, db_ref, carry):
        @pl.when(pl.program_id(0) == 0)
        def _():
            carry[...] = jnp.zeros_like(carry)
            db_ref[...] = jnp.zeros_like(db_ref)

        dls = _dot(dF_ref[...], _tri(ts, True), NN, HI) + carry[...]
        carry[...] = dls[:, 0:1]
        dl = dls * _sigmoid(-lg_ref[...])
        dl_ref[...] = dl
        db_ref[...] += jnp.sum(dl, axis=1, keepdims=True)

    blk = pl.BlockSpec((H, ts), lambda i: (0, nb - 1 - i))
    return pl.pallas_call(
        body, name=name, grid=(nb,),
        in_specs=[blk, blk],
        out_specs=[blk, pl.BlockSpec((H, 1), lambda i: (0, 0))],
        out_shape=[jax.ShapeDtypeStruct((H, S), F32), jax.ShapeDtypeStruct((H, 1), F32)],
        scratch_shapes=[pltpu.VMEM((H, 1), F32)],
        compiler_params=_cparams("arbitrary"),
    )(dF, lg)


def _causal_keep(i, j, tq, tk):
    rows = lax.broadcasted_iota(jnp.int32, (tq, tk), 0)
    cols = lax.broadcasted_iota(jnp.int32, (tq, tk), 1)
    return jnp.logical_or(j < i, cols <= rows)


def _causal_grid(S, t):
    n = S // t
    assert n % 2 == 0, (S, t)

    def by_query(r, c):
        second = c > r
        return jnp.where(second, n - 1 - r, r), jnp.where(second, c - r - 1, c)

    def by_key(r, c):
        second = c >= n - r
        return jnp.where(second, c - 1, r + c), jnp.where(second, n - 1 - r, r)

    return n, (n // 2, n + 1), by_query, by_key


def _fox_lse(u, F3, H, name, carry=None, t=512):
    S = u.shape[0]
    t = _tile(S, t)
    n, tri, ij, _ = _causal_grid(S, t)
    scale = HEAD_DIM ** -0.5

    def body(q_ref, k_ref, fk_ref, lse_ref, m_s, l_s):
        i, j = ij(pl.program_id(1), pl.program_id(2))

        @pl.when(j == 0)
        def _():
            m_s[...] = jnp.full_like(m_s, NEG)
            l_s[...] = jnp.zeros_like(l_s)

        s = _dot(q_ref[...].astype(BF16), k_ref[...].astype(BF16), NT) * scale - fk_ref[...]
        s = jnp.where(_causal_keep(i, j, t, t), s, NEG)
        m_new = jnp.maximum(m_s[...], jnp.max(s, axis=-1, keepdims=True))
        l_s[...] = jnp.exp(m_s[...] - m_new) * l_s[...] + jnp.sum(jnp.exp(s - m_new), axis=-1, keepdims=True)
        m_s[...] = m_new

        @pl.when(j == i)
        def _():
            lse_ref[...] = m_s[...] + jnp.log(l_s[...])

    outs, carried = _call(
        body, name=name, grid=(H,) + tri,
        in_specs=[pl.BlockSpec((t, HEAD_DIM), lambda h, r, c: (ij(r, c)[0], h)),
                  pl.BlockSpec((t, HEAD_DIM), lambda h, r, c: (ij(r, c)[1], H + h)),
                  pl.BlockSpec((None, 1, t), lambda h, r, c: (h, 0, ij(r, c)[1]))],
        out_specs=[pl.BlockSpec((None, t, 1), lambda h, r, c: (h, ij(r, c)[0], 0))],
        out_shape=[jax.ShapeDtypeStruct((H, S, 1), F32)],
        scratch_shapes=[pltpu.VMEM((t, 1), F32), pltpu.VMEM((t, 1), F32)],
        sem=("parallel", "parallel", "arbitrary"), args=(u, u, F3), carry=carry)
    return outs[0], carried


def _fox_probs(q_ref, k_ref, fk_ref, lse_ref, i, j, t, scale):
    s = _dot(q_ref[...].astype(BF16), k_ref[...].astype(BF16), NT) * scale - fk_ref[...]
    p = jnp.where(_causal_keep(i, j, t, t), jnp.exp(s - lse_ref[...]), 0.0)
    return p, p.astype(BF16)


def _softmax_bwd(p, pb, dp, delta):
    return pb.astype(F32) * dp - p * delta


def _fox_fwd(u, F3, lse, H, name, t=512):
    S = u.shape[0]
    W = H * HEAD_DIM
    t = _tile(S, t)
    n, tri, ij, _ = _causal_grid(S, t)
    scale = HEAD_DIM ** -0.5

    def body(q_ref, k_ref, v_ref, fk_ref, lse_ref, o_ref, acc_s):
        i, j = ij(pl.program_id(1), pl.program_id(2))

        @pl.when(j == 0)
        def _():
            acc_s[...] = jnp.zeros_like(acc_s)

        _, pb = _fox_probs(q_ref, k_ref, fk_ref, lse_ref, i, j, t, scale)
        acc_s[...] += _dot(pb, v_ref[...].astype(BF16), NN)

        @pl.when(j == i)
        def _():
            o_ref[...] = acc_s[...]

    qblk = pl.BlockSpec((t, HEAD_DIM), lambda h, r, c: (ij(r, c)[0], h))
    kv = lambda off: pl.BlockSpec((t, HEAD_DIM), lambda h, r, c: (ij(r, c)[1], off * H + h))
    return pl.pallas_call(
        body, name=name, grid=(H,) + tri,
        in_specs=[qblk, kv(1), kv(2),
                  pl.BlockSpec((None, 1, t), lambda h, r, c: (h, 0, ij(r, c)[1])),
                  pl.BlockSpec((None, t, 1), lambda h, r, c: (h, ij(r, c)[0], 0))],
        out_specs=qblk,
        out_shape=jax.ShapeDtypeStruct((S, W), F32),
        scratch_shapes=[pltpu.VMEM((t, HEAD_DIM), F32)],
        compiler_params=_cparams("parallel", "parallel", "arbitrary"),
    )(u, u, u, F3, lse)


def _fox_bwd_dq(u, F3, o, do, do_off, lse, H, name, t=512):
    S = u.shape[0]
    W = H * HEAD_DIM
    t = _tile(S, t)
    n, tri, ij, _ = _causal_grid(S, t)
    scale = HEAD_DIM ** -0.5

    def body(q_ref, k_ref, v_ref, fk_ref, o_ref, do_ref, lse_ref, dq_ref, acc_s, dl_s):
        i, j = ij(pl.program_id(1), pl.program_id(2))

        @pl.when(j == 0)
        def _():
            acc_s[...] = jnp.zeros_like(acc_s)
            dl_s[...] = jnp.sum(_round_bf16(do_ref[...]) * o_ref[...], axis=-1, keepdims=True)

        p, pb = _fox_probs(q_ref, k_ref, fk_ref, lse_ref, i, j, t, scale)
        dp = _dot(do_ref[...].astype(BF16), v_ref[...].astype(BF16), NT)
        ds = _softmax_bwd(p, pb, dp, dl_s[...])
        acc_s[...] += _dot(ds.astype(BF16), k_ref[...].astype(BF16), NN)

        @pl.when(j == i)
        def _():
            dq_ref[...] = acc_s[...] * scale

    qblk = pl.BlockSpec((t, HEAD_DIM), lambda h, r, c: (ij(r, c)[0], h))
    kv = lambda off: pl.BlockSpec((t, HEAD_DIM), lambda h, r, c: (ij(r, c)[1], off * H + h))
    return pl.pallas_call(
        body, name=name, grid=(H,) + tri,
        in_specs=[qblk, kv(1), kv(2), pl.BlockSpec((None, 1, t), lambda h, r, c: (h, 0, ij(r, c)[1])),
                  qblk, pl.BlockSpec((t, HEAD_DIM), lambda h, r, c: (ij(r, c)[0], do_off + h)),
                  pl.BlockSpec((None, t, 1), lambda h, r, c: (h, ij(r, c)[0], 0))],
        out_specs=qblk,
        out_shape=jax.ShapeDtypeStruct((S, W), F32),
        scratch_shapes=[pltpu.VMEM((t, HEAD_DIM), F32), pltpu.VMEM((t, 1), F32)],
        compiler_params=_cparams("parallel", "parallel", "arbitrary"),
    )(u, u, u, F3, o, do, lse)


def _fox_bwd_dkv(u, F3, o, do, do_off, lse, H, name, t=512):
    S = u.shape[0]
    W = H * HEAD_DIM
    t = _tile(S, t)
    n, tri, _, ij = _causal_grid(S, t)
    scale = HEAD_DIM ** -0.5

    def body(q_ref, k_ref, v_ref, fk_ref, o_ref, do_ref, lse_ref, dk_ref, dv_ref, dF_ref, dk_s, dv_s, dF_s):
        i, j = ij(pl.program_id(1), pl.program_id(2))

        @pl.when(i == j)
        def _():
            dk_s[...] = jnp.zeros_like(dk_s)
            dv_s[...] = jnp.zeros_like(dv_s)
            dF_s[...] = jnp.zeros_like(dF_s)

        q = q_ref[...].astype(BF16)
        dob = do_ref[...].astype(BF16)
        p, pb = _fox_probs(q_ref, k_ref, fk_ref, lse_ref, i, j, t, scale)
        dp = _dot(dob, v_ref[...].astype(BF16), NT)
        delta = jnp.sum(dob.astype(F32) * o_ref[...], axis=-1, keepdims=True)
        ds = _softmax_bwd(p, pb, dp, delta)
        dv_s[...] += _dot(pb, dob, TN)
        dk_s[...] += _dot(ds.astype(BF16), q, TN)
        dF_s[...] -= jnp.sum(ds, axis=0, keepdims=True)

        @pl.when(i == n - 1)
        def _():
            dk_ref[...] = dk_s[...] * scale
            dv_ref[...] = dv_s[...]
            dF_ref[...] = dF_s[...]

    qblk = pl.BlockSpec((t, HEAD_DIM), lambda h, r, c: (ij(r, c)[0], h))
    kv = lambda off: pl.BlockSpec((t, HEAD_DIM), lambda h, r, c: (ij(r, c)[1], off * H + h))
    oblk = pl.BlockSpec((t, HEAD_DIM), lambda h, r, c: (ij(r, c)[1], h))
    fblk = pl.BlockSpec((None, 1, t), lambda h, r, c: (h, 0, ij(r, c)[1]))
    return pl.pallas_call(
        body, name=name, grid=(H,) + tri,
        in_specs=[qblk, kv(1), kv(2), fblk, qblk, pl.BlockSpec((t, HEAD_DIM), lambda h, r, c: (ij(r, c)[0], do_off + h)),
                  pl.BlockSpec((None, t, 1), lambda h, r, c: (h, ij(r, c)[0], 0))],
        out_specs=[oblk, oblk, fblk],
        out_shape=[jax.ShapeDtypeStruct((S, W), F32), jax.ShapeDtypeStruct((S, W), F32), jax.ShapeDtypeStruct((H, 1, S), F32)],
        scratch_shapes=[pltpu.VMEM((t, HEAD_DIM), F32), pltpu.VMEM((t, HEAD_DIM), F32), pltpu.VMEM((1, t), F32)],
        compiler_params=_cparams("parallel", "parallel", "arbitrary"),
    )(u, u, u, F3, o, do, lse)


def _rel_index_matrix():
    a = lax.broadcasted_iota(jnp.int32, (REL_TABLE, CA_SKEW), 0)
    j = lax.broadcasted_iota(jnp.int32, (REL_TABLE, CA_SKEW), 1)
    rel = jnp.where(j < CA_WIN, CA_LEFT_CHUNKS * CHUNK - j, REL_CLIP)
    idx = jnp.clip(jnp.minimum(rel, REL_CLIP) + (CHUNK - 1), 0, REL_TABLE - 1)
    return (a == idx).astype(F32)


def _skew(x, sign):
    r = lax.broadcasted_iota(jnp.int32, x.shape, 0)
    for b in range(int(math.log2(CA_TILE))):
        sh = (1 << b) if sign > 0 else CA_SKEW - (1 << b)
        x = jnp.where((r >> b) & 1 == 1, pltpu.roll(x, sh, 1), x)
    return x


def _band_valid():
    shift = int(math.log2(CHUNK))
    r = lax.broadcasted_iota(jnp.int32, (CA_TILE, CA_WIN), 0) >> shift
    m = lax.broadcasted_iota(jnp.int32, (CA_TILE, CA_WIN), 1) >> shift
    return jnp.logical_and(m >= r, m <= r + CA_LEFT_CHUNKS)


def _relbias_fwd(table, name):
    H = table.shape[0]

    def body(t_ref, b_ref):
        rowv = _dot(t_ref[...], _rel_index_matrix(), NN, HI)
        valid = _band_valid()
        for h in range(H):
            x = _skew(jnp.broadcast_to(rowv[h:h + 1, :], (CA_TILE, CA_SKEW)), +1)
            b_ref[h] = jnp.where(valid, x[:, :CA_WIN], NEG)

    return pl.pallas_call(
        body, name=name,
        out_shape=jax.ShapeDtypeStruct((H, CA_TILE, CA_WIN), F32),
        compiler_params=pltpu.CompilerParams(vmem_limit_bytes=VMEM_LIMIT_V7X),
    )(table)


def _relbias_bwd(dB, name):
    H = dB.shape[0]
    HP = -(-H // 8) * 8

    def body(d_ref, dt_ref, rows):
        rows[...] = jnp.zeros_like(rows)
        for h in range(H):
            x = jnp.concatenate([d_ref[h], jnp.zeros((CA_TILE, CA_SKEW - CA_WIN), F32)], axis=1)
            rows[pl.ds(h, 1), :] = jnp.sum(_skew(x, -1), axis=0, keepdims=True)
        dt_ref[...] = _dot(rows[...], _rel_index_matrix(), NT, HI)[:H]

    return pl.pallas_call(
        body, name=name,
        out_shape=jax.ShapeDtypeStruct((H, REL_TABLE), F32),
        scratch_shapes=[pltpu.VMEM((HP, CA_SKEW), F32)],
        compiler_params=pltpu.CompilerParams(vmem_limit_bytes=VMEM_LIMIT_V7X),
    )(dB)


CA_PIECES = CA_WIN // CA_TILE


def _ca_fwd(u, bias, H, name):
    S = u.shape[0]
    W = H * HEAD_DIM
    T = CA_TILE
    n = S // T
    scale = HEAD_DIM ** -0.5

    def body(q_ref, k0, k1, k2, v0, v1, v2, b_ref, o_ref, lse_ref):
        i = pl.program_id(1)
        q = q_ref[...].astype(BF16)
        ss = []
        for pce, k_ref in enumerate((k0, k1, k2)):
            s = _dot(q, k_ref[...].astype(BF16), NT) * scale + b_ref[:, pce * T:(pce + 1) * T]
            ss.append(jnp.where(i + pce >= CA_PIECES - 1, s, NEG))
        m = jnp.maximum(jnp.maximum(jnp.max(ss[0], -1, keepdims=True), jnp.max(ss[1], -1, keepdims=True)), jnp.max(ss[2], -1, keepdims=True))
        ps = [jnp.exp(s - m) for s in ss]
        l = sum(jnp.sum(p, -1, keepdims=True) for p in ps)
        inv = 1.0 / l
        o_ref[...] = sum(_dot((p * inv).astype(BF16), v_ref[...].astype(BF16), NN) for p, v_ref in zip(ps, (v0, v1, v2)))
        lse_ref[...] = m + jnp.log(l)

    qblk = pl.BlockSpec((T, HEAD_DIM), lambda h, i: (i, h))
    kv = lambda off, back: pl.BlockSpec((T, HEAD_DIM), lambda h, i: (jnp.maximum(i - back, 0), off * H + h))
    return pl.pallas_call(
        body, name=name, grid=(H, n),
        in_specs=[qblk, kv(1, 2), kv(1, 1), kv(1, 0), kv(2, 2), kv(2, 1), kv(2, 0),
                  pl.BlockSpec((None, T, CA_WIN), lambda h, i: (h, 0, 0))],
        out_specs=[qblk, pl.BlockSpec((None, T, 1), lambda h, i: (h, i, 0))],
        out_shape=[jax.ShapeDtypeStruct((S, W), F32), jax.ShapeDtypeStruct((H, S, 1), F32)],
        compiler_params=_cparams("parallel", "arbitrary"),
    )(u, u, u, u, u, u, u, bias)


def _ca_bwd_dq(u, bias, o, do, do_off, lse, H, name):
    S = u.shape[0]
    W = H * HEAD_DIM
    T = CA_TILE
    n = S // T
    scale = HEAD_DIM ** -0.5

    def body(q_ref, k0, k1, k2, v0, v1, v2, b_ref, o_ref, do_ref, lse_ref, dq_ref, db_ref):
        i = pl.program_id(1)

        @pl.when(i == 0)
        def _():
            db_ref[...] = jnp.zeros_like(db_ref)

        q = q_ref[...].astype(BF16)
        do = do_ref[...]
        dob = do.astype(BF16)
        delta = jnp.sum(dob.astype(F32) * o_ref[...], axis=-1, keepdims=True)
        dq = jnp.zeros((T, HEAD_DIM), F32)
        for pce, (k_ref, v_ref) in enumerate(((k0, v0), (k1, v1), (k2, v2))):
            k = k_ref[...].astype(BF16)
            s = _dot(q, k, NT) * scale + b_ref[:, pce * T:(pce + 1) * T]
            p = jnp.where(i + pce >= CA_PIECES - 1, jnp.exp(s - lse_ref[...]), 0.0)
            ds = _softmax_bwd(p, p.astype(BF16), _dot(dob, v_ref[...].astype(BF16), NT), delta)
            db_ref[:, pce * T:(pce + 1) * T] += ds
            dq = dq + _dot(ds.astype(BF16), k, NN)
        dq_ref[...] = dq * scale

    qblk = pl.BlockSpec((T, HEAD_DIM), lambda h, i: (i, h))
    kv = lambda off, back: pl.BlockSpec((T, HEAD_DIM), lambda h, i: (jnp.maximum(i - back, 0), off * H + h))
    bblk = pl.BlockSpec((None, T, CA_WIN), lambda h, i: (h, 0, 0))
    return pl.pallas_call(
        body, name=name, grid=(H, n),
        in_specs=[qblk, kv(1, 2), kv(1, 1), kv(1, 0), kv(2, 2), kv(2, 1), kv(2, 0), bblk,
                  qblk, pl.BlockSpec((T, HEAD_DIM), lambda h, i: (i, do_off + h)), pl.BlockSpec((None, T, 1), lambda h, i: (h, i, 0))],
        out_specs=[qblk, bblk],
        out_shape=[jax.ShapeDtypeStruct((S, W), F32), jax.ShapeDtypeStruct((H, T, CA_WIN), F32)],
        compiler_params=_cparams("parallel", "arbitrary"),
    )(u, u, u, u, u, u, u, bias, o, do, lse)


def _ca_bwd_dkv(u, bias, o, do, do_off, lse, H, name):
    S = u.shape[0]
    W = H * HEAD_DIM
    T = CA_TILE
    n = S // T
    scale = HEAD_DIM ** -0.5

    def body(k_ref, v_ref, q0, q1, q2, o0, o1, o2, d0, d1, d2, l0, l1, l2, b_ref, dk_ref, dv_ref):
        i = pl.program_id(1)
        k = k_ref[...].astype(BF16)
        v = v_ref[...].astype(BF16)
        dk = jnp.zeros((T, HEAD_DIM), F32)
        dv = jnp.zeros((T, HEAD_DIM), F32)
        for ahead, (q_ref, o_ref, do_ref, lse_ref) in enumerate(((q0, o0, d0, l0), (q1, o1, d1, l1), (q2, o2, d2, l2))):
            pce = CA_PIECES - 1 - ahead
            q = q_ref[...].astype(BF16)
            do = do_ref[...]
            dob = do.astype(BF16)
            s = _dot(q, k, NT) * scale + b_ref[:, pce * T:(pce + 1) * T]
            p = jnp.where(i + ahead <= n - 1, jnp.exp(s - lse_ref[...]), 0.0)
            delta = jnp.sum(dob.astype(F32) * o_ref[...], axis=-1, keepdims=True)
            ds = _softmax_bwd(p, p.astype(BF16), _dot(dob, v, NT), delta)
            dv = dv + _dot(p.astype(BF16), dob, TN)
            dk = dk + _dot(ds.astype(BF16), q, TN)
        dk_ref[...] = dk * scale
        dv_ref[...] = dv

    kv = lambda off: pl.BlockSpec((T, HEAD_DIM), lambda h, i: (i, off * H + h))
    qa = lambda ahead, off=0: pl.BlockSpec((T, HEAD_DIM), lambda h, i: (jnp.minimum(i + ahead, n - 1), off + h))
    la = lambda ahead: pl.BlockSpec((None, T, 1), lambda h, i: (h, jnp.minimum(i + ahead, n - 1), 0))
    oblk = pl.BlockSpec((T, HEAD_DIM), lambda h, i: (i, h))
    return pl.pallas_call(
        body, name=name, grid=(H, n),
        in_specs=[kv(1), kv(2), qa(0), qa(1), qa(2), qa(0), qa(1), qa(2), qa(0, do_off), qa(1, do_off), qa(2, do_off), la(0), la(1), la(2),
                  pl.BlockSpec((None, T, CA_WIN), lambda h, i: (h, 0, 0))],
        out_specs=[oblk, oblk],
        out_shape=[jax.ShapeDtypeStruct((S, W), F32)] * 2,
        compiler_params=_cparams("parallel", "parallel"),
    )(u, u, u, u, u, o, o, o, do, do, do, lse, lse, lse, bias)


def _sum_parts(parts, name):
    _, R, C = parts.shape

    def body(p_ref, o_ref):
        acc = p_ref[0].astype(F32)
        for d in range(1, N_DEV):
            acc = acc + p_ref[d].astype(F32)
        o_ref[...] = acc

    return pl.pallas_call(
        body, name=name, out_shape=jax.ShapeDtypeStruct((R, C), F32),
        compiler_params=pltpu.CompilerParams(vmem_limit_bytes=VMEM_LIMIT_V7X),
    )(parts)


def _adamw(parts, w, m, v, name, tr=256):
    P, R, C = parts.shape
    tr = _tile(R, tr)
    c1 = 1.0 / (1.0 - ADAM_B1 ** ADAM_STEP)
    c2 = 1.0 / (1.0 - ADAM_B2 ** ADAM_STEP)

    def body(p_ref, w_ref, m_ref, v_ref, g_ref, d_ref, nm_ref, nv_ref):
        g = p_ref[0].astype(F32)
        for d in range(1, P):
            g = g + p_ref[d].astype(F32)
        nm = ADAM_B1 * m_ref[...] + (1.0 - ADAM_B1) * g
        nv = ADAM_B2 * v_ref[...] + (1.0 - ADAM_B2) * (g * g)
        g_ref[...] = g
        nm_ref[...] = nm
        nv_ref[...] = nv
        d_ref[...] = -ADAM_LR * ((nm * c1) / (jnp.sqrt(nv * c2) + ADAM_EPS) + ADAM_WD * w_ref[...])

    blk = pl.BlockSpec((tr, C), lambda i: (i, 0))
    return pl.pallas_call(
        body, name=name, grid=(R // tr,),
        in_specs=[pl.BlockSpec((P, tr, C), lambda i: (0, i, 0)), blk, blk, blk],
        out_specs=[blk] * 4,
        out_shape=[jax.ShapeDtypeStruct((R, C), F32)] * 4,
        compiler_params=_cparams("parallel"),
    )(parts, w, m, v)


def _peer(d):
    x, y, c = lax.axis_index("x"), lax.axis_index("y"), lax.axis_index("c")
    px = (1 - x) if d & 4 else x
    py = (1 - y) if d & 2 else y
    pc = (1 - c) if d & 1 else c
    return (px, py, pc), 4 * px + 2 * py + pc


N_PEER = N_DEV - 1


def _exchange_copies(ins, outs, sems, scatter):
    send_sems, recv_sems, local_sems = sems
    me = 4 * lax.axis_index("x") + 2 * lax.axis_index("y") + lax.axis_index("c")
    starts, waits = [], []
    for t in range(len(ins)):
        loc = pltpu.make_async_copy(ins[t].at[me] if scatter else ins[t], outs[t].at[me], local_sems.at[t])
        starts.append(loc.start)
        waits.append(loc.wait)
        for d in range(1, N_DEV):
            peer, pidx = _peer(d)
            src = ins[t].at[pidx] if scatter else ins[t]
            k = t * N_PEER + d - 1
            common = dict(src_ref=src, send_sem=send_sems.at[k], recv_sem=recv_sems.at[k], device_id=peer, device_id_type=MESH_ID)
            starts.append(pltpu.make_async_remote_copy(dst_ref=outs[t].at[me], **common).start)
            waits.append(pltpu.make_async_remote_copy(dst_ref=outs[t].at[pidx], **common).wait)
    return starts, waits


def _exchange_scratch(n):
    return [pltpu.SemaphoreType.DMA((n * N_PEER,)), pltpu.SemaphoreType.DMA((n * N_PEER,)), pltpu.SemaphoreType.DMA((n,))]


def _exchange_shapes(arrays, scatter):
    return [jax.ShapeDtypeStruct(a.shape if scatter else (N_DEV,) + a.shape, a.dtype) for a in arrays]


def _exchange(arrays, scatter, name):
    n = len(arrays)

    def body(*refs):
        starts, waits = _exchange_copies(refs[:n], refs[n:2 * n], refs[2 * n:], scatter)
        for f in starts:
            f()
        for f in waits:
            f()

    hbm = pl.BlockSpec(memory_space=pltpu.HBM)
    return pl.pallas_call(
        body, name=name,
        in_specs=[hbm] * n, out_specs=[hbm] * n,
        out_shape=_exchange_shapes(arrays, scatter),
        scratch_shapes=_exchange_scratch(n),
        compiler_params=pltpu.CompilerParams(has_side_effects=True),
    )(*arrays)


def _call(body, *, name, grid, in_specs, out_specs, out_shape, scratch_shapes, sem, args, carry=None):
    if carry is None:
        outs = pl.pallas_call(body, name=name, grid=grid, in_specs=in_specs, out_specs=out_specs, out_shape=out_shape,
                              scratch_shapes=scratch_shapes, compiler_params=_cparams(*sem))(*args)
        return list(outs), []
    arrays, scatter = carry
    nc, n_in, n_out, n_scr = len(arrays), len(in_specs), len(out_specs), len(scratch_shapes)

    def wrapped(*refs):
        ins, cins = refs[:n_in], refs[n_in:n_in + nc]
        outs, couts = refs[n_in + nc:n_in + nc + n_out], refs[n_in + nc + n_out:n_in + 2 * nc + n_out]
        scr = refs[n_in + 2 * nc + n_out:n_in + 2 * nc + n_out + n_scr]
        ids = [pl.program_id(k) for k in range(len(grid))]
        first = functools.reduce(jnp.logical_and, [i == 0 for i in ids])
        last = functools.reduce(jnp.logical_and, [i == g - 1 for i, g in zip(ids, grid)])
        starts, waits = _exchange_copies(cins, couts, refs[-3:], scatter)

        @pl.when(first)
        def _():
            for f in starts:
                f()

        body(*ins, *outs, *scr)

        @pl.when(last)
        def _():
            for f in waits:
                f()

    hbm = pl.BlockSpec(memory_space=pltpu.HBM)
    res = pl.pallas_call(
        wrapped, name=name, grid=grid,
        in_specs=list(in_specs) + [hbm] * nc, out_specs=list(out_specs) + [hbm] * nc,
        out_shape=list(out_shape) + _exchange_shapes(arrays, scatter),
        scratch_shapes=list(scratch_shapes) + _exchange_scratch(nc),
        compiler_params=_cparams(*(("arbitrary",) * len(grid))),
    )(*args, *arrays)
    return list(res[:n_out]), list(res[n_out:])


def _relu2_fwd(acc):
    r = jnp.maximum(acc, 0.0)
    return acc, r * r


def _relu2_bwd(acc, z):
    return (acc * 2.0 * jnp.maximum(z.astype(F32), 0.0),)


def _add_scaled(scale):
    def epi(acc, t):
        return (acc + scale * t,)
    return epi


def _local_step(x, target, P, plan):
    S, D = x.shape
    H = (D // 2) // HEAD_DIM
    W = H * HEAD_DIM
    alpha = (2 * 2) ** 0.25
    g = {}

    def mm(a, b, *, name, **kw):
        res, carried = _mm(a, b, name=name, carry=plan.carry(name, g), **kw)
        plan.arrived(name, carried, P, g)
        return res

    u0 = mm(x, P["ev_w_in"], mode="nn", name="ev_in")
    a_out, cv = _conv_fwd(u0, P["ev_conv_w"], P["ev_conv_b"], P["ev_conv_ln_g"], P["ev_conv_ln_b"], "conv_fwd")
    b_out, h_raw, h_states = _hgrn_fwd(u0, P["hgrn_lb_logits"], P["ev_gnorm_g"], "hgrn_fwd")
    cat0 = jnp.concatenate([a_out.astype(BF16), b_out.astype(BF16)], axis=1)
    mix0 = mm(cat0, P["ev_w_out"], mode="nn", name="ev_out")
    x1, xh1, rs1 = _ln_fwd(x, mix0, P["ln_mix_g"][0:1], P["ln_mix_b"][0:1], alpha, "ln_mix0")
    z0, hh0 = mm(x1, P["mlp_w1_0"], mode="nn", name="mlp_up0", out_dtypes=(BF16, BF16), epi=_relu2_fwd)
    m0 = mm(hh0, P["mlp_w2_0"], mode="nn", name="mlp_down0")
    x2, xh2, rs2 = _ln_fwd(x1, m0, P["ln_mlp_g"][0:1], P["ln_mlp_b"][0:1], alpha, "ln_mlp0")

    uc = mm(x2, P["od_w_c"], mode="nn", name="od_in_c")
    ud = mm(x2, P["od_w_d"], mode="nn", name="od_in_d")
    f_logit, F = _fgate_fwd(x2, P["od_w_f_t"], P["fox_b_f"].reshape(H, 1), "fgate_fwd")
    F3 = F.reshape(H, 1, S)
    c_lse, carried = _fox_lse(uc, F3, H, "fox_lse", carry=plan.carry("fox_lse", g))
    plan.arrived("fox_lse", carried, P, g)
    c_out = _fox_fwd(uc, F3, c_lse, H, "fox_fwd")
    bias = _relbias_fwd(P["rel_bias"], "relbias_fwd")
    d_out, d_lse = _ca_fwd(ud, bias, H, "ca_fwd")
    cat1 = jnp.concatenate([c_out.astype(BF16), d_out.astype(BF16)], axis=1)
    mix1 = mm(cat1, P["od_w_out"], mode="nn", name="od_out")
    x3, xh3, rs3 = _ln_fwd(x2, mix1, P["ln_mix_g"][1:2], P["ln_mix_b"][1:2], alpha, "ln_mix1")
    z1, hh1 = mm(x3, P["mlp_w1_1"], mode="nn", name="mlp_up1", out_dtypes=(BF16, BF16), epi=_relu2_fwd)
    m1 = mm(hh1, P["mlp_w2_1"], mode="nn", name="mlp_down1")
    x4, xh4, rs4 = _ln_fwd(x3, m1, P["ln_mlp_g"][1:2], P["ln_mlp_b"][1:2], alpha, "ln_mlp1")

    dy, loss = _loss_head(x4, target, "loss_head")

    dzm1, dg_, db_ = _ln_bwd(dy, None, 0.0, xh4, rs4, P["ln_mlp_g"][1:2], "ln_mlp1_bwd")
    g["ln_mlp_g1"], g["ln_mlp_b1"] = dg_, db_
    g["mlp_w2_1"] = mm(hh1, dzm1, mode="tn", name="mlp_down1_dw", out_dtypes=(BF16,))
    dz1 = mm(dzm1, P["mlp_w2_1"], mode="nt", name="mlp_down1_dx", out_dtypes=(BF16,), extras=(z1,), epi=_relu2_bwd)
    g["mlp_w1_1"] = mm(x3, dz1, mode="tn", name="mlp_up1_dw", out_dtypes=(BF16,))
    dx3 = mm(dz1, P["mlp_w1_1"], mode="nt", name="mlp_up1_dx", extras=(dzm1,), epi=_add_scaled(alpha))
    dzx1, dg_, db_ = _ln_bwd(dx3, None, 0.0, xh3, rs3, P["ln_mix_g"][1:2], "ln_mix1_bwd")
    g["ln_mix_g1"], g["ln_mix_b1"] = dg_, db_
    g["od_w_out"] = mm(cat1, dzx1, mode="tn", name="od_out_dw", out_dtypes=(BF16,))
    dcat1 = mm(dzx1, P["od_w_out"], mode="nt", name="od_out_dx")
    dq_d, dbias = _ca_bwd_dq(ud, bias, d_out, dcat1, H, d_lse, H, "ca_bwd_dq")
    dk_d, dv_d = _ca_bwd_dkv(ud, bias, d_out, dcat1, H, d_lse, H, "ca_bwd_dkv")
    g["rel_bias"] = _relbias_bwd(dbias, "relbias_bwd")
    dud = jnp.concatenate([dq_d, dk_d, dv_d], axis=1)
    dq_c = _fox_bwd_dq(uc, F3, c_out, dcat1, 0, c_lse, H, "fox_bwd_dq")
    dk_c, dv_c, dF3 = _fox_bwd_dkv(uc, F3, c_out, dcat1, 0, c_lse, H, "fox_bwd_dkv")
    duc = jnp.concatenate([dq_c, dk_c, dv_c], axis=1)
    dfl, dbf = _fgate_bwd(dF3.reshape(H, S), f_logit, "fgate_bwd")
    g["fox_b_f"] = dbf.reshape(1, H)
    g["od_w_c"] = mm(x2, duc, mode="tn", name="od_in_c_dw", out_dtypes=(BF16,))
    g["od_w_d"] = mm(x2, dud, mode="tn", name="od_in_d_dw", out_dtypes=(BF16,))
    g["od_w_f_t"] = mm(dfl, x2, mode="nn", name="od_in_f_dw", out_dtypes=(BF16,), exact_products=True)
    dx2 = mm(duc, P["od_w_c"], mode="nt", name="od_in_c_dx", extras=(dzx1,), epi=_add_scaled(alpha))
    dx2 = mm(dud, P["od_w_d"], mode="nt", name="od_in_d_dx", extras=(dx2,), epi=_add_scaled(1.0))
    dx2 = mm(dfl, P["od_w_f_t"], mode="tn", name="od_in_f_dx", extras=(dx2,), epi=_add_scaled(1.0), exact_products=True)

    dzm0, dg_, db_ = _ln_bwd(dx2, None, 0.0, xh2, rs2, P["ln_mlp_g"][0:1], "ln_mlp0_bwd")
    g["ln_mlp_g0"], g["ln_mlp_b0"] = dg_, db_
    g["mlp_w2_0"] = mm(hh0, dzm0, mode="tn", name="mlp_down0_dw", out_dtypes=(BF16,))
    dz0 = mm(dzm0, P["mlp_w2_0"], mode="nt", name="mlp_down0_dx", out_dtypes=(BF16,), extras=(z0,), epi=_relu2_bwd)
    g["mlp_w1_0"] = mm(x1, dz0, mode="tn", name="mlp_up0_dw", out_dtypes=(BF16,))
    dx1 = mm(dz0, P["mlp_w1_0"], mode="nt", name="mlp_up0_dx", extras=(dzm0,), epi=_add_scaled(alpha))
    dzx0, dg_, db_ = _ln_bwd(dx1, None, 0.0, xh1, rs1, P["ln_mix_g"][0:1], "ln_mix0_bwd")
    g["ln_mix_g0"], g["ln_mix_b0"] = dg_, db_
    g["ev_w_out"] = mm(cat0, dzx0, mode="tn", name="ev_out_dw", out_dtypes=(BF16,))
    dcat0 = mm(dzx0, P["ev_w_out"], mode="nt", name="ev_out_dx")
    dcv, g["ev_conv_ln_g"], g["ev_conv_ln_b"], g["ev_conv_b"] = _conv_bwd_norm(dcat0, 0, cv, P["ev_conv_ln_g"], P["ev_conv_ln_b"], "conv_bwd_norm")
    du_a, du_g, g["ev_conv_w"] = _conv_bwd_taps(dcv, u0, P["ev_conv_w"], "conv_bwd_taps")
    dhq, dhf, dhi, dhg, g["ev_gnorm_g"], g["hgrn_lb_logits"] = _hgrn_bwd(dcat0, 1, u0, h_raw, h_states, P["hgrn_lb_logits"], P["ev_gnorm_g"], "hgrn_bwd")
    du0 = jnp.concatenate([du_a, du_g, dhq, dhf, dhi, dhg], axis=1)
    g["ev_w_in"] = mm(x, du0, mode="tn", name="ev_in_dw", out_dtypes=(BF16,))
    dx0 = mm(du0, P["ev_w_in"], mode="nt", name="ev_in_dx", extras=(dzx0,), epi=_add_scaled(alpha))
    return loss, dx0, g


_NAMES = ['ev_w_in', 'ev_conv_w', 'ev_conv_b', 'ev_conv_ln_g', 'ev_conv_ln_b', 'hgrn_lb_logits', 'ev_gnorm_g', 'ev_w_out',
          'od_w_in', 'fox_b_f', 'rel_bias', 'od_w_out', 'ln_mix_g', 'ln_mix_b', 'mlp_w1', 'mlp_w2', 'ln_mlp_g', 'ln_mlp_b']
_SMALL = ['ev_conv_b', 'ev_conv_ln_g', 'ev_conv_ln_b', 'hgrn_lb_logits', 'ev_gnorm_g', 'fox_b_f', 'ln_mix_g', 'ln_mix_b',
          'ln_mlp_g', 'ln_mlp_b', 'ev_conv_w', 'rel_bias']
_PACK_COLS = 2048


def _cols_to_full(gathered):
    nd, K, n = gathered.shape
    return jnp.transpose(gathered, (1, 0, 2)).reshape(K, nd * n)


def _full_to_cols(full):
    K, N = full.shape
    return jnp.transpose(full.reshape(K, N_DEV, N // N_DEV), (1, 0, 2))


_ROW_SHARDED = ("ev_w_out", "od_w_out", "mlp_w2_0", "mlp_w2_1")


def _full_weights(name, gathered, heads):
    if name in _ROW_SHARDED:
        return {name: gathered.reshape(-1, gathered.shape[-1])}
    full = _cols_to_full(gathered)
    if name != "od_w_in":
        return {name: full}
    w = heads * HEAD_DIM
    return {"od_w_c": full[:, :3 * w], "od_w_f_t": jnp.transpose(full[:, 3 * w:3 * w + heads]), "od_w_d": full[:, 3 * w + heads:]}


def _grad_blocks(name, g):
    if name in _ROW_SHARDED:
        return g[name].reshape(N_DEV, -1, g[name].shape[-1])
    if name == "od_w_in":
        return _full_to_cols(jnp.concatenate([g["od_w_c"], jnp.transpose(g["od_w_f_t"]), g["od_w_d"]], axis=1))
    return _full_to_cols(g[name])


class _Plan:
    GATHER = {"ev_in": ("mlp_w1_0",), "mlp_up0": ("mlp_w2_0",), "mlp_down0": ("od_w_in",), "od_in_c": ("od_w_out",),
              "fox_lse": ("mlp_w1_1", "mlp_w2_1")}
    SCATTER = {"mlp_down1_dx": "mlp_w2_1", "mlp_up1_dx": "mlp_w1_1", "od_out_dx": "od_w_out", "od_in_c_dx": "od_w_in",
               "mlp_down0_dx": "mlp_w2_0", "mlp_up0_dx": "mlp_w1_0", "ev_out_dx": "ev_w_out", "ev_in_dx": "ev_w_in"}

    def __init__(self, shards, heads):
        self.shards, self.heads = shards, heads

    def carry(self, call, g):
        if call in self.GATHER:
            return [self.shards[n] for n in self.GATHER[call]], False
        if call in self.SCATTER:
            return [_grad_blocks(self.SCATTER[call], g)], True
        return None

    def arrived(self, call, carried, P, g):
        if call in self.GATHER:
            for n, gathered in zip(self.GATHER[call], carried):
                P.update(_full_weights(n, gathered, self.heads))
        elif call in self.SCATTER:
            g[self.SCATTER[call]] = carried[0]


def _pack(parts):
    flat = jnp.concatenate([p.reshape(-1).astype(F32) for p in parts])
    rows = -(-flat.shape[0] // (_PACK_COLS * 8)) * 8
    return jnp.pad(flat, (0, rows * _PACK_COLS - flat.shape[0])).reshape(rows, _PACK_COLS)


def _unpack(packed, shapes):
    flat = packed.reshape(-1)
    out, off = [], 0
    for s in shapes:
        n = math.prod(s)
        out.append(flat[off:off + n].reshape(s))
        off += n
    return out


def kernel(x, ev_w_in, ev_conv_w, ev_conv_b, ev_conv_ln_g, ev_conv_ln_b, hgrn_lb_logits, ev_gnorm_g, ev_w_out, od_w_in, fox_b_f, rel_bias, od_w_out, ln_mix_g, ln_mix_b, mlp_w1, mlp_w2, ln_mlp_g, ln_mlp_b, loss_target, m_ev_w_in, m_ev_conv_w, m_ev_conv_b, m_ev_conv_ln_g, m_ev_conv_ln_b, m_hgrn_lb_logits, m_ev_gnorm_g, m_ev_w_out, m_od_w_in, m_fox_b_f, m_rel_bias, m_od_w_out, m_ln_mix_g, m_ln_mix_b, m_mlp_w1, m_mlp_w2, m_ln_mlp_g, m_ln_mlp_b, v_ev_w_in, v_ev_conv_w, v_ev_conv_b, v_ev_conv_ln_g, v_ev_conv_ln_b, v_hgrn_lb_logits, v_ev_gnorm_g, v_ev_w_out, v_od_w_in, v_fox_b_f, v_rel_bias, v_od_w_out, v_ln_mix_g, v_ln_mix_b, v_mlp_w1, v_mlp_w2, v_ln_mlp_g, v_ln_mlp_b):
    args = locals()
    w = {n: args[n] for n in _NAMES}
    m = {n: args["m_" + n] for n in _NAMES}
    v = {n: args["v_" + n] for n in _NAMES}
    me = 4 * lax.axis_index("x") + 2 * lax.axis_index("y") + lax.axis_index("c")
    S, D = x.shape[1], x.shape[2]
    H = (D // 2) // HEAD_DIM
    W = H * HEAD_DIM
    n_layers = mlp_w1.shape[0]
    assert n_layers == 2 and ev_w_in.shape[0] == 1 and od_w_in.shape[0] == 1

    shards = {"ev_w_in": ev_w_in[0].astype(BF16), "ev_w_out": ev_w_out[0].astype(BF16),
              "od_w_in": od_w_in[0].astype(BF16), "od_w_out": od_w_out[0].astype(BF16)}
    for l in range(n_layers):
        shards["mlp_w1_%d" % l] = mlp_w1[l].astype(BF16)
        shards["mlp_w2_%d" % l] = mlp_w2[l].astype(BF16)
    first = ["ev_w_in", "ev_w_out", "ev_conv_w", "rel_bias"]
    G = _exchange([shards["ev_w_in"], shards["ev_w_out"], ev_conv_w[0], rel_bias[0]], False, "gather_first")
    P = {
        "ev_conv_b": ev_conv_b, "ev_conv_ln_g": ev_conv_ln_g, "ev_conv_ln_b": ev_conv_ln_b,
        "hgrn_lb_logits": hgrn_lb_logits, "ev_gnorm_g": ev_gnorm_g, "fox_b_f": fox_b_f,
        "ln_mix_g": ln_mix_g, "ln_mix_b": ln_mix_b, "ln_mlp_g": ln_mlp_g, "ln_mlp_b": ln_mlp_b,
    }
    for name, gathered in zip(first, G):
        P.update(_full_weights(name, gathered, H))

    loss, grad_x, g = _local_step(x[0], loss_target[0], P, _Plan(shards, H))
    recv = [g["ev_w_in"], g["ev_w_out"], g["od_w_in"], g["od_w_out"]]
    recv += [g["mlp_w1_%d" % l] for l in range(n_layers)] + [g["mlp_w2_%d" % l] for l in range(n_layers)]

    small = {
        "ev_conv_b": g["ev_conv_b"], "ev_conv_ln_g": g["ev_conv_ln_g"], "ev_conv_ln_b": g["ev_conv_ln_b"],
        "hgrn_lb_logits": g["hgrn_lb_logits"], "ev_gnorm_g": g["ev_gnorm_g"], "fox_b_f": g["fox_b_f"],
        "ln_mix_g": jnp.concatenate([g["ln_mix_g0"], g["ln_mix_g1"]]), "ln_mix_b": jnp.concatenate([g["ln_mix_b0"], g["ln_mix_b1"]]),
        "ln_mlp_g": jnp.concatenate([g["ln_mlp_g0"], g["ln_mlp_g1"]]), "ln_mlp_b": jnp.concatenate([g["ln_mlp_b0"], g["ln_mlp_b1"]]),
        "ev_conv_w": g["ev_conv_w"], "rel_bias": g["rel_bias"],
    }
    full_shapes = [small[n].shape for n in _SMALL]
    small_all = _exchange([_pack([small[n] for n in _SMALL])], False, "gather_small_grads")[0]
    small_sum = _unpack(_sum_parts(small_all, "sum_small_grads"), full_shapes)
    small_g = dict(zip(_SMALL, small_sum))
    cw = small_g["ev_conv_w"]
    small_g["ev_conv_w"] = lax.dynamic_slice_in_dim(cw, me * (cw.shape[1] // N_DEV), cw.shape[1] // N_DEV, axis=1)
    rb = small_g["rel_bias"]
    small_g["rel_bias"] = lax.dynamic_slice_in_dim(rb, me * (rb.shape[1] // N_DEV), rb.shape[1] // N_DEV, axis=1)

    out_g, out_d, out_m, out_v = {}, {}, {}, {}

    def put(name, res, shape):
        out_g[name], out_d[name], out_m[name], out_v[name] = [r.reshape(shape) for r in res]

    big = [("ev_w_in", recv[0], None), ("ev_w_out", recv[1], None), ("od_w_in", recv[2], None), ("od_w_out", recv[3], None)]
    for name, parts, _ in big:
        shp = w[name].shape
        put(name, _adamw(parts, w[name][0], m[name][0], v[name][0], "adamw_" + name), shp)
    for name, base in (("mlp_w1", 4), ("mlp_w2", 4 + n_layers)):
        res = [_adamw(recv[base + l], w[name][l], m[name][l], v[name][l], "adamw_%s_%d" % (name, l)) for l in range(n_layers)]
        put(name, [jnp.stack([res[l][k] for l in range(n_layers)]) for k in range(4)], w[name].shape)
    shapes = [w[n].shape for n in _SMALL]
    packed = _adamw(_pack([small_g[n] for n in _SMALL])[None], _pack([w[n] for n in _SMALL]), _pack([m[n] for n in _SMALL]),
                    _pack([v[n] for n in _SMALL]), "adamw_small")
    for k, dst in enumerate((out_g, out_d, out_m, out_v)):
        for n, a in zip(_SMALL, _unpack(packed[k], shapes)):
            dst[n] = a

    loss = lax.psum(loss[0, 0], ("x", "y", "c"))
    return (loss, grad_x[None], *[out_g[n] for n in _NAMES], *[out_d[n] for n in _NAMES],
            *[out_m[n] for n in _NAMES], *[out_v[n] for n in _NAMES])
```

```python
import functools
import math

import jax
import jax.numpy as jnp
from jax import lax
from jax.experimental import pallas as pl
from jax.experimental.pallas import tpu as pltpu

F32 = jnp.float32
BF16 = jnp.bfloat16
HI = lax.Precision.HIGHEST
MESH_ID = pl.DeviceIdType.MESH

N_DEV = 8
LN_EPS = 1e-5
CHUNK = 64
HEAD_DIM = 128
CONV_WIDTH = 31
CONV_HALO = 32
CA_LEFT_CHUNKS = 8
CA_TILE = 256
CA_WIN = CA_TILE + CA_LEFT_CHUNKS * CHUNK
CA_SKEW = 1024
REL_CLIP = 256
REL_TABLE = (CHUNK - 1) + REL_CLIP + 1
NEG = -1e30
ADAM_LR = 0.001
ADAM_B1 = 0.9
ADAM_B2 = 0.999
ADAM_EPS = 1e-08
ADAM_WD = 0.01
ADAM_STEP = 10
VMEM_LIMIT_V7X = 56 * 1024 * 1024


def _cparams(*sem):
    return pltpu.CompilerParams(dimension_semantics=sem, vmem_limit_bytes=VMEM_LIMIT_V7X)


def _tile(n, t):
    if n <= t:
        return n
    for c in range(t - t % 128, 0, -128):
        if n % c == 0:
            return c
    return n


def _sigmoid(x):
    return 1.0 / (1.0 + jnp.exp(-x))


def _dot(a, b, dims, precision=None):
    return lax.dot_general(a, b, (dims, ((), ())), preferred_element_type=F32, precision=precision)


def _round_bf16(x):
    return x.astype(BF16).astype(F32)


NN = ((1,), (0,))
NT = ((1,), (1,))
TN = ((0,), (0,))


def _mm(a, b, *, mode, name, out_dtypes=(F32,), extras=(), epi=None, exact_products=False, carry=None, tm=1024, tn=1024, tk=2048):
    dims = {"nn": NN, "nt": NT, "tn": TN}[mode]
    if mode == "tn":
        K, M = a.shape
    else:
        M, K = a.shape
    N = b.shape[0] if mode == "nt" else b.shape[1]
    tm, tn, tk = _tile(M, tm), _tile(N, tn), _tile(K, tk)
    nk = K // tk
    n_ex, n_out = len(extras), len(out_dtypes)

    def body(*refs):
        a_ref, b_ref = refs[0], refs[1]
        ex_refs = refs[2:2 + n_ex]
        o_refs = refs[2 + n_ex:2 + n_ex + n_out]
        if exact_products:
            d = _dot(_round_bf16(a_ref[...]), _round_bf16(b_ref[...]), dims, HI)
        else:
            d = _dot(a_ref[...].astype(BF16), b_ref[...].astype(BF16), dims)

        def finish(acc):
            outs = (acc,) if epi is None else epi(acc, *[r[...] for r in ex_refs])
            for o_ref, o in zip(o_refs, outs):
                o_ref[...] = o.astype(o_ref.dtype)

        if nk == 1:
            finish(d)
        else:
            acc_ref = refs[-1]
            k = pl.program_id(2)

            @pl.when(k == 0)
            def _():
                acc_ref[...] = d

            @pl.when(k > 0)
            def _():
                acc_ref[...] += d

            @pl.when(k == nk - 1)
            def _():
                finish(acc_ref[...])

    a_spec = pl.BlockSpec((tk, tm), lambda i, j, k: (k, i)) if mode == "tn" else pl.BlockSpec((tm, tk), lambda i, j, k: (i, k))
    b_spec = pl.BlockSpec((tn, tk), lambda i, j, k: (j, k)) if mode == "nt" else pl.BlockSpec((tk, tn), lambda i, j, k: (k, j))
    mn_spec = pl.BlockSpec((tm, tn), lambda i, j, k: (i, j))
    outs, carried = _call(
        body, name=name, grid=(M // tm, N // tn, nk),
        in_specs=[a_spec, b_spec] + [mn_spec] * n_ex,
        out_specs=[mn_spec] * n_out,
        out_shape=[jax.ShapeDtypeStruct((M, N), dt) for dt in out_dtypes],
        scratch_shapes=[pltpu.VMEM((tm, tn), F32)] if nk > 1 else [],
        sem=("parallel", "parallel", "arbitrary"), args=(a, b, *extras), carry=carry)
    return (outs[0] if n_out == 1 else outs), carried


def _ln_fwd(x, r, g, b, alpha, name, tr=256):
    S, D = x.shape
    tr = _tile(S, tr)

    def body(x_ref, r_ref, g_ref, b_ref, y_ref, xh_ref, rs_ref):
        z = alpha * x_ref[...] + r_ref[...]
        zc = z - jnp.mean(z, axis=-1, keepdims=True)
        rs = lax.rsqrt(jnp.mean(zc * zc, axis=-1, keepdims=True) + LN_EPS)
        xh = zc * rs
        xh_ref[...] = xh
        rs_ref[...] = rs
        y_ref[...] = xh * g_ref[...] + b_ref[...]

    row = pl.BlockSpec((tr, D), lambda i: (i, 0))
    par = pl.BlockSpec((1, D), lambda i: (0, 0))
    return pl.pallas_call(
        body, name=name, grid=(S // tr,),
        in_specs=[row, row, par, par],
        out_specs=[row, row, pl.BlockSpec((tr, 1), lambda i: (i, 0))],
        out_shape=[jax.ShapeDtypeStruct((S, D), F32), jax.ShapeDtypeStruct((S, D), F32), jax.ShapeDtypeStruct((S, 1), F32)],
        compiler_params=_cparams("parallel"),
    )(x, r, g, b)


def _ln_bwd(dy, dy2, scale2, xh, rs, g, name, tr=256):
    S, D = dy.shape
    tr = _tile(S, tr)
    two = dy2 is not None

    def body(*refs):
        if two:
            dy_ref, dy2_ref, xh_ref, rs_ref, g_ref, dz_ref, dg_ref, db_ref = refs
            dyt = dy_ref[...] + scale2 * dy2_ref[...]
        else:
            dy_ref, xh_ref, rs_ref, g_ref, dz_ref, dg_ref, db_ref = refs
            dyt = dy_ref[...]
        xh = xh_ref[...]
        dxh = dyt * g_ref[...]
        m1 = jnp.mean(dxh, axis=-1, keepdims=True)
        m2 = jnp.mean(dxh * xh, axis=-1, keepdims=True)
        dz_ref[...] = rs_ref[...] * (dxh - m1 - xh * m2)

        @pl.when(pl.program_id(0) == 0)
        def _():
            dg_ref[...] = jnp.zeros_like(dg_ref)
            db_ref[...] = jnp.zeros_like(db_ref)

        dg_ref[...] += jnp.sum(dyt * xh, axis=0, keepdims=True)
        db_ref[...] += jnp.sum(dyt, axis=0, keepdims=True)

    row = pl.BlockSpec((tr, D), lambda i: (i, 0))
    par = pl.BlockSpec((1, D), lambda i: (0, 0))
    ins = [dy] + ([dy2] if two else []) + [xh, rs, g]
    return pl.pallas_call(
        body, name=name, grid=(S // tr,),
        in_specs=[row] * (2 if two else 1) + [row, pl.BlockSpec((tr, 1), lambda i: (i, 0)), par],
        out_specs=[row, par, par],
        out_shape=[jax.ShapeDtypeStruct((S, D), F32), jax.ShapeDtypeStruct((1, D), F32), jax.ShapeDtypeStruct((1, D), F32)],
        compiler_params=_cparams("arbitrary"),
    )(*ins)


def _loss_head(y, target, name, tr=256):
    S, D = y.shape
    tr = _tile(S, tr)

    def body(y_ref, t_ref, dy_ref, loss_ref):
        e = y_ref[...] - t_ref[...]
        dy_ref[...] = e * (1.0 / D)

        @pl.when(pl.program_id(0) == 0)
        def _():
            loss_ref[...] = jnp.zeros_like(loss_ref)

        loss_ref[...] += jnp.sum(jnp.sum(e * e, axis=-1, keepdims=True), axis=0, keepdims=True) * (0.5 / D)

    row = pl.BlockSpec((tr, D), lambda i: (i, 0))
    return pl.pallas_call(
        body, name=name, grid=(S // tr,),
        in_specs=[row, row],
        out_specs=[row, pl.BlockSpec((1, 1), lambda i: (0, 0))],
        out_shape=[jax.ShapeDtypeStruct((S, D), F32), jax.ShapeDtypeStruct((1, 1), F32)],
        compiler_params=_cparams("arbitrary"),
    )(y, target)


def _conv_fwd(u, w, cb, lg, lb, name, tt=512):
    S = u.shape[0]
    C = w.shape[1]
    tt = _tile(S, tt)
    hpt = tt // CONV_HALO

    def body(a_ref, g_ref, ap_ref, gp_ref, w_ref, cb_ref, lg_ref, lb_ref, out_ref, cv_ref, hext):
        i = pl.program_id(0)
        hext[pl.ds(CONV_HALO, tt), :] = a_ref[...] * _sigmoid(g_ref[...])
        hp = ap_ref[...] * _sigmoid(gp_ref[...])
        hext[pl.ds(0, CONV_HALO), :] = jnp.where(i > 0, hp, 0.0)
        acc = jnp.zeros((tt, C), F32)
        for k in range(CONV_WIDTH):
            acc = acc + w_ref[pl.ds(k, 1), :] * hext[pl.ds(CONV_HALO - (CONV_WIDTH - 1) + k, tt), :]
        cv = acc + cb_ref[...]
        cv_ref[...] = cv
        zc = cv - jnp.mean(cv, axis=-1, keepdims=True)
        n = zc * lax.rsqrt(jnp.mean(zc * zc, axis=-1, keepdims=True) + LN_EPS) * lg_ref[...] + lb_ref[...]
        out_ref[...] = n * _sigmoid(n)

    cur = lambda cb_: pl.BlockSpec((tt, C), lambda i: (i, cb_))
    prev = lambda cb_: pl.BlockSpec((CONV_HALO, C), lambda i: (jnp.maximum(i * hpt - 1, 0), cb_))
    par = pl.BlockSpec((1, C), lambda i: (0, 0))
    row = pl.BlockSpec((tt, C), lambda i: (i, 0))
    return pl.pallas_call(
        body, name=name, grid=(S // tt,),
        in_specs=[cur(0), cur(1), prev(0), prev(1), pl.BlockSpec((CONV_WIDTH, C), lambda i: (0, 0)), par, par, par],
        out_specs=[row, row],
        out_shape=[jax.ShapeDtypeStruct((S, C), F32)] * 2,
        scratch_shapes=[pltpu.VMEM((tt + CONV_HALO, C), F32)],
        compiler_params=_cparams("parallel"),
    )(u, u, u, u, w, cb, lg, lb)


def _conv_bwd_norm(da, da_col, cv, lg, lb, name, tt=512):
    S, C = cv.shape
    tt = _tile(S, tt)

    def body(da_ref, cv_ref, lg_ref, lb_ref, dcv_ref, dlg_ref, dlb_ref, dcb_ref):
        cv = cv_ref[...]
        zc = cv - jnp.mean(cv, axis=-1, keepdims=True)
        rs = lax.rsqrt(jnp.mean(zc * zc, axis=-1, keepdims=True) + LN_EPS)
        xh = zc * rs
        n = xh * lg_ref[...] + lb_ref[...]
        sg = _sigmoid(n)
        dn = da_ref[...] * sg * (1.0 + n * (1.0 - sg))
        dxh = dn * lg_ref[...]
        m1 = jnp.mean(dxh, axis=-1, keepdims=True)
        m2 = jnp.mean(dxh * xh, axis=-1, keepdims=True)
        dcv = rs * (dxh - m1 - xh * m2)
        dcv_ref[...] = dcv

        @pl.when(pl.program_id(0) == 0)
        def _():
            dlg_ref[...] = jnp.zeros_like(dlg_ref)
            dlb_ref[...] = jnp.zeros_like(dlb_ref)
            dcb_ref[...] = jnp.zeros_like(dcb_ref)

        dlg_ref[...] += jnp.sum(dn * xh, axis=0, keepdims=True)
        dlb_ref[...] += jnp.sum(dn, axis=0, keepdims=True)
        dcb_ref[...] += jnp.sum(dcv, axis=0, keepdims=True)

    row = pl.BlockSpec((tt, C), lambda i: (i, 0))
    par = pl.BlockSpec((1, C), lambda i: (0, 0))
    return pl.pallas_call(
        body, name=name, grid=(S // tt,),
        in_specs=[pl.BlockSpec((tt, C), lambda i: (i, da_col)), row, par, par],
        out_specs=[row, par, par, par],
        out_shape=[jax.ShapeDtypeStruct((S, C), F32)] + [jax.ShapeDtypeStruct((1, C), F32)] * 3,
        compiler_params=_cparams("arbitrary"),
    )(da, cv, lg, lb)


def _conv_bwd_taps(dcv, u, w, name, tt=512):
    S, C = dcv.shape
    tt = _tile(S, tt)
    hpt = tt // CONV_HALO
    nt = S // tt
    WPAD = 32

    def body(dc_ref, dn_ref, a_ref, g_ref, ap_ref, gp_ref, w_ref, da_ref, dg_ref, dw_ref, hext, dext):
        i = pl.program_id(0)
        a = a_ref[...]
        sg = _sigmoid(g_ref[...])
        hext[pl.ds(CONV_HALO, tt), :] = a * sg
        hp = ap_ref[...] * _sigmoid(gp_ref[...])
        hext[pl.ds(0, CONV_HALO), :] = jnp.where(i > 0, hp, 0.0)
        dc = dc_ref[...]
        dext[pl.ds(0, tt), :] = dc
        dext[pl.ds(tt, CONV_HALO), :] = jnp.where(i < nt - 1, dn_ref[...], 0.0)

        @pl.when(i == 0)
        def _():
            dw_ref[...] = jnp.zeros_like(dw_ref)

        dh = jnp.zeros((tt, C), F32)
        for k in range(CONV_WIDTH):
            dh = dh + w_ref[pl.ds(k, 1), :] * dext[pl.ds(CONV_WIDTH - 1 - k, tt), :]
            hk = hext[pl.ds(CONV_HALO - (CONV_WIDTH - 1) + k, tt), :]
            dw_ref[pl.ds(k, 1), :] += jnp.sum(dc * hk, axis=0, keepdims=True)
        da_ref[...] = dh * sg
        dg_ref[...] = dh * a * sg * (1.0 - sg)

    row = pl.BlockSpec((tt, C), lambda i: (i, 0))
    nxt = pl.BlockSpec((CONV_HALO, C), lambda i: (jnp.minimum((i + 1) * hpt, S // CONV_HALO - 1), 0))
    cur = lambda cb_: pl.BlockSpec((tt, C), lambda i: (i, cb_))
    prev = lambda cb_: pl.BlockSpec((CONV_HALO, C), lambda i: (jnp.maximum(i * hpt - 1, 0), cb_))
    da, dg, dw = pl.pallas_call(
        body, name=name, grid=(nt,),
        in_specs=[row, nxt, cur(0), cur(1), prev(0), prev(1), pl.BlockSpec((CONV_WIDTH, C), lambda i: (0, 0))],
        out_specs=[row, row, pl.BlockSpec((WPAD, C), lambda i: (0, 0))],
        out_shape=[jax.ShapeDtypeStruct((S, C), F32)] * 2 + [jax.ShapeDtypeStruct((WPAD, C), F32)],
        scratch_shapes=[pltpu.VMEM((tt + CONV_HALO, C), F32)] * 2,
        compiler_params=_cparams("arbitrary"),
    )(dcv, dcv, u, u, u, u, w)
    return da, dg, dw[:CONV_WIDTH]


def _lower_bound(logits):
    e = jnp.exp(logits - jnp.max(logits, axis=0, keepdims=True))
    p = e / jnp.sum(e, axis=0, keepdims=True)
    return p[0:1, :], p


def _tri(n, lower):
    r = lax.broadcasted_iota(jnp.int32, (n, n), 0)
    c = lax.broadcasted_iota(jnp.int32, (n, n), 1)
    return ((c <= r) if lower else (c >= r)).astype(F32)


def _hgrn_gates(xq, xf, lb):
    sq = _sigmoid(xq)
    q = xq * sq
    sf = _sigmoid(xf)
    f = lb + (1.0 - lb) * sf
    logf = jnp.log(f)
    L = _dot(_tri(CHUNK, True), logf, NN, HI)
    Lend = L[CHUNK - 1:CHUNK, :]
    eL = jnp.exp(L)
    enL = jnp.exp(-L)
    eLe = jnp.exp(Lend - L)
    kk = 1.0 - f
    return dict(sq=sq, q=q, sf=sf, f=f, L=L, Lend=Lend, eL=eL, enL=enL, eLe=eLe, kk=kk,
                qe=q * eL, ke=kk * enL, kd=kk * eLe)


def _hgrn_fwd(u, lbl, gn, name, tb=256):
    S = u.shape[0]
    W = gn.shape[1]
    H = W // HEAD_DIM
    tb = _tile(S, tb)
    cpb = tb // CHUNK
    nc = S // CHUNK
    R = lbl.shape[0]

    def body(q_ref, f_ref, i_ref, g_ref, lbl_ref, gn_ref, out_ref, raw_ref, st_ref, state):
        @pl.when(pl.program_id(0) == 0)
        def _():
            state[...] = jnp.zeros_like(state)

        lb, _ = _lower_bound(lbl_ref[...])
        tril = _tri(CHUNK, True) > 0.5

        def chunk(c, carry):
            rows = pl.ds(pl.multiple_of(c * CHUNK, CHUNK), CHUNK)
            G = _hgrn_gates(q_ref[rows, :], f_ref[rows, :], lb)
            v = i_ref[rows, :]
            xg = g_ref[rows, :]
            outs = []
            for h in range(H):
                ln = slice(h * HEAD_DIM, (h + 1) * HEAD_DIM)
                qe, ke, kd, vh = G["qe"][:, ln].astype(BF16), G["ke"][:, ln].astype(BF16), G["kd"][:, ln].astype(BF16), v[:, ln].astype(BF16)
                st = state[h]
                st_ref[c, h] = st
                A = jnp.where(tril, _dot(qe, ke, NT), 0.0)
                o = _dot(A.astype(BF16), vh, NN) + _dot(qe, st.astype(BF16), NT)
                state[h] = jnp.exp(G["Lend"][:, ln]) * st + _dot(vh, kd, TN)
                outs.append(o)
            o = jnp.concatenate(outs, axis=1)
            raw_ref[rows, :] = o
            ns = []
            for h in range(H):
                oh = outs[h]
                ns.append(oh * lax.rsqrt(jnp.mean(oh * oh, axis=-1, keepdims=True) + LN_EPS))
            n = jnp.concatenate(ns, axis=1)
            out_ref[rows, :] = n * gn_ref[...] * (xg * _sigmoid(xg))
            return carry

        lax.fori_loop(0, cpb, chunk, 0)

    col = lambda cb_: pl.BlockSpec((tb, W), lambda i: (i, cb_))
    row = pl.BlockSpec((tb, W), lambda i: (i, 0))
    return pl.pallas_call(
        body, name=name, grid=(S // tb,),
        in_specs=[col(2), col(3), col(4), col(5), pl.BlockSpec((R, W), lambda i: (0, 0)), pl.BlockSpec((1, W), lambda i: (0, 0))],
        out_specs=[row, row, pl.BlockSpec((cpb, H, HEAD_DIM, HEAD_DIM), lambda i: (i, 0, 0, 0))],
        out_shape=[jax.ShapeDtypeStruct((S, W), F32), jax.ShapeDtypeStruct((S, W), F32),
                   jax.ShapeDtypeStruct((nc, H, HEAD_DIM, HEAD_DIM), F32)],
        scratch_shapes=[pltpu.VMEM((H, HEAD_DIM, HEAD_DIM), F32)],
        compiler_params=_cparams("arbitrary"),
    )(u, u, u, u, lbl, gn)


def _hgrn_bwd(dout, dout_col, u, raw, states, lbl, gn, name, tb=256):
    S = u.shape[0]
    W = gn.shape[1]
    H = W // HEAD_DIM
    tb = _tile(S, tb)
    cpb = tb // CHUNK
    nb = S // tb
    R = lbl.shape[0]

    def body(do_ref, q_ref, f_ref, i_ref, g_ref, raw_ref, st_ref, lbl_ref, gn_ref,
             dq_ref, df_ref, di_ref, dg_ref, dgn_ref, dlbl_ref, dstate, dlb_acc):
        @pl.when(pl.program_id(0) == 0)
        def _():
            dstate[...] = jnp.zeros_like(dstate)
            dlb_acc[...] = jnp.zeros_like(dlb_acc)
            dgn_ref[...] = jnp.zeros_like(dgn_ref)

        lb, p = _lower_bound(lbl_ref[...])
        tril = _tri(CHUNK, True) > 0.5
        triu = _tri(CHUNK, False)
        gn_row = gn_ref[...]

        def chunk(cc, carry):
            c = cpb - 1 - cc
            rows = pl.ds(pl.multiple_of(c * CHUNK, CHUNK), CHUNK)
            xq, xf = q_ref[rows, :], f_ref[rows, :]
            G = _hgrn_gates(xq, xf, lb)
            v = i_ref[rows, :]
            xg = g_ref[rows, :]
            dy = do_ref[rows, :]
            o = raw_ref[rows, :]
            sgg = _sigmoid(xg)
            silu_g = xg * sgg
            do_parts, n_parts = [], []
            for h in range(H):
                ln = slice(h * HEAD_DIM, (h + 1) * HEAD_DIM)
                oh = o[:, ln]
                r = lax.rsqrt(jnp.mean(oh * oh, axis=-1, keepdims=True) + LN_EPS)
                nh = oh * r
                dn = dy[:, ln] * gn_row[:, ln] * silu_g[:, ln]
                do_parts.append(r * (dn - nh * jnp.mean(dn * nh, axis=-1, keepdims=True)))
                n_parts.append(nh)
            n = jnp.concatenate(n_parts, axis=1)
            dgn_ref[...] += jnp.sum(dy * n * silu_g, axis=0, keepdims=True)
            dg_ref[rows, :] = dy * n * gn_row * sgg * (1.0 + xg * (1.0 - sgg))
            dqe_p, dke_p, dkd_p, dv_p, dle_p = [], [], [], [], []
            for h in range(H):
                ln = slice(h * HEAD_DIM, (h + 1) * HEAD_DIM)
                qe, ke, kd, vh = G["qe"][:, ln].astype(BF16), G["ke"][:, ln].astype(BF16), G["kd"][:, ln].astype(BF16), v[:, ln].astype(BF16)
                doh = do_parts[h].astype(BF16)
                st = st_ref[c, h]
                dst = dstate[h]
                dec = jnp.exp(G["Lend"][:, ln])
                A = jnp.where(tril, _dot(qe, ke, NT), 0.0).astype(BF16)
                dA = jnp.where(tril, _dot(doh, vh, NT), 0.0).astype(BF16)
                dv_p.append(_dot(A, doh, TN) + _dot(kd, dst.astype(BF16), NT))
                dqe_p.append(_dot(dA, ke, NN) + _dot(doh, st.astype(BF16), NN))
                dke_p.append(_dot(dA, qe, TN))
                dkd_p.append(_dot(vh, dst.astype(BF16), NN))
                dle_p.append(jnp.sum(dst * dec * st, axis=0, keepdims=True))
                dstate[h] = dec * dst + _dot(doh, qe, TN)
            dqe = jnp.concatenate(dqe_p, axis=1)
            dke = jnp.concatenate(dke_p, axis=1)
            dkd = jnp.concatenate(dkd_p, axis=1)
            qe_r, ke_r, kd_r = _round_bf16(G["qe"]), _round_bf16(G["ke"]), _round_bf16(G["kd"])
            dLend = jnp.concatenate(dle_p, axis=1) + jnp.sum(dkd * kd_r, axis=0, keepdims=True)
            dL = dqe * qe_r - dke * ke_r - dkd * kd_r
            dlogf = _dot(triu, dL, NN, HI) + dLend
            dkk = dke * G["enL"] + dkd * G["eLe"]
            dfv = dlogf / G["f"] - dkk
            sf = G["sf"]
            df_ref[rows, :] = dfv * (1.0 - lb) * sf * (1.0 - sf)
            dlb_acc[...] += jnp.sum(dfv * (1.0 - sf), axis=0, keepdims=True)
            sq = G["sq"]
            dq_ref[rows, :] = dqe * G["eL"] * sq * (1.0 + xq * (1.0 - sq))
            di_ref[rows, :] = jnp.concatenate(dv_p, axis=1)
            return carry

        lax.fori_loop(0, cpb, chunk, 0)

        onehot0 = (lax.broadcasted_iota(jnp.int32, (R, W), 0) == 0).astype(F32)
        dlbl_ref[...] = p * (onehot0 - p[0:1, :]) * dlb_acc[...]

    rev = lambda i: nb - 1 - i
    col = lambda cb_: pl.BlockSpec((tb, W), lambda i: (rev(i), cb_))
    row = pl.BlockSpec((tb, W), lambda i: (rev(i), 0))
    par = pl.BlockSpec((1, W), lambda i: (0, 0))
    parR = pl.BlockSpec((R, W), lambda i: (0, 0))
    return pl.pallas_call(
        body, name=name, grid=(nb,),
        in_specs=[pl.BlockSpec((tb, W), lambda i: (rev(i), dout_col)), col(2), col(3), col(4), col(5), row,
                  pl.BlockSpec((cpb, H, HEAD_DIM, HEAD_DIM), lambda i: (rev(i), 0, 0, 0)), parR, par],
        out_specs=[row, row, row, row, par, parR],
        out_shape=[jax.ShapeDtypeStruct((S, W), F32)] * 4 + [jax.ShapeDtypeStruct((1, W), F32), jax.ShapeDtypeStruct((R, W), F32)],
        scratch_shapes=[pltpu.VMEM((H, HEAD_DIM, HEAD_DIM), F32), pltpu.VMEM((1, W), F32)],
        compiler_params=_cparams("arbitrary"),
    )(dout, u, u, u, u, raw, states, lbl, gn)


def _fgate_fwd(xb, wft, bf, name, ts=512):
    S, D = xb.shape
    H = wft.shape[0]
    ts = _tile(S, ts)

    def body(x_ref, w_ref, b_ref, lg_ref, F_ref, carry):
        @pl.when(pl.program_id(0) == 0)
        def _():
            carry[...] = jnp.zeros_like(carry)

        lg = _dot(_round_bf16(w_ref[...]), _round_bf16(x_ref[...]), NT, HI) + b_ref[...]
        lg_ref[...] = lg
        ls = jnp.minimum(lg, 0.0) - jnp.log(1.0 + jnp.exp(-jnp.abs(lg)))
        F = _dot(ls, _tri(ts, False), NN, HI) + carry[...]
        F_ref[...] = F
        carry[...] = F[:, ts - 1:ts]

    return pl.pallas_call(
        body, name=name, grid=(S // ts,),
        in_specs=[pl.BlockSpec((ts, D), lambda i: (i, 0)), pl.BlockSpec((H, D), lambda i: (0, 0)), pl.BlockSpec((H, 1), lambda i: (0, 0))],
        out_specs=[pl.BlockSpec((H, ts), lambda i: (0, i))] * 2,
        out_shape=[jax.ShapeDtypeStruct((H, S), F32)] * 2,
        scratch_shapes=[pltpu.VMEM((H, 1), F32)],
        compiler_params=_cparams("arbitrary"),
    )(xb, wft, bf)


def _fgate_bwd(dF, lg, name, ts=512):
    H, S = dF.shape
    ts = _tile(S, ts)
    nb = S // ts

    def body(dF_ref, lg_ref, dl_ref, db_ref, carry):
        @pl.when(pl.program_id(0) == 0)
        def _():
            carry[...] = jnp.zeros_like(carry)
            db_ref[...] = jnp.zeros_like(db_ref)

        dls = _dot(dF_ref[...], _tri(ts, True), NN, HI) + carry[...]
        carry[...] = dls[:, 0:1]
        dl = dls * _sigmoid(-lg_ref[...])
        dl_ref[...] = dl
        db_ref[...] += jnp.sum(dl, axis=1, keepdims=True)

    blk = pl.BlockSpec((H, ts), lambda i: (0, nb - 1 - i))
    return pl.pallas_call(
        body, name=name, grid=(nb,),
        in_specs=[blk, blk],
        out_specs=[blk, pl.BlockSpec((H, 1), lambda i: (0, 0))],
        out_shape=[jax.ShapeDtypeStruct((H, S), F32), jax.ShapeDtypeStruct((H, 1), F32)],
        scratch_shapes=[pltpu.VMEM((H, 1), F32)],
        compiler_params=_cparams("arbitrary"),
    )(dF, lg)


def _causal_keep(i, j, tq, tk):
    rows = lax.broadcasted_iota(jnp.int32, (tq, tk), 0)
    cols = lax.broadcasted_iota(jnp.int32, (tq, tk), 1)
    return jnp.logical_or(j < i, cols <= rows)


def _causal_grid(S, t):
    n = S // t
    assert n % 2 == 0, (S, t)

    def by_query(r, c):
        second = c > r
        return jnp.where(second, n - 1 - r, r), jnp.where(second, c - r - 1, c)

    def by_key(r, c):
        second = c >= n - r
        return jnp.where(second, c - 1, r + c), jnp.where(second, n - 1 - r, r)

    return n, (n // 2, n + 1), by_query, by_key


def _fox_scores(q_ref, k_ref, fk_ref, i, j, t, scale):
    s = _dot(q_ref[...].astype(BF16), k_ref[...].astype(BF16), NT) * scale - fk_ref[...]
    return lax.cond(i == j, lambda x: jnp.where(_causal_keep(i, j, t, t), x, NEG), lambda x: x, s)


def _softmax_bwd(p, pb, dp, delta):
    return pb.astype(F32) * dp - p * delta


def _fox_fwd(u, F3, H, name, carry=None, t=512):
    S = u.shape[0]
    W = H * HEAD_DIM
    t = _tile(S, t)
    n, tri, ij, _ = _causal_grid(S, t)
    scale = HEAD_DIM ** -0.5

    def body(q_ref, k_ref, v_ref, fk_ref, o_ref, lse_ref, m_s, l_s, acc_s):
        i, j = ij(pl.program_id(1), pl.program_id(2))

        @pl.when(j == 0)
        def _():
            m_s[...] = jnp.full_like(m_s, NEG)
            l_s[...] = jnp.zeros_like(l_s)
            acc_s[...] = jnp.zeros_like(acc_s)

        s = _fox_scores(q_ref, k_ref, fk_ref, i, j, t, scale)
        m_new = jnp.maximum(m_s[...], jnp.max(s, axis=-1, keepdims=True))
        a = jnp.exp(m_s[...] - m_new)
        p = jnp.exp(s - m_new)
        hi = p.astype(BF16)
        lo = (p - hi.astype(F32)).astype(BF16)
        v = v_ref[...].astype(BF16)
        l_s[...] = a * l_s[...] + jnp.sum(p, axis=-1, keepdims=True)
        acc_s[...] = a * acc_s[...] + (_dot(hi, v, NN) + _dot(lo, v, NN))
        m_s[...] = m_new

        @pl.when(j == i)
        def _():
            o_ref[...] = acc_s[...] / l_s[...]
            lse_ref[...] = m_s[...] + jnp.log(l_s[...])

    qblk = pl.BlockSpec((t, HEAD_DIM), lambda h, r, c: (ij(r, c)[0], h))
    kv = lambda off: pl.BlockSpec((t, HEAD_DIM), lambda h, r, c: (ij(r, c)[1], off * H + h))
    outs, carried = _call(
        body, name=name, grid=(H,) + tri,
        in_specs=[qblk, kv(1), kv(2), pl.BlockSpec((None, 1, t), lambda h, r, c: (h, 0, ij(r, c)[1]))],
        out_specs=[qblk, pl.BlockSpec((None, t, 1), lambda h, r, c: (h, ij(r, c)[0], 0))],
        out_shape=[jax.ShapeDtypeStruct((S, W), F32), jax.ShapeDtypeStruct((H, S, 1), F32)],
        scratch_shapes=[pltpu.VMEM((t, 1), F32), pltpu.VMEM((t, 1), F32), pltpu.VMEM((t, HEAD_DIM), F32)],
        sem=("parallel", "parallel", "arbitrary"), args=(u, u, u, F3), carry=carry)
    return outs[0], outs[1], carried


def _fox_bwd(u, F3, o, do, do_off, lse, H, name, t=512):
    S = u.shape[0]
    W = H * HEAD_DIM
    t = _tile(S, t)
    n, tri, _, ij = _causal_grid(S, t)
    scale = HEAD_DIM ** -0.5

    def body(q_ref, k_ref, v_ref, fk_ref, o_ref, do_ref, lse_ref, dq_ref, dk_ref, dv_ref, dF_ref, dk_s, dv_s, dF_s):
        r, c = pl.program_id(1), pl.program_id(2)
        i, j = ij(r, c)

        @pl.when(jnp.logical_and(r == 0, c == 0))
        def _():
            dq_ref[...] = jnp.zeros_like(dq_ref)

        @pl.when(i == j)
        def _():
            dk_s[...] = jnp.zeros_like(dk_s)
            dv_s[...] = jnp.zeros_like(dv_s)
            dF_s[...] = jnp.zeros_like(dF_s)

        q = q_ref[...].astype(BF16)
        k = k_ref[...].astype(BF16)
        dob = do_ref[...].astype(BF16)
        p = jnp.exp(_fox_scores(q_ref, k_ref, fk_ref, i, j, t, scale) - lse_ref[...])
        dp = _dot(dob, v_ref[...].astype(BF16), NT)
        delta = jnp.sum(dob.astype(F32) * o_ref[...], axis=-1, keepdims=True)
        ds = p * (dp - delta)
        dsb = ds.astype(BF16)
        dv_s[...] += _dot(p.astype(BF16), dob, TN)
        dk_s[...] += _dot(dsb, q, TN)
        dF_s[...] -= jnp.sum(ds, axis=0, keepdims=True)
        rows = pl.ds(pl.multiple_of(i * t, t), t)
        dq_ref[rows, :] += _dot(dsb, k, NN) * scale

        @pl.when(i == n - 1)
        def _():
            dk_ref[...] = dk_s[...] * scale
            dv_ref[...] = dv_s[...]
            dF_ref[...] = dF_s[...]

    qblk = pl.BlockSpec((t, HEAD_DIM), lambda h, r, c: (ij(r, c)[0], h))
    kv = lambda off: pl.BlockSpec((t, HEAD_DIM), lambda h, r, c: (ij(r, c)[1], off * H + h))
    oblk = pl.BlockSpec((t, HEAD_DIM), lambda h, r, c: (ij(r, c)[1], h))
    fblk = pl.BlockSpec((None, 1, t), lambda h, r, c: (h, 0, ij(r, c)[1]))
    return pl.pallas_call(
        body, name=name, grid=(H,) + tri,
        in_specs=[qblk, kv(1), kv(2), fblk, qblk, pl.BlockSpec((t, HEAD_DIM), lambda h, r, c: (ij(r, c)[0], do_off + h)),
                  pl.BlockSpec((None, t, 1), lambda h, r, c: (h, ij(r, c)[0], 0))],
        out_specs=[pl.BlockSpec((S, HEAD_DIM), lambda h, r, c: (0, h)), oblk, oblk, fblk],
        out_shape=[jax.ShapeDtypeStruct((S, W), F32)] * 3 + [jax.ShapeDtypeStruct((H, 1, S), F32)],
        scratch_shapes=[pltpu.VMEM((t, HEAD_DIM), F32), pltpu.VMEM((t, HEAD_DIM), F32), pltpu.VMEM((1, t), F32)],
        compiler_params=_cparams("parallel", "arbitrary", "arbitrary"),
    )(u, u, u, F3, o, do, lse)


def _rel_index_matrix():
    a = lax.broadcasted_iota(jnp.int32, (REL_TABLE, CA_SKEW), 0)
    j = lax.broadcasted_iota(jnp.int32, (REL_TABLE, CA_SKEW), 1)
    rel = jnp.where(j < CA_WIN, CA_LEFT_CHUNKS * CHUNK - j, REL_CLIP)
    idx = jnp.clip(jnp.minimum(rel, REL_CLIP) + (CHUNK - 1), 0, REL_TABLE - 1)
    return (a == idx).astype(F32)


def _skew(x, sign):
    r = lax.broadcasted_iota(jnp.int32, x.shape, 0)
    for b in range(int(math.log2(CA_TILE))):
        sh = (1 << b) if sign > 0 else CA_SKEW - (1 << b)
        x = jnp.where((r >> b) & 1 == 1, pltpu.roll(x, sh, 1), x)
    return x


def _band_valid():
    shift = int(math.log2(CHUNK))
    r = lax.broadcasted_iota(jnp.int32, (CA_TILE, CA_WIN), 0) >> shift
    m = lax.broadcasted_iota(jnp.int32, (CA_TILE, CA_WIN), 1) >> shift
    return jnp.logical_and(m >= r, m <= r + CA_LEFT_CHUNKS)


def _relbias_fwd(table, name):
    H = table.shape[0]

    def body(t_ref, b_ref):
        rowv = _dot(t_ref[...], _rel_index_matrix(), NN, HI)
        valid = _band_valid()
        for h in range(H):
            x = _skew(jnp.broadcast_to(rowv[h:h + 1, :], (CA_TILE, CA_SKEW)), +1)
            b_ref[h] = jnp.where(valid, x[:, :CA_WIN], NEG)

    return pl.pallas_call(
        body, name=name,
        out_shape=jax.ShapeDtypeStruct((H, CA_TILE, CA_WIN), F32),
        compiler_params=pltpu.CompilerParams(vmem_limit_bytes=VMEM_LIMIT_V7X),
    )(table)


def _relbias_bwd(dB, name):
    H = dB.shape[0]
    HP = -(-H // 8) * 8

    def body(d_ref, dt_ref, rows):
        rows[...] = jnp.zeros_like(rows)
        for h in range(H):
            x = jnp.concatenate([d_ref[h], jnp.zeros((CA_TILE, CA_SKEW - CA_WIN), F32)], axis=1)
            rows[pl.ds(h, 1), :] = jnp.sum(_skew(x, -1), axis=0, keepdims=True)
        dt_ref[...] = _dot(rows[...], _rel_index_matrix(), NT, HI)[:H]

    return pl.pallas_call(
        body, name=name,
        out_shape=jax.ShapeDtypeStruct((H, REL_TABLE), F32),
        scratch_shapes=[pltpu.VMEM((HP, CA_SKEW), F32)],
        compiler_params=pltpu.CompilerParams(vmem_limit_bytes=VMEM_LIMIT_V7X),
    )(dB)


CA_PIECES = CA_WIN // CA_TILE


def _ca_fwd(u, bias, H, name):
    S = u.shape[0]
    W = H * HEAD_DIM
    T = CA_TILE
    n = S // T
    scale = HEAD_DIM ** -0.5

    def body(q_ref, k0, k1, k2, v0, v1, v2, b_ref, o_ref, lse_ref):
        i = pl.program_id(1)
        q = q_ref[...].astype(BF16)
        ss = []
        for pce, k_ref in enumerate((k0, k1, k2)):
            s = _dot(q, k_ref[...].astype(BF16), NT) * scale + b_ref[:, pce * T:(pce + 1) * T]
            ss.append(jnp.where(i + pce >= CA_PIECES - 1, s, NEG))
        m = jnp.maximum(jnp.maximum(jnp.max(ss[0], -1, keepdims=True), jnp.max(ss[1], -1, keepdims=True)), jnp.max(ss[2], -1, keepdims=True))
        ps = [jnp.exp(s - m) for s in ss]
        l = sum(jnp.sum(p, -1, keepdims=True) for p in ps)
        inv = 1.0 / l
        o_ref[...] = sum(_dot((p * inv).astype(BF16), v_ref[...].astype(BF16), NN) for p, v_ref in zip(ps, (v0, v1, v2)))
        lse_ref[...] = m + jnp.log(l)

    qblk = pl.BlockSpec((T, HEAD_DIM), lambda h, i: (i, h))
    kv = lambda off, back: pl.BlockSpec((T, HEAD_DIM), lambda h, i: (jnp.maximum(i - back, 0), off * H + h))
    return pl.pallas_call(
        body, name=name, grid=(H, n),
        in_specs=[qblk, kv(1, 2), kv(1, 1), kv(1, 0), kv(2, 2), kv(2, 1), kv(2, 0),
                  pl.BlockSpec((None, T, CA_WIN), lambda h, i: (h, 0, 0))],
        out_specs=[qblk, pl.BlockSpec((None, T, 1), lambda h, i: (h, i, 0))],
        out_shape=[jax.ShapeDtypeStruct((S, W), F32), jax.ShapeDtypeStruct((H, S, 1), F32)],
        compiler_params=_cparams("parallel", "arbitrary"),
    )(u, u, u, u, u, u, u, bias)


def _ca_bwd(u, bias, o, do, do_off, lse, H, name):
    S = u.shape[0]
    W = H * HEAD_DIM
    T = CA_TILE
    n = S // T
    scale = HEAD_DIM ** -0.5

    def body(q_ref, k0, k1, k2, v0, v1, v2, b_ref, o_ref, do_ref, lse_ref, dq_ref, dk_ref, dv_ref, db_ref):
        i = pl.program_id(1)

        @pl.when(i == 0)
        def _():
            db_ref[...] = jnp.zeros_like(db_ref)
            dk_ref[...] = jnp.zeros_like(dk_ref)
            dv_ref[...] = jnp.zeros_like(dv_ref)

        q = q_ref[...].astype(BF16)
        dob = do_ref[...].astype(BF16)
        delta = jnp.sum(dob.astype(F32) * o_ref[...], axis=-1, keepdims=True)
        dq = jnp.zeros((T, HEAD_DIM), F32)
        for pce, (k_ref, v_ref) in enumerate(((k0, v0), (k1, v1), (k2, v2))):
            k = k_ref[...].astype(BF16)
            s = _dot(q, k, NT) * scale + b_ref[:, pce * T:(pce + 1) * T]
            p = jnp.where(i + pce >= CA_PIECES - 1, jnp.exp(s - lse_ref[...]), 0.0)
            pb = p.astype(BF16)
            ds = _softmax_bwd(p, pb, _dot(dob, v_ref[...].astype(BF16), NT), delta)
            dsb = ds.astype(BF16)
            db_ref[:, pce * T:(pce + 1) * T] += ds
            dq = dq + _dot(dsb, k, NN)
            rows = pl.ds(pl.multiple_of(jnp.maximum(i - (CA_PIECES - 1) + pce, 0) * T, T), T)
            dk_ref[rows, :] += _dot(dsb, q, TN) * scale
            dv_ref[rows, :] += _dot(pb, dob, TN)
        dq_ref[...] = dq * scale

    qblk = pl.BlockSpec((T, HEAD_DIM), lambda h, i: (i, h))
    kv = lambda off, back: pl.BlockSpec((T, HEAD_DIM), lambda h, i: (jnp.maximum(i - back, 0), off * H + h))
    bblk = pl.BlockSpec((None, T, CA_WIN), lambda h, i: (h, 0, 0))
    head = pl.BlockSpec((S, HEAD_DIM), lambda h, i: (0, h))
    return pl.pallas_call(
        body, name=name, grid=(H, n),
        in_specs=[qblk, kv(1, 2), kv(1, 1), kv(1, 0), kv(2, 2), kv(2, 1), kv(2, 0), bblk,
                  qblk, pl.BlockSpec((T, HEAD_DIM), lambda h, i: (i, do_off + h)), pl.BlockSpec((None, T, 1), lambda h, i: (h, i, 0))],
        out_specs=[qblk, head, head, bblk],
        out_shape=[jax.ShapeDtypeStruct((S, W), F32)] * 3 + [jax.ShapeDtypeStruct((H, T, CA_WIN), F32)],
        compiler_params=_cparams("parallel", "arbitrary"),
    )(u, u, u, u, u, u, u, bias, o, do, lse)


def _sum_parts(parts, name):
    _, R, C = parts.shape

    def body(p_ref, o_ref):
        acc = p_ref[0].astype(F32)
        for d in range(1, N_DEV):
            acc = acc + p_ref[d].astype(F32)
        o_ref[...] = acc

    return pl.pallas_call(
        body, name=name, out_shape=jax.ShapeDtypeStruct((R, C), F32),
        compiler_params=pltpu.CompilerParams(vmem_limit_bytes=VMEM_LIMIT_V7X),
    )(parts)


def _adamw(parts, w, m, v, name, tr=256):
    P, R, C = parts.shape
    tr = _tile(R, tr)
    c1 = 1.0 / (1.0 - ADAM_B1 ** ADAM_STEP)
    c2 = 1.0 / (1.0 - ADAM_B2 ** ADAM_STEP)

    def body(p_ref, w_ref, m_ref, v_ref, g_ref, d_ref, nm_ref, nv_ref):
        g = p_ref[0].astype(F32)
        for d in range(1, P):
            g = g + p_ref[d].astype(F32)
        nm = ADAM_B1 * m_ref[...] + (1.0 - ADAM_B1) * g
        nv = ADAM_B2 * v_ref[...] + (1.0 - ADAM_B2) * (g * g)
        g_ref[...] = g
        nm_ref[...] = nm
        nv_ref[...] = nv
        d_ref[...] = -ADAM_LR * ((nm * c1) / (jnp.sqrt(nv * c2) + ADAM_EPS) + ADAM_WD * w_ref[...])

    blk = pl.BlockSpec((tr, C), lambda i: (i, 0))
    return pl.pallas_call(
        body, name=name, grid=(R // tr,),
        in_specs=[pl.BlockSpec((P, tr, C), lambda i: (0, i, 0)), blk, blk, blk],
        out_specs=[blk] * 4,
        out_shape=[jax.ShapeDtypeStruct((R, C), F32)] * 4,
        compiler_params=_cparams("parallel"),
    )(parts, w, m, v)


def _peer(d):
    x, y, c = lax.axis_index("x"), lax.axis_index("y"), lax.axis_index("c")
    px = (1 - x) if d & 4 else x
    py = (1 - y) if d & 2 else y
    pc = (1 - c) if d & 1 else c
    return (px, py, pc), 4 * px + 2 * py + pc


N_PEER = N_DEV - 1


def _exchange_copies(ins, outs, sems, scatter):
    send_sems, recv_sems, local_sems = sems
    me = 4 * lax.axis_index("x") + 2 * lax.axis_index("y") + lax.axis_index("c")
    starts, waits = [], []
    for t in range(len(ins)):
        loc = pltpu.make_async_copy(ins[t].at[me] if scatter else ins[t], outs[t].at[me], local_sems.at[t])
        starts.append(loc.start)
        waits.append(loc.wait)
        for d in range(1, N_DEV):
            peer, pidx = _peer(d)
            src = ins[t].at[pidx] if scatter else ins[t]
            k = t * N_PEER + d - 1
            common = dict(src_ref=src, send_sem=send_sems.at[k], recv_sem=recv_sems.at[k], device_id=peer, device_id_type=MESH_ID)
            starts.append(pltpu.make_async_remote_copy(dst_ref=outs[t].at[me], **common).start)
            waits.append(pltpu.make_async_remote_copy(dst_ref=outs[t].at[pidx], **common).wait)
    return starts, waits


def _exchange_scratch(n):
    return [pltpu.SemaphoreType.DMA((n * N_PEER,)), pltpu.SemaphoreType.DMA((n * N_PEER,)), pltpu.SemaphoreType.DMA((n,))]


def _exchange_shapes(arrays, scatter):
    return [jax.ShapeDtypeStruct(a.shape if scatter else (N_DEV,) + a.shape, a.dtype) for a in arrays]


def _exchange(arrays, scatter, name):
    n = len(arrays)

    def body(*refs):
        starts, waits = _exchange_copies(refs[:n], refs[n:2 * n], refs[2 * n:], scatter)
        for f in starts:
            f()
        for f in waits:
            f()

    hbm = pl.BlockSpec(memory_space=pltpu.HBM)
    return pl.pallas_call(
        body, name=name,
        in_specs=[hbm] * n, out_specs=[hbm] * n,
        out_shape=_exchange_shapes(arrays, scatter),
        scratch_shapes=_exchange_scratch(n),
        compiler_params=pltpu.CompilerParams(has_side_effects=True),
    )(*arrays)


def _call(body, *, name, grid, in_specs, out_specs, out_shape, scratch_shapes, sem, args, carry=None):
    if carry is None:
        outs = pl.pallas_call(body, name=name, grid=grid, in_specs=in_specs, out_specs=out_specs, out_shape=out_shape,
                              scratch_shapes=scratch_shapes, compiler_params=_cparams(*sem))(*args)
        return list(outs), []
    arrays, scatter = carry
    nc, n_in, n_out, n_scr = len(arrays), len(in_specs), len(out_specs), len(scratch_shapes)

    def wrapped(*refs):
        ins, cins = refs[:n_in], refs[n_in:n_in + nc]
        outs, couts = refs[n_in + nc:n_in + nc + n_out], refs[n_in + nc + n_out:n_in + 2 * nc + n_out]
        scr = refs[n_in + 2 * nc + n_out:n_in + 2 * nc + n_out + n_scr]
        ids = [pl.program_id(k) for k in range(len(grid))]
        first = functools.reduce(jnp.logical_and, [i == 0 for i in ids])
        last = functools.reduce(jnp.logical_and, [i == g - 1 for i, g in zip(ids, grid)])
        starts, waits = _exchange_copies(cins, couts, refs[-3:], scatter)

        @pl.when(first)
        def _():
            for f in starts:
                f()

        body(*ins, *outs, *scr)

        @pl.when(last)
        def _():
            for f in waits:
                f()

    hbm = pl.BlockSpec(memory_space=pltpu.HBM)
    res = pl.pallas_call(
        wrapped, name=name, grid=grid,
        in_specs=list(in_specs) + [hbm] * nc, out_specs=list(out_specs) + [hbm] * nc,
        out_shape=list(out_shape) + _exchange_shapes(arrays, scatter),
        scratch_shapes=list(scratch_shapes) + _exchange_scratch(nc),
        compiler_params=_cparams(*(("arbitrary",) * len(grid))),
    )(*args, *arrays)
    return list(res[:n_out]), list(res[n_out:])


def _relu2_fwd(acc):
    r = jnp.maximum(acc, 0.0)
    return acc, r * r


def _relu2_bwd(acc, z):
    return (acc * 2.0 * jnp.maximum(z.astype(F32), 0.0),)


def _add_scaled(scale):
    def epi(acc, t):
        return (acc + scale * t,)
    return epi


def _local_step(x, target, P, plan):
    S, D = x.shape
    H = (D // 2) // HEAD_DIM
    W = H * HEAD_DIM
    alpha = (2 * 2) ** 0.25
    g = {}

    def mm(a, b, *, name, **kw):
        res, carried = _mm(a, b, name=name, carry=plan.carry(name, g), **kw)
        plan.arrived(name, carried, P, g)
        return res

    u0 = mm(x, P["ev_w_in"], mode="nn", name="ev_in")
    a_out, cv = _conv_fwd(u0, P["ev_conv_w"], P["ev_conv_b"], P["ev_conv_ln_g"], P["ev_conv_ln_b"], "conv_fwd")
    b_out, h_raw, h_states = _hgrn_fwd(u0, P["hgrn_lb_logits"], P["ev_gnorm_g"], "hgrn_fwd")
    cat0 = jnp.concatenate([a_out.astype(BF16), b_out.astype(BF16)], axis=1)
    mix0 = mm(cat0, P["ev_w_out"], mode="nn", name="ev_out")
    x1, xh1, rs1 = _ln_fwd(x, mix0, P["ln_mix_g"][0:1], P["ln_mix_b"][0:1], alpha, "ln_mix0")
    z0, hh0 = mm(x1, P["mlp_w1_0"], mode="nn", name="mlp_up0", out_dtypes=(BF16, BF16), epi=_relu2_fwd)
    m0 = mm(hh0, P["mlp_w2_0"], mode="nn", name="mlp_down0")
    x2, xh2, rs2 = _ln_fwd(x1, m0, P["ln_mlp_g"][0:1], P["ln_mlp_b"][0:1], alpha, "ln_mlp0")

    uc = mm(x2, P["od_w_c"], mode="nn", name="od_in_c")
    ud = mm(x2, P["od_w_d"], mode="nn", name="od_in_d")
    f_logit, F = _fgate_fwd(x2, P["od_w_f_t"], P["fox_b_f"].reshape(H, 1), "fgate_fwd")
    F3 = F.reshape(H, 1, S)
    c_out, c_lse, carried = _fox_fwd(uc, F3, H, "fox_fwd", carry=plan.carry("fox_fwd", g))
    plan.arrived("fox_fwd", carried, P, g)
    bias = _relbias_fwd(P["rel_bias"], "relbias_fwd")
    d_out, d_lse = _ca_fwd(ud, bias, H, "ca_fwd")
    cat1 = jnp.concatenate([c_out.astype(BF16), d_out.astype(BF16)], axis=1)
    mix1 = mm(cat1, P["od_w_out"], mode="nn", name="od_out")
    x3, xh3, rs3 = _ln_fwd(x2, mix1, P["ln_mix_g"][1:2], P["ln_mix_b"][1:2], alpha, "ln_mix1")
    z1, hh1 = mm(x3, P["mlp_w1_1"], mode="nn", name="mlp_up1", out_dtypes=(BF16, BF16), epi=_relu2_fwd)
    m1 = mm(hh1, P["mlp_w2_1"], mode="nn", name="mlp_down1")
    x4, xh4, rs4 = _ln_fwd(x3, m1, P["ln_mlp_g"][1:2], P["ln_mlp_b"][1:2], alpha, "ln_mlp1")

    dy, loss = _loss_head(x4, target, "loss_head")

    dzm1, dg_, db_ = _ln_bwd(dy, None, 0.0, xh4, rs4, P["ln_mlp_g"][1:2], "ln_mlp1_bwd")
    g["ln_mlp_g1"], g["ln_mlp_b1"] = dg_, db_
    g["mlp_w2_1"] = mm(hh1, dzm1, mode="tn", name="mlp_down1_dw", out_dtypes=(BF16,))
    dz1 = mm(dzm1, P["mlp_w2_1"], mode="nt", name="mlp_down1_dx", out_dtypes=(BF16,), extras=(z1,), epi=_relu2_bwd)
    g["mlp_w1_1"] = mm(x3, dz1, mode="tn", name="mlp_up1_dw", out_dtypes=(BF16,))
    dx3 = mm(dz1, P["mlp_w1_1"], mode="nt", name="mlp_up1_dx", extras=(dzm1,), epi=_add_scaled(alpha))
    dzx1, dg_, db_ = _ln_bwd(dx3, None, 0.0, xh3, rs3, P["ln_mix_g"][1:2], "ln_mix1_bwd")
    g["ln_mix_g1"], g["ln_mix_b1"] = dg_, db_
    g["od_w_out"] = mm(cat1, dzx1, mode="tn", name="od_out_dw", out_dtypes=(BF16,))
    dcat1 = mm(dzx1, P["od_w_out"], mode="nt", name="od_out_dx")
    dq_d, dk_d, dv_d, dbias = _ca_bwd(ud, bias, d_out, dcat1, H, d_lse, H, "ca_bwd")
    g["rel_bias"] = _relbias_bwd(dbias, "relbias_bwd")
    dud = jnp.concatenate([dq_d, dk_d, dv_d], axis=1)
    dq_c, dk_c, dv_c, dF3 = _fox_bwd(uc, F3, c_out, dcat1, 0, c_lse, H, "fox_bwd")
    duc = jnp.concatenate([dq_c, dk_c, dv_c], axis=1)
    dfl, dbf = _fgate_bwd(dF3.reshape(H, S), f_logit, "fgate_bwd")
    g["fox_b_f"] = dbf.reshape(1, H)
    g["od_w_c"] = mm(x2, duc, mode="tn", name="od_in_c_dw", out_dtypes=(BF16,))
    g["od_w_d"] = mm(x2, dud, mode="tn", name="od_in_d_dw", out_dtypes=(BF16,))
    g["od_w_f_t"] = mm(dfl, x2, mode="nn", name="od_in_f_dw", out_dtypes=(BF16,), exact_products=True)
    dx2 = mm(duc, P["od_w_c"], mode="nt", name="od_in_c_dx", extras=(dzx1,), epi=_add_scaled(alpha))
    dx2 = mm(dud, P["od_w_d"], mode="nt", name="od_in_d_dx", extras=(dx2,), epi=_add_scaled(1.0))
    dx2 = mm(dfl, P["od_w_f_t"], mode="tn", name="od_in_f_dx", extras=(dx2,), epi=_add_scaled(1.0), exact_products=True)

    dzm0, dg_, db_ = _ln_bwd(dx2, None, 0.0, xh2, rs2, P["ln_mlp_g"][0:1], "ln_mlp0_bwd")
    g["ln_mlp_g0"], g["ln_mlp_b0"] = dg_, db_
    g["mlp_w2_0"] = mm(hh0, dzm0, mode="tn", name="mlp_down0_dw", out_dtypes=(BF16,))
    dz0 = mm(dzm0, P["mlp_w2_0"], mode="nt", name="mlp_down0_dx", out_dtypes=(BF16,), extras=(z0,), epi=_relu2_bwd)
    g["mlp_w1_0"] = mm(x1, dz0, mode="tn", name="mlp_up0_dw", out_dtypes=(BF16,))
    dx1 = mm(dz0, P["mlp_w1_0"], mode="nt", name="mlp_up0_dx", extras=(dzm0,), epi=_add_scaled(alpha))
    dzx0, dg_, db_ = _ln_bwd(dx1, None, 0.0, xh1, rs1, P["ln_mix_g"][0:1], "ln_mix0_bwd")
    g["ln_mix_g0"], g["ln_mix_b0"] = dg_, db_
    g["ev_w_out"] = mm(cat0, dzx0, mode="tn", name="ev_out_dw", out_dtypes=(BF16,))
    dcat0 = mm(dzx0, P["ev_w_out"], mode="nt", name="ev_out_dx")
    dcv, g["ev_conv_ln_g"], g["ev_conv_ln_b"], g["ev_conv_b"] = _conv_bwd_norm(dcat0, 0, cv, P["ev_conv_ln_g"], P["ev_conv_ln_b"], "conv_bwd_norm")
    du_a, du_g, g["ev_conv_w"] = _conv_bwd_taps(dcv, u0, P["ev_conv_w"], "conv_bwd_taps")
    dhq, dhf, dhi, dhg, g["ev_gnorm_g"], g["hgrn_lb_logits"] = _hgrn_bwd(dcat0, 1, u0, h_raw, h_states, P["hgrn_lb_logits"], P["ev_gnorm_g"], "hgrn_bwd")
    du0 = jnp.concatenate([du_a, du_g, dhq, dhf, dhi, dhg], axis=1)
    g["ev_w_in"] = mm(x, du0, mode="tn", name="ev_in_dw", out_dtypes=(BF16,))
    dx0 = mm(du0, P["ev_w_in"], mode="nt", name="ev_in_dx", extras=(dzx0,), epi=_add_scaled(alpha))
    return loss, dx0, g


_NAMES = ['ev_w_in', 'ev_conv_w', 'ev_conv_b', 'ev_conv_ln_g', 'ev_conv_ln_b', 'hgrn_lb_logits', 'ev_gnorm_g', 'ev_w_out',
          'od_w_in', 'fox_b_f', 'rel_bias', 'od_w_out', 'ln_mix_g', 'ln_mix_b', 'mlp_w1', 'mlp_w2', 'ln_mlp_g', 'ln_mlp_b']
_SMALL = ['ev_conv_b', 'ev_conv_ln_g', 'ev_conv_ln_b', 'hgrn_lb_logits', 'ev_gnorm_g', 'fox_b_f', 'ln_mix_g', 'ln_mix_b',
          'ln_mlp_g', 'ln_mlp_b', 'ev_conv_w', 'rel_bias']
_PACK_COLS = 2048


def _cols_to_full(gathered):
    nd, K, n = gathered.shape
    return jnp.transpose(gathered, (1, 0, 2)).reshape(K, nd * n)


def _full_to_cols(full):
    K, N = full.shape
    return jnp.transpose(full.reshape(K, N_DEV, N // N_DEV), (1, 0, 2))


_ROW_SHARDED = ("ev_w_out", "od_w_out", "mlp_w2_0", "mlp_w2_1")


def _full_weights(name, gathered, heads):
    if name in _ROW_SHARDED:
        return {name: gathered.reshape(-1, gathered.shape[-1])}
    full = _cols_to_full(gathered)
    if name != "od_w_in":
        return {name: full}
    w = heads * HEAD_DIM
    return {"od_w_c": full[:, :3 * w], "od_w_f_t": jnp.transpose(full[:, 3 * w:3 * w + heads]), "od_w_d": full[:, 3 * w + heads:]}


def _grad_blocks(name, g):
    if name in _ROW_SHARDED:
        return g[name].reshape(N_DEV, -1, g[name].shape[-1])
    if name == "od_w_in":
        return _full_to_cols(jnp.concatenate([g["od_w_c"], jnp.transpose(g["od_w_f_t"]), g["od_w_d"]], axis=1))
    return _full_to_cols(g[name])


class _Plan:
    GATHER = {"ev_in": ("mlp_w1_0",), "mlp_up0": ("mlp_w2_0",), "mlp_down0": ("od_w_in",), "od_in_c": ("od_w_out",),
              "fox_fwd": ("mlp_w1_1", "mlp_w2_1")}
    SCATTER = {"mlp_down1_dx": "mlp_w2_1", "mlp_up1_dx": "mlp_w1_1", "od_out_dx": "od_w_out", "od_in_c_dx": "od_w_in",
               "mlp_down0_dx": "mlp_w2_0", "mlp_up0_dx": "mlp_w1_0", "ev_out_dx": "ev_w_out", "ev_in_dx": "ev_w_in"}

    def __init__(self, shards, heads):
        self.shards, self.heads = shards, heads

    def carry(self, call, g):
        if call in self.GATHER:
            return [self.shards[n] for n in self.GATHER[call]], False
        if call in self.SCATTER:
            return [_grad_blocks(self.SCATTER[call], g)], True
        return None

    def arrived(self, call, carried, P, g):
        if call in self.GATHER:
            for n, gathered in zip(self.GATHER[call], carried):
                P.update(_full_weights(n, gathered, self.heads))
        elif call in self.SCATTER:
            g[self.SCATTER[call]] = carried[0]


def _pack(parts):
    flat = jnp.concatenate([p.reshape(-1).astype(F32) for p in parts])
    rows = -(-flat.shape[0] // (_PACK_COLS * 8)) * 8
    return jnp.pad(flat, (0, rows * _PACK_COLS - flat.shape[0])).reshape(rows, _PACK_COLS)


def _unpack(packed, shapes):
    flat = packed.reshape(-1)
    out, off = [], 0
    for s in shapes:
        n = math.prod(s)
        out.append(flat[off:off + n].reshape(s))
        off += n
    return out


def kernel(x, ev_w_in, ev_conv_w, ev_conv_b, ev_conv_ln_g, ev_conv_ln_b, hgrn_lb_logits, ev_gnorm_g, ev_w_out, od_w_in, fox_b_f, rel_bias, od_w_out, ln_mix_g, ln_mix_b, mlp_w1, mlp_w2, ln_mlp_g, ln_mlp_b, loss_target, m_ev_w_in, m_ev_conv_w, m_ev_conv_b, m_ev_conv_ln_g, m_ev_conv_ln_b, m_hgrn_lb_logits, m_ev_gnorm_g, m_ev_w_out, m_od_w_in, m_fox_b_f, m_rel_bias, m_od_w_out, m_ln_mix_g, m_ln_mix_b, m_mlp_w1, m_mlp_w2, m_ln_mlp_g, m_ln_mlp_b, v_ev_w_in, v_ev_conv_w, v_ev_conv_b, v_ev_conv_ln_g, v_ev_conv_ln_b, v_hgrn_lb_logits, v_ev_gnorm_g, v_ev_w_out, v_od_w_in, v_fox_b_f, v_rel_bias, v_od_w_out, v_ln_mix_g, v_ln_mix_b, v_mlp_w1, v_mlp_w2, v_ln_mlp_g, v_ln_mlp_b):
    args = locals()
    w = {n: args[n] for n in _NAMES}
    m = {n: args["m_" + n] for n in _NAMES}
    v = {n: args["v_" + n] for n in _NAMES}
    me = 4 * lax.axis_index("x") + 2 * lax.axis_index("y") + lax.axis_index("c")
    S, D = x.shape[1], x.shape[2]
    H = (D // 2) // HEAD_DIM
    W = H * HEAD_DIM
    n_layers = mlp_w1.shape[0]
    assert n_layers == 2 and ev_w_in.shape[0] == 1 and od_w_in.shape[0] == 1

    shards = {"ev_w_in": ev_w_in[0].astype(BF16), "ev_w_out": ev_w_out[0].astype(BF16),
              "od_w_in": od_w_in[0].astype(BF16), "od_w_out": od_w_out[0].astype(BF16)}
    for l in range(n_layers):
        shards["mlp_w1_%d" % l] = mlp_w1[l].astype(BF16)
        shards["mlp_w2_%d" % l] = mlp_w2[l].astype(BF16)
    first = ["ev_w_in", "ev_w_out", "ev_conv_w", "rel_bias"]
    G = _exchange([shards["ev_w_in"], shards["ev_w_out"], ev_conv_w[0], rel_bias[0]], False, "gather_first")
    P = {
        "ev_conv_b": ev_conv_b, "ev_conv_ln_g": ev_conv_ln_g, "ev_conv_ln_b": ev_conv_ln_b,
        "hgrn_lb_logits": hgrn_lb_logits, "ev_gnorm_g": ev_gnorm_g, "fox_b_f": fox_b_f,
        "ln_mix_g": ln_mix_g, "ln_mix_b": ln_mix_b, "ln_mlp_g": ln_mlp_g, "ln_mlp_b": ln_mlp_b,
    }
    for name, gathered in zip(first, G):
        P.update(_full_weights(name, gathered, H))

    loss, grad_x, g = _local_step(x[0], loss_target[0], P, _Plan(shards, H))
    recv = [g["ev_w_in"], g["ev_w_out"], g["od_w_in"], g["od_w_out"]]
    recv += [g["mlp_w1_%d" % l] for l in range(n_layers)] + [g["mlp_w2_%d" % l] for l in range(n_layers)]

    small = {
        "ev_conv_b": g["ev_conv_b"], "ev_conv_ln_g": g["ev_conv_ln_g"], "ev_conv_ln_b": g["ev_conv_ln_b"],
        "hgrn_lb_logits": g["hgrn_lb_logits"], "ev_gnorm_g": g["ev_gnorm_g"], "fox_b_f": g["fox_b_f"],
        "ln_mix_g": jnp.concatenate([g["ln_mix_g0"], g["ln_mix_g1"]]), "ln_mix_b": jnp.concatenate([g["ln_mix_b0"], g["ln_mix_b1"]]),
        "ln_mlp_g": jnp.concatenate([g["ln_mlp_g0"], g["ln_mlp_g1"]]), "ln_mlp_b": jnp.concatenate([g["ln_mlp_b0"], g["ln_mlp_b1"]]),
        "ev_conv_w": g["ev_conv_w"], "rel_bias": g["rel_bias"],
    }
    full_shapes = [small[n].shape for n in _SMALL]
    small_all = _exchange([_pack([small[n] for n in _SMALL])], False, "gather_small_grads")[0]
    small_sum = _unpack(_sum_parts(small_all, "sum_small_grads"), full_shapes)
    small_g = dict(zip(_SMALL, small_sum))
    cw = small_g["ev_conv_w"]
    small_g["ev_conv_w"] = lax.dynamic_slice_in_dim(cw, me * (cw.shape[1] // N_DEV), cw.shape[1] // N_DEV, axis=1)
    rb = small_g["rel_bias"]
    small_g["rel_bias"] = lax.dynamic_slice_in_dim(rb, me * (rb.shape[1] // N_DEV), rb.shape[1] // N_DEV, axis=1)

    out_g, out_d, out_m, out_v = {}, {}, {}, {}

    def put(name, res, shape):
        out_g[name], out_d[name], out_m[name], out_v[name] = [r.reshape(shape) for r in res]

    big = [("ev_w_in", recv[0], None), ("ev_w_out", recv[1], None), ("od_w_in", recv[2], None), ("od_w_out", recv[3], None)]
    for name, parts, _ in big:
        shp = w[name].shape
        put(name, _adamw(parts, w[name][0], m[name][0], v[name][0], "adamw_" + name), shp)
    for name, base in (("mlp_w1", 4), ("mlp_w2", 4 + n_layers)):
        res = [_adamw(recv[base + l], w[name][l], m[name][l], v[name][l], "adamw_%s_%d" % (name, l)) for l in range(n_layers)]
        put(name, [jnp.stack([res[l][k] for l in range(n_layers)]) for k in range(4)], w[name].shape)
    shapes = [w[n].shape for n in _SMALL]
    packed = _adamw(_pack([small_g[n] for n in _SMALL])[None], _pack([w[n] for n in _SMALL]), _pack([m[n] for n in _SMALL]),
                    _pack([v[n] for n in _SMALL]), "adamw_small")
    for k, dst in enumerate((out_g, out_d, out_m, out_v)):
        for n, a in zip(_SMALL, _unpack(packed[k], shapes)):
            dst[n] = a

    loss = lax.psum(loss[0, 0], ("x", "y", "c"))
    return (loss, grad_x[None], *[out_g[n] for n in _NAMES], *[out_d[n] for n in _NAMES],
            *[out_m[n] for n in _NAMES], *[out_v[n] for n in _NAMES])
```

```python
import functools
import math

import jax
import jax.numpy as jnp
from jax import lax
from jax.experimental import pallas as pl
from jax.experimental.pallas import tpu as pltpu

F32 = jnp.float32
BF16 = jnp.bfloat16
HI = lax.Precision.HIGHEST
MESH_ID = pl.DeviceIdType.MESH

N_DEV = 8
LN_EPS = 1e-5
CHUNK = 64
HEAD_DIM = 128
CONV_WIDTH = 31
CONV_HALO = 32
CA_LEFT_CHUNKS = 8
CA_TILE = 256
CA_WIN = CA_TILE + CA_LEFT_CHUNKS * CHUNK
CA_SKEW = 1024
REL_CLIP = 256
REL_TABLE = (CHUNK - 1) + REL_CLIP + 1
NEG = -1e30
ADAM_LR = 0.001
ADAM_B1 = 0.9
ADAM_B2 = 0.999
ADAM_EPS = 1e-08
ADAM_WD = 0.01
ADAM_STEP = 10
VMEM_LIMIT_V7X = 56 * 1024 * 1024


def _cparams(*sem):
    return pltpu.CompilerParams(dimension_semantics=sem, vmem_limit_bytes=VMEM_LIMIT_V7X)


def _tile(n, t):
    if n <= t:
        return n
    for c in range(t - t % 128, 0, -128):
        if n % c == 0:
            return c
    return n


def _sigmoid(x):
    return 1.0 / (1.0 + jnp.exp(-x))


def _dot(a, b, dims, precision=None):
    return lax.dot_general(a, b, (dims, ((), ())), preferred_element_type=F32, precision=precision)


def _round_bf16(x):
    return x.astype(BF16).astype(F32)


NN = ((1,), (0,))
NT = ((1,), (1,))
TN = ((0,), (0,))


def _mm(a, b, *, mode, name, out_dtypes=(F32,), extras=(), epi=None, exact_products=False, carry=None, tm=1024, tn=1024, tk=2048):
    dims = {"nn": NN, "nt": NT, "tn": TN}[mode]
    if mode == "tn":
        K, M = a.shape
    else:
        M, K = a.shape
    N = b.shape[0] if mode == "nt" else b.shape[1]
    tm, tn, tk = _tile(M, tm), _tile(N, tn), _tile(K, tk)
    nk = K // tk
    n_ex, n_out = len(extras), len(out_dtypes)

    def body(*refs):
        a_ref, b_ref = refs[0], refs[1]
        ex_refs = refs[2:2 + n_ex]
        o_refs = refs[2 + n_ex:2 + n_ex + n_out]
        if exact_products:
            d = _dot(_round_bf16(a_ref[...]), _round_bf16(b_ref[...]), dims, HI)
        else:
            d = _dot(a_ref[...].astype(BF16), b_ref[...].astype(BF16), dims)

        def finish(acc):
            outs = (acc,) if epi is None else epi(acc, *[r[...] for r in ex_refs])
            for o_ref, o in zip(o_refs, outs):
                o_ref[...] = o.astype(o_ref.dtype)

        if nk == 1:
            finish(d)
        else:
            acc_ref = refs[-1]
            k = pl.program_id(2)

            @pl.when(k == 0)
            def _():
                acc_ref[...] = d

            @pl.when(k > 0)
            def _():
                acc_ref[...] += d

            @pl.when(k == nk - 1)
            def _():
                finish(acc_ref[...])

    a_spec = pl.BlockSpec((tk, tm), lambda i, j, k: (k, i)) if mode == "tn" else pl.BlockSpec((tm, tk), lambda i, j, k: (i, k))
    b_spec = pl.BlockSpec((tn, tk), lambda i, j, k: (j, k)) if mode == "nt" else pl.BlockSpec((tk, tn), lambda i, j, k: (k, j))
    mn_spec = pl.BlockSpec((tm, tn), lambda i, j, k: (i, j))
    outs, carried = _call(
        body, name=name, grid=(M // tm, N // tn, nk),
        in_specs=[a_spec, b_spec] + [mn_spec] * n_ex,
        out_specs=[mn_spec] * n_out,
        out_shape=[jax.ShapeDtypeStruct((M, N), dt) for dt in out_dtypes],
        scratch_shapes=[pltpu.VMEM((tm, tn), F32)] if nk > 1 else [],
        sem=("parallel", "parallel", "arbitrary"), args=(a, b, *extras), carry=carry)
    return (outs[0] if n_out == 1 else outs), carried


def _ln_fwd(x, r, g, b, alpha, name, tr=256):
    S, D = x.shape
    tr = _tile(S, tr)

    def body(x_ref, r_ref, g_ref, b_ref, y_ref, xh_ref, rs_ref):
        z = alpha * x_ref[...] + r_ref[...]
        zc = z - jnp.mean(z, axis=-1, keepdims=True)
        rs = lax.rsqrt(jnp.mean(zc * zc, axis=-1, keepdims=True) + LN_EPS)
        xh = zc * rs
        xh_ref[...] = xh
        rs_ref[...] = rs
        y_ref[...] = xh * g_ref[...] + b_ref[...]

    row = pl.BlockSpec((tr, D), lambda i: (i, 0))
    par = pl.BlockSpec((1, D), lambda i: (0, 0))
    return pl.pallas_call(
        body, name=name, grid=(S // tr,),
        in_specs=[row, row, par, par],
        out_specs=[row, row, pl.BlockSpec((tr, 1), lambda i: (i, 0))],
        out_shape=[jax.ShapeDtypeStruct((S, D), F32), jax.ShapeDtypeStruct((S, D), F32), jax.ShapeDtypeStruct((S, 1), F32)],
        compiler_params=_cparams("parallel"),
    )(x, r, g, b)


def _ln_bwd(dy, dy2, scale2, xh, rs, g, name, tr=256):
    S, D = dy.shape
    tr = _tile(S, tr)
    two = dy2 is not None

    def body(*refs):
        if two:
            dy_ref, dy2_ref, xh_ref, rs_ref, g_ref, dz_ref, dg_ref, db_ref = refs
            dyt = dy_ref[...] + scale2 * dy2_ref[...]
        else:
            dy_ref, xh_ref, rs_ref, g_ref, dz_ref, dg_ref, db_ref = refs
            dyt = dy_ref[...]
        xh = xh_ref[...]
        dxh = dyt * g_ref[...]
        m1 = jnp.mean(dxh, axis=-1, keepdims=True)
        m2 = jnp.mean(dxh * xh, axis=-1, keepdims=True)
        dz_ref[...] = rs_ref[...] * (dxh - m1 - xh * m2)

        @pl.when(pl.program_id(0) == 0)
        def _():
            dg_ref[...] = jnp.zeros_like(dg_ref)
            db_ref[...] = jnp.zeros_like(db_ref)

        dg_ref[...] += jnp.sum(dyt * xh, axis=0, keepdims=True)
        db_ref[...] += jnp.sum(dyt, axis=0, keepdims=True)

    row = pl.BlockSpec((tr, D), lambda i: (i, 0))
    par = pl.BlockSpec((1, D), lambda i: (0, 0))
    ins = [dy] + ([dy2] if two else []) + [xh, rs, g]
    return pl.pallas_call(
        body, name=name, grid=(S // tr,),
        in_specs=[row] * (2 if two else 1) + [row, pl.BlockSpec((tr, 1), lambda i: (i, 0)), par],
        out_specs=[row, par, par],
        out_shape=[jax.ShapeDtypeStruct((S, D), F32), jax.ShapeDtypeStruct((1, D), F32), jax.ShapeDtypeStruct((1, D), F32)],
        compiler_params=_cparams("arbitrary"),
    )(*ins)


def _loss_head(y, target, name, tr=256):
    S, D = y.shape
    tr = _tile(S, tr)

    def body(y_ref, t_ref, dy_ref, loss_ref):
        e = y_ref[...] - t_ref[...]
        dy_ref[...] = e * (1.0 / D)

        @pl.when(pl.program_id(0) == 0)
        def _():
            loss_ref[...] = jnp.zeros_like(loss_ref)

        loss_ref[...] += jnp.sum(jnp.sum(e * e, axis=-1, keepdims=True), axis=0, keepdims=True) * (0.5 / D)

    row = pl.BlockSpec((tr, D), lambda i: (i, 0))
    return pl.pallas_call(
        body, name=name, grid=(S // tr,),
        in_specs=[row, row],
        out_specs=[row, pl.BlockSpec((1, 1), lambda i: (0, 0))],
        out_shape=[jax.ShapeDtypeStruct((S, D), F32), jax.ShapeDtypeStruct((1, 1), F32)],
        compiler_params=_cparams("arbitrary"),
    )(y, target)


SUBLANES = 8


def _tap_reads(ext, shifted, tt, offset_of_tap):
    for r in range(SUBLANES):
        taps = [k for k in range(CONV_WIDTH) if offset_of_tap(k) % SUBLANES == r]
        if r == 0:
            src = ext
        else:
            shifted[...] = ext[pl.ds(r, tt + CONV_HALO - SUBLANES), :]
            src = shifted
        for k in taps:
            yield src, k, offset_of_tap(k) - r


def _conv_fwd(u, w, cb, lg, lb, name, tt=512):
    S = u.shape[0]
    C = w.shape[1]
    tt = _tile(S, tt)
    hpt = tt // CONV_HALO

    def body(a_ref, g_ref, ap_ref, gp_ref, w_ref, cb_ref, lg_ref, lb_ref, out_ref, cv_ref, hext, shifted):
        i = pl.program_id(0)
        hext[pl.ds(CONV_HALO, tt), :] = a_ref[...] * _sigmoid(g_ref[...])
        hp = ap_ref[...] * _sigmoid(gp_ref[...])
        hext[pl.ds(0, CONV_HALO), :] = jnp.where(i > 0, hp, 0.0)
        acc = jnp.zeros((tt, C), F32)
        for src, k, row0 in _tap_reads(hext, shifted, tt, lambda k: CONV_HALO - (CONV_WIDTH - 1) + k):
            acc = acc + w_ref[pl.ds(k, 1), :] * src[pl.ds(row0, tt), :]
        cv = acc + cb_ref[...]
        cv_ref[...] = cv
        zc = cv - jnp.mean(cv, axis=-1, keepdims=True)
        n = zc * lax.rsqrt(jnp.mean(zc * zc, axis=-1, keepdims=True) + LN_EPS) * lg_ref[...] + lb_ref[...]
        out_ref[...] = n * _sigmoid(n)

    cur = lambda cb_: pl.BlockSpec((tt, C), lambda i: (i, cb_))
    prev = lambda cb_: pl.BlockSpec((CONV_HALO, C), lambda i: (jnp.maximum(i * hpt - 1, 0), cb_))
    par = pl.BlockSpec((1, C), lambda i: (0, 0))
    row = pl.BlockSpec((tt, C), lambda i: (i, 0))
    return pl.pallas_call(
        body, name=name, grid=(S // tt,),
        in_specs=[cur(0), cur(1), prev(0), prev(1), pl.BlockSpec((CONV_WIDTH, C), lambda i: (0, 0)), par, par, par],
        out_specs=[row, row],
        out_shape=[jax.ShapeDtypeStruct((S, C), F32)] * 2,
        scratch_shapes=[pltpu.VMEM((tt + CONV_HALO, C), F32), pltpu.VMEM((tt + CONV_HALO - SUBLANES, C), F32)],
        compiler_params=_cparams("parallel"),
    )(u, u, u, u, w, cb, lg, lb)


def _conv_bwd_norm(da, da_col, cv, lg, lb, name, tt=512):
    S, C = cv.shape
    tt = _tile(S, tt)

    def body(da_ref, cv_ref, lg_ref, lb_ref, dcv_ref, dlg_ref, dlb_ref, dcb_ref):
        cv = cv_ref[...]
        zc = cv - jnp.mean(cv, axis=-1, keepdims=True)
        rs = lax.rsqrt(jnp.mean(zc * zc, axis=-1, keepdims=True) + LN_EPS)
        xh = zc * rs
        n = xh * lg_ref[...] + lb_ref[...]
        sg = _sigmoid(n)
        dn = da_ref[...] * sg * (1.0 + n * (1.0 - sg))
        dxh = dn * lg_ref[...]
        m1 = jnp.mean(dxh, axis=-1, keepdims=True)
        m2 = jnp.mean(dxh * xh, axis=-1, keepdims=True)
        dcv = rs * (dxh - m1 - xh * m2)
        dcv_ref[...] = dcv

        @pl.when(pl.program_id(0) == 0)
        def _():
            dlg_ref[...] = jnp.zeros_like(dlg_ref)
            dlb_ref[...] = jnp.zeros_like(dlb_ref)
            dcb_ref[...] = jnp.zeros_like(dcb_ref)

        dlg_ref[...] += jnp.sum(dn * xh, axis=0, keepdims=True)
        dlb_ref[...] += jnp.sum(dn, axis=0, keepdims=True)
        dcb_ref[...] += jnp.sum(dcv, axis=0, keepdims=True)

    row = pl.BlockSpec((tt, C), lambda i: (i, 0))
    par = pl.BlockSpec((1, C), lambda i: (0, 0))
    return pl.pallas_call(
        body, name=name, grid=(S // tt,),
        in_specs=[pl.BlockSpec((tt, C), lambda i: (i, da_col)), row, par, par],
        out_specs=[row, par, par, par],
        out_shape=[jax.ShapeDtypeStruct((S, C), F32)] + [jax.ShapeDtypeStruct((1, C), F32)] * 3,
        compiler_params=_cparams("arbitrary"),
    )(da, cv, lg, lb)


def _conv_bwd_taps(dcv, u, w, name, tt=512):
    S, C = dcv.shape
    tt = _tile(S, tt)
    hpt = tt // CONV_HALO
    nt = S // tt
    WPAD = 32

    def body(dc_ref, dn_ref, a_ref, g_ref, ap_ref, gp_ref, w_ref, da_ref, dg_ref, dw_ref, hext, dext, shifted):
        i = pl.program_id(0)
        a = a_ref[...]
        sg = _sigmoid(g_ref[...])
        hext[pl.ds(CONV_HALO, tt), :] = a * sg
        hp = ap_ref[...] * _sigmoid(gp_ref[...])
        hext[pl.ds(0, CONV_HALO), :] = jnp.where(i > 0, hp, 0.0)
        dc = dc_ref[...]
        dext[pl.ds(0, tt), :] = dc
        dext[pl.ds(tt, CONV_HALO), :] = jnp.where(i < nt - 1, dn_ref[...], 0.0)

        @pl.when(i == 0)
        def _():
            dw_ref[...] = jnp.zeros_like(dw_ref)

        dh = jnp.zeros((tt, C), F32)
        for src, k, row0 in _tap_reads(dext, shifted, tt, lambda k: CONV_WIDTH - 1 - k):
            dh = dh + w_ref[pl.ds(k, 1), :] * src[pl.ds(row0, tt), :]
        for src, k, row0 in _tap_reads(hext, shifted, tt, lambda k: CONV_HALO - (CONV_WIDTH - 1) + k):
            dw_ref[pl.ds(k, 1), :] += jnp.sum(dc * src[pl.ds(row0, tt), :], axis=0, keepdims=True)
        da_ref[...] = dh * sg
        dg_ref[...] = dh * a * sg * (1.0 - sg)

    row = pl.BlockSpec((tt, C), lambda i: (i, 0))
    nxt = pl.BlockSpec((CONV_HALO, C), lambda i: (jnp.minimum((i + 1) * hpt, S // CONV_HALO - 1), 0))
    cur = lambda cb_: pl.BlockSpec((tt, C), lambda i: (i, cb_))
    prev = lambda cb_: pl.BlockSpec((CONV_HALO, C), lambda i: (jnp.maximum(i * hpt - 1, 0), cb_))
    da, dg, dw = pl.pallas_call(
        body, name=name, grid=(nt,),
        in_specs=[row, nxt, cur(0), cur(1), prev(0), prev(1), pl.BlockSpec((CONV_WIDTH, C), lambda i: (0, 0))],
        out_specs=[row, row, pl.BlockSpec((WPAD, C), lambda i: (0, 0))],
        out_shape=[jax.ShapeDtypeStruct((S, C), F32)] * 2 + [jax.ShapeDtypeStruct((WPAD, C), F32)],
        scratch_shapes=[pltpu.VMEM((tt + CONV_HALO, C), F32)] * 2 + [pltpu.VMEM((tt + CONV_HALO - SUBLANES, C), F32)],
        compiler_params=_cparams("arbitrary"),
    )(dcv, dcv, u, u, u, u, w)
    return da, dg, dw[:CONV_WIDTH]


def _lower_bound(logits):
    e = jnp.exp(logits - jnp.max(logits, axis=0, keepdims=True))
    p = e / jnp.sum(e, axis=0, keepdims=True)
    return p[0:1, :], p


def _tri(n, lower):
    r = lax.broadcasted_iota(jnp.int32, (n, n), 0)
    c = lax.broadcasted_iota(jnp.int32, (n, n), 1)
    return ((c <= r) if lower else (c >= r)).astype(F32)


def _hgrn_gates(xq, xf, lb):
    sq = _sigmoid(xq)
    q = xq * sq
    sf = _sigmoid(xf)
    f = lb + (1.0 - lb) * sf
    logf = jnp.log(f)
    L = _dot(_tri(CHUNK, True), logf, NN, HI)
    Lend = L[CHUNK - 1:CHUNK, :]
    eL = jnp.exp(L)
    enL = jnp.exp(-L)
    eLe = jnp.exp(Lend - L)
    kk = 1.0 - f
    return dict(sq=sq, q=q, sf=sf, f=f, L=L, Lend=Lend, eL=eL, enL=enL, eLe=eLe, kk=kk,
                qe=q * eL, ke=kk * enL, kd=kk * eLe)


def _hgrn_fwd(u, lbl, gn, name, tb=256):
    S = u.shape[0]
    W = gn.shape[1]
    H = W // HEAD_DIM
    tb = _tile(S, tb)
    cpb = tb // CHUNK
    nc = S // CHUNK
    R = lbl.shape[0]

    def body(q_ref, f_ref, i_ref, g_ref, lbl_ref, gn_ref, out_ref, raw_ref, st_ref, state):
        @pl.when(pl.program_id(0) == 0)
        def _():
            state[...] = jnp.zeros_like(state)

        lb, _ = _lower_bound(lbl_ref[...])
        tril = _tri(CHUNK, True) > 0.5

        def chunk(c, carry):
            rows = pl.ds(pl.multiple_of(c * CHUNK, CHUNK), CHUNK)
            G = _hgrn_gates(q_ref[rows, :], f_ref[rows, :], lb)
            v = i_ref[rows, :]
            xg = g_ref[rows, :]
            outs = []
            for h in range(H):
                ln = slice(h * HEAD_DIM, (h + 1) * HEAD_DIM)
                qe, ke, kd, vh = G["qe"][:, ln].astype(BF16), G["ke"][:, ln].astype(BF16), G["kd"][:, ln].astype(BF16), v[:, ln].astype(BF16)
                st = state[h]
                st_ref[c, h] = st
                A = jnp.where(tril, _dot(qe, ke, NT), 0.0)
                o = _dot(A.astype(BF16), vh, NN) + _dot(qe, st.astype(BF16), NT)
                state[h] = jnp.exp(G["Lend"][:, ln]) * st + _dot(vh, kd, TN)
                outs.append(o)
            o = jnp.concatenate(outs, axis=1)
            raw_ref[rows, :] = o
            ns = []
            for h in range(H):
                oh = outs[h]
                ns.append(oh * lax.rsqrt(jnp.mean(oh * oh, axis=-1, keepdims=True) + LN_EPS))
            n = jnp.concatenate(ns, axis=1)
            out_ref[rows, :] = n * gn_ref[...] * (xg * _sigmoid(xg))
            return carry

        lax.fori_loop(0, cpb, chunk, 0)

    col = lambda cb_: pl.BlockSpec((tb, W), lambda i: (i, cb_))
    row = pl.BlockSpec((tb, W), lambda i: (i, 0))
    return pl.pallas_call(
        body, name=name, grid=(S // tb,),
        in_specs=[col(2), col(3), col(4), col(5), pl.BlockSpec((R, W), lambda i: (0, 0)), pl.BlockSpec((1, W), lambda i: (0, 0))],
        out_specs=[row, row, pl.BlockSpec((cpb, H, HEAD_DIM, HEAD_DIM), lambda i: (i, 0, 0, 0))],
        out_shape=[jax.ShapeDtypeStruct((S, W), F32), jax.ShapeDtypeStruct((S, W), F32),
                   jax.ShapeDtypeStruct((nc, H, HEAD_DIM, HEAD_DIM), F32)],
        scratch_shapes=[pltpu.VMEM((H, HEAD_DIM, HEAD_DIM), F32)],
        compiler_params=_cparams("arbitrary"),
    )(u, u, u, u, lbl, gn)


def _hgrn_bwd(dout, dout_col, u, raw, states, lbl, gn, name, tb=256):
    S = u.shape[0]
    W = gn.shape[1]
    H = W // HEAD_DIM
    tb = _tile(S, tb)
    cpb = tb // CHUNK
    nb = S // tb
    R = lbl.shape[0]

    def body(do_ref, q_ref, f_ref, i_ref, g_ref, raw_ref, st_ref, lbl_ref, gn_ref,
             dq_ref, df_ref, di_ref, dg_ref, dgn_ref, dlbl_ref, dstate, dlb_acc):
        @pl.when(pl.program_id(0) == 0)
        def _():
            dstate[...] = jnp.zeros_like(dstate)
            dlb_acc[...] = jnp.zeros_like(dlb_acc)
            dgn_ref[...] = jnp.zeros_like(dgn_ref)

        lb, p = _lower_bound(lbl_ref[...])
        tril = _tri(CHUNK, True) > 0.5
        triu = _tri(CHUNK, False)
        gn_row = gn_ref[...]

        def chunk(cc, carry):
            c = cpb - 1 - cc
            rows = pl.ds(pl.multiple_of(c * CHUNK, CHUNK), CHUNK)
            xq, xf = q_ref[rows, :], f_ref[rows, :]
            G = _hgrn_gates(xq, xf, lb)
            v = i_ref[rows, :]
            xg = g_ref[rows, :]
            dy = do_ref[rows, :]
            o = raw_ref[rows, :]
            sgg = _sigmoid(xg)
            silu_g = xg * sgg
            do_parts, n_parts = [], []
            for h in range(H):
                ln = slice(h * HEAD_DIM, (h + 1) * HEAD_DIM)
                oh = o[:, ln]
                r = lax.rsqrt(jnp.mean(oh * oh, axis=-1, keepdims=True) + LN_EPS)
                nh = oh * r
                dn = dy[:, ln] * gn_row[:, ln] * silu_g[:, ln]
                do_parts.append(r * (dn - nh * jnp.mean(dn * nh, axis=-1, keepdims=True)))
                n_parts.append(nh)
            n = jnp.concatenate(n_parts, axis=1)
            dgn_ref[...] += jnp.sum(dy * n * silu_g, axis=0, keepdims=True)
            dg_ref[rows, :] = dy * n * gn_row * sgg * (1.0 + xg * (1.0 - sgg))
            dqe_p, dke_p, dkd_p, dv_p, dle_p = [], [], [], [], []
            for h in range(H):
                ln = slice(h * HEAD_DIM, (h + 1) * HEAD_DIM)
                qe, ke, kd, vh = G["qe"][:, ln].astype(BF16), G["ke"][:, ln].astype(BF16), G["kd"][:, ln].astype(BF16), v[:, ln].astype(BF16)
                doh = do_parts[h].astype(BF16)
                st = st_ref[c, h]
                dst = dstate[h]
                dec = jnp.exp(G["Lend"][:, ln])
                A = jnp.where(tril, _dot(qe, ke, NT), 0.0).astype(BF16)
                dA = jnp.where(tril, _dot(doh, vh, NT), 0.0).astype(BF16)
                dv_p.append(_dot(A, doh, TN) + _dot(kd, dst.astype(BF16), NT))
                dqe_p.append(_dot(dA, ke, NN) + _dot(doh, st.astype(BF16), NN))
                dke_p.append(_dot(dA, qe, TN))
                dkd_p.append(_dot(vh, dst.astype(BF16), NN))
                dle_p.append(jnp.sum(dst * dec * st, axis=0, keepdims=True))
                dstate[h] = dec * dst + _dot(doh, qe, TN)
            dqe = jnp.concatenate(dqe_p, axis=1)
            dke = jnp.concatenate(dke_p, axis=1)
            dkd = jnp.concatenate(dkd_p, axis=1)
            qe_r, ke_r, kd_r = _round_bf16(G["qe"]), _round_bf16(G["ke"]), _round_bf16(G["kd"])
            dLend = jnp.concatenate(dle_p, axis=1) + jnp.sum(dkd * kd_r, axis=0, keepdims=True)
            dL = dqe * qe_r - dke * ke_r - dkd * kd_r
            dlogf = _dot(triu, dL, NN, HI) + dLend
            dkk = dke * G["enL"] + dkd * G["eLe"]
            dfv = dlogf / G["f"] - dkk
            sf = G["sf"]
            df_ref[rows, :] = dfv * (1.0 - lb) * sf * (1.0 - sf)
            dlb_acc[...] += jnp.sum(dfv * (1.0 - sf), axis=0, keepdims=True)
            sq = G["sq"]
            dq_ref[rows, :] = dqe * G["eL"] * sq * (1.0 + xq * (1.0 - sq))
            di_ref[rows, :] = jnp.concatenate(dv_p, axis=1)
            return carry

        lax.fori_loop(0, cpb, chunk, 0)

        onehot0 = (lax.broadcasted_iota(jnp.int32, (R, W), 0) == 0).astype(F32)
        dlbl_ref[...] = p * (onehot0 - p[0:1, :]) * dlb_acc[...]

    rev = lambda i: nb - 1 - i
    col = lambda cb_: pl.BlockSpec((tb, W), lambda i: (rev(i), cb_))
    row = pl.BlockSpec((tb, W), lambda i: (rev(i), 0))
    par = pl.BlockSpec((1, W), lambda i: (0, 0))
    parR = pl.BlockSpec((R, W), lambda i: (0, 0))
    return pl.pallas_call(
        body, name=name, grid=(nb,),
        in_specs=[pl.BlockSpec((tb, W), lambda i: (rev(i), dout_col)), col(2), col(3), col(4), col(5), row,
                  pl.BlockSpec((cpb, H, HEAD_DIM, HEAD_DIM), lambda i: (rev(i), 0, 0, 0)), parR, par],
        out_specs=[row, row, row, row, par, parR],
        out_shape=[jax.ShapeDtypeStruct((S, W), F32)] * 4 + [jax.ShapeDtypeStruct((1, W), F32), jax.ShapeDtypeStruct((R, W), F32)],
        scratch_shapes=[pltpu.VMEM((H, HEAD_DIM, HEAD_DIM), F32), pltpu.VMEM((1, W), F32)],
        compiler_params=_cparams("arbitrary"),
    )(dout, u, u, u, u, raw, states, lbl, gn)


def _fgate_fwd(xb, wft, bf, name, ts=512):
    S, D = xb.shape
    H = wft.shape[0]
    ts = _tile(S, ts)

    def body(x_ref, w_ref, b_ref, lg_ref, F_ref, carry):
        @pl.when(pl.program_id(0) == 0)
        def _():
            carry[...] = jnp.zeros_like(carry)

        lg = _dot(_round_bf16(w_ref[...]), _round_bf16(x_ref[...]), NT, HI) + b_ref[...]
        lg_ref[...] = lg
        ls = jnp.minimum(lg, 0.0) - jnp.log(1.0 + jnp.exp(-jnp.abs(lg)))
        F = _dot(ls, _tri(ts, False), NN, HI) + carry[...]
        F_ref[...] = F
        carry[...] = F[:, ts - 1:ts]

    return pl.pallas_call(
        body, name=name, grid=(S // ts,),
        in_specs=[pl.BlockSpec((ts, D), lambda i: (i, 0)), pl.BlockSpec((H, D), lambda i: (0, 0)), pl.BlockSpec((H, 1), lambda i: (0, 0))],
        out_specs=[pl.BlockSpec((H, ts), lambda i: (0, i))] * 2,
        out_shape=[jax.ShapeDtypeStruct((H, S), F32)] * 2,
        scratch_shapes=[pltpu.VMEM((H, 1), F32)],
        compiler_params=_cparams("arbitrary"),
    )(xb, wft, bf)


def _fgate_bwd(dF, lg, name, ts=512):
    H, S = dF.shape
    ts = _tile(S, ts)
    nb = S // ts

    def body(dF_ref, lg_ref, dl_ref, db_ref, carry):
        @pl.when(pl.program_id(0) == 0)
        def _():
            carry[...] = jnp.zeros_like(carry)
            db_ref[...] = jnp.zeros_like(db_ref)

        dls = _dot(dF_ref[...], _tri(ts, True), NN, HI) + carry[...]
        carry[...] = dls[:, 0:1]
        dl = dls * _sigmoid(-lg_ref[...])
        dl_ref[...] = dl
        db_ref[...] += jnp.sum(dl, axis=1, keepdims=True)

    blk = pl.BlockSpec((H, ts), lambda i: (0, nb - 1 - i))
    return pl.pallas_call(
        body, name=name, grid=(nb,),
        in_specs=[blk, blk],
        out_specs=[blk, pl.BlockSpec((H, 1), lambda i: (0, 0))],
        out_shape=[jax.ShapeDtypeStruct((H, S), F32), jax.ShapeDtypeStruct((H, 1), F32)],
        scratch_shapes=[pltpu.VMEM((H, 1), F32)],
        compiler_params=_cparams("arbitrary"),
    )(dF, lg)


def _causal_keep(i, j, tq, tk):
    rows = lax.broadcasted_iota(jnp.int32, (tq, tk), 0)
    cols = lax.broadcasted_iota(jnp.int32, (tq, tk), 1)
    return jnp.logical_or(j < i, cols <= rows)


def _causal_grid(S, t):
    n = S // t
    assert n % 2 == 0, (S, t)

    def by_query(r, c):
        second = c > r
        return jnp.where(second, n - 1 - r, r), jnp.where(second, c - r - 1, c)

    def by_key(r, c):
        second = c >= n - r
        return jnp.where(second, c - 1, r + c), jnp.where(second, n - 1 - r, r)

    return n, (n // 2, n + 1), by_query, by_key


def _fox_scores(q_ref, k_ref, fk_ref, i, j, t, scale):
    s = _dot(q_ref[...].astype(BF16), k_ref[...].astype(BF16), NT) * scale - fk_ref[...]
    return lax.cond(i == j, lambda x: jnp.where(_causal_keep(i, j, t, t), x, NEG), lambda x: x, s)


def _softmax_bwd(p, pb, dp, delta):
    return pb.astype(F32) * dp - p * delta


FOX_TILE = 1024


def _fox_tile(S):
    return _tile(S, min(FOX_TILE, S // 2))


def _fox_fwd(u, F3, H, name, carry=None):
    S = u.shape[0]
    W = H * HEAD_DIM
    t = _fox_tile(S)
    n, tri, ij, _ = _causal_grid(S, t)
    scale = HEAD_DIM ** -0.5

    def body(q_ref, k_ref, v_ref, fk_ref, o_ref, lse_ref, m_s, l_s, acc_s):
        i, j = ij(pl.program_id(1), pl.program_id(2))

        @pl.when(j == 0)
        def _():
            m_s[...] = jnp.full_like(m_s, NEG)
            l_s[...] = jnp.zeros_like(l_s)
            acc_s[...] = jnp.zeros_like(acc_s)

        s = _fox_scores(q_ref, k_ref, fk_ref, i, j, t, scale)
        m_new = jnp.maximum(m_s[...], jnp.max(s, axis=-1, keepdims=True))
        a = jnp.exp(m_s[...] - m_new)
        p = jnp.exp(s - m_new)
        hi = p.astype(BF16)
        lo = (p - hi.astype(F32)).astype(BF16)
        v = v_ref[...].astype(BF16)
        l_s[...] = a * l_s[...] + jnp.sum(p, axis=-1, keepdims=True)
        acc_s[...] = a * acc_s[...] + (_dot(hi, v, NN) + _dot(lo, v, NN))
        m_s[...] = m_new

        @pl.when(j == i)
        def _():
            o_ref[...] = acc_s[...] / l_s[...]
            lse_ref[...] = m_s[...] + jnp.log(l_s[...])

    qblk = pl.BlockSpec((t, HEAD_DIM), lambda h, r, c: (ij(r, c)[0], h))
    kv = lambda off: pl.BlockSpec((t, HEAD_DIM), lambda h, r, c: (ij(r, c)[1], off * H + h))
    outs, carried = _call(
        body, name=name, grid=(H,) + tri,
        in_specs=[qblk, kv(1), kv(2), pl.BlockSpec((None, 1, t), lambda h, r, c: (h, 0, ij(r, c)[1]))],
        out_specs=[qblk, pl.BlockSpec((None, t, 1), lambda h, r, c: (h, ij(r, c)[0], 0))],
        out_shape=[jax.ShapeDtypeStruct((S, W), F32), jax.ShapeDtypeStruct((H, S, 1), F32)],
        scratch_shapes=[pltpu.VMEM((t, 1), F32), pltpu.VMEM((t, 1), F32), pltpu.VMEM((t, HEAD_DIM), F32)],
        sem=("parallel", "parallel", "arbitrary"), args=(u, u, u, F3), carry=carry)
    return outs[0], outs[1], carried


def _fox_bwd(u, F3, o, do, do_off, lse, H, name):
    S = u.shape[0]
    W = H * HEAD_DIM
    t = _fox_tile(S)
    n, tri, _, ij = _causal_grid(S, t)
    scale = HEAD_DIM ** -0.5

    def body(q_ref, k_ref, v_ref, fk_ref, o_ref, do_ref, lse_ref, dq_ref, dk_ref, dv_ref, dF_ref, dk_s, dv_s, dF_s):
        r, c = pl.program_id(1), pl.program_id(2)
        i, j = ij(r, c)

        @pl.when(jnp.logical_and(r == 0, c == 0))
        def _():
            dq_ref[...] = jnp.zeros_like(dq_ref)

        @pl.when(i == j)
        def _():
            dk_s[...] = jnp.zeros_like(dk_s)
            dv_s[...] = jnp.zeros_like(dv_s)
            dF_s[...] = jnp.zeros_like(dF_s)

        q = q_ref[...].astype(BF16)
        k = k_ref[...].astype(BF16)
        dob = do_ref[...].astype(BF16)
        p = jnp.exp(_fox_scores(q_ref, k_ref, fk_ref, i, j, t, scale) - lse_ref[...])
        dp = _dot(dob, v_ref[...].astype(BF16), NT)
        delta = jnp.sum(dob.astype(F32) * o_ref[...], axis=-1, keepdims=True)
        ds = p * (dp - delta)
        dsb = ds.astype(BF16)
        dv_s[...] += _dot(p.astype(BF16), dob, TN)
        dk_s[...] += _dot(dsb, q, TN)
        dF_s[...] -= jnp.sum(ds, axis=0, keepdims=True)
        rows = pl.ds(pl.multiple_of(i * t, t), t)
        dq_ref[rows, :] += _dot(dsb, k, NN) * scale

        @pl.when(i == n - 1)
        def _():
            dk_ref[...] = dk_s[...] * scale
            dv_ref[...] = dv_s[...]
            dF_ref[...] = dF_s[...]

    qblk = pl.BlockSpec((t, HEAD_DIM), lambda h, r, c: (ij(r, c)[0], h))
    kv = lambda off: pl.BlockSpec((t, HEAD_DIM), lambda h, r, c: (ij(r, c)[1], off * H + h))
    oblk = pl.BlockSpec((t, HEAD_DIM), lambda h, r, c: (ij(r, c)[1], h))
    fblk = pl.BlockSpec((None, 1, t), lambda h, r, c: (h, 0, ij(r, c)[1]))
    return pl.pallas_call(
        body, name=name, grid=(H,) + tri,
        in_specs=[qblk, kv(1), kv(2), fblk, qblk, pl.BlockSpec((t, HEAD_DIM), lambda h, r, c: (ij(r, c)[0], do_off + h)),
                  pl.BlockSpec((None, t, 1), lambda h, r, c: (h, ij(r, c)[0], 0))],
        out_specs=[pl.BlockSpec((S, HEAD_DIM), lambda h, r, c: (0, h)), oblk, oblk, fblk],
        out_shape=[jax.ShapeDtypeStruct((S, W), F32)] * 3 + [jax.ShapeDtypeStruct((H, 1, S), F32)],
        scratch_shapes=[pltpu.VMEM((t, HEAD_DIM), F32), pltpu.VMEM((t, HEAD_DIM), F32), pltpu.VMEM((1, t), F32)],
        compiler_params=_cparams("parallel", "arbitrary", "arbitrary"),
    )(u, u, u, F3, o, do, lse)


def _rel_index_matrix():
    a = lax.broadcasted_iota(jnp.int32, (REL_TABLE, CA_SKEW), 0)
    j = lax.broadcasted_iota(jnp.int32, (REL_TABLE, CA_SKEW), 1)
    rel = jnp.where(j < CA_WIN, CA_LEFT_CHUNKS * CHUNK - j, REL_CLIP)
    idx = jnp.clip(jnp.minimum(rel, REL_CLIP) + (CHUNK - 1), 0, REL_TABLE - 1)
    return (a == idx).astype(F32)


def _skew(x, sign):
    r = lax.broadcasted_iota(jnp.int32, x.shape, 0)
    for b in range(int(math.log2(CA_TILE))):
        sh = (1 << b) if sign > 0 else CA_SKEW - (1 << b)
        x = jnp.where((r >> b) & 1 == 1, pltpu.roll(x, sh, 1), x)
    return x


def _band_valid():
    shift = int(math.log2(CHUNK))
    r = lax.broadcasted_iota(jnp.int32, (CA_TILE, CA_WIN), 0) >> shift
    m = lax.broadcasted_iota(jnp.int32, (CA_TILE, CA_WIN), 1) >> shift
    return jnp.logical_and(m >= r, m <= r + CA_LEFT_CHUNKS)


def _relbias_fwd(table, name):
    H = table.shape[0]

    def body(t_ref, b_ref):
        rowv = _dot(t_ref[...], _rel_index_matrix(), NN, HI)
        valid = _band_valid()
        for h in range(H):
            x = _skew(jnp.broadcast_to(rowv[h:h + 1, :], (CA_TILE, CA_SKEW)), +1)
            b_ref[h] = jnp.where(valid, x[:, :CA_WIN], NEG)

    return pl.pallas_call(
        body, name=name,
        out_shape=jax.ShapeDtypeStruct((H, CA_TILE, CA_WIN), F32),
        compiler_params=pltpu.CompilerParams(vmem_limit_bytes=VMEM_LIMIT_V7X),
    )(table)


def _relbias_bwd(dB, name):
    H = dB.shape[0]
    HP = -(-H // 8) * 8

    def body(d_ref, dt_ref, rows):
        rows[...] = jnp.zeros_like(rows)
        for h in range(H):
            x = jnp.concatenate([d_ref[h], jnp.zeros((CA_TILE, CA_SKEW - CA_WIN), F32)], axis=1)
            rows[pl.ds(h, 1), :] = jnp.sum(_skew(x, -1), axis=0, keepdims=True)
        dt_ref[...] = _dot(rows[...], _rel_index_matrix(), NT, HI)[:H]

    return pl.pallas_call(
        body, name=name,
        out_shape=jax.ShapeDtypeStruct((H, REL_TABLE), F32),
        scratch_shapes=[pltpu.VMEM((HP, CA_SKEW), F32)],
        compiler_params=pltpu.CompilerParams(vmem_limit_bytes=VMEM_LIMIT_V7X),
    )(dB)


CA_PIECES = CA_WIN // CA_TILE


def _ca_fwd(u, bias, H, name):
    S = u.shape[0]
    W = H * HEAD_DIM
    T = CA_TILE
    n = S // T
    scale = HEAD_DIM ** -0.5

    def body(q_ref, k0, k1, k2, v0, v1, v2, b_ref, o_ref, lse_ref):
        i = pl.program_id(1)
        q = q_ref[...].astype(BF16)
        ss = []
        for pce, k_ref in enumerate((k0, k1, k2)):
            s = _dot(q, k_ref[...].astype(BF16), NT) * scale + b_ref[:, pce * T:(pce + 1) * T]
            ss.append(jnp.where(i + pce >= CA_PIECES - 1, s, NEG))
        m = jnp.maximum(jnp.maximum(jnp.max(ss[0], -1, keepdims=True), jnp.max(ss[1], -1, keepdims=True)), jnp.max(ss[2], -1, keepdims=True))
        ps = [jnp.exp(s - m) for s in ss]
        l = sum(jnp.sum(p, -1, keepdims=True) for p in ps)
        inv = 1.0 / l
        o_ref[...] = sum(_dot((p * inv).astype(BF16), v_ref[...].astype(BF16), NN) for p, v_ref in zip(ps, (v0, v1, v2)))
        lse_ref[...] = m + jnp.log(l)

    qblk = pl.BlockSpec((T, HEAD_DIM), lambda h, i: (i, h))
    kv = lambda off, back: pl.BlockSpec((T, HEAD_DIM), lambda h, i: (jnp.maximum(i - back, 0), off * H + h))
    return pl.pallas_call(
        body, name=name, grid=(H, n),
        in_specs=[qblk, kv(1, 2), kv(1, 1), kv(1, 0), kv(2, 2), kv(2, 1), kv(2, 0),
                  pl.BlockSpec((None, T, CA_WIN), lambda h, i: (h, 0, 0))],
        out_specs=[qblk, pl.BlockSpec((None, T, 1), lambda h, i: (h, i, 0))],
        out_shape=[jax.ShapeDtypeStruct((S, W), F32), jax.ShapeDtypeStruct((H, S, 1), F32)],
        compiler_params=_cparams("parallel", "arbitrary"),
    )(u, u, u, u, u, u, u, bias)


def _ca_bwd(u, bias, o, do, do_off, lse, H, name):
    S = u.shape[0]
    W = H * HEAD_DIM
    T = CA_TILE
    n = S // T
    scale = HEAD_DIM ** -0.5

    def body(q_ref, k0, k1, k2, v0, v1, v2, b_ref, o_ref, do_ref, lse_ref, dq_ref, dk_ref, dv_ref, db_ref):
        i = pl.program_id(1)

        @pl.when(i == 0)
        def _():
            db_ref[...] = jnp.zeros_like(db_ref)
            dk_ref[...] = jnp.zeros_like(dk_ref)
            dv_ref[...] = jnp.zeros_like(dv_ref)

        q = q_ref[...].astype(BF16)
        dob = do_ref[...].astype(BF16)
        delta = jnp.sum(dob.astype(F32) * o_ref[...], axis=-1, keepdims=True)
        dq = jnp.zeros((T, HEAD_DIM), F32)
        for pce, (k_ref, v_ref) in enumerate(((k0, v0), (k1, v1), (k2, v2))):
            k = k_ref[...].astype(BF16)
            s = _dot(q, k, NT) * scale + b_ref[:, pce * T:(pce + 1) * T]
            p = jnp.where(i + pce >= CA_PIECES - 1, jnp.exp(s - lse_ref[...]), 0.0)
            pb = p.astype(BF16)
            ds = _softmax_bwd(p, pb, _dot(dob, v_ref[...].astype(BF16), NT), delta)
            dsb = ds.astype(BF16)
            db_ref[:, pce * T:(pce + 1) * T] += ds
            dq = dq + _dot(dsb, k, NN)
            rows = pl.ds(pl.multiple_of(jnp.maximum(i - (CA_PIECES - 1) + pce, 0) * T, T), T)
            dk_ref[rows, :] += _dot(dsb, q, TN) * scale
            dv_ref[rows, :] += _dot(pb, dob, TN)
        dq_ref[...] = dq * scale

    qblk = pl.BlockSpec((T, HEAD_DIM), lambda h, i: (i, h))
    kv = lambda off, back: pl.BlockSpec((T, HEAD_DIM), lambda h, i: (jnp.maximum(i - back, 0), off * H + h))
    bblk = pl.BlockSpec((None, T, CA_WIN), lambda h, i: (h, 0, 0))
    head = pl.BlockSpec((S, HEAD_DIM), lambda h, i: (0, h))
    return pl.pallas_call(
        body, name=name, grid=(H, n),
        in_specs=[qblk, kv(1, 2), kv(1, 1), kv(1, 0), kv(2, 2), kv(2, 1), kv(2, 0), bblk,
                  qblk, pl.BlockSpec((T, HEAD_DIM), lambda h, i: (i, do_off + h)), pl.BlockSpec((None, T, 1), lambda h, i: (h, i, 0))],
        out_specs=[qblk, head, head, bblk],
        out_shape=[jax.ShapeDtypeStruct((S, W), F32)] * 3 + [jax.ShapeDtypeStruct((H, T, CA_WIN), F32)],
        compiler_params=_cparams("parallel", "arbitrary"),
    )(u, u, u, u, u, u, u, bias, o, do, lse)


def _sum_parts(parts, name):
    _, R, C = parts.shape

    def body(p_ref, o_ref):
        acc = p_ref[0].astype(F32)
        for d in range(1, N_DEV):
            acc = acc + p_ref[d].astype(F32)
        o_ref[...] = acc

    return pl.pallas_call(
        body, name=name, out_shape=jax.ShapeDtypeStruct((R, C), F32),
        compiler_params=pltpu.CompilerParams(vmem_limit_bytes=VMEM_LIMIT_V7X),
    )(parts)


def _adamw(parts, w, m, v, name, tr=256):
    P, R, C = parts.shape
    tr = _tile(R, tr)
    c1 = 1.0 / (1.0 - ADAM_B1 ** ADAM_STEP)
    c2 = 1.0 / (1.0 - ADAM_B2 ** ADAM_STEP)

    def body(p_ref, w_ref, m_ref, v_ref, g_ref, d_ref, nm_ref, nv_ref):
        g = p_ref[0].astype(F32)
        for d in range(1, P):
            g = g + p_ref[d].astype(F32)
        nm = ADAM_B1 * m_ref[...] + (1.0 - ADAM_B1) * g
        nv = ADAM_B2 * v_ref[...] + (1.0 - ADAM_B2) * (g * g)
        g_ref[...] = g
        nm_ref[...] = nm
        nv_ref[...] = nv
        d_ref[...] = -ADAM_LR * ((nm * c1) / (jnp.sqrt(nv * c2) + ADAM_EPS) + ADAM_WD * w_ref[...])

    blk = pl.BlockSpec((tr, C), lambda i: (i, 0))
    return pl.pallas_call(
        body, name=name, grid=(R // tr,),
        in_specs=[pl.BlockSpec((P, tr, C), lambda i: (0, i, 0)), blk, blk, blk],
        out_specs=[blk] * 4,
        out_shape=[jax.ShapeDtypeStruct((R, C), F32)] * 4,
        compiler_params=_cparams("parallel"),
    )(parts, w, m, v)


def _peer(d):
    x, y, c = lax.axis_index("x"), lax.axis_index("y"), lax.axis_index("c")
    px = (1 - x) if d & 4 else x
    py = (1 - y) if d & 2 else y
    pc = (1 - c) if d & 1 else c
    return (px, py, pc), 4 * px + 2 * py + pc


N_PEER = N_DEV - 1


def _exchange_copies(ins, outs, sems, scatter):
    send_sems, recv_sems, local_sems = sems
    me = 4 * lax.axis_index("x") + 2 * lax.axis_index("y") + lax.axis_index("c")
    starts, waits = [], []
    for t in range(len(ins)):
        loc = pltpu.make_async_copy(ins[t].at[me] if scatter else ins[t], outs[t].at[me], local_sems.at[t])
        starts.append(loc.start)
        waits.append(loc.wait)
        for d in range(1, N_DEV):
            peer, pidx = _peer(d)
            src = ins[t].at[pidx] if scatter else ins[t]
            k = t * N_PEER + d - 1
            common = dict(src_ref=src, send_sem=send_sems.at[k], recv_sem=recv_sems.at[k], device_id=peer, device_id_type=MESH_ID)
            starts.append(pltpu.make_async_remote_copy(dst_ref=outs[t].at[me], **common).start)
            waits.append(pltpu.make_async_remote_copy(dst_ref=outs[t].at[pidx], **common).wait)
    return starts, waits


def _exchange_scratch(n):
    return [pltpu.SemaphoreType.DMA((n * N_PEER,)), pltpu.SemaphoreType.DMA((n * N_PEER,)), pltpu.SemaphoreType.DMA((n,))]


def _exchange_shapes(arrays, scatter):
    return [jax.ShapeDtypeStruct(a.shape if scatter else (N_DEV,) + a.shape, a.dtype) for a in arrays]


def _exchange(arrays, scatter, name):
    n = len(arrays)

    def body(*refs):
        starts, waits = _exchange_copies(refs[:n], refs[n:2 * n], refs[2 * n:], scatter)
        for f in starts:
            f()
        for f in waits:
            f()

    hbm = pl.BlockSpec(memory_space=pltpu.HBM)
    return pl.pallas_call(
        body, name=name,
        in_specs=[hbm] * n, out_specs=[hbm] * n,
        out_shape=_exchange_shapes(arrays, scatter),
        scratch_shapes=_exchange_scratch(n),
        compiler_params=pltpu.CompilerParams(has_side_effects=True),
    )(*arrays)


def _call(body, *, name, grid, in_specs, out_specs, out_shape, scratch_shapes, sem, args, carry=None):
    if carry is None:
        outs = pl.pallas_call(body, name=name, grid=grid, in_specs=in_specs, out_specs=out_specs, out_shape=out_shape,
                              scratch_shapes=scratch_shapes, compiler_params=_cparams(*sem))(*args)
        return list(outs), []
    arrays, scatter = carry
    nc, n_in, n_out, n_scr = len(arrays), len(in_specs), len(out_specs), len(scratch_shapes)

    def wrapped(*refs):
        ins, cins = refs[:n_in], refs[n_in:n_in + nc]
        outs, couts = refs[n_in + nc:n_in + nc + n_out], refs[n_in + nc + n_out:n_in + 2 * nc + n_out]
        scr = refs[n_in + 2 * nc + n_out:n_in + 2 * nc + n_out + n_scr]
        ids = [pl.program_id(k) for k in range(len(grid))]
        first = functools.reduce(jnp.logical_and, [i == 0 for i in ids])
        last = functools.reduce(jnp.logical_and, [i == g - 1 for i, g in zip(ids, grid)])
        starts, waits = _exchange_copies(cins, couts, refs[-3:], scatter)

        @pl.when(first)
        def _():
            for f in starts:
                f()

        body(*ins, *outs, *scr)

        @pl.when(last)
        def _():
            for f in waits:
                f()

    hbm = pl.BlockSpec(memory_space=pltpu.HBM)
    res = pl.pallas_call(
        wrapped, name=name, grid=grid,
        in_specs=list(in_specs) + [hbm] * nc, out_specs=list(out_specs) + [hbm] * nc,
        out_shape=list(out_shape) + _exchange_shapes(arrays, scatter),
        scratch_shapes=list(scratch_shapes) + _exchange_scratch(nc),
        compiler_params=_cparams(*(("arbitrary",) * len(grid))),
    )(*args, *arrays)
    return list(res[:n_out]), list(res[n_out:])


def _relu2_fwd(acc):
    r = jnp.maximum(acc, 0.0)
    return acc, r * r


def _relu2_bwd(acc, z):
    return (acc * 2.0 * jnp.maximum(z.astype(F32), 0.0),)


def _add_scaled(scale):
    def epi(acc, t):
        return (acc + scale * t,)
    return epi


def _local_step(x, target, P, plan):
    S, D = x.shape
    H = (D // 2) // HEAD_DIM
    W = H * HEAD_DIM
    alpha = (2 * 2) ** 0.25
    g = {}

    def mm(a, b, *, name, **kw):
        res, carried = _mm(a, b, name=name, carry=plan.carry(name, g), **kw)
        plan.arrived(name, carried, P, g)
        return res

    u0 = mm(x, P["ev_w_in"], mode="nn", name="ev_in")
    a_out, cv = _conv_fwd(u0, P["ev_conv_w"], P["ev_conv_b"], P["ev_conv_ln_g"], P["ev_conv_ln_b"], "conv_fwd")
    b_out, h_raw, h_states = _hgrn_fwd(u0, P["hgrn_lb_logits"], P["ev_gnorm_g"], "hgrn_fwd")
    cat0 = jnp.concatenate([a_out.astype(BF16), b_out.astype(BF16)], axis=1)
    mix0 = mm(cat0, P["ev_w_out"], mode="nn", name="ev_out")
    x1, xh1, rs1 = _ln_fwd(x, mix0, P["ln_mix_g"][0:1], P["ln_mix_b"][0:1], alpha, "ln_mix0")
    z0, hh0 = mm(x1, P["mlp_w1_0"], mode="nn", name="mlp_up0", out_dtypes=(BF16, BF16), epi=_relu2_fwd)
    m0 = mm(hh0, P["mlp_w2_0"], mode="nn", name="mlp_down0")
    x2, xh2, rs2 = _ln_fwd(x1, m0, P["ln_mlp_g"][0:1], P["ln_mlp_b"][0:1], alpha, "ln_mlp0")

    uc = mm(x2, P["od_w_c"], mode="nn", name="od_in_c")
    ud = mm(x2, P["od_w_d"], mode="nn", name="od_in_d")
    f_logit, F = _fgate_fwd(x2, P["od_w_f_t"], P["fox_b_f"].reshape(H, 1), "fgate_fwd")
    F3 = F.reshape(H, 1, S)
    c_out, c_lse, carried = _fox_fwd(uc, F3, H, "fox_fwd", carry=plan.carry("fox_fwd", g))
    plan.arrived("fox_fwd", carried, P, g)
    bias = _relbias_fwd(P["rel_bias"], "relbias_fwd")
    d_out, d_lse = _ca_fwd(ud, bias, H, "ca_fwd")
    cat1 = jnp.concatenate([c_out.astype(BF16), d_out.astype(BF16)], axis=1)
    mix1 = mm(cat1, P["od_w_out"], mode="nn", name="od_out")
    x3, xh3, rs3 = _ln_fwd(x2, mix1, P["ln_mix_g"][1:2], P["ln_mix_b"][1:2], alpha, "ln_mix1")
    z1, hh1 = mm(x3, P["mlp_w1_1"], mode="nn", name="mlp_up1", out_dtypes=(BF16, BF16), epi=_relu2_fwd)
    m1 = mm(hh1, P["mlp_w2_1"], mode="nn", name="mlp_down1")
    x4, xh4, rs4 = _ln_fwd(x3, m1, P["ln_mlp_g"][1:2], P["ln_mlp_b"][1:2], alpha, "ln_mlp1")

    dy, loss = _loss_head(x4, target, "loss_head")

    dzm1, dg_, db_ = _ln_bwd(dy, None, 0.0, xh4, rs4, P["ln_mlp_g"][1:2], "ln_mlp1_bwd")
    g["ln_mlp_g1"], g["ln_mlp_b1"] = dg_, db_
    g["mlp_w2_1"] = mm(hh1, dzm1, mode="tn", name="mlp_down1_dw", out_dtypes=(BF16,))
    dz1 = mm(dzm1, P["mlp_w2_1"], mode="nt", name="mlp_down1_dx", out_dtypes=(BF16,), extras=(z1,), epi=_relu2_bwd)
    g["mlp_w1_1"] = mm(x3, dz1, mode="tn", name="mlp_up1_dw", out_dtypes=(BF16,))
    dx3 = mm(dz1, P["mlp_w1_1"], mode="nt", name="mlp_up1_dx", extras=(dzm1,), epi=_add_scaled(alpha))
    dzx1, dg_, db_ = _ln_bwd(dx3, None, 0.0, xh3, rs3, P["ln_mix_g"][1:2], "ln_mix1_bwd")
    g["ln_mix_g1"], g["ln_mix_b1"] = dg_, db_
    g["od_w_out"] = mm(cat1, dzx1, mode="tn", name="od_out_dw", out_dtypes=(BF16,))
    dcat1 = mm(dzx1, P["od_w_out"], mode="nt", name="od_out_dx")
    dq_d, dk_d, dv_d, dbias = _ca_bwd(ud, bias, d_out, dcat1, H, d_lse, H, "ca_bwd")
    g["rel_bias"] = _relbias_bwd(dbias, "relbias_bwd")
    dud = jnp.concatenate([dq_d, dk_d, dv_d], axis=1)
    dq_c, dk_c, dv_c, dF3 = _fox_bwd(uc, F3, c_out, dcat1, 0, c_lse, H, "fox_bwd")
    duc = jnp.concatenate([dq_c, dk_c, dv_c], axis=1)
    dfl, dbf = _fgate_bwd(dF3.reshape(H, S), f_logit, "fgate_bwd")
    g["fox_b_f"] = dbf.reshape(1, H)
    g["od_w_c"] = mm(x2, duc, mode="tn", name="od_in_c_dw", out_dtypes=(BF16,))
    g["od_w_d"] = mm(x2, dud, mode="tn", name="od_in_d_dw", out_dtypes=(BF16,))
    g["od_w_f_t"] = mm(dfl, x2, mode="nn", name="od_in_f_dw", out_dtypes=(BF16,), exact_products=True)
    dx2 = mm(duc, P["od_w_c"], mode="nt", name="od_in_c_dx", extras=(dzx1,), epi=_add_scaled(alpha))
    dx2 = mm(dud, P["od_w_d"], mode="nt", name="od_in_d_dx", extras=(dx2,), epi=_add_scaled(1.0))
    dx2 = mm(dfl, P["od_w_f_t"], mode="tn", name="od_in_f_dx", extras=(dx2,), epi=_add_scaled(1.0), exact_products=True)

    dzm0, dg_, db_ = _ln_bwd(dx2, None, 0.0, xh2, rs2, P["ln_mlp_g"][0:1], "ln_mlp0_bwd")
    g["ln_mlp_g0"], g["ln_mlp_b0"] = dg_, db_
    g["mlp_w2_0"] = mm(hh0, dzm0, mode="tn", name="mlp_down0_dw", out_dtypes=(BF16,))
    dz0 = mm(dzm0, P["mlp_w2_0"], mode="nt", name="mlp_down0_dx", out_dtypes=(BF16,), extras=(z0,), epi=_relu2_bwd)
    g["mlp_w1_0"] = mm(x1, dz0, mode="tn", name="mlp_up0_dw", out_dtypes=(BF16,))
    dx1 = mm(dz0, P["mlp_w1_0"], mode="nt", name="mlp_up0_dx", extras=(dzm0,), epi=_add_scaled(alpha))
    dzx0, dg_, db_ = _ln_bwd(dx1, None, 0.0, xh1, rs1, P["ln_mix_g"][0:1], "ln_mix0_bwd")
    g["ln_mix_g0"], g["ln_mix_b0"] = dg_, db_
    g["ev_w_out"] = mm(cat0, dzx0, mode="tn", name="ev_out_dw", out_dtypes=(BF16,))
    dcat0 = mm(dzx0, P["ev_w_out"], mode="nt", name="ev_out_dx")
    dcv, g["ev_conv_ln_g"], g["ev_conv_ln_b"], g["ev_conv_b"] = _conv_bwd_norm(dcat0, 0, cv, P["ev_conv_ln_g"], P["ev_conv_ln_b"], "conv_bwd_norm")
    du_a, du_g, g["ev_conv_w"] = _conv_bwd_taps(dcv, u0, P["ev_conv_w"], "conv_bwd_taps")
    dhq, dhf, dhi, dhg, g["ev_gnorm_g"], g["hgrn_lb_logits"] = _hgrn_bwd(dcat0, 1, u0, h_raw, h_states, P["hgrn_lb_logits"], P["ev_gnorm_g"], "hgrn_bwd")
    du0 = jnp.concatenate([du_a, du_g, dhq, dhf, dhi, dhg], axis=1)
    g["ev_w_in"] = mm(x, du0, mode="tn", name="ev_in_dw", out_dtypes=(BF16,))
    dx0 = mm(du0, P["ev_w_in"], mode="nt", name="ev_in_dx", extras=(dzx0,), epi=_add_scaled(alpha))
    return loss, dx0, g


_NAMES = ['ev_w_in', 'ev_conv_w', 'ev_conv_b', 'ev_conv_ln_g', 'ev_conv_ln_b', 'hgrn_lb_logits', 'ev_gnorm_g', 'ev_w_out',
          'od_w_in', 'fox_b_f', 'rel_bias', 'od_w_out', 'ln_mix_g', 'ln_mix_b', 'mlp_w1', 'mlp_w2', 'ln_mlp_g', 'ln_mlp_b']
_SMALL = ['ev_conv_b', 'ev_conv_ln_g', 'ev_conv_ln_b', 'hgrn_lb_logits', 'ev_gnorm_g', 'fox_b_f', 'ln_mix_g', 'ln_mix_b',
          'ln_mlp_g', 'ln_mlp_b', 'ev_conv_w', 'rel_bias']
_PACK_COLS = 2048


def _cols_to_full(gathered):
    nd, K, n = gathered.shape
    return jnp.transpose(gathered, (1, 0, 2)).reshape(K, nd * n)


def _full_to_cols(full):
    K, N = full.shape
    return jnp.transpose(full.reshape(K, N_DEV, N // N_DEV), (1, 0, 2))


_ROW_SHARDED = ("ev_w_out", "od_w_out", "mlp_w2_0", "mlp_w2_1")


def _full_weights(name, gathered, heads):
    if name in _ROW_SHARDED:
        return {name: gathered.reshape(-1, gathered.shape[-1])}
    full = _cols_to_full(gathered)
    if name != "od_w_in":
        return {name: full}
    w = heads * HEAD_DIM
    return {"od_w_c": full[:, :3 * w], "od_w_f_t": jnp.transpose(full[:, 3 * w:3 * w + heads]), "od_w_d": full[:, 3 * w + heads:]}


def _grad_blocks(name, g):
    if name in _ROW_SHARDED:
        return g[name].reshape(N_DEV, -1, g[name].shape[-1])
    if name == "od_w_in":
        return _full_to_cols(jnp.concatenate([g["od_w_c"], jnp.transpose(g["od_w_f_t"]), g["od_w_d"]], axis=1))
    return _full_to_cols(g[name])


class _Plan:
    GATHER = {"ev_in": ("mlp_w1_0",), "mlp_up0": ("mlp_w2_0",), "mlp_down0": ("od_w_in",), "od_in_c": ("od_w_out",),
              "fox_fwd": ("mlp_w1_1", "mlp_w2_1")}
    SCATTER = {"mlp_down1_dx": "mlp_w2_1", "mlp_up1_dx": "mlp_w1_1", "od_out_dx": "od_w_out", "od_in_c_dx": "od_w_in",
               "mlp_down0_dx": "mlp_w2_0", "mlp_up0_dx": "mlp_w1_0", "ev_out_dx": "ev_w_out", "ev_in_dx": "ev_w_in"}

    def __init__(self, shards, heads):
        self.shards, self.heads = shards, heads

    def carry(self, call, g):
        if call in self.GATHER:
            return [self.shards[n] for n in self.GATHER[call]], False
        if call in self.SCATTER:
            return [_grad_blocks(self.SCATTER[call], g)], True
        return None

    def arrived(self, call, carried, P, g):
        if call in self.GATHER:
            for n, gathered in zip(self.GATHER[call], carried):
                P.update(_full_weights(n, gathered, self.heads))
        elif call in self.SCATTER:
            g[self.SCATTER[call]] = carried[0]


def _pack(parts):
    flat = jnp.concatenate([p.reshape(-1).astype(F32) for p in parts])
    rows = -(-flat.shape[0] // (_PACK_COLS * 8)) * 8
    return jnp.pad(flat, (0, rows * _PACK_COLS - flat.shape[0])).reshape(rows, _PACK_COLS)


def _unpack(packed, shapes):
    flat = packed.reshape(-1)
    out, off = [], 0
    for s in shapes:
        n = math.prod(s)
        out.append(flat[off:off + n].reshape(s))
        off += n
    return out


def kernel(x, ev_w_in, ev_conv_w, ev_conv_b, ev_conv_ln_g, ev_conv_ln_b, hgrn_lb_logits, ev_gnorm_g, ev_w_out, od_w_in, fox_b_f, rel_bias, od_w_out, ln_mix_g, ln_mix_b, mlp_w1, mlp_w2, ln_mlp_g, ln_mlp_b, loss_target, m_ev_w_in, m_ev_conv_w, m_ev_conv_b, m_ev_conv_ln_g, m_ev_conv_ln_b, m_hgrn_lb_logits, m_ev_gnorm_g, m_ev_w_out, m_od_w_in, m_fox_b_f, m_rel_bias, m_od_w_out, m_ln_mix_g, m_ln_mix_b, m_mlp_w1, m_mlp_w2, m_ln_mlp_g, m_ln_mlp_b, v_ev_w_in, v_ev_conv_w, v_ev_conv_b, v_ev_conv_ln_g, v_ev_conv_ln_b, v_hgrn_lb_logits, v_ev_gnorm_g, v_ev_w_out, v_od_w_in, v_fox_b_f, v_rel_bias, v_od_w_out, v_ln_mix_g, v_ln_mix_b, v_mlp_w1, v_mlp_w2, v_ln_mlp_g, v_ln_mlp_b):
    args = locals()
    w = {n: args[n] for n in _NAMES}
    m = {n: args["m_" + n] for n in _NAMES}
    v = {n: args["v_" + n] for n in _NAMES}
    me = 4 * lax.axis_index("x") + 2 * lax.axis_index("y") + lax.axis_index("c")
    S, D = x.shape[1], x.shape[2]
    H = (D // 2) // HEAD_DIM
    W = H * HEAD_DIM
    n_layers = mlp_w1.shape[0]
    assert n_layers == 2 and ev_w_in.shape[0] == 1 and od_w_in.shape[0] == 1

    shards = {"ev_w_in": ev_w_in[0].astype(BF16), "ev_w_out": ev_w_out[0].astype(BF16),
              "od_w_in": od_w_in[0].astype(BF16), "od_w_out": od_w_out[0].astype(BF16)}
    for l in range(n_layers):
        shards["mlp_w1_%d" % l] = mlp_w1[l].astype(BF16)
        shards["mlp_w2_%d" % l] = mlp_w2[l].astype(BF16)
    first = ["ev_w_in", "ev_w_out", "ev_conv_w", "rel_bias"]
    G = _exchange([shards["ev_w_in"], shards["ev_w_out"], ev_conv_w[0], rel_bias[0]], False, "gather_first")
    P = {
        "ev_conv_b": ev_conv_b, "ev_conv_ln_g": ev_conv_ln_g, "ev_conv_ln_b": ev_conv_ln_b,
        "hgrn_lb_logits": hgrn_lb_logits, "ev_gnorm_g": ev_gnorm_g, "fox_b_f": fox_b_f,
        "ln_mix_g": ln_mix_g, "ln_mix_b": ln_mix_b, "ln_mlp_g": ln_mlp_g, "ln_mlp_b": ln_mlp_b,
    }
    for name, gathered in zip(first, G):
        P.update(_full_weights(name, gathered, H))

    loss, grad_x, g = _local_step(x[0], loss_target[0], P, _Plan(shards, H))
    recv = [g["ev_w_in"], g["ev_w_out"], g["od_w_in"], g["od_w_out"]]
    recv += [g["mlp_w1_%d" % l] for l in range(n_layers)] + [g["mlp_w2_%d" % l] for l in range(n_layers)]

    small = {
        "ev_conv_b": g["ev_conv_b"], "ev_conv_ln_g": g["ev_conv_ln_g"], "ev_conv_ln_b": g["ev_conv_ln_b"],
        "hgrn_lb_logits": g["hgrn_lb_logits"], "ev_gnorm_g": g["ev_gnorm_g"], "fox_b_f": g["fox_b_f"],
        "ln_mix_g": jnp.concatenate([g["ln_mix_g0"], g["ln_mix_g1"]]), "ln_mix_b": jnp.concatenate([g["ln_mix_b0"], g["ln_mix_b1"]]),
        "ln_mlp_g": jnp.concatenate([g["ln_mlp_g0"], g["ln_mlp_g1"]]), "ln_mlp_b": jnp.concatenate([g["ln_mlp_b0"], g["ln_mlp_b1"]]),
        "ev_conv_w": g["ev_conv_w"], "rel_bias": g["rel_bias"],
    }
    full_shapes = [small[n].shape for n in _SMALL]
    small_all = _exchange([_pack([small[n] for n in _SMALL])], False, "gather_small_grads")[0]
    small_sum = _unpack(_sum_parts(small_all, "sum_small_grads"), full_shapes)
    small_g = dict(zip(_SMALL, small_sum))
    cw = small_g["ev_conv_w"]
    small_g["ev_conv_w"] = lax.dynamic_slice_in_dim(cw, me * (cw.shape[1] // N_DEV), cw.shape[1] // N_DEV, axis=1)
    rb = small_g["rel_bias"]
    small_g["rel_bias"] = lax.dynamic_slice_in_dim(rb, me * (rb.shape[1] // N_DEV), rb.shape[1] // N_DEV, axis=1)

    out_g, out_d, out_m, out_v = {}, {}, {}, {}

    def put(name, res, shape):
        out_g[name], out_d[name], out_m[name], out_v[name] = [r.reshape(shape) for r in res]

    big = [("ev_w_in", recv[0], None), ("ev_w_out", recv[1], None), ("od_w_in", recv[2], None), ("od_w_out", recv[3], None)]
    for name, parts, _ in big:
        shp = w[name].shape
        put(name, _adamw(parts, w[name][0], m[name][0], v[name][0], "adamw_" + name), shp)
    for name, base in (("mlp_w1", 4), ("mlp_w2", 4 + n_layers)):
        res = [_adamw(recv[base + l], w[name][l], m[name][l], v[name][l], "adamw_%s_%d" % (name, l)) for l in range(n_layers)]
        put(name, [jnp.stack([res[l][k] for l in range(n_layers)]) for k in range(4)], w[name].shape)
    shapes = [w[n].shape for n in _SMALL]
    packed = _adamw(_pack([small_g[n] for n in _SMALL])[None], _pack([w[n] for n in _SMALL]), _pack([m[n] for n in _SMALL]),
                    _pack([v[n] for n in _SMALL]), "adamw_small")
    for k, dst in enumerate((out_g, out_d, out_m, out_v)):
        for n, a in zip(_SMALL, _unpack(packed[k], shapes)):
            dst[n] = a

    loss = lax.psum(loss[0, 0], ("x", "y", "c"))
    return (loss, grad_x[None], *[out_g[n] for n in _NAMES], *[out_d[n] for n in _NAMES],
            *[out_m[n] for n in _NAMES], *[out_v[n] for n in _NAMES])
```

```python
import functools
import math

import jax
import jax.numpy as jnp
from jax import lax
from jax.experimental import pallas as pl
from jax.experimental.pallas import tpu as pltpu

F32 = jnp.float32
BF16 = jnp.bfloat16
HI = lax.Precision.HIGHEST
MESH_ID = pl.DeviceIdType.MESH

N_DEV = 8
LN_EPS = 1e-5
CHUNK = 64
HEAD_DIM = 128
CONV_WIDTH = 31
CONV_HALO = 32
CA_LEFT_CHUNKS = 8
CA_TILE = 256
CA_WIN = CA_TILE + CA_LEFT_CHUNKS * CHUNK
CA_SKEW = 1024
REL_CLIP = 256
REL_TABLE = (CHUNK - 1) + REL_CLIP + 1
NEG = -1e30
ADAM_LR = 0.001
ADAM_B1 = 0.9
ADAM_B2 = 0.999
ADAM_EPS = 1e-08
ADAM_WD = 0.01
ADAM_STEP = 10
VMEM_LIMIT_V7X = 56 * 1024 * 1024


def _cparams(*sem):
    return pltpu.CompilerParams(dimension_semantics=sem, vmem_limit_bytes=VMEM_LIMIT_V7X)


def _tile(n, t):
    if n <= t:
        return n
    for c in range(t - t % 128, 0, -128):
        if n % c == 0:
            return c
    return n


def _sigmoid(x):
    return 1.0 / (1.0 + jnp.exp(-x))


def _dot(a, b, dims, precision=None):
    return lax.dot_general(a, b, (dims, ((), ())), preferred_element_type=F32, precision=precision)


def _round_bf16(x):
    return x.astype(BF16).astype(F32)


NN = ((1,), (0,))
NT = ((1,), (1,))
TN = ((0,), (0,))


def _select(idx, loads):
    if len(loads) == 1:
        return loads[0]()
    mid = len(loads) // 2
    return lax.cond(idx < mid, lambda: _select(idx, loads[:mid]), lambda: _select(idx - mid, loads[mid:]))


def _mm(a, b, *, mode, name, out_dtypes=(F32,), extras=(), epi=None, exact_products=False, carry=None,
        b_blocked=False, out_blocked=False, tm=1024, tn=1024, tk=2048):
    dims = {"nn": NN, "nt": NT, "tn": TN}[mode]
    a_parts = a if isinstance(a, tuple) else (a,)
    n_a = len(a_parts)
    a_split_k = n_a > 1 and mode != "tn"
    a_split_m = n_a > 1 and mode == "tn"
    if mode == "tn":
        K, M = a_parts[0].shape[0], a_parts[0].shape[1] * n_a
    else:
        M, K = a_parts[0].shape[0], a_parts[0].shape[1] * n_a
    if b_blocked:
        nb, _, shard = b.shape
        N = b.shape[1] if mode == "nt" else nb * shard
    else:
        N = b.shape[0] if mode == "nt" else b.shape[1]
    tm, tn, tk = _tile(M, tm), _tile(N, tn), _tile(K, tk)
    if a_split_k:
        tk = K // n_a
    if a_split_m:
        tm = M // n_a
    if b_blocked and mode == "nt":
        tk = shard
    if b_blocked and mode == "nn":
        tn = shard
    if out_blocked:
        tn = N // N_DEV
    nk = K // tk
    n_ex, n_out = len(extras), len(out_dtypes)

    def body(*refs):
        a_refs, b_ref = refs[:n_a], refs[n_a]
        ex_refs = refs[n_a + 1:n_a + 1 + n_ex]
        o_refs = refs[n_a + 1 + n_ex:n_a + 1 + n_ex + n_out]
        cast = _round_bf16 if exact_products else (lambda t: t.astype(BF16))
        part = pl.program_id(0) if a_split_m else pl.program_id(2)
        av = _select(part, [functools.partial(lambda r: cast(r[...]), r) for r in a_refs])
        d = _dot(av, cast(b_ref[...]), dims, HI if exact_products else None)

        def finish(acc):
            outs = (acc,) if epi is None else epi(acc, *[r[...] for r in ex_refs])
            for o_ref, o in zip(o_refs, outs):
                o_ref[...] = o.astype(o_ref.dtype)

        if nk == 1:
            finish(d)
        else:
            acc_ref = refs[-1]
            k = pl.program_id(2)

            @pl.when(k == 0)
            def _():
                acc_ref[...] = d

            @pl.when(k > 0)
            def _():
                acc_ref[...] += d

            @pl.when(k == nk - 1)
            def _():
                finish(acc_ref[...])

    if a_split_k:
        a_specs = [pl.BlockSpec((tm, tk), lambda i, j, k: (i, 0))] * n_a
    elif a_split_m:
        a_specs = [pl.BlockSpec((tk, tm), functools.partial(lambda p, i, j, k: (jnp.where(i == p, k, 0), 0), p)) for p in range(n_a)]
    elif mode == "tn":
        a_specs = [pl.BlockSpec((tk, tm), lambda i, j, k: (k, i))]
    else:
        a_specs = [pl.BlockSpec((tm, tk), lambda i, j, k: (i, k))]
    if b_blocked:
        b_spec = pl.BlockSpec((None, tn, tk), lambda i, j, k: (k, j, 0)) if mode == "nt" else pl.BlockSpec((None, tk, tn), lambda i, j, k: (j, k, 0))
    else:
        b_spec = pl.BlockSpec((tn, tk), lambda i, j, k: (j, k)) if mode == "nt" else pl.BlockSpec((tk, tn), lambda i, j, k: (k, j))
    mn_spec = pl.BlockSpec((tm, tn), lambda i, j, k: (i, j))
    if out_blocked:
        out_specs = [pl.BlockSpec((None, tm, tn), lambda i, j, k: (j, i, 0))] * n_out
        out_shape = [jax.ShapeDtypeStruct((N_DEV, M, tn), dt) for dt in out_dtypes]
    else:
        out_specs = [mn_spec] * n_out
        out_shape = [jax.ShapeDtypeStruct((M, N), dt) for dt in out_dtypes]
    outs, carried = _call(
        body, name=name, grid=(M // tm, N // tn, nk),
        in_specs=a_specs + [b_spec] + [mn_spec] * n_ex,
        out_specs=out_specs, out_shape=out_shape,
        scratch_shapes=[pltpu.VMEM((tm, tn), F32)] if nk > 1 else [],
        sem=("parallel", "parallel", "arbitrary"), args=(*a_parts, b, *extras), carry=carry)
    return (outs[0] if n_out == 1 else outs), carried


def _ln_fwd(x, r, g, b, alpha, name, tr=256):
    S, D = x.shape
    tr = _tile(S, tr)

    def body(x_ref, r_ref, g_ref, b_ref, y_ref, xh_ref, rs_ref):
        z = alpha * x_ref[...] + r_ref[...]
        zc = z - jnp.mean(z, axis=-1, keepdims=True)
        rs = lax.rsqrt(jnp.mean(zc * zc, axis=-1, keepdims=True) + LN_EPS)
        xh = zc * rs
        xh_ref[...] = xh
        rs_ref[...] = rs
        y_ref[...] = xh * g_ref[...] + b_ref[...]

    row = pl.BlockSpec((tr, D), lambda i: (i, 0))
    par = pl.BlockSpec((1, D), lambda i: (0, 0))
    return pl.pallas_call(
        body, name=name, grid=(S // tr,),
        in_specs=[row, row, par, par],
        out_specs=[row, row, pl.BlockSpec((tr, 1), lambda i: (i, 0))],
        out_shape=[jax.ShapeDtypeStruct((S, D), F32), jax.ShapeDtypeStruct((S, D), F32), jax.ShapeDtypeStruct((S, 1), F32)],
        compiler_params=_cparams("parallel"),
    )(x, r, g, b)


def _ln_bwd(dy, dy2, scale2, xh, rs, g, name, tr=256):
    S, D = dy.shape
    tr = _tile(S, tr)
    two = dy2 is not None

    def body(*refs):
        if two:
            dy_ref, dy2_ref, xh_ref, rs_ref, g_ref, dz_ref, dg_ref, db_ref = refs
            dyt = dy_ref[...] + scale2 * dy2_ref[...]
        else:
            dy_ref, xh_ref, rs_ref, g_ref, dz_ref, dg_ref, db_ref = refs
            dyt = dy_ref[...]
        xh = xh_ref[...]
        dxh = dyt * g_ref[...]
        m1 = jnp.mean(dxh, axis=-1, keepdims=True)
        m2 = jnp.mean(dxh * xh, axis=-1, keepdims=True)
        dz_ref[...] = rs_ref[...] * (dxh - m1 - xh * m2)

        @pl.when(pl.program_id(0) == 0)
        def _():
            dg_ref[...] = jnp.zeros_like(dg_ref)
            db_ref[...] = jnp.zeros_like(db_ref)

        dg_ref[...] += jnp.sum(dyt * xh, axis=0, keepdims=True)
        db_ref[...] += jnp.sum(dyt, axis=0, keepdims=True)

    row = pl.BlockSpec((tr, D), lambda i: (i, 0))
    par = pl.BlockSpec((1, D), lambda i: (0, 0))
    ins = [dy] + ([dy2] if two else []) + [xh, rs, g]
    return pl.pallas_call(
        body, name=name, grid=(S // tr,),
        in_specs=[row] * (2 if two else 1) + [row, pl.BlockSpec((tr, 1), lambda i: (i, 0)), par],
        out_specs=[row, par, par],
        out_shape=[jax.ShapeDtypeStruct((S, D), F32), jax.ShapeDtypeStruct((1, D), F32), jax.ShapeDtypeStruct((1, D), F32)],
        compiler_params=_cparams("arbitrary"),
    )(*ins)


def _loss_head(y, target, name, tr=256):
    S, D = y.shape
    tr = _tile(S, tr)

    def body(y_ref, t_ref, dy_ref, loss_ref):
        e = y_ref[...] - t_ref[...]
        dy_ref[...] = e * (1.0 / D)

        @pl.when(pl.program_id(0) == 0)
        def _():
            loss_ref[...] = jnp.zeros_like(loss_ref)

        loss_ref[...] += jnp.sum(jnp.sum(e * e, axis=-1, keepdims=True), axis=0, keepdims=True) * (0.5 / D)

    row = pl.BlockSpec((tr, D), lambda i: (i, 0))
    return pl.pallas_call(
        body, name=name, grid=(S // tr,),
        in_specs=[row, row],
        out_specs=[row, pl.BlockSpec((1, 1), lambda i: (0, 0))],
        out_shape=[jax.ShapeDtypeStruct((S, D), F32), jax.ShapeDtypeStruct((1, 1), F32)],
        compiler_params=_cparams("arbitrary"),
    )(y, target)


SUBLANES = 8


def _tap_reads(ext, shifted, tt, offset_of_tap):
    for r in range(SUBLANES):
        taps = [k for k in range(CONV_WIDTH) if offset_of_tap(k) % SUBLANES == r]
        if r == 0:
            src = ext
        else:
            shifted[...] = ext[pl.ds(r, tt + CONV_HALO - SUBLANES), :]
            src = shifted
        for k in taps:
            yield src, k, offset_of_tap(k) - r


def _conv_fwd(u, w, cb, lg, lb, name, tt=512):
    S = u.shape[0]
    C = w.shape[1]
    tt = _tile(S, tt)
    hpt = tt // CONV_HALO

    def body(a_ref, g_ref, ap_ref, gp_ref, w_ref, cb_ref, lg_ref, lb_ref, out_ref, cv_ref, hext, shifted):
        i = pl.program_id(0)
        hext[pl.ds(CONV_HALO, tt), :] = a_ref[...] * _sigmoid(g_ref[...])
        hp = ap_ref[...] * _sigmoid(gp_ref[...])
        hext[pl.ds(0, CONV_HALO), :] = jnp.where(i > 0, hp, 0.0)
        acc = jnp.zeros((tt, C), F32)
        for src, k, row0 in _tap_reads(hext, shifted, tt, lambda k: CONV_HALO - (CONV_WIDTH - 1) + k):
            acc = acc + w_ref[pl.ds(k, 1), :] * src[pl.ds(row0, tt), :]
        cv = acc + cb_ref[...]
        cv_ref[...] = cv
        zc = cv - jnp.mean(cv, axis=-1, keepdims=True)
        n = zc * lax.rsqrt(jnp.mean(zc * zc, axis=-1, keepdims=True) + LN_EPS) * lg_ref[...] + lb_ref[...]
        out_ref[...] = n * _sigmoid(n)

    cur = lambda cb_: pl.BlockSpec((tt, C), lambda i: (i, cb_))
    prev = lambda cb_: pl.BlockSpec((CONV_HALO, C), lambda i: (jnp.maximum(i * hpt - 1, 0), cb_))
    par = pl.BlockSpec((1, C), lambda i: (0, 0))
    row = pl.BlockSpec((tt, C), lambda i: (i, 0))
    return pl.pallas_call(
        body, name=name, grid=(S // tt,),
        in_specs=[cur(0), cur(1), prev(0), prev(1), pl.BlockSpec((CONV_WIDTH, C), lambda i: (0, 0)), par, par, par],
        out_specs=[row, row],
        out_shape=[jax.ShapeDtypeStruct((S, C), F32)] * 2,
        scratch_shapes=[pltpu.VMEM((tt + CONV_HALO, C), F32), pltpu.VMEM((tt + CONV_HALO - SUBLANES, C), F32)],
        compiler_params=_cparams("parallel"),
    )(u, u, u, u, w, cb, lg, lb)


def _conv_bwd_norm(da, da_col, cv, lg, lb, name, tt=512):
    S, C = cv.shape
    tt = _tile(S, tt)

    def body(da_ref, cv_ref, lg_ref, lb_ref, dcv_ref, dlg_ref, dlb_ref, dcb_ref):
        cv = cv_ref[...]
        zc = cv - jnp.mean(cv, axis=-1, keepdims=True)
        rs = lax.rsqrt(jnp.mean(zc * zc, axis=-1, keepdims=True) + LN_EPS)
        xh = zc * rs
        n = xh * lg_ref[...] + lb_ref[...]
        sg = _sigmoid(n)
        dn = da_ref[...] * sg * (1.0 + n * (1.0 - sg))
        dxh = dn * lg_ref[...]
        m1 = jnp.mean(dxh, axis=-1, keepdims=True)
        m2 = jnp.mean(dxh * xh, axis=-1, keepdims=True)
        dcv = rs * (dxh - m1 - xh * m2)
        dcv_ref[...] = dcv

        @pl.when(pl.program_id(0) == 0)
        def _():
            dlg_ref[...] = jnp.zeros_like(dlg_ref)
            dlb_ref[...] = jnp.zeros_like(dlb_ref)
            dcb_ref[...] = jnp.zeros_like(dcb_ref)

        dlg_ref[...] += jnp.sum(dn * xh, axis=0, keepdims=True)
        dlb_ref[...] += jnp.sum(dn, axis=0, keepdims=True)
        dcb_ref[...] += jnp.sum(dcv, axis=0, keepdims=True)

    row = pl.BlockSpec((tt, C), lambda i: (i, 0))
    par = pl.BlockSpec((1, C), lambda i: (0, 0))
    return pl.pallas_call(
        body, name=name, grid=(S // tt,),
        in_specs=[pl.BlockSpec((tt, C), lambda i: (i, da_col)), row, par, par],
        out_specs=[row, par, par, par],
        out_shape=[jax.ShapeDtypeStruct((S, C), F32)] + [jax.ShapeDtypeStruct((1, C), F32)] * 3,
        compiler_params=_cparams("arbitrary"),
    )(da, cv, lg, lb)


def _conv_bwd_taps(dcv, u, w, name, tt=512):
    S, C = dcv.shape
    tt = _tile(S, tt)
    hpt = tt // CONV_HALO
    nt = S // tt
    WPAD = 32

    def body(dc_ref, dn_ref, a_ref, g_ref, ap_ref, gp_ref, w_ref, da_ref, dg_ref, dw_ref, hext, dext, shifted):
        i = pl.program_id(0)
        a = a_ref[...]
        sg = _sigmoid(g_ref[...])
        hext[pl.ds(CONV_HALO, tt), :] = a * sg
        hp = ap_ref[...] * _sigmoid(gp_ref[...])
        hext[pl.ds(0, CONV_HALO), :] = jnp.where(i > 0, hp, 0.0)
        dc = dc_ref[...]
        dext[pl.ds(0, tt), :] = dc
        dext[pl.ds(tt, CONV_HALO), :] = jnp.where(i < nt - 1, dn_ref[...], 0.0)

        @pl.when(i == 0)
        def _():
            dw_ref[...] = jnp.zeros_like(dw_ref)

        dh = jnp.zeros((tt, C), F32)
        for src, k, row0 in _tap_reads(dext, shifted, tt, lambda k: CONV_WIDTH - 1 - k):
            dh = dh + w_ref[pl.ds(k, 1), :] * src[pl.ds(row0, tt), :]
        for src, k, row0 in _tap_reads(hext, shifted, tt, lambda k: CONV_HALO - (CONV_WIDTH - 1) + k):
            dw_ref[pl.ds(k, 1), :] += jnp.sum(dc * src[pl.ds(row0, tt), :], axis=0, keepdims=True)
        da_ref[...] = dh * sg
        dg_ref[...] = dh * a * sg * (1.0 - sg)

    row = pl.BlockSpec((tt, C), lambda i: (i, 0))
    nxt = pl.BlockSpec((CONV_HALO, C), lambda i: (jnp.minimum((i + 1) * hpt, S // CONV_HALO - 1), 0))
    cur = lambda cb_: pl.BlockSpec((tt, C), lambda i: (i, cb_))
    prev = lambda cb_: pl.BlockSpec((CONV_HALO, C), lambda i: (jnp.maximum(i * hpt - 1, 0), cb_))
    da, dg, dw = pl.pallas_call(
        body, name=name, grid=(nt,),
        in_specs=[row, nxt, cur(0), cur(1), prev(0), prev(1), pl.BlockSpec((CONV_WIDTH, C), lambda i: (0, 0))],
        out_specs=[row, row, pl.BlockSpec((WPAD, C), lambda i: (0, 0))],
        out_shape=[jax.ShapeDtypeStruct((S, C), F32)] * 2 + [jax.ShapeDtypeStruct((WPAD, C), F32)],
        scratch_shapes=[pltpu.VMEM((tt + CONV_HALO, C), F32)] * 2 + [pltpu.VMEM((tt + CONV_HALO - SUBLANES, C), F32)],
        compiler_params=_cparams("arbitrary"),
    )(dcv, dcv, u, u, u, u, w)
    return da, dg, dw[:CONV_WIDTH]


def _lower_bound(logits):
    e = jnp.exp(logits - jnp.max(logits, axis=0, keepdims=True))
    p = e / jnp.sum(e, axis=0, keepdims=True)
    return p[0:1, :], p


def _tri(n, lower):
    r = lax.broadcasted_iota(jnp.int32, (n, n), 0)
    c = lax.broadcasted_iota(jnp.int32, (n, n), 1)
    return ((c <= r) if lower else (c >= r)).astype(F32)


def _hgrn_gates(xq, xf, lb):
    sq = _sigmoid(xq)
    q = xq * sq
    sf = _sigmoid(xf)
    f = lb + (1.0 - lb) * sf
    logf = jnp.log(f)
    L = _dot(_tri(CHUNK, True), logf, NN, HI)
    Lend = L[CHUNK - 1:CHUNK, :]
    eL = jnp.exp(L)
    enL = jnp.exp(-L)
    eLe = jnp.exp(Lend - L)
    kk = 1.0 - f
    return dict(sq=sq, q=q, sf=sf, f=f, L=L, Lend=Lend, eL=eL, enL=enL, eLe=eLe, kk=kk,
                qe=q * eL, ke=kk * enL, kd=kk * eLe)


def _hgrn_fwd(u, lbl, gn, name, tb=256):
    S = u.shape[0]
    W = gn.shape[1]
    H = W // HEAD_DIM
    tb = _tile(S, tb)
    cpb = tb // CHUNK
    nc = S // CHUNK
    R = lbl.shape[0]

    def body(q_ref, f_ref, i_ref, g_ref, lbl_ref, gn_ref, out_ref, raw_ref, st_ref, state):
        @pl.when(pl.program_id(0) == 0)
        def _():
            state[...] = jnp.zeros_like(state)

        lb, _ = _lower_bound(lbl_ref[...])
        tril = _tri(CHUNK, True) > 0.5

        def chunk(c, carry):
            rows = pl.ds(pl.multiple_of(c * CHUNK, CHUNK), CHUNK)
            G = _hgrn_gates(q_ref[rows, :], f_ref[rows, :], lb)
            v = i_ref[rows, :]
            xg = g_ref[rows, :]
            outs = []
            for h in range(H):
                ln = slice(h * HEAD_DIM, (h + 1) * HEAD_DIM)
                qe, ke, kd, vh = G["qe"][:, ln].astype(BF16), G["ke"][:, ln].astype(BF16), G["kd"][:, ln].astype(BF16), v[:, ln].astype(BF16)
                st = state[h]
                st_ref[c, h] = st
                A = jnp.where(tril, _dot(qe, ke, NT), 0.0)
                o = _dot(A.astype(BF16), vh, NN) + _dot(qe, st.astype(BF16), NT)
                state[h] = jnp.exp(G["Lend"][:, ln]) * st + _dot(vh, kd, TN)
                outs.append(o)
            o = jnp.concatenate(outs, axis=1)
            raw_ref[rows, :] = o
            ns = []
            for h in range(H):
                oh = outs[h]
                ns.append(oh * lax.rsqrt(jnp.mean(oh * oh, axis=-1, keepdims=True) + LN_EPS))
            n = jnp.concatenate(ns, axis=1)
            out_ref[rows, :] = n * gn_ref[...] * (xg * _sigmoid(xg))
            return carry

        lax.fori_loop(0, cpb, chunk, 0)

    col = lambda cb_: pl.BlockSpec((tb, W), lambda i: (i, cb_))
    row = pl.BlockSpec((tb, W), lambda i: (i, 0))
    return pl.pallas_call(
        body, name=name, grid=(S // tb,),
        in_specs=[col(2), col(3), col(4), col(5), pl.BlockSpec((R, W), lambda i: (0, 0)), pl.BlockSpec((1, W), lambda i: (0, 0))],
        out_specs=[row, row, pl.BlockSpec((cpb, H, HEAD_DIM, HEAD_DIM), lambda i: (i, 0, 0, 0))],
        out_shape=[jax.ShapeDtypeStruct((S, W), F32), jax.ShapeDtypeStruct((S, W), F32),
                   jax.ShapeDtypeStruct((nc, H, HEAD_DIM, HEAD_DIM), F32)],
        scratch_shapes=[pltpu.VMEM((H, HEAD_DIM, HEAD_DIM), F32)],
        compiler_params=_cparams("arbitrary"),
    )(u, u, u, u, lbl, gn)


def _hgrn_bwd(dout, dout_col, u, raw, states, lbl, gn, name, tb=256):
    S = u.shape[0]
    W = gn.shape[1]
    H = W // HEAD_DIM
    tb = _tile(S, tb)
    cpb = tb // CHUNK
    nb = S // tb
    R = lbl.shape[0]

    def body(do_ref, q_ref, f_ref, i_ref, g_ref, raw_ref, st_ref, lbl_ref, gn_ref,
             dq_ref, df_ref, di_ref, dg_ref, dgn_ref, dlbl_ref, dstate, dlb_acc):
        @pl.when(pl.program_id(0) == 0)
        def _():
            dstate[...] = jnp.zeros_like(dstate)
            dlb_acc[...] = jnp.zeros_like(dlb_acc)
            dgn_ref[...] = jnp.zeros_like(dgn_ref)

        lb, p = _lower_bound(lbl_ref[...])
        tril = _tri(CHUNK, True) > 0.5
        triu = _tri(CHUNK, False)
        gn_row = gn_ref[...]

        def chunk(cc, carry):
            c = cpb - 1 - cc
            rows = pl.ds(pl.multiple_of(c * CHUNK, CHUNK), CHUNK)
            xq, xf = q_ref[rows, :], f_ref[rows, :]
            G = _hgrn_gates(xq, xf, lb)
            v = i_ref[rows, :]
            xg = g_ref[rows, :]
            dy = do_ref[rows, :]
            o = raw_ref[rows, :]
            sgg = _sigmoid(xg)
            silu_g = xg * sgg
            do_parts, n_parts = [], []
            for h in range(H):
                ln = slice(h * HEAD_DIM, (h + 1) * HEAD_DIM)
                oh = o[:, ln]
                r = lax.rsqrt(jnp.mean(oh * oh, axis=-1, keepdims=True) + LN_EPS)
                nh = oh * r
                dn = dy[:, ln] * gn_row[:, ln] * silu_g[:, ln]
                do_parts.append(r * (dn - nh * jnp.mean(dn * nh, axis=-1, keepdims=True)))
                n_parts.append(nh)
            n = jnp.concatenate(n_parts, axis=1)
            dgn_ref[...] += jnp.sum(dy * n * silu_g, axis=0, keepdims=True)
            dg_ref[rows, :] = dy * n * gn_row * sgg * (1.0 + xg * (1.0 - sgg))
            dqe_p, dke_p, dkd_p, dv_p, dle_p = [], [], [], [], []
            for h in range(H):
                ln = slice(h * HEAD_DIM, (h + 1) * HEAD_DIM)
                qe, ke, kd, vh = G["qe"][:, ln].astype(BF16), G["ke"][:, ln].astype(BF16), G["kd"][:, ln].astype(BF16), v[:, ln].astype(BF16)
                doh = do_parts[h].astype(BF16)
                st = st_ref[c, h]
                dst = dstate[h]
                dec = jnp.exp(G["Lend"][:, ln])
                A = jnp.where(tril, _dot(qe, ke, NT), 0.0).astype(BF16)
                dA = jnp.where(tril, _dot(doh, vh, NT), 0.0).astype(BF16)
                dv_p.append(_dot(A, doh, TN) + _dot(kd, dst.astype(BF16), NT))
                dqe_p.append(_dot(dA, ke, NN) + _dot(doh, st.astype(BF16), NN))
                dke_p.append(_dot(dA, qe, TN))
                dkd_p.append(_dot(vh, dst.astype(BF16), NN))
                dle_p.append(jnp.sum(dst * dec * st, axis=0, keepdims=True))
                dstate[h] = dec * dst + _dot(doh, qe, TN)
            dqe = jnp.concatenate(dqe_p, axis=1)
            dke = jnp.concatenate(dke_p, axis=1)
            dkd = jnp.concatenate(dkd_p, axis=1)
            qe_r, ke_r, kd_r = _round_bf16(G["qe"]), _round_bf16(G["ke"]), _round_bf16(G["kd"])
            dLend = jnp.concatenate(dle_p, axis=1) + jnp.sum(dkd * kd_r, axis=0, keepdims=True)
            dL = dqe * qe_r - dke * ke_r - dkd * kd_r
            dlogf = _dot(triu, dL, NN, HI) + dLend
            dkk = dke * G["enL"] + dkd * G["eLe"]
            dfv = dlogf / G["f"] - dkk
            sf = G["sf"]
            df_ref[rows, :] = dfv * (1.0 - lb) * sf * (1.0 - sf)
            dlb_acc[...] += jnp.sum(dfv * (1.0 - sf), axis=0, keepdims=True)
            sq = G["sq"]
            dq_ref[rows, :] = dqe * G["eL"] * sq * (1.0 + xq * (1.0 - sq))
            di_ref[rows, :] = jnp.concatenate(dv_p, axis=1)
            return carry

        lax.fori_loop(0, cpb, chunk, 0)

        onehot0 = (lax.broadcasted_iota(jnp.int32, (R, W), 0) == 0).astype(F32)
        dlbl_ref[...] = p * (onehot0 - p[0:1, :]) * dlb_acc[...]

    rev = lambda i: nb - 1 - i
    col = lambda cb_: pl.BlockSpec((tb, W), lambda i: (rev(i), cb_))
    row = pl.BlockSpec((tb, W), lambda i: (rev(i), 0))
    par = pl.BlockSpec((1, W), lambda i: (0, 0))
    parR = pl.BlockSpec((R, W), lambda i: (0, 0))
    return pl.pallas_call(
        body, name=name, grid=(nb,),
        in_specs=[pl.BlockSpec((tb, W), lambda i: (rev(i), dout_col)), col(2), col(3), col(4), col(5), row,
                  pl.BlockSpec((cpb, H, HEAD_DIM, HEAD_DIM), lambda i: (rev(i), 0, 0, 0)), parR, par],
        out_specs=[row, row, row, row, par, parR],
        out_shape=[jax.ShapeDtypeStruct((S, W), F32)] * 4 + [jax.ShapeDtypeStruct((1, W), F32), jax.ShapeDtypeStruct((R, W), F32)],
        scratch_shapes=[pltpu.VMEM((H, HEAD_DIM, HEAD_DIM), F32), pltpu.VMEM((1, W), F32)],
        compiler_params=_cparams("arbitrary"),
    )(dout, u, u, u, u, raw, states, lbl, gn)


def _fgate_fwd(xb, wft, bf, name, ts=512):
    S, D = xb.shape
    H = wft.shape[0]
    ts = _tile(S, ts)

    def body(x_ref, w_ref, b_ref, lg_ref, F_ref, carry):
        @pl.when(pl.program_id(0) == 0)
        def _():
            carry[...] = jnp.zeros_like(carry)

        lg = _dot(_round_bf16(w_ref[...]), _round_bf16(x_ref[...]), NT, HI) + b_ref[...]
        lg_ref[...] = lg
        ls = jnp.minimum(lg, 0.0) - jnp.log(1.0 + jnp.exp(-jnp.abs(lg)))
        F = _dot(ls, _tri(ts, False), NN, HI) + carry[...]
        F_ref[...] = F
        carry[...] = F[:, ts - 1:ts]

    return pl.pallas_call(
        body, name=name, grid=(S // ts,),
        in_specs=[pl.BlockSpec((ts, D), lambda i: (i, 0)), pl.BlockSpec((H, D), lambda i: (0, 0)), pl.BlockSpec((H, 1), lambda i: (0, 0))],
        out_specs=[pl.BlockSpec((H, ts), lambda i: (0, i))] * 2,
        out_shape=[jax.ShapeDtypeStruct((H, S), F32)] * 2,
        scratch_shapes=[pltpu.VMEM((H, 1), F32)],
        compiler_params=_cparams("arbitrary"),
    )(xb, wft, bf)


def _fgate_bwd(dF, lg, name, ts=512):
    H, S = dF.shape
    ts = _tile(S, ts)
    nb = S // ts

    def body(dF_ref, lg_ref, dl_ref, db_ref, carry):
        @pl.when(pl.program_id(0) == 0)
        def _():
            carry[...] = jnp.zeros_like(carry)
            db_ref[...] = jnp.zeros_like(db_ref)

        dls = _dot(dF_ref[...], _tri(ts, True), NN, HI) + carry[...]
        carry[...] = dls[:, 0:1]
        dl = dls * _sigmoid(-lg_ref[...])
        dl_ref[...] = dl
        db_ref[...] += jnp.sum(dl, axis=1, keepdims=True)

    blk = pl.BlockSpec((H, ts), lambda i: (0, nb - 1 - i))
    return pl.pallas_call(
        body, name=name, grid=(nb,),
        in_specs=[blk, blk],
        out_specs=[blk, pl.BlockSpec((H, 1), lambda i: (0, 0))],
        out_shape=[jax.ShapeDtypeStruct((H, S), F32), jax.ShapeDtypeStruct((H, 1), F32)],
        scratch_shapes=[pltpu.VMEM((H, 1), F32)],
        compiler_params=_cparams("arbitrary"),
    )(dF, lg)


def _causal_keep(i, j, tq, tk):
    rows = lax.broadcasted_iota(jnp.int32, (tq, tk), 0)
    cols = lax.broadcasted_iota(jnp.int32, (tq, tk), 1)
    return jnp.logical_or(j < i, cols <= rows)


def _causal_grid(S, t):
    n = S // t
    assert n % 2 == 0, (S, t)

    def by_query(r, c):
        second = c > r
        return jnp.where(second, n - 1 - r, r), jnp.where(second, c - r - 1, c)

    def by_key(r, c):
        second = c >= n - r
        return jnp.where(second, c - 1, r + c), jnp.where(second, n - 1 - r, r)

    return n, (n // 2, n + 1), by_query, by_key


def _fox_scores(q_ref, k_ref, fk_ref, i, j, t, scale):
    s = _dot(q_ref[...].astype(BF16), k_ref[...].astype(BF16), NT) * scale - fk_ref[...]
    return lax.cond(i == j, lambda x: jnp.where(_causal_keep(i, j, t, t), x, NEG), lambda x: x, s)


def _softmax_bwd(p, pb, dp, delta):
    return pb.astype(F32) * dp - p * delta


FOX_TILE = 1024


def _fox_tile(S):
    return _tile(S, min(FOX_TILE, S // 2))


def _fox_fwd(u, F3, H, name, carry=None):
    S = u.shape[0]
    W = H * HEAD_DIM
    t = _fox_tile(S)
    n, tri, ij, _ = _causal_grid(S, t)
    scale = HEAD_DIM ** -0.5

    def body(q_ref, k_ref, v_ref, fk_ref, o_ref, lse_ref, m_s, l_s, acc_s):
        i, j = ij(pl.program_id(1), pl.program_id(2))

        @pl.when(j == 0)
        def _():
            m_s[...] = jnp.full_like(m_s, NEG)
            l_s[...] = jnp.zeros_like(l_s)
            acc_s[...] = jnp.zeros_like(acc_s)

        s = _fox_scores(q_ref, k_ref, fk_ref, i, j, t, scale)
        m_new = jnp.maximum(m_s[...], jnp.max(s, axis=-1, keepdims=True))
        a = jnp.exp(m_s[...] - m_new)
        p = jnp.exp(s - m_new)
        hi = p.astype(BF16)
        lo = (p - hi.astype(F32)).astype(BF16)
        v = v_ref[...].astype(BF16)
        l_s[...] = a * l_s[...] + jnp.sum(p, axis=-1, keepdims=True)
        acc_s[...] = a * acc_s[...] + (_dot(hi, v, NN) + _dot(lo, v, NN))
        m_s[...] = m_new

        @pl.when(j == i)
        def _():
            o_ref[...] = acc_s[...] / l_s[...]
            lse_ref[...] = m_s[...] + jnp.log(l_s[...])

    qblk = pl.BlockSpec((t, HEAD_DIM), lambda h, r, c: (ij(r, c)[0], h))
    kv = lambda off: pl.BlockSpec((t, HEAD_DIM), lambda h, r, c: (ij(r, c)[1], off * H + h))
    outs, carried = _call(
        body, name=name, grid=(H,) + tri,
        in_specs=[qblk, kv(1), kv(2), pl.BlockSpec((None, 1, t), lambda h, r, c: (h, 0, ij(r, c)[1]))],
        out_specs=[qblk, pl.BlockSpec((None, t, 1), lambda h, r, c: (h, ij(r, c)[0], 0))],
        out_shape=[jax.ShapeDtypeStruct((S, W), F32), jax.ShapeDtypeStruct((H, S, 1), F32)],
        scratch_shapes=[pltpu.VMEM((t, 1), F32), pltpu.VMEM((t, 1), F32), pltpu.VMEM((t, HEAD_DIM), F32)],
        sem=("parallel", "parallel", "arbitrary"), args=(u, u, u, F3), carry=carry)
    return outs[0], outs[1], carried


def _fox_bwd(u, F3, o, do, do_off, lse, H, name):
    S = u.shape[0]
    W = H * HEAD_DIM
    t = _fox_tile(S)
    n, tri, _, ij = _causal_grid(S, t)
    scale = HEAD_DIM ** -0.5

    def body(q_ref, k_ref, v_ref, fk_ref, o_ref, do_ref, lse_ref, dq_ref, dk_ref, dv_ref, dF_ref, dk_s, dv_s, dF_s):
        r, c = pl.program_id(1), pl.program_id(2)
        i, j = ij(r, c)

        @pl.when(jnp.logical_and(r == 0, c == 0))
        def _():
            dq_ref[...] = jnp.zeros_like(dq_ref)

        @pl.when(i == j)
        def _():
            dk_s[...] = jnp.zeros_like(dk_s)
            dv_s[...] = jnp.zeros_like(dv_s)
            dF_s[...] = jnp.zeros_like(dF_s)

        q = q_ref[...].astype(BF16)
        k = k_ref[...].astype(BF16)
        dob = do_ref[...].astype(BF16)
        p = jnp.exp(_fox_scores(q_ref, k_ref, fk_ref, i, j, t, scale) - lse_ref[...])
        dp = _dot(dob, v_ref[...].astype(BF16), NT)
        delta = jnp.sum(dob.astype(F32) * o_ref[...], axis=-1, keepdims=True)
        ds = p * (dp - delta)
        dsb = ds.astype(BF16)
        dv_s[...] += _dot(p.astype(BF16), dob, TN)
        dk_s[...] += _dot(dsb, q, TN)
        dF_s[...] -= jnp.sum(ds, axis=0, keepdims=True)
        rows = pl.ds(pl.multiple_of(i * t, t), t)
        dq_ref[rows, :] += _dot(dsb, k, NN) * scale

        @pl.when(i == n - 1)
        def _():
            dk_ref[...] = dk_s[...] * scale
            dv_ref[...] = dv_s[...]
            dF_ref[...] = dF_s[...]

    qblk = pl.BlockSpec((t, HEAD_DIM), lambda h, r, c: (ij(r, c)[0], h))
    kv = lambda off: pl.BlockSpec((t, HEAD_DIM), lambda h, r, c: (ij(r, c)[1], off * H + h))
    oblk = pl.BlockSpec((t, HEAD_DIM), lambda h, r, c: (ij(r, c)[1], h))
    fblk = pl.BlockSpec((None, 1, t), lambda h, r, c: (h, 0, ij(r, c)[1]))
    return pl.pallas_call(
        body, name=name, grid=(H,) + tri,
        in_specs=[qblk, kv(1), kv(2), fblk, qblk, pl.BlockSpec((t, HEAD_DIM), lambda h, r, c: (ij(r, c)[0], do_off + h)),
                  pl.BlockSpec((None, t, 1), lambda h, r, c: (h, ij(r, c)[0], 0))],
        out_specs=[pl.BlockSpec((S, HEAD_DIM), lambda h, r, c: (0, h)), oblk, oblk, fblk],
        out_shape=[jax.ShapeDtypeStruct((S, W), F32)] * 3 + [jax.ShapeDtypeStruct((H, 1, S), F32)],
        scratch_shapes=[pltpu.VMEM((t, HEAD_DIM), F32), pltpu.VMEM((t, HEAD_DIM), F32), pltpu.VMEM((1, t), F32)],
        compiler_params=_cparams("parallel", "arbitrary", "arbitrary"),
    )(u, u, u, F3, o, do, lse)


def _rel_index_matrix():
    a = lax.broadcasted_iota(jnp.int32, (REL_TABLE, CA_SKEW), 0)
    j = lax.broadcasted_iota(jnp.int32, (REL_TABLE, CA_SKEW), 1)
    rel = jnp.where(j < CA_WIN, CA_LEFT_CHUNKS * CHUNK - j, REL_CLIP)
    idx = jnp.clip(jnp.minimum(rel, REL_CLIP) + (CHUNK - 1), 0, REL_TABLE - 1)
    return (a == idx).astype(F32)


def _skew(x, sign):
    r = lax.broadcasted_iota(jnp.int32, x.shape, 0)
    for b in range(int(math.log2(CA_TILE))):
        sh = (1 << b) if sign > 0 else CA_SKEW - (1 << b)
        x = jnp.where((r >> b) & 1 == 1, pltpu.roll(x, sh, 1), x)
    return x


def _band_valid():
    shift = int(math.log2(CHUNK))
    r = lax.broadcasted_iota(jnp.int32, (CA_TILE, CA_WIN), 0) >> shift
    m = lax.broadcasted_iota(jnp.int32, (CA_TILE, CA_WIN), 1) >> shift
    return jnp.logical_and(m >= r, m <= r + CA_LEFT_CHUNKS)


def _relbias_fwd(table, name):
    H = table.shape[0]

    def body(t_ref, b_ref):
        rowv = _dot(t_ref[...], _rel_index_matrix(), NN, HI)
        valid = _band_valid()
        for h in range(H):
            x = _skew(jnp.broadcast_to(rowv[h:h + 1, :], (CA_TILE, CA_SKEW)), +1)
            b_ref[h] = jnp.where(valid, x[:, :CA_WIN], NEG)

    return pl.pallas_call(
        body, name=name,
        out_shape=jax.ShapeDtypeStruct((H, CA_TILE, CA_WIN), F32),
        compiler_params=pltpu.CompilerParams(vmem_limit_bytes=VMEM_LIMIT_V7X),
    )(table)


def _relbias_bwd(dB, name):
    H = dB.shape[0]
    HP = -(-H // 8) * 8

    def body(d_ref, dt_ref, rows):
        rows[...] = jnp.zeros_like(rows)
        for h in range(H):
            x = jnp.concatenate([d_ref[h], jnp.zeros((CA_TILE, CA_SKEW - CA_WIN), F32)], axis=1)
            rows[pl.ds(h, 1), :] = jnp.sum(_skew(x, -1), axis=0, keepdims=True)
        dt_ref[...] = _dot(rows[...], _rel_index_matrix(), NT, HI)[:H]

    return pl.pallas_call(
        body, name=name,
        out_shape=jax.ShapeDtypeStruct((H, REL_TABLE), F32),
        scratch_shapes=[pltpu.VMEM((HP, CA_SKEW), F32)],
        compiler_params=pltpu.CompilerParams(vmem_limit_bytes=VMEM_LIMIT_V7X),
    )(dB)


CA_PIECES = CA_WIN // CA_TILE


def _ca_fwd(u, bias, H, name):
    S = u.shape[0]
    W = H * HEAD_DIM
    T = CA_TILE
    n = S // T
    scale = HEAD_DIM ** -0.5

    def body(q_ref, k0, k1, k2, v0, v1, v2, b_ref, o_ref, lse_ref):
        i = pl.program_id(1)
        q = q_ref[...].astype(BF16)
        ss = []
        for pce, k_ref in enumerate((k0, k1, k2)):
            s = _dot(q, k_ref[...].astype(BF16), NT) * scale + b_ref[:, pce * T:(pce + 1) * T]
            ss.append(jnp.where(i + pce >= CA_PIECES - 1, s, NEG))
        m = jnp.maximum(jnp.maximum(jnp.max(ss[0], -1, keepdims=True), jnp.max(ss[1], -1, keepdims=True)), jnp.max(ss[2], -1, keepdims=True))
        ps = [jnp.exp(s - m) for s in ss]
        l = sum(jnp.sum(p, -1, keepdims=True) for p in ps)
        inv = 1.0 / l
        o_ref[...] = sum(_dot((p * inv).astype(BF16), v_ref[...].astype(BF16), NN) for p, v_ref in zip(ps, (v0, v1, v2)))
        lse_ref[...] = m + jnp.log(l)

    qblk = pl.BlockSpec((T, HEAD_DIM), lambda h, i: (i, h))
    kv = lambda off, back: pl.BlockSpec((T, HEAD_DIM), lambda h, i: (jnp.maximum(i - back, 0), off * H + h))
    return pl.pallas_call(
        body, name=name, grid=(H, n),
        in_specs=[qblk, kv(1, 2), kv(1, 1), kv(1, 0), kv(2, 2), kv(2, 1), kv(2, 0),
                  pl.BlockSpec((None, T, CA_WIN), lambda h, i: (h, 0, 0))],
        out_specs=[qblk, pl.BlockSpec((None, T, 1), lambda h, i: (h, i, 0))],
        out_shape=[jax.ShapeDtypeStruct((S, W), F32), jax.ShapeDtypeStruct((H, S, 1), F32)],
        compiler_params=_cparams("parallel", "arbitrary"),
    )(u, u, u, u, u, u, u, bias)


def _ca_bwd(u, bias, o, do, do_off, lse, H, name):
    S = u.shape[0]
    W = H * HEAD_DIM
    T = CA_TILE
    n = S // T
    scale = HEAD_DIM ** -0.5

    def body(q_ref, k0, k1, k2, v0, v1, v2, b_ref, o_ref, do_ref, lse_ref, dq_ref, dk_ref, dv_ref, db_ref):
        i = pl.program_id(1)

        @pl.when(i == 0)
        def _():
            db_ref[...] = jnp.zeros_like(db_ref)
            dk_ref[...] = jnp.zeros_like(dk_ref)
            dv_ref[...] = jnp.zeros_like(dv_ref)

        q = q_ref[...].astype(BF16)
        dob = do_ref[...].astype(BF16)
        delta = jnp.sum(dob.astype(F32) * o_ref[...], axis=-1, keepdims=True)
        dq = jnp.zeros((T, HEAD_DIM), F32)
        for pce, (k_ref, v_ref) in enumerate(((k0, v0), (k1, v1), (k2, v2))):
            k = k_ref[...].astype(BF16)
            s = _dot(q, k, NT) * scale + b_ref[:, pce * T:(pce + 1) * T]
            p = jnp.where(i + pce >= CA_PIECES - 1, jnp.exp(s - lse_ref[...]), 0.0)
            pb = p.astype(BF16)
            ds = _softmax_bwd(p, pb, _dot(dob, v_ref[...].astype(BF16), NT), delta)
            dsb = ds.astype(BF16)
            db_ref[:, pce * T:(pce + 1) * T] += ds
            dq = dq + _dot(dsb, k, NN)
            rows = pl.ds(pl.multiple_of(jnp.maximum(i - (CA_PIECES - 1) + pce, 0) * T, T), T)
            dk_ref[rows, :] += _dot(dsb, q, TN) * scale
            dv_ref[rows, :] += _dot(pb, dob, TN)
        dq_ref[...] = dq * scale

    qblk = pl.BlockSpec((T, HEAD_DIM), lambda h, i: (i, h))
    kv = lambda off, back: pl.BlockSpec((T, HEAD_DIM), lambda h, i: (jnp.maximum(i - back, 0), off * H + h))
    bblk = pl.BlockSpec((None, T, CA_WIN), lambda h, i: (h, 0, 0))
    head = pl.BlockSpec((S, HEAD_DIM), lambda h, i: (0, h))
    return pl.pallas_call(
        body, name=name, grid=(H, n),
        in_specs=[qblk, kv(1, 2), kv(1, 1), kv(1, 0), kv(2, 2), kv(2, 1), kv(2, 0), bblk,
                  qblk, pl.BlockSpec((T, HEAD_DIM), lambda h, i: (i, do_off + h)), pl.BlockSpec((None, T, 1), lambda h, i: (h, i, 0))],
        out_specs=[qblk, head, head, bblk],
        out_shape=[jax.ShapeDtypeStruct((S, W), F32)] * 3 + [jax.ShapeDtypeStruct((H, T, CA_WIN), F32)],
        compiler_params=_cparams("parallel", "arbitrary"),
    )(u, u, u, u, u, u, u, bias, o, do, lse)


def _sum_parts(parts, name):
    _, R, C = parts.shape

    def body(p_ref, o_ref):
        acc = p_ref[0].astype(F32)
        for d in range(1, N_DEV):
            acc = acc + p_ref[d].astype(F32)
        o_ref[...] = acc

    return pl.pallas_call(
        body, name=name, out_shape=jax.ShapeDtypeStruct((R, C), F32),
        compiler_params=pltpu.CompilerParams(vmem_limit_bytes=VMEM_LIMIT_V7X),
    )(parts)


def _adamw(parts, w, m, v, name, tr=128):
    L, R, C = w.shape
    P = parts[0].shape[0]
    tr = _tile(R, tr)
    c1 = 1.0 / (1.0 - ADAM_B1 ** ADAM_STEP)
    c2 = 1.0 / (1.0 - ADAM_B2 ** ADAM_STEP)

    def body(*refs):
        p_refs = refs[:L]
        w_ref, m_ref, v_ref, g_ref, d_ref, nm_ref, nv_ref = refs[L:]

        def total(p_ref):
            g = p_ref[0].astype(F32)
            for d in range(1, P):
                g = g + p_ref[d].astype(F32)
            return g

        g = _select(pl.program_id(0), [functools.partial(total, r) for r in p_refs])
        nm = ADAM_B1 * m_ref[...] + (1.0 - ADAM_B1) * g
        nv = ADAM_B2 * v_ref[...] + (1.0 - ADAM_B2) * (g * g)
        g_ref[...] = g
        nm_ref[...] = nm
        nv_ref[...] = nv
        d_ref[...] = -ADAM_LR * ((nm * c1) / (jnp.sqrt(nv * c2) + ADAM_EPS) + ADAM_WD * w_ref[...])

    blk = pl.BlockSpec((None, tr, C), lambda l, i: (l, i, 0))
    p_specs = [pl.BlockSpec((P, tr, C), functools.partial(lambda p, l, i: (0, jnp.where(l == p, i, 0), 0), p)) for p in range(L)]
    return pl.pallas_call(
        body, name=name, grid=(L, R // tr),
        in_specs=p_specs + [blk, blk, blk],
        out_specs=[blk] * 4,
        out_shape=[jax.ShapeDtypeStruct((L, R, C), F32)] * 4,
        compiler_params=_cparams("arbitrary", "arbitrary"),
    )(*parts, w, m, v)


def _peer(d):
    x, y, c = lax.axis_index("x"), lax.axis_index("y"), lax.axis_index("c")
    px = (1 - x) if d & 4 else x
    py = (1 - y) if d & 2 else y
    pc = (1 - c) if d & 1 else c
    return (px, py, pc), 4 * px + 2 * py + pc


N_PEER = N_DEV - 1


def _exchange_copies(ins, outs, sems, scatter):
    send_sems, recv_sems, local_sems = sems
    me = 4 * lax.axis_index("x") + 2 * lax.axis_index("y") + lax.axis_index("c")
    starts, waits = [], []
    for t in range(len(ins)):
        loc = pltpu.make_async_copy(ins[t].at[me] if scatter else ins[t], outs[t].at[me], local_sems.at[t])
        starts.append(loc.start)
        waits.append(loc.wait)
        for d in range(1, N_DEV):
            peer, pidx = _peer(d)
            src = ins[t].at[pidx] if scatter else ins[t]
            k = t * N_PEER + d - 1
            common = dict(src_ref=src, send_sem=send_sems.at[k], recv_sem=recv_sems.at[k], device_id=peer, device_id_type=MESH_ID)
            starts.append(pltpu.make_async_remote_copy(dst_ref=outs[t].at[me], **common).start)
            waits.append(pltpu.make_async_remote_copy(dst_ref=outs[t].at[pidx], **common).wait)
    return starts, waits


def _exchange_scratch(n):
    return [pltpu.SemaphoreType.DMA((n * N_PEER,)), pltpu.SemaphoreType.DMA((n * N_PEER,)), pltpu.SemaphoreType.DMA((n,))]


def _exchange_shapes(arrays, scatter):
    return [jax.ShapeDtypeStruct(a.shape if scatter else (N_DEV,) + a.shape, a.dtype) for a in arrays]


def _exchange(arrays, scatter, name):
    n = len(arrays)

    def body(*refs):
        starts, waits = _exchange_copies(refs[:n], refs[n:2 * n], refs[2 * n:], scatter)
        for f in starts:
            f()
        for f in waits:
            f()

    hbm = pl.BlockSpec(memory_space=pltpu.HBM)
    return pl.pallas_call(
        body, name=name,
        in_specs=[hbm] * n, out_specs=[hbm] * n,
        out_shape=_exchange_shapes(arrays, scatter),
        scratch_shapes=_exchange_scratch(n),
        compiler_params=pltpu.CompilerParams(has_side_effects=True),
    )(*arrays)


def _call(body, *, name, grid, in_specs, out_specs, out_shape, scratch_shapes, sem, args, carry=None):
    if carry is None:
        outs = pl.pallas_call(body, name=name, grid=grid, in_specs=in_specs, out_specs=out_specs, out_shape=out_shape,
                              scratch_shapes=scratch_shapes, compiler_params=_cparams(*sem))(*args)
        return list(outs), []
    arrays, scatter = carry
    nc, n_in, n_out, n_scr = len(arrays), len(in_specs), len(out_specs), len(scratch_shapes)

    def wrapped(*refs):
        ins, cins = refs[:n_in], refs[n_in:n_in + nc]
        outs, couts = refs[n_in + nc:n_in + nc + n_out], refs[n_in + nc + n_out:n_in + 2 * nc + n_out]
        scr = refs[n_in + 2 * nc + n_out:n_in + 2 * nc + n_out + n_scr]
        ids = [pl.program_id(k) for k in range(len(grid))]
        first = functools.reduce(jnp.logical_and, [i == 0 for i in ids])
        last = functools.reduce(jnp.logical_and, [i == g - 1 for i, g in zip(ids, grid)])
        starts, waits = _exchange_copies(cins, couts, refs[-3:], scatter)

        @pl.when(first)
        def _():
            for f in starts:
                f()

        body(*ins, *outs, *scr)

        @pl.when(last)
        def _():
            for f in waits:
                f()

    hbm = pl.BlockSpec(memory_space=pltpu.HBM)
    res = pl.pallas_call(
        wrapped, name=name, grid=grid,
        in_specs=list(in_specs) + [hbm] * nc, out_specs=list(out_specs) + [hbm] * nc,
        out_shape=list(out_shape) + _exchange_shapes(arrays, scatter),
        scratch_shapes=list(scratch_shapes) + _exchange_scratch(nc),
        compiler_params=_cparams(*(("arbitrary",) * len(grid))),
    )(*args, *arrays)
    return list(res[:n_out]), list(res[n_out:])


def _relu2_fwd(acc):
    r = jnp.maximum(acc, 0.0)
    return acc, r * r


def _relu2_bwd(acc, z):
    return (acc * 2.0 * jnp.maximum(z.astype(F32), 0.0),)


def _add_scaled(scale):
    def epi(acc, t):
        return (acc + scale * t,)
    return epi


def _local_step(x, target, P, plan):
    S, D = x.shape
    H = (D // 2) // HEAD_DIM
    W = H * HEAD_DIM
    alpha = (2 * 2) ** 0.25
    g = {}

    def mm(a, b, *, name, **kw):
        res, carried = _mm(a, b, name=name, carry=plan.carry(name, g), **kw)
        plan.arrived(name, carried, P, g)
        return res

    u0 = mm(x, P["ev_w_in"], mode="nn", name="ev_in", b_blocked=True)
    a_out, cv = _conv_fwd(u0, P["ev_conv_w"], P["ev_conv_b"], P["ev_conv_ln_g"], P["ev_conv_ln_b"], "conv_fwd")
    b_out, h_raw, h_states = _hgrn_fwd(u0, P["hgrn_lb_logits"], P["ev_gnorm_g"], "hgrn_fwd")
    cat0 = (a_out, b_out)
    mix0 = mm(cat0, P["ev_w_out"], mode="nn", name="ev_out")
    x1, xh1, rs1 = _ln_fwd(x, mix0, P["ln_mix_g"][0:1], P["ln_mix_b"][0:1], alpha, "ln_mix0")
    z0, hh0 = mm(x1, P["mlp_w1_0"], mode="nn", name="mlp_up0", out_dtypes=(BF16, BF16), epi=_relu2_fwd, b_blocked=True)
    m0 = mm(hh0, P["mlp_w2_0"], mode="nn", name="mlp_down0")
    x2, xh2, rs2 = _ln_fwd(x1, m0, P["ln_mlp_g"][0:1], P["ln_mlp_b"][0:1], alpha, "ln_mlp0")

    uc = mm(x2, P["od_w_c"], mode="nn", name="od_in_c")
    ud = mm(x2, P["od_w_d"], mode="nn", name="od_in_d")
    f_logit, F = _fgate_fwd(x2, P["od_w_f_t"], P["fox_b_f"].reshape(H, 1), "fgate_fwd")
    F3 = F.reshape(H, 1, S)
    c_out, c_lse, carried = _fox_fwd(uc, F3, H, "fox_fwd", carry=plan.carry("fox_fwd", g))
    plan.arrived("fox_fwd", carried, P, g)
    bias = _relbias_fwd(P["rel_bias"], "relbias_fwd")
    d_out, d_lse = _ca_fwd(ud, bias, H, "ca_fwd")
    cat1 = (c_out, d_out)
    mix1 = mm(cat1, P["od_w_out"], mode="nn", name="od_out")
    x3, xh3, rs3 = _ln_fwd(x2, mix1, P["ln_mix_g"][1:2], P["ln_mix_b"][1:2], alpha, "ln_mix1")
    z1, hh1 = mm(x3, P["mlp_w1_1"], mode="nn", name="mlp_up1", out_dtypes=(BF16, BF16), epi=_relu2_fwd, b_blocked=True)
    m1 = mm(hh1, P["mlp_w2_1"], mode="nn", name="mlp_down1")
    x4, xh4, rs4 = _ln_fwd(x3, m1, P["ln_mlp_g"][1:2], P["ln_mlp_b"][1:2], alpha, "ln_mlp1")

    dy, loss = _loss_head(x4, target, "loss_head")

    dzm1, dg_, db_ = _ln_bwd(dy, None, 0.0, xh4, rs4, P["ln_mlp_g"][1:2], "ln_mlp1_bwd")
    g["ln_mlp_g1"], g["ln_mlp_b1"] = dg_, db_
    g["mlp_w2_1"] = mm(hh1, dzm1, mode="tn", name="mlp_down1_dw", out_dtypes=(BF16,))
    dz1 = mm(dzm1, P["mlp_w2_1"], mode="nt", name="mlp_down1_dx", out_dtypes=(BF16,), extras=(z1,), epi=_relu2_bwd)
    g["mlp_w1_1"] = mm(x3, dz1, mode="tn", name="mlp_up1_dw", out_dtypes=(BF16,), out_blocked=True)
    dx3 = mm(dz1, P["mlp_w1_1"], mode="nt", name="mlp_up1_dx", extras=(dzm1,), epi=_add_scaled(alpha), b_blocked=True)
    dzx1, dg_, db_ = _ln_bwd(dx3, None, 0.0, xh3, rs3, P["ln_mix_g"][1:2], "ln_mix1_bwd")
    g["ln_mix_g1"], g["ln_mix_b1"] = dg_, db_
    g["od_w_out"] = mm(cat1, dzx1, mode="tn", name="od_out_dw", out_dtypes=(BF16,), tk=1024)
    dcat1 = mm(dzx1, P["od_w_out"], mode="nt", name="od_out_dx")
    dq_d, dk_d, dv_d, dbias = _ca_bwd(ud, bias, d_out, dcat1, H, d_lse, H, "ca_bwd")
    g["rel_bias"] = _relbias_bwd(dbias, "relbias_bwd")
    dud = jnp.concatenate([dq_d, dk_d, dv_d], axis=1)
    dq_c, dk_c, dv_c, dF3 = _fox_bwd(uc, F3, c_out, dcat1, 0, c_lse, H, "fox_bwd")
    duc = jnp.concatenate([dq_c, dk_c, dv_c], axis=1)
    dfl, dbf = _fgate_bwd(dF3.reshape(H, S), f_logit, "fgate_bwd")
    g["fox_b_f"] = dbf.reshape(1, H)
    g["od_w_c"] = mm(x2, duc, mode="tn", name="od_in_c_dw", out_dtypes=(BF16,))
    g["od_w_d"] = mm(x2, dud, mode="tn", name="od_in_d_dw", out_dtypes=(BF16,))
    g["od_w_f_t"] = mm(dfl, x2, mode="nn", name="od_in_f_dw", out_dtypes=(BF16,), exact_products=True)
    dx2 = mm(duc, P["od_w_c"], mode="nt", name="od_in_c_dx", extras=(dzx1,), epi=_add_scaled(alpha))
    dx2 = mm(dud, P["od_w_d"], mode="nt", name="od_in_d_dx", extras=(dx2,), epi=_add_scaled(1.0))
    dx2 = mm(dfl, P["od_w_f_t"], mode="tn", name="od_in_f_dx", extras=(dx2,), epi=_add_scaled(1.0), exact_products=True)

    dzm0, dg_, db_ = _ln_bwd(dx2, None, 0.0, xh2, rs2, P["ln_mlp_g"][0:1], "ln_mlp0_bwd")
    g["ln_mlp_g0"], g["ln_mlp_b0"] = dg_, db_
    g["mlp_w2_0"] = mm(hh0, dzm0, mode="tn", name="mlp_down0_dw", out_dtypes=(BF16,))
    dz0 = mm(dzm0, P["mlp_w2_0"], mode="nt", name="mlp_down0_dx", out_dtypes=(BF16,), extras=(z0,), epi=_relu2_bwd)
    g["mlp_w1_0"] = mm(x1, dz0, mode="tn", name="mlp_up0_dw", out_dtypes=(BF16,), out_blocked=True)
    dx1 = mm(dz0, P["mlp_w1_0"], mode="nt", name="mlp_up0_dx", extras=(dzm0,), epi=_add_scaled(alpha), b_blocked=True)
    dzx0, dg_, db_ = _ln_bwd(dx1, None, 0.0, xh1, rs1, P["ln_mix_g"][0:1], "ln_mix0_bwd")
    g["ln_mix_g0"], g["ln_mix_b0"] = dg_, db_
    g["ev_w_out"] = mm(cat0, dzx0, mode="tn", name="ev_out_dw", out_dtypes=(BF16,), tk=1024)
    dcat0 = mm(dzx0, P["ev_w_out"], mode="nt", name="ev_out_dx")
    dcv, g["ev_conv_ln_g"], g["ev_conv_ln_b"], g["ev_conv_b"] = _conv_bwd_norm(dcat0, 0, cv, P["ev_conv_ln_g"], P["ev_conv_ln_b"], "conv_bwd_norm")
    du_a, du_g, g["ev_conv_w"] = _conv_bwd_taps(dcv, u0, P["ev_conv_w"], "conv_bwd_taps")
    dhq, dhf, dhi, dhg, g["ev_gnorm_g"], g["hgrn_lb_logits"] = _hgrn_bwd(dcat0, 1, u0, h_raw, h_states, P["hgrn_lb_logits"], P["ev_gnorm_g"], "hgrn_bwd")
    du0 = jnp.concatenate([du_a, du_g, dhq, dhf, dhi, dhg], axis=1)
    g["ev_w_in"] = mm(x, du0, mode="tn", name="ev_in_dw", out_dtypes=(BF16,), out_blocked=True)
    dx0 = mm(du0, P["ev_w_in"], mode="nt", name="ev_in_dx", extras=(dzx0,), epi=_add_scaled(alpha), b_blocked=True)
    return loss, dx0, g


_NAMES = ['ev_w_in', 'ev_conv_w', 'ev_conv_b', 'ev_conv_ln_g', 'ev_conv_ln_b', 'hgrn_lb_logits', 'ev_gnorm_g', 'ev_w_out',
          'od_w_in', 'fox_b_f', 'rel_bias', 'od_w_out', 'ln_mix_g', 'ln_mix_b', 'mlp_w1', 'mlp_w2', 'ln_mlp_g', 'ln_mlp_b']
_SMALL = ['ev_conv_b', 'ev_conv_ln_g', 'ev_conv_ln_b', 'hgrn_lb_logits', 'ev_gnorm_g', 'fox_b_f', 'ln_mix_g', 'ln_mix_b',
          'ln_mlp_g', 'ln_mlp_b', 'ev_conv_w', 'rel_bias']
_PACK_COLS = 2048


def _cols_to_full(gathered):
    nd, K, n = gathered.shape
    return jnp.transpose(gathered, (1, 0, 2)).reshape(K, nd * n)


def _full_to_cols(full):
    K, N = full.shape
    return jnp.transpose(full.reshape(K, N_DEV, N // N_DEV), (1, 0, 2))


_ROW_SHARDED = ("ev_w_out", "od_w_out", "mlp_w2_0", "mlp_w2_1")
_BLOCKED = ("ev_w_in", "mlp_w1_0", "mlp_w1_1")


def _full_weights(name, gathered, heads):
    if name in _BLOCKED:
        return {name: gathered}
    if name in _ROW_SHARDED:
        return {name: gathered.reshape(-1, gathered.shape[-1])}
    full = _cols_to_full(gathered)
    if name != "od_w_in":
        return {name: full}
    w = heads * HEAD_DIM
    return {"od_w_c": full[:, :3 * w], "od_w_f_t": jnp.transpose(full[:, 3 * w:3 * w + heads]), "od_w_d": full[:, 3 * w + heads:]}


def _grad_blocks(name, g):
    if name in _BLOCKED:
        return g[name]
    if name in _ROW_SHARDED:
        return g[name].reshape(N_DEV, -1, g[name].shape[-1])
    if name == "od_w_in":
        return _full_to_cols(jnp.concatenate([g["od_w_c"], jnp.transpose(g["od_w_f_t"]), g["od_w_d"]], axis=1))
    return _full_to_cols(g[name])


class _Plan:
    GATHER = {"ev_in": ("mlp_w1_0",), "mlp_up0": ("mlp_w2_0",), "mlp_down0": ("od_w_in",), "od_in_c": ("od_w_out",),
              "fox_fwd": ("mlp_w1_1", "mlp_w2_1")}
    SCATTER = {"mlp_down1_dx": "mlp_w2_1", "mlp_up1_dx": "mlp_w1_1", "od_out_dx": "od_w_out", "od_in_c_dx": "od_w_in",
               "mlp_down0_dx": "mlp_w2_0", "mlp_up0_dx": "mlp_w1_0", "ev_out_dx": "ev_w_out", "ev_in_dx": "ev_w_in"}

    def __init__(self, shards, heads):
        self.shards, self.heads = shards, heads

    def carry(self, call, g):
        if call in self.GATHER:
            return [self.shards[n] for n in self.GATHER[call]], False
        if call in self.SCATTER:
            return [_grad_blocks(self.SCATTER[call], g)], True
        return None

    def arrived(self, call, carried, P, g):
        if call in self.GATHER:
            for n, gathered in zip(self.GATHER[call], carried):
                P.update(_full_weights(n, gathered, self.heads))
        elif call in self.SCATTER:
            g[self.SCATTER[call]] = carried[0]


def _pack(parts):
    flat = jnp.concatenate([p.reshape(-1).astype(F32) for p in parts])
    rows = -(-flat.shape[0] // (_PACK_COLS * 8)) * 8
    return jnp.pad(flat, (0, rows * _PACK_COLS - flat.shape[0])).reshape(rows, _PACK_COLS)


def _unpack(packed, shapes):
    flat = packed.reshape(-1)
    out, off = [], 0
    for s in shapes:
        n = math.prod(s)
        out.append(flat[off:off + n].reshape(s))
        off += n
    return out


def kernel(x, ev_w_in, ev_conv_w, ev_conv_b, ev_conv_ln_g, ev_conv_ln_b, hgrn_lb_logits, ev_gnorm_g, ev_w_out, od_w_in, fox_b_f, rel_bias, od_w_out, ln_mix_g, ln_mix_b, mlp_w1, mlp_w2, ln_mlp_g, ln_mlp_b, loss_target, m_ev_w_in, m_ev_conv_w, m_ev_conv_b, m_ev_conv_ln_g, m_ev_conv_ln_b, m_hgrn_lb_logits, m_ev_gnorm_g, m_ev_w_out, m_od_w_in, m_fox_b_f, m_rel_bias, m_od_w_out, m_ln_mix_g, m_ln_mix_b, m_mlp_w1, m_mlp_w2, m_ln_mlp_g, m_ln_mlp_b, v_ev_w_in, v_ev_conv_w, v_ev_conv_b, v_ev_conv_ln_g, v_ev_conv_ln_b, v_hgrn_lb_logits, v_ev_gnorm_g, v_ev_w_out, v_od_w_in, v_fox_b_f, v_rel_bias, v_od_w_out, v_ln_mix_g, v_ln_mix_b, v_mlp_w1, v_mlp_w2, v_ln_mlp_g, v_ln_mlp_b):
    args = locals()
    w = {n: args[n] for n in _NAMES}
    m = {n: args["m_" + n] for n in _NAMES}
    v = {n: args["v_" + n] for n in _NAMES}
    me = 4 * lax.axis_index("x") + 2 * lax.axis_index("y") + lax.axis_index("c")
    S, D = x.shape[1], x.shape[2]
    H = (D // 2) // HEAD_DIM
    W = H * HEAD_DIM
    n_layers = mlp_w1.shape[0]
    assert n_layers == 2 and ev_w_in.shape[0] == 1 and od_w_in.shape[0] == 1

    shards = {"ev_w_in": ev_w_in[0].astype(BF16), "ev_w_out": ev_w_out[0].astype(BF16),
              "od_w_in": od_w_in[0].astype(BF16), "od_w_out": od_w_out[0].astype(BF16)}
    for l in range(n_layers):
        shards["mlp_w1_%d" % l] = mlp_w1[l].astype(BF16)
        shards["mlp_w2_%d" % l] = mlp_w2[l].astype(BF16)
    first = ["ev_w_in", "ev_w_out", "ev_conv_w", "rel_bias"]
    G = _exchange([shards["ev_w_in"], shards["ev_w_out"], ev_conv_w[0], rel_bias[0]], False, "gather_first")
    P = {
        "ev_conv_b": ev_conv_b, "ev_conv_ln_g": ev_conv_ln_g, "ev_conv_ln_b": ev_conv_ln_b,
        "hgrn_lb_logits": hgrn_lb_logits, "ev_gnorm_g": ev_gnorm_g, "fox_b_f": fox_b_f,
        "ln_mix_g": ln_mix_g, "ln_mix_b": ln_mix_b, "ln_mlp_g": ln_mlp_g, "ln_mlp_b": ln_mlp_b,
    }
    for name, gathered in zip(first, G):
        P.update(_full_weights(name, gathered, H))

    loss, grad_x, g = _local_step(x[0], loss_target[0], P, _Plan(shards, H))
    recv = [g["ev_w_in"], g["ev_w_out"], g["od_w_in"], g["od_w_out"]]
    recv += [g["mlp_w1_%d" % l] for l in range(n_layers)] + [g["mlp_w2_%d" % l] for l in range(n_layers)]

    small = {
        "ev_conv_b": g["ev_conv_b"], "ev_conv_ln_g": g["ev_conv_ln_g"], "ev_conv_ln_b": g["ev_conv_ln_b"],
        "hgrn_lb_logits": g["hgrn_lb_logits"], "ev_gnorm_g": g["ev_gnorm_g"], "fox_b_f": g["fox_b_f"],
        "ln_mix_g": jnp.concatenate([g["ln_mix_g0"], g["ln_mix_g1"]]), "ln_mix_b": jnp.concatenate([g["ln_mix_b0"], g["ln_mix_b1"]]),
        "ln_mlp_g": jnp.concatenate([g["ln_mlp_g0"], g["ln_mlp_g1"]]), "ln_mlp_b": jnp.concatenate([g["ln_mlp_b0"], g["ln_mlp_b1"]]),
        "ev_conv_w": g["ev_conv_w"], "rel_bias": g["rel_bias"],
    }
    full_shapes = [small[n].shape for n in _SMALL]
    small_all = _exchange([_pack([small[n] for n in _SMALL])], False, "gather_small_grads")[0]
    small_sum = _unpack(_sum_parts(small_all, "sum_small_grads"), full_shapes)
    small_g = dict(zip(_SMALL, small_sum))
    cw = small_g["ev_conv_w"]
    small_g["ev_conv_w"] = lax.dynamic_slice_in_dim(cw, me * (cw.shape[1] // N_DEV), cw.shape[1] // N_DEV, axis=1)
    rb = small_g["rel_bias"]
    small_g["rel_bias"] = lax.dynamic_slice_in_dim(rb, me * (rb.shape[1] // N_DEV), rb.shape[1] // N_DEV, axis=1)

    out_g, out_d, out_m, out_v = {}, {}, {}, {}
    big = [("ev_w_in", recv[0:1]), ("ev_w_out", recv[1:2]), ("od_w_in", recv[2:3]), ("od_w_out", recv[3:4]),
           ("mlp_w1", recv[4:4 + n_layers]), ("mlp_w2", recv[4 + n_layers:4 + 2 * n_layers])]
    for name, parts in big:
        out_g[name], out_d[name], out_m[name], out_v[name] = _adamw(parts, w[name], m[name], v[name], "adamw_" + name)
    shapes = [w[n].shape for n in _SMALL]
    packed = _adamw([_pack([small_g[n] for n in _SMALL])[None]], _pack([w[n] for n in _SMALL])[None], _pack([m[n] for n in _SMALL])[None],
                    _pack([v[n] for n in _SMALL])[None], "adamw_small")
    for k, dst in enumerate((out_g, out_d, out_m, out_v)):
        for n, a in zip(_SMALL, _unpack(packed[k], shapes)):
            dst[n] = a

    loss = lax.psum(loss[0, 0], ("x", "y", "c"))
    return (loss, grad_x[None], *[out_g[n] for n in _NAMES], *[out_d[n] for n in _NAMES],
            *[out_m[n] for n in _NAMES], *[out_v[n] for n in _NAMES])
```

```python
import functools
import math

import jax
import jax.numpy as jnp
from jax import lax
from jax.experimental import pallas as pl
from jax.experimental.pallas import tpu as pltpu

F32 = jnp.float32
BF16 = jnp.bfloat16
HI = lax.Precision.HIGHEST
MESH_ID = pl.DeviceIdType.MESH

N_DEV = 8
LN_EPS = 1e-5
CHUNK = 64
HEAD_DIM = 128
CONV_WIDTH = 31
CONV_HALO = 32
CA_LEFT_CHUNKS = 8
CA_TILE = 256
CA_WIN = CA_TILE + CA_LEFT_CHUNKS * CHUNK
CA_SKEW = 1024
REL_CLIP = 256
REL_TABLE = (CHUNK - 1) + REL_CLIP + 1
NEG = -1e30
ADAM_LR = 0.001
ADAM_B1 = 0.9
ADAM_B2 = 0.999
ADAM_EPS = 1e-08
ADAM_WD = 0.01
ADAM_STEP = 10
VMEM_LIMIT_V7X = 56 * 1024 * 1024


def _cparams(*sem):
    return pltpu.CompilerParams(dimension_semantics=sem, vmem_limit_bytes=VMEM_LIMIT_V7X)


def _tile(n, t):
    if n <= t:
        return n
    for c in range(t - t % 128, 0, -128):
        if n % c == 0:
            return c
    return n


def _sigmoid(x):
    return 1.0 / (1.0 + jnp.exp(-x))


def _dot(a, b, dims, precision=None):
    return lax.dot_general(a, b, (dims, ((), ())), preferred_element_type=F32, precision=precision)


def _round_bf16(x):
    return x.astype(BF16).astype(F32)


NN = ((1,), (0,))
NT = ((1,), (1,))
TN = ((0,), (0,))


def _select(idx, loads):
    if len(loads) == 1:
        return loads[0]()
    mid = len(loads) // 2
    return lax.cond(idx < mid, lambda: _select(idx, loads[:mid]), lambda: _select(idx - mid, loads[mid:]))


def _mm(a, b, *, mode, name, out_dtypes=(F32,), extras=(), epi=None, exact_products=False, carry=None,
        b_blocked=False, out_blocked=False, tm=1024, tn=1024, tk=2048):
    dims = {"nn": NN, "nt": NT, "tn": TN}[mode]
    a_parts = a if isinstance(a, tuple) else (a,)
    n_a = len(a_parts)
    a_split_k = n_a > 1 and mode != "tn"
    a_split_m = n_a > 1 and mode == "tn"
    if mode == "tn":
        K, M = a_parts[0].shape[0], a_parts[0].shape[1] * n_a
    else:
        M, K = a_parts[0].shape[0], a_parts[0].shape[1] * n_a
    if b_blocked:
        nb, _, shard = b.shape
        N = b.shape[1] if mode == "nt" else nb * shard
    else:
        N = b.shape[0] if mode == "nt" else b.shape[1]
    tm, tn, tk = _tile(M, tm), _tile(N, tn), _tile(K, tk)
    if a_split_k:
        tk = K // n_a
    if a_split_m:
        tm = M // n_a
    per_step = 1
    if b_blocked and mode == "nt":
        per_step = max(s for s in range(1, nb + 1) if nb % s == 0 and s * shard <= max(tk, shard))
        tk = per_step * shard
    if b_blocked and mode == "nn":
        tn = shard
    if out_blocked:
        tn = N // N_DEV
    nk = K // tk
    n_ex, n_out = len(extras), len(out_dtypes)

    def body(*refs):
        a_refs, b_ref = refs[:n_a], refs[n_a]
        ex_refs = refs[n_a + 1:n_a + 1 + n_ex]
        o_refs = refs[n_a + 1 + n_ex:n_a + 1 + n_ex + n_out]
        cast = _round_bf16 if exact_products else (lambda t: t.astype(BF16))
        part = pl.program_id(0) if a_split_m else pl.program_id(2)
        av = _select(part, [functools.partial(lambda r: cast(r[...]), r) for r in a_refs])
        precision = HI if exact_products else None
        if per_step > 1:
            d = sum(_dot(av[:, q * shard:(q + 1) * shard], cast(b_ref[q]), dims, precision) for q in range(per_step))
        else:
            d = _dot(av, cast(b_ref[...]), dims, precision)

        def finish(acc):
            outs = (acc,) if epi is None else epi(acc, *[r[...] for r in ex_refs])
            for o_ref, o in zip(o_refs, outs):
                o_ref[...] = o.astype(o_ref.dtype)

        if nk == 1:
            finish(d)
        else:
            acc_ref = refs[-1]
            k = pl.program_id(2)

            @pl.when(k == 0)
            def _():
                acc_ref[...] = d

            @pl.when(k > 0)
            def _():
                acc_ref[...] += d

            @pl.when(k == nk - 1)
            def _():
                finish(acc_ref[...])

    if a_split_k:
        a_specs = [pl.BlockSpec((tm, tk), lambda i, j, k: (i, 0))] * n_a
    elif a_split_m:
        a_specs = [pl.BlockSpec((tk, tm), functools.partial(lambda p, i, j, k: (jnp.where(i == p, k, 0), 0), p)) for p in range(n_a)]
    elif mode == "tn":
        a_specs = [pl.BlockSpec((tk, tm), lambda i, j, k: (k, i))]
    else:
        a_specs = [pl.BlockSpec((tm, tk), lambda i, j, k: (i, k))]
    if b_blocked and mode == "nt" and per_step > 1:
        b_spec = pl.BlockSpec((per_step, tn, shard), lambda i, j, k: (k, j, 0))
    elif b_blocked:
        b_spec = pl.BlockSpec((None, tn, tk), lambda i, j, k: (k, j, 0)) if mode == "nt" else pl.BlockSpec((None, tk, tn), lambda i, j, k: (j, k, 0))
    else:
        b_spec = pl.BlockSpec((tn, tk), lambda i, j, k: (j, k)) if mode == "nt" else pl.BlockSpec((tk, tn), lambda i, j, k: (k, j))
    mn_spec = pl.BlockSpec((tm, tn), lambda i, j, k: (i, j))
    if out_blocked:
        out_specs = [pl.BlockSpec((None, tm, tn), lambda i, j, k: (j, i, 0))] * n_out
        out_shape = [jax.ShapeDtypeStruct((N_DEV, M, tn), dt) for dt in out_dtypes]
    else:
        out_specs = [mn_spec] * n_out
        out_shape = [jax.ShapeDtypeStruct((M, N), dt) for dt in out_dtypes]
    outs, carried = _call(
        body, name=name, grid=(M // tm, N // tn, nk),
        in_specs=a_specs + [b_spec] + [mn_spec] * n_ex,
        out_specs=out_specs, out_shape=out_shape,
        scratch_shapes=[pltpu.VMEM((tm, tn), F32)] if nk > 1 else [],
        sem=("parallel", "parallel", "arbitrary"), args=(*a_parts, b, *extras), carry=carry)
    return (outs[0] if n_out == 1 else outs), carried


def _ln_fwd(x, r, g, b, alpha, name, tr=256):
    S, D = x.shape
    tr = _tile(S, tr)

    def body(x_ref, r_ref, g_ref, b_ref, y_ref, xh_ref, rs_ref):
        z = alpha * x_ref[...] + r_ref[...]
        zc = z - jnp.mean(z, axis=-1, keepdims=True)
        rs = lax.rsqrt(jnp.mean(zc * zc, axis=-1, keepdims=True) + LN_EPS)
        xh = zc * rs
        xh_ref[...] = xh
        rs_ref[...] = rs
        y_ref[...] = xh * g_ref[...] + b_ref[...]

    row = pl.BlockSpec((tr, D), lambda i: (i, 0))
    par = pl.BlockSpec((1, D), lambda i: (0, 0))
    return pl.pallas_call(
        body, name=name, grid=(S // tr,),
        in_specs=[row, row, par, par],
        out_specs=[row, row, pl.BlockSpec((tr, 1), lambda i: (i, 0))],
        out_shape=[jax.ShapeDtypeStruct((S, D), F32), jax.ShapeDtypeStruct((S, D), F32), jax.ShapeDtypeStruct((S, 1), F32)],
        compiler_params=_cparams("parallel"),
    )(x, r, g, b)


def _ln_bwd(dy, dy2, scale2, xh, rs, g, name, tr=256):
    S, D = dy.shape
    tr = _tile(S, tr)
    two = dy2 is not None

    def body(*refs):
        if two:
            dy_ref, dy2_ref, xh_ref, rs_ref, g_ref, dz_ref, dg_ref, db_ref = refs
            dyt = dy_ref[...] + scale2 * dy2_ref[...]
        else:
            dy_ref, xh_ref, rs_ref, g_ref, dz_ref, dg_ref, db_ref = refs
            dyt = dy_ref[...]
        xh = xh_ref[...]
        dxh = dyt * g_ref[...]
        m1 = jnp.mean(dxh, axis=-1, keepdims=True)
        m2 = jnp.mean(dxh * xh, axis=-1, keepdims=True)
        dz_ref[...] = rs_ref[...] * (dxh - m1 - xh * m2)

        @pl.when(pl.program_id(0) == 0)
        def _():
            dg_ref[...] = jnp.zeros_like(dg_ref)
            db_ref[...] = jnp.zeros_like(db_ref)

        dg_ref[...] += jnp.sum(dyt * xh, axis=0, keepdims=True)
        db_ref[...] += jnp.sum(dyt, axis=0, keepdims=True)

    row = pl.BlockSpec((tr, D), lambda i: (i, 0))
    par = pl.BlockSpec((1, D), lambda i: (0, 0))
    ins = [dy] + ([dy2] if two else []) + [xh, rs, g]
    return pl.pallas_call(
        body, name=name, grid=(S // tr,),
        in_specs=[row] * (2 if two else 1) + [row, pl.BlockSpec((tr, 1), lambda i: (i, 0)), par],
        out_specs=[row, par, par],
        out_shape=[jax.ShapeDtypeStruct((S, D), F32), jax.ShapeDtypeStruct((1, D), F32), jax.ShapeDtypeStruct((1, D), F32)],
        compiler_params=_cparams("arbitrary"),
    )(*ins)


def _loss_head(y, target, name, tr=256):
    S, D = y.shape
    tr = _tile(S, tr)

    def body(y_ref, t_ref, dy_ref, loss_ref):
        e = y_ref[...] - t_ref[...]
        dy_ref[...] = e * (1.0 / D)

        @pl.when(pl.program_id(0) == 0)
        def _():
            loss_ref[...] = jnp.zeros_like(loss_ref)

        loss_ref[...] += jnp.sum(jnp.sum(e * e, axis=-1, keepdims=True), axis=0, keepdims=True) * (0.5 / D)

    row = pl.BlockSpec((tr, D), lambda i: (i, 0))
    return pl.pallas_call(
        body, name=name, grid=(S // tr,),
        in_specs=[row, row],
        out_specs=[row, pl.BlockSpec((1, 1), lambda i: (0, 0))],
        out_shape=[jax.ShapeDtypeStruct((S, D), F32), jax.ShapeDtypeStruct((1, 1), F32)],
        compiler_params=_cparams("arbitrary"),
    )(y, target)


SUBLANES = 8


def _tap_reads(ext, shifted, tt, offset_of_tap):
    for r in range(SUBLANES):
        taps = [k for k in range(CONV_WIDTH) if offset_of_tap(k) % SUBLANES == r]
        if r == 0:
            src = ext
        else:
            shifted[...] = ext[pl.ds(r, tt + CONV_HALO - SUBLANES), :]
            src = shifted
        for k in taps:
            yield src, k, offset_of_tap(k) - r


def _conv_fwd(u, w, cb, lg, lb, name, tt=512):
    S = u.shape[0]
    C = w.shape[1]
    tt = _tile(S, tt)
    hpt = tt // CONV_HALO

    def body(a_ref, g_ref, ap_ref, gp_ref, w_ref, cb_ref, lg_ref, lb_ref, out_ref, cv_ref, hext, shifted):
        i = pl.program_id(0)
        hext[pl.ds(CONV_HALO, tt), :] = a_ref[...] * _sigmoid(g_ref[...])
        hp = ap_ref[...] * _sigmoid(gp_ref[...])
        hext[pl.ds(0, CONV_HALO), :] = jnp.where(i > 0, hp, 0.0)
        acc = jnp.zeros((tt, C), F32)
        for src, k, row0 in _tap_reads(hext, shifted, tt, lambda k: CONV_HALO - (CONV_WIDTH - 1) + k):
            acc = acc + w_ref[pl.ds(k, 1), :] * src[pl.ds(row0, tt), :]
        cv = acc + cb_ref[...]
        cv_ref[...] = cv
        zc = cv - jnp.mean(cv, axis=-1, keepdims=True)
        n = zc * lax.rsqrt(jnp.mean(zc * zc, axis=-1, keepdims=True) + LN_EPS) * lg_ref[...] + lb_ref[...]
        out_ref[...] = (n * _sigmoid(n)).astype(BF16)

    cur = lambda cb_: pl.BlockSpec((tt, C), lambda i: (i, cb_))
    prev = lambda cb_: pl.BlockSpec((CONV_HALO, C), lambda i: (jnp.maximum(i * hpt - 1, 0), cb_))
    par = pl.BlockSpec((1, C), lambda i: (0, 0))
    row = pl.BlockSpec((tt, C), lambda i: (i, 0))
    return pl.pallas_call(
        body, name=name, grid=(S // tt,),
        in_specs=[cur(0), cur(1), prev(0), prev(1), pl.BlockSpec((CONV_WIDTH, C), lambda i: (0, 0)), par, par, par],
        out_specs=[row, row],
        out_shape=[jax.ShapeDtypeStruct((S, C), BF16), jax.ShapeDtypeStruct((S, C), F32)],
        scratch_shapes=[pltpu.VMEM((tt + CONV_HALO, C), F32), pltpu.VMEM((tt + CONV_HALO - SUBLANES, C), F32)],
        compiler_params=_cparams("parallel"),
    )(u, u, u, u, w, cb, lg, lb)


def _conv_bwd_norm(da, da_col, cv, lg, lb, name, tt=512):
    S, C = cv.shape
    tt = _tile(S, tt)

    def body(da_ref, cv_ref, lg_ref, lb_ref, dcv_ref, dlg_ref, dlb_ref, dcb_ref):
        cv = cv_ref[...]
        zc = cv - jnp.mean(cv, axis=-1, keepdims=True)
        rs = lax.rsqrt(jnp.mean(zc * zc, axis=-1, keepdims=True) + LN_EPS)
        xh = zc * rs
        n = xh * lg_ref[...] + lb_ref[...]
        sg = _sigmoid(n)
        dn = da_ref[...] * sg * (1.0 + n * (1.0 - sg))
        dxh = dn * lg_ref[...]
        m1 = jnp.mean(dxh, axis=-1, keepdims=True)
        m2 = jnp.mean(dxh * xh, axis=-1, keepdims=True)
        dcv = rs * (dxh - m1 - xh * m2)
        dcv_ref[...] = dcv

        @pl.when(pl.program_id(0) == 0)
        def _():
            dlg_ref[...] = jnp.zeros_like(dlg_ref)
            dlb_ref[...] = jnp.zeros_like(dlb_ref)
            dcb_ref[...] = jnp.zeros_like(dcb_ref)

        dlg_ref[...] += jnp.sum(dn * xh, axis=0, keepdims=True)
        dlb_ref[...] += jnp.sum(dn, axis=0, keepdims=True)
        dcb_ref[...] += jnp.sum(dcv, axis=0, keepdims=True)

    row = pl.BlockSpec((tt, C), lambda i: (i, 0))
    par = pl.BlockSpec((1, C), lambda i: (0, 0))
    return pl.pallas_call(
        body, name=name, grid=(S // tt,),
        in_specs=[pl.BlockSpec((tt, C), lambda i: (i, da_col)), row, par, par],
        out_specs=[row, par, par, par],
        out_shape=[jax.ShapeDtypeStruct((S, C), F32)] + [jax.ShapeDtypeStruct((1, C), F32)] * 3,
        compiler_params=_cparams("arbitrary"),
    )(da, cv, lg, lb)


def _conv_bwd_taps(dcv, u, w, name, tt=512):
    S, C = dcv.shape
    tt = _tile(S, tt)
    hpt = tt // CONV_HALO
    nt = S // tt
    WPAD = 32

    def body(dc_ref, dn_ref, a_ref, g_ref, ap_ref, gp_ref, w_ref, da_ref, dg_ref, dw_ref, hext, dext, shifted):
        i = pl.program_id(0)
        a = a_ref[...]
        sg = _sigmoid(g_ref[...])
        hext[pl.ds(CONV_HALO, tt), :] = a * sg
        hp = ap_ref[...] * _sigmoid(gp_ref[...])
        hext[pl.ds(0, CONV_HALO), :] = jnp.where(i > 0, hp, 0.0)
        dc = dc_ref[...]
        dext[pl.ds(0, tt), :] = dc
        dext[pl.ds(tt, CONV_HALO), :] = jnp.where(i < nt - 1, dn_ref[...], 0.0)

        @pl.when(i == 0)
        def _():
            dw_ref[...] = jnp.zeros_like(dw_ref)

        dh = jnp.zeros((tt, C), F32)
        for src, k, row0 in _tap_reads(dext, shifted, tt, lambda k: CONV_WIDTH - 1 - k):
            dh = dh + w_ref[pl.ds(k, 1), :] * src[pl.ds(row0, tt), :]
        for src, k, row0 in _tap_reads(hext, shifted, tt, lambda k: CONV_HALO - (CONV_WIDTH - 1) + k):
            dw_ref[pl.ds(k, 1), :] += jnp.sum(dc * src[pl.ds(row0, tt), :], axis=0, keepdims=True)
        da_ref[...] = dh * sg
        dg_ref[...] = dh * a * sg * (1.0 - sg)

    row = pl.BlockSpec((tt, C), lambda i: (i, 0))
    nxt = pl.BlockSpec((CONV_HALO, C), lambda i: (jnp.minimum((i + 1) * hpt, S // CONV_HALO - 1), 0))
    cur = lambda cb_: pl.BlockSpec((tt, C), lambda i: (i, cb_))
    prev = lambda cb_: pl.BlockSpec((CONV_HALO, C), lambda i: (jnp.maximum(i * hpt - 1, 0), cb_))
    da, dg, dw = pl.pallas_call(
        body, name=name, grid=(nt,),
        in_specs=[row, nxt, cur(0), cur(1), prev(0), prev(1), pl.BlockSpec((CONV_WIDTH, C), lambda i: (0, 0))],
        out_specs=[row, row, pl.BlockSpec((WPAD, C), lambda i: (0, 0))],
        out_shape=[jax.ShapeDtypeStruct((S, C), F32)] * 2 + [jax.ShapeDtypeStruct((WPAD, C), F32)],
        scratch_shapes=[pltpu.VMEM((tt + CONV_HALO, C), F32)] * 2 + [pltpu.VMEM((tt + CONV_HALO - SUBLANES, C), F32)],
        compiler_params=_cparams("arbitrary"),
    )(dcv, dcv, u, u, u, u, w)
    return da, dg, dw[:CONV_WIDTH]


def _lower_bound(logits):
    e = jnp.exp(logits - jnp.max(logits, axis=0, keepdims=True))
    p = e / jnp.sum(e, axis=0, keepdims=True)
    return p[0:1, :], p


def _tri(n, lower):
    r = lax.broadcasted_iota(jnp.int32, (n, n), 0)
    c = lax.broadcasted_iota(jnp.int32, (n, n), 1)
    return ((c <= r) if lower else (c >= r)).astype(F32)


def _hgrn_gates(xq, xf, lb):
    sq = _sigmoid(xq)
    q = xq * sq
    sf = _sigmoid(xf)
    f = lb + (1.0 - lb) * sf
    logf = jnp.log(f)
    L = _dot(_tri(CHUNK, True), logf, NN, HI)
    Lend = L[CHUNK - 1:CHUNK, :]
    eL = jnp.exp(L)
    enL = jnp.exp(-L)
    eLe = jnp.exp(Lend - L)
    kk = 1.0 - f
    return dict(sq=sq, q=q, sf=sf, f=f, L=L, Lend=Lend, eL=eL, enL=enL, eLe=eLe, kk=kk,
                qe=q * eL, ke=kk * enL, kd=kk * eLe)


def _hgrn_fwd(u, lbl, gn, name, tb=256):
    S = u.shape[0]
    W = gn.shape[1]
    H = W // HEAD_DIM
    tb = _tile(S, tb)
    cpb = tb // CHUNK
    nc = S // CHUNK
    R = lbl.shape[0]

    def body(q_ref, f_ref, i_ref, g_ref, lbl_ref, gn_ref, out_ref, raw_ref, st_ref, state):
        @pl.when(pl.program_id(0) == 0)
        def _():
            state[...] = jnp.zeros_like(state)

        lb, _ = _lower_bound(lbl_ref[...])
        tril = _tri(CHUNK, True) > 0.5

        def chunk(c, carry):
            rows = pl.ds(pl.multiple_of(c * CHUNK, CHUNK), CHUNK)
            G = _hgrn_gates(q_ref[rows, :], f_ref[rows, :], lb)
            v = i_ref[rows, :]
            xg = g_ref[rows, :]
            outs = []
            for h in range(H):
                ln = slice(h * HEAD_DIM, (h + 1) * HEAD_DIM)
                qe, ke, kd, vh = G["qe"][:, ln].astype(BF16), G["ke"][:, ln].astype(BF16), G["kd"][:, ln].astype(BF16), v[:, ln].astype(BF16)
                st = state[h]
                st_ref[c, h] = st
                A = jnp.where(tril, _dot(qe, ke, NT), 0.0)
                o = _dot(A.astype(BF16), vh, NN) + _dot(qe, st.astype(BF16), NT)
                state[h] = jnp.exp(G["Lend"][:, ln]) * st + _dot(vh, kd, TN)
                outs.append(o)
            o = jnp.concatenate(outs, axis=1)
            raw_ref[rows, :] = o
            ns = []
            for h in range(H):
                oh = outs[h]
                ns.append(oh * lax.rsqrt(jnp.mean(oh * oh, axis=-1, keepdims=True) + LN_EPS))
            n = jnp.concatenate(ns, axis=1)
            out_ref[rows, :] = (n * gn_ref[...] * (xg * _sigmoid(xg))).astype(BF16)
            return carry

        lax.fori_loop(0, cpb, chunk, 0)

    col = lambda cb_: pl.BlockSpec((tb, W), lambda i: (i, cb_))
    row = pl.BlockSpec((tb, W), lambda i: (i, 0))
    return pl.pallas_call(
        body, name=name, grid=(S // tb,),
        in_specs=[col(2), col(3), col(4), col(5), pl.BlockSpec((R, W), lambda i: (0, 0)), pl.BlockSpec((1, W), lambda i: (0, 0))],
        out_specs=[row, row, pl.BlockSpec((cpb, H, HEAD_DIM, HEAD_DIM), lambda i: (i, 0, 0, 0))],
        out_shape=[jax.ShapeDtypeStruct((S, W), BF16), jax.ShapeDtypeStruct((S, W), F32),
                   jax.ShapeDtypeStruct((nc, H, HEAD_DIM, HEAD_DIM), F32)],
        scratch_shapes=[pltpu.VMEM((H, HEAD_DIM, HEAD_DIM), F32)],
        compiler_params=_cparams("arbitrary"),
    )(u, u, u, u, lbl, gn)


def _hgrn_bwd(dout, dout_col, u, raw, states, lbl, gn, name, tb=256):
    S = u.shape[0]
    W = gn.shape[1]
    H = W // HEAD_DIM
    tb = _tile(S, tb)
    cpb = tb // CHUNK
    nb = S // tb
    R = lbl.shape[0]

    def body(do_ref, q_ref, f_ref, i_ref, g_ref, raw_ref, st_ref, lbl_ref, gn_ref,
             dq_ref, df_ref, di_ref, dg_ref, dgn_ref, dlbl_ref, dstate, dlb_acc):
        @pl.when(pl.program_id(0) == 0)
        def _():
            dstate[...] = jnp.zeros_like(dstate)
            dlb_acc[...] = jnp.zeros_like(dlb_acc)
            dgn_ref[...] = jnp.zeros_like(dgn_ref)

        lb, p = _lower_bound(lbl_ref[...])
        tril = _tri(CHUNK, True) > 0.5
        triu = _tri(CHUNK, False)
        gn_row = gn_ref[...]

        def chunk(cc, carry):
            c = cpb - 1 - cc
            rows = pl.ds(pl.multiple_of(c * CHUNK, CHUNK), CHUNK)
            xq, xf = q_ref[rows, :], f_ref[rows, :]
            G = _hgrn_gates(xq, xf, lb)
            v = i_ref[rows, :]
            xg = g_ref[rows, :]
            dy = do_ref[rows, :]
            o = raw_ref[rows, :]
            sgg = _sigmoid(xg)
            silu_g = xg * sgg
            do_parts, n_parts = [], []
            for h in range(H):
                ln = slice(h * HEAD_DIM, (h + 1) * HEAD_DIM)
                oh = o[:, ln]
                r = lax.rsqrt(jnp.mean(oh * oh, axis=-1, keepdims=True) + LN_EPS)
                nh = oh * r
                dn = dy[:, ln] * gn_row[:, ln] * silu_g[:, ln]
                do_parts.append(r * (dn - nh * jnp.mean(dn * nh, axis=-1, keepdims=True)))
                n_parts.append(nh)
            n = jnp.concatenate(n_parts, axis=1)
            dgn_ref[...] += jnp.sum(dy * n * silu_g, axis=0, keepdims=True)
            dg_ref[rows, :] = dy * n * gn_row * sgg * (1.0 + xg * (1.0 - sgg))
            dqe_p, dke_p, dkd_p, dv_p, dle_p = [], [], [], [], []
            for h in range(H):
                ln = slice(h * HEAD_DIM, (h + 1) * HEAD_DIM)
                qe, ke, kd, vh = G["qe"][:, ln].astype(BF16), G["ke"][:, ln].astype(BF16), G["kd"][:, ln].astype(BF16), v[:, ln].astype(BF16)
                doh = do_parts[h].astype(BF16)
                st = st_ref[c, h]
                dst = dstate[h]
                dec = jnp.exp(G["Lend"][:, ln])
                A = jnp.where(tril, _dot(qe, ke, NT), 0.0).astype(BF16)
                dA = jnp.where(tril, _dot(doh, vh, NT), 0.0).astype(BF16)
                dv_p.append(_dot(A, doh, TN) + _dot(kd, dst.astype(BF16), NT))
                dqe_p.append(_dot(dA, ke, NN) + _dot(doh, st.astype(BF16), NN))
                dke_p.append(_dot(dA, qe, TN))
                dkd_p.append(_dot(vh, dst.astype(BF16), NN))
                dle_p.append(jnp.sum(dst * dec * st, axis=0, keepdims=True))
                dstate[h] = dec * dst + _dot(doh, qe, TN)
            dqe = jnp.concatenate(dqe_p, axis=1)
            dke = jnp.concatenate(dke_p, axis=1)
            dkd = jnp.concatenate(dkd_p, axis=1)
            qe_r, ke_r, kd_r = _round_bf16(G["qe"]), _round_bf16(G["ke"]), _round_bf16(G["kd"])
            dLend = jnp.concatenate(dle_p, axis=1) + jnp.sum(dkd * kd_r, axis=0, keepdims=True)
            dL = dqe * qe_r - dke * ke_r - dkd * kd_r
            dlogf = _dot(triu, dL, NN, HI) + dLend
            dkk = dke * G["enL"] + dkd * G["eLe"]
            dfv = dlogf / G["f"] - dkk
            sf = G["sf"]
            df_ref[rows, :] = dfv * (1.0 - lb) * sf * (1.0 - sf)
            dlb_acc[...] += jnp.sum(dfv * (1.0 - sf), axis=0, keepdims=True)
            sq = G["sq"]
            dq_ref[rows, :] = dqe * G["eL"] * sq * (1.0 + xq * (1.0 - sq))
            di_ref[rows, :] = jnp.concatenate(dv_p, axis=1)
            return carry

        lax.fori_loop(0, cpb, chunk, 0)

        onehot0 = (lax.broadcasted_iota(jnp.int32, (R, W), 0) == 0).astype(F32)
        dlbl_ref[...] = p * (onehot0 - p[0:1, :]) * dlb_acc[...]

    rev = lambda i: nb - 1 - i
    col = lambda cb_: pl.BlockSpec((tb, W), lambda i: (rev(i), cb_))
    row = pl.BlockSpec((tb, W), lambda i: (rev(i), 0))
    par = pl.BlockSpec((1, W), lambda i: (0, 0))
    parR = pl.BlockSpec((R, W), lambda i: (0, 0))
    return pl.pallas_call(
        body, name=name, grid=(nb,),
        in_specs=[pl.BlockSpec((tb, W), lambda i: (rev(i), dout_col)), col(2), col(3), col(4), col(5), row,
                  pl.BlockSpec((cpb, H, HEAD_DIM, HEAD_DIM), lambda i: (rev(i), 0, 0, 0)), parR, par],
        out_specs=[row, row, row, row, par, parR],
        out_shape=[jax.ShapeDtypeStruct((S, W), F32)] * 4 + [jax.ShapeDtypeStruct((1, W), F32), jax.ShapeDtypeStruct((R, W), F32)],
        scratch_shapes=[pltpu.VMEM((H, HEAD_DIM, HEAD_DIM), F32), pltpu.VMEM((1, W), F32)],
        compiler_params=_cparams("arbitrary"),
    )(dout, u, u, u, u, raw, states, lbl, gn)


def _fgate_fwd(xb, wft, bf, name, ts=512):
    S, D = xb.shape
    H = wft.shape[0]
    ts = _tile(S, ts)

    def body(x_ref, w_ref, b_ref, lg_ref, F_ref, carry):
        @pl.when(pl.program_id(0) == 0)
        def _():
            carry[...] = jnp.zeros_like(carry)

        lg = _dot(_round_bf16(w_ref[...]), _round_bf16(x_ref[...]), NT, HI) + b_ref[...]
        lg_ref[...] = lg
        ls = jnp.minimum(lg, 0.0) - jnp.log(1.0 + jnp.exp(-jnp.abs(lg)))
        F = _dot(ls, _tri(ts, False), NN, HI) + carry[...]
        F_ref[...] = F
        carry[...] = F[:, ts - 1:ts]

    return pl.pallas_call(
        body, name=name, grid=(S // ts,),
        in_specs=[pl.BlockSpec((ts, D), lambda i: (i, 0)), pl.BlockSpec((H, D), lambda i: (0, 0)), pl.BlockSpec((H, 1), lambda i: (0, 0))],
        out_specs=[pl.BlockSpec((H, ts), lambda i: (0, i))] * 2,
        out_shape=[jax.ShapeDtypeStruct((H, S), F32)] * 2,
        scratch_shapes=[pltpu.VMEM((H, 1), F32)],
        compiler_params=_cparams("arbitrary"),
    )(xb, wft, bf)


def _fgate_bwd(dF, lg, name, ts=512):
    H, S = dF.shape
    ts = _tile(S, ts)
    nb = S // ts

    def body(dF_ref, lg_ref, dl_ref, db_ref, carry):
        @pl.when(pl.program_id(0) == 0)
        def _():
            carry[...] = jnp.zeros_like(carry)
            db_ref[...] = jnp.zeros_like(db_ref)

        dls = _dot(dF_ref[...], _tri(ts, True), NN, HI) + carry[...]
        carry[...] = dls[:, 0:1]
        dl = dls * _sigmoid(-lg_ref[...])
        dl_ref[...] = dl
        db_ref[...] += jnp.sum(dl, axis=1, keepdims=True)

    blk = pl.BlockSpec((H, ts), lambda i: (0, nb - 1 - i))
    return pl.pallas_call(
        body, name=name, grid=(nb,),
        in_specs=[blk, blk],
        out_specs=[blk, pl.BlockSpec((H, 1), lambda i: (0, 0))],
        out_shape=[jax.ShapeDtypeStruct((H, S), F32), jax.ShapeDtypeStruct((H, 1), F32)],
        scratch_shapes=[pltpu.VMEM((H, 1), F32)],
        compiler_params=_cparams("arbitrary"),
    )(dF, lg)


def _causal_keep(i, j, tq, tk):
    rows = lax.broadcasted_iota(jnp.int32, (tq, tk), 0)
    cols = lax.broadcasted_iota(jnp.int32, (tq, tk), 1)
    return jnp.logical_or(j < i, cols <= rows)


def _causal_grid(S, t):
    n = S // t
    assert n % 2 == 0, (S, t)

    def by_query(r, c):
        second = c > r
        return jnp.where(second, n - 1 - r, r), jnp.where(second, c - r - 1, c)

    def by_key(r, c):
        second = c >= n - r
        return jnp.where(second, c - 1, r + c), jnp.where(second, n - 1 - r, r)

    return n, (n // 2, n + 1), by_query, by_key


def _fox_scores(q_ref, k_ref, fk_ref, i, j, t, scale):
    s = _dot(q_ref[...].astype(BF16), k_ref[...].astype(BF16), NT) * scale - fk_ref[...]
    return lax.cond(i == j, lambda x: jnp.where(_causal_keep(i, j, t, t), x, NEG), lambda x: x, s)


def _softmax_bwd(p, pb, dp, delta):
    return pb.astype(F32) * dp - p * delta


FOX_TILE = 1024


def _fox_tile(S):
    return _tile(S, min(FOX_TILE, S // 2))


def _fox_fwd(u, F3, H, name, carry=None):
    S = u.shape[0]
    W = H * HEAD_DIM
    t = _fox_tile(S)
    n, tri, ij, _ = _causal_grid(S, t)
    scale = HEAD_DIM ** -0.5

    def body(q_ref, k_ref, v_ref, fk_ref, o_ref, ob_ref, lse_ref, m_s, l_s, acc_s):
        i, j = ij(pl.program_id(1), pl.program_id(2))

        @pl.when(j == 0)
        def _():
            m_s[...] = jnp.full_like(m_s, NEG)
            l_s[...] = jnp.zeros_like(l_s)
            acc_s[...] = jnp.zeros_like(acc_s)

        s = _fox_scores(q_ref, k_ref, fk_ref, i, j, t, scale)
        m_new = jnp.maximum(m_s[...], jnp.max(s, axis=-1, keepdims=True))
        a = jnp.exp(m_s[...] - m_new)
        p = jnp.exp(s - m_new)
        hi = p.astype(BF16)
        lo = (p - hi.astype(F32)).astype(BF16)
        v = v_ref[...].astype(BF16)
        l_s[...] = a * l_s[...] + jnp.sum(p, axis=-1, keepdims=True)
        acc_s[...] = a * acc_s[...] + (_dot(hi, v, NN) + _dot(lo, v, NN))
        m_s[...] = m_new

        @pl.when(j == i)
        def _():
            o = acc_s[...] / l_s[...]
            o_ref[...] = o
            ob_ref[...] = o.astype(BF16)
            lse_ref[...] = m_s[...] + jnp.log(l_s[...])

    qblk = pl.BlockSpec((t, HEAD_DIM), lambda h, r, c: (ij(r, c)[0], h))
    kv = lambda off: pl.BlockSpec((t, HEAD_DIM), lambda h, r, c: (ij(r, c)[1], off * H + h))
    outs, carried = _call(
        body, name=name, grid=(H,) + tri,
        in_specs=[qblk, kv(1), kv(2), pl.BlockSpec((None, 1, t), lambda h, r, c: (h, 0, ij(r, c)[1]))],
        out_specs=[qblk, qblk, pl.BlockSpec((None, t, 1), lambda h, r, c: (h, ij(r, c)[0], 0))],
        out_shape=[jax.ShapeDtypeStruct((S, W), F32), jax.ShapeDtypeStruct((S, W), BF16), jax.ShapeDtypeStruct((H, S, 1), F32)],
        scratch_shapes=[pltpu.VMEM((t, 1), F32), pltpu.VMEM((t, 1), F32), pltpu.VMEM((t, HEAD_DIM), F32)],
        sem=("parallel", "parallel", "arbitrary"), args=(u, u, u, F3), carry=carry)
    return outs[0], outs[1], outs[2], carried


def _fox_bwd(u, F3, o, do, do_off, lse, H, name):
    S = u.shape[0]
    W = H * HEAD_DIM
    t = _fox_tile(S)
    n, tri, _, ij = _causal_grid(S, t)
    scale = HEAD_DIM ** -0.5

    def body(q_ref, k_ref, v_ref, fk_ref, o_ref, do_ref, lse_ref, dq_ref, dk_ref, dv_ref, dF_ref, dk_s, dv_s, dF_s):
        r, c = pl.program_id(1), pl.program_id(2)
        i, j = ij(r, c)

        @pl.when(jnp.logical_and(r == 0, c == 0))
        def _():
            dq_ref[...] = jnp.zeros_like(dq_ref)

        @pl.when(i == j)
        def _():
            dk_s[...] = jnp.zeros_like(dk_s)
            dv_s[...] = jnp.zeros_like(dv_s)
            dF_s[...] = jnp.zeros_like(dF_s)

        q = q_ref[...].astype(BF16)
        k = k_ref[...].astype(BF16)
        dob = do_ref[...].astype(BF16)
        p = jnp.exp(_fox_scores(q_ref, k_ref, fk_ref, i, j, t, scale) - lse_ref[...])
        dp = _dot(dob, v_ref[...].astype(BF16), NT)
        delta = jnp.sum(dob.astype(F32) * o_ref[...], axis=-1, keepdims=True)
        ds = p * (dp - delta)
        dsb = ds.astype(BF16)
        dv_s[...] += _dot(p.astype(BF16), dob, TN)
        dk_s[...] += _dot(dsb, q, TN)
        dF_s[...] -= jnp.sum(ds, axis=0, keepdims=True)
        rows = pl.ds(pl.multiple_of(i * t, t), t)
        dq_ref[rows, :] += _dot(dsb, k, NN) * scale

        @pl.when(i == n - 1)
        def _():
            dk_ref[...] = dk_s[...] * scale
            dv_ref[...] = dv_s[...]
            dF_ref[...] = dF_s[...]

    qblk = pl.BlockSpec((t, HEAD_DIM), lambda h, r, c: (ij(r, c)[0], h))
    kv = lambda off: pl.BlockSpec((t, HEAD_DIM), lambda h, r, c: (ij(r, c)[1], off * H + h))
    oblk = pl.BlockSpec((t, HEAD_DIM), lambda h, r, c: (ij(r, c)[1], h))
    fblk = pl.BlockSpec((None, 1, t), lambda h, r, c: (h, 0, ij(r, c)[1]))
    return pl.pallas_call(
        body, name=name, grid=(H,) + tri,
        in_specs=[qblk, kv(1), kv(2), fblk, qblk, pl.BlockSpec((t, HEAD_DIM), lambda h, r, c: (ij(r, c)[0], do_off + h)),
                  pl.BlockSpec((None, t, 1), lambda h, r, c: (h, ij(r, c)[0], 0))],
        out_specs=[pl.BlockSpec((S, HEAD_DIM), lambda h, r, c: (0, h)), oblk, oblk, fblk],
        out_shape=[jax.ShapeDtypeStruct((S, W), F32)] * 3 + [jax.ShapeDtypeStruct((H, 1, S), F32)],
        scratch_shapes=[pltpu.VMEM((t, HEAD_DIM), F32), pltpu.VMEM((t, HEAD_DIM), F32), pltpu.VMEM((1, t), F32)],
        compiler_params=_cparams("parallel", "arbitrary", "arbitrary"),
    )(u, u, u, F3, o, do, lse)


def _rel_index_matrix():
    a = lax.broadcasted_iota(jnp.int32, (REL_TABLE, CA_SKEW), 0)
    j = lax.broadcasted_iota(jnp.int32, (REL_TABLE, CA_SKEW), 1)
    rel = jnp.where(j < CA_WIN, CA_LEFT_CHUNKS * CHUNK - j, REL_CLIP)
    idx = jnp.clip(jnp.minimum(rel, REL_CLIP) + (CHUNK - 1), 0, REL_TABLE - 1)
    return (a == idx).astype(F32)


def _skew(x, sign):
    r = lax.broadcasted_iota(jnp.int32, x.shape, 0)
    for b in range(int(math.log2(CA_TILE))):
        sh = (1 << b) if sign > 0 else CA_SKEW - (1 << b)
        x = jnp.where((r >> b) & 1 == 1, pltpu.roll(x, sh, 1), x)
    return x


def _band_valid():
    shift = int(math.log2(CHUNK))
    r = lax.broadcasted_iota(jnp.int32, (CA_TILE, CA_WIN), 0) >> shift
    m = lax.broadcasted_iota(jnp.int32, (CA_TILE, CA_WIN), 1) >> shift
    return jnp.logical_and(m >= r, m <= r + CA_LEFT_CHUNKS)


def _relbias_fwd(table, name):
    H = table.shape[0]

    def body(t_ref, b_ref):
        rowv = _dot(t_ref[...], _rel_index_matrix(), NN, HI)
        valid = _band_valid()
        for h in range(H):
            x = _skew(jnp.broadcast_to(rowv[h:h + 1, :], (CA_TILE, CA_SKEW)), +1)
            b_ref[h] = jnp.where(valid, x[:, :CA_WIN], NEG)

    return pl.pallas_call(
        body, name=name,
        out_shape=jax.ShapeDtypeStruct((H, CA_TILE, CA_WIN), F32),
        compiler_params=pltpu.CompilerParams(vmem_limit_bytes=VMEM_LIMIT_V7X),
    )(table)


def _relbias_bwd(dB, name):
    H = dB.shape[0]
    HP = -(-H // 8) * 8

    def body(d_ref, dt_ref, rows):
        rows[...] = jnp.zeros_like(rows)
        for h in range(H):
            x = jnp.concatenate([d_ref[h], jnp.zeros((CA_TILE, CA_SKEW - CA_WIN), F32)], axis=1)
            rows[pl.ds(h, 1), :] = jnp.sum(_skew(x, -1), axis=0, keepdims=True)
        dt_ref[...] = _dot(rows[...], _rel_index_matrix(), NT, HI)[:H]

    return pl.pallas_call(
        body, name=name,
        out_shape=jax.ShapeDtypeStruct((H, REL_TABLE), F32),
        scratch_shapes=[pltpu.VMEM((HP, CA_SKEW), F32)],
        compiler_params=pltpu.CompilerParams(vmem_limit_bytes=VMEM_LIMIT_V7X),
    )(dB)


CA_PIECES = CA_WIN // CA_TILE


def _ca_fwd(u, bias, H, name):
    S = u.shape[0]
    W = H * HEAD_DIM
    T = CA_TILE
    n = S // T
    scale = HEAD_DIM ** -0.5

    def body(q_ref, k0, k1, k2, v0, v1, v2, b_ref, o_ref, ob_ref, lse_ref):
        i = pl.program_id(1)
        q = q_ref[...].astype(BF16)
        ss = []
        for pce, k_ref in enumerate((k0, k1, k2)):
            s = _dot(q, k_ref[...].astype(BF16), NT) * scale + b_ref[:, pce * T:(pce + 1) * T]
            ss.append(jnp.where(i + pce >= CA_PIECES - 1, s, NEG))
        m = jnp.maximum(jnp.maximum(jnp.max(ss[0], -1, keepdims=True), jnp.max(ss[1], -1, keepdims=True)), jnp.max(ss[2], -1, keepdims=True))
        ps = [jnp.exp(s - m) for s in ss]
        l = sum(jnp.sum(p, -1, keepdims=True) for p in ps)
        inv = 1.0 / l
        o = sum(_dot((p * inv).astype(BF16), v_ref[...].astype(BF16), NN) for p, v_ref in zip(ps, (v0, v1, v2)))
        o_ref[...] = o
        ob_ref[...] = o.astype(BF16)
        lse_ref[...] = m + jnp.log(l)

    qblk = pl.BlockSpec((T, HEAD_DIM), lambda h, i: (i, h))
    kv = lambda off, back: pl.BlockSpec((T, HEAD_DIM), lambda h, i: (jnp.maximum(i - back, 0), off * H + h))
    return pl.pallas_call(
        body, name=name, grid=(H, n),
        in_specs=[qblk, kv(1, 2), kv(1, 1), kv(1, 0), kv(2, 2), kv(2, 1), kv(2, 0),
                  pl.BlockSpec((None, T, CA_WIN), lambda h, i: (h, 0, 0))],
        out_specs=[qblk, qblk, pl.BlockSpec((None, T, 1), lambda h, i: (h, i, 0))],
        out_shape=[jax.ShapeDtypeStruct((S, W), F32), jax.ShapeDtypeStruct((S, W), BF16), jax.ShapeDtypeStruct((H, S, 1), F32)],
        compiler_params=_cparams("parallel", "arbitrary"),
    )(u, u, u, u, u, u, u, bias)


def _ca_bwd(u, bias, o, do, do_off, lse, H, name):
    S = u.shape[0]
    W = H * HEAD_DIM
    T = CA_TILE
    n = S // T
    scale = HEAD_DIM ** -0.5

    def body(q_ref, k0, k1, k2, v0, v1, v2, b_ref, o_ref, do_ref, lse_ref, dq_ref, dk_ref, dv_ref, db_ref):
        i = pl.program_id(1)

        @pl.when(i == 0)
        def _():
            db_ref[...] = jnp.zeros_like(db_ref)
            dk_ref[...] = jnp.zeros_like(dk_ref)
            dv_ref[...] = jnp.zeros_like(dv_ref)

        q = q_ref[...].astype(BF16)
        dob = do_ref[...].astype(BF16)
        delta = jnp.sum(dob.astype(F32) * o_ref[...], axis=-1, keepdims=True)
        dq = jnp.zeros((T, HEAD_DIM), F32)
        for pce, (k_ref, v_ref) in enumerate(((k0, v0), (k1, v1), (k2, v2))):
            k = k_ref[...].astype(BF16)
            s = _dot(q, k, NT) * scale + b_ref[:, pce * T:(pce + 1) * T]
            p = jnp.where(i + pce >= CA_PIECES - 1, jnp.exp(s - lse_ref[...]), 0.0)
            pb = p.astype(BF16)
            ds = _softmax_bwd(p, pb, _dot(dob, v_ref[...].astype(BF16), NT), delta)
            dsb = ds.astype(BF16)
            db_ref[:, pce * T:(pce + 1) * T] += ds
            dq = dq + _dot(dsb, k, NN)
            rows = pl.ds(pl.multiple_of(jnp.maximum(i - (CA_PIECES - 1) + pce, 0) * T, T), T)
            dk_ref[rows, :] += _dot(dsb, q, TN) * scale
            dv_ref[rows, :] += _dot(pb, dob, TN)
        dq_ref[...] = dq * scale

    qblk = pl.BlockSpec((T, HEAD_DIM), lambda h, i: (i, h))
    kv = lambda off, back: pl.BlockSpec((T, HEAD_DIM), lambda h, i: (jnp.maximum(i - back, 0), off * H + h))
    bblk = pl.BlockSpec((None, T, CA_WIN), lambda h, i: (h, 0, 0))
    head = pl.BlockSpec((S, HEAD_DIM), lambda h, i: (0, h))
    return pl.pallas_call(
        body, name=name, grid=(H, n),
        in_specs=[qblk, kv(1, 2), kv(1, 1), kv(1, 0), kv(2, 2), kv(2, 1), kv(2, 0), bblk,
                  qblk, pl.BlockSpec((T, HEAD_DIM), lambda h, i: (i, do_off + h)), pl.BlockSpec((None, T, 1), lambda h, i: (h, i, 0))],
        out_specs=[qblk, head, head, bblk],
        out_shape=[jax.ShapeDtypeStruct((S, W), F32)] * 3 + [jax.ShapeDtypeStruct((H, T, CA_WIN), F32)],
        compiler_params=_cparams("parallel", "arbitrary"),
    )(u, u, u, u, u, u, u, bias, o, do, lse)


def _sum_parts(parts, name):
    _, R, C = parts.shape

    def body(p_ref, o_ref):
        acc = p_ref[0].astype(F32)
        for d in range(1, N_DEV):
            acc = acc + p_ref[d].astype(F32)
        o_ref[...] = acc

    return pl.pallas_call(
        body, name=name, out_shape=jax.ShapeDtypeStruct((R, C), F32),
        compiler_params=pltpu.CompilerParams(vmem_limit_bytes=VMEM_LIMIT_V7X),
    )(parts)


def _adamw(parts, w, m, v, name, tr=128):
    L, R, C = w.shape
    P = parts[0].shape[0]
    tr = _tile(R, tr)
    c1 = 1.0 / (1.0 - ADAM_B1 ** ADAM_STEP)
    c2 = 1.0 / (1.0 - ADAM_B2 ** ADAM_STEP)

    def body(*refs):
        p_refs = refs[:L]
        w_ref, m_ref, v_ref, g_ref, d_ref, nm_ref, nv_ref = refs[L:]

        def total(p_ref):
            g = p_ref[0].astype(F32)
            for d in range(1, P):
                g = g + p_ref[d].astype(F32)
            return g

        g = _select(pl.program_id(0), [functools.partial(total, r) for r in p_refs])
        nm = ADAM_B1 * m_ref[...] + (1.0 - ADAM_B1) * g
        nv = ADAM_B2 * v_ref[...] + (1.0 - ADAM_B2) * (g * g)
        g_ref[...] = g
        nm_ref[...] = nm
        nv_ref[...] = nv
        d_ref[...] = -ADAM_LR * ((nm * c1) / (jnp.sqrt(nv * c2) + ADAM_EPS) + ADAM_WD * w_ref[...])

    blk = pl.BlockSpec((None, tr, C), lambda l, i: (l, i, 0))
    p_specs = [pl.BlockSpec((P, tr, C), functools.partial(lambda p, l, i: (0, jnp.where(l == p, i, 0), 0), p)) for p in range(L)]
    return pl.pallas_call(
        body, name=name, grid=(L, R // tr),
        in_specs=p_specs + [blk, blk, blk],
        out_specs=[blk] * 4,
        out_shape=[jax.ShapeDtypeStruct((L, R, C), F32)] * 4,
        compiler_params=_cparams("arbitrary", "arbitrary"),
    )(*parts, w, m, v)


def _peer(d):
    x, y, c = lax.axis_index("x"), lax.axis_index("y"), lax.axis_index("c")
    px = (1 - x) if d & 4 else x
    py = (1 - y) if d & 2 else y
    pc = (1 - c) if d & 1 else c
    return (px, py, pc), 4 * px + 2 * py + pc


N_PEER = N_DEV - 1


def _exchange_copies(ins, outs, sems, scatter):
    send_sems, recv_sems, local_sems = sems
    me = 4 * lax.axis_index("x") + 2 * lax.axis_index("y") + lax.axis_index("c")
    starts, waits = [], []
    for t in range(len(ins)):
        loc = pltpu.make_async_copy(ins[t].at[me] if scatter else ins[t], outs[t].at[me], local_sems.at[t])
        starts.append(loc.start)
        waits.append(loc.wait)
        for d in range(1, N_DEV):
            peer, pidx = _peer(d)
            src = ins[t].at[pidx] if scatter else ins[t]
            k = t * N_PEER + d - 1
            common = dict(src_ref=src, send_sem=send_sems.at[k], recv_sem=recv_sems.at[k], device_id=peer, device_id_type=MESH_ID)
            starts.append(pltpu.make_async_remote_copy(dst_ref=outs[t].at[me], **common).start)
            waits.append(pltpu.make_async_remote_copy(dst_ref=outs[t].at[pidx], **common).wait)
    return starts, waits


def _exchange_scratch(n):
    return [pltpu.SemaphoreType.DMA((n * N_PEER,)), pltpu.SemaphoreType.DMA((n * N_PEER,)), pltpu.SemaphoreType.DMA((n,))]


def _exchange_shapes(arrays, scatter):
    return [jax.ShapeDtypeStruct(a.shape if scatter else (N_DEV,) + a.shape, a.dtype) for a in arrays]


def _exchange(arrays, scatter, name):
    n = len(arrays)

    def body(*refs):
        starts, waits = _exchange_copies(refs[:n], refs[n:2 * n], refs[2 * n:], scatter)
        for f in starts:
            f()
        for f in waits:
            f()

    hbm = pl.BlockSpec(memory_space=pltpu.HBM)
    return pl.pallas_call(
        body, name=name,
        in_specs=[hbm] * n, out_specs=[hbm] * n,
        out_shape=_exchange_shapes(arrays, scatter),
        scratch_shapes=_exchange_scratch(n),
        compiler_params=pltpu.CompilerParams(has_side_effects=True),
    )(*arrays)


def _call(body, *, name, grid, in_specs, out_specs, out_shape, scratch_shapes, sem, args, carry=None):
    if carry is None:
        outs = pl.pallas_call(body, name=name, grid=grid, in_specs=in_specs, out_specs=out_specs, out_shape=out_shape,
                              scratch_shapes=scratch_shapes, compiler_params=_cparams(*sem))(*args)
        return list(outs), []
    arrays, scatter = carry
    nc, n_in, n_out, n_scr = len(arrays), len(in_specs), len(out_specs), len(scratch_shapes)

    def wrapped(*refs):
        ins, cins = refs[:n_in], refs[n_in:n_in + nc]
        outs, couts = refs[n_in + nc:n_in + nc + n_out], refs[n_in + nc + n_out:n_in + 2 * nc + n_out]
        scr = refs[n_in + 2 * nc + n_out:n_in + 2 * nc + n_out + n_scr]
        ids = [pl.program_id(k) for k in range(len(grid))]
        first = functools.reduce(jnp.logical_and, [i == 0 for i in ids])
        last = functools.reduce(jnp.logical_and, [i == g - 1 for i, g in zip(ids, grid)])
        starts, waits = _exchange_copies(cins, couts, refs[-3:], scatter)

        @pl.when(first)
        def _():
            for f in starts:
                f()

        body(*ins, *outs, *scr)

        @pl.when(last)
        def _():
            for f in waits:
                f()

    hbm = pl.BlockSpec(memory_space=pltpu.HBM)
    res = pl.pallas_call(
        wrapped, name=name, grid=grid,
        in_specs=list(in_specs) + [hbm] * nc, out_specs=list(out_specs) + [hbm] * nc,
        out_shape=list(out_shape) + _exchange_shapes(arrays, scatter),
        scratch_shapes=list(scratch_shapes) + _exchange_scratch(nc),
        compiler_params=_cparams(*(("arbitrary",) * len(grid))),
    )(*args, *arrays)
    return list(res[:n_out]), list(res[n_out:])


def _relu2_fwd(acc):
    r = jnp.maximum(acc, 0.0)
    return acc, r * r


def _relu2_bwd(acc, z):
    return (acc * 2.0 * jnp.maximum(z.astype(F32), 0.0),)


def _add_scaled(scale):
    def epi(acc, t):
        return (acc + scale * t,)
    return epi


def _local_step(x, target, P, plan):
    S, D = x.shape
    H = (D // 2) // HEAD_DIM
    W = H * HEAD_DIM
    alpha = (2 * 2) ** 0.25
    g = {}

    def mm(a, b, *, name, **kw):
        res, carried = _mm(a, b, name=name, carry=plan.carry(name, g), **kw)
        plan.arrived(name, carried, P, g)
        return res

    u0 = mm(x, P["ev_w_in"], mode="nn", name="ev_in", b_blocked=True)
    a_out, cv = _conv_fwd(u0, P["ev_conv_w"], P["ev_conv_b"], P["ev_conv_ln_g"], P["ev_conv_ln_b"], "conv_fwd")
    b_out, h_raw, h_states = _hgrn_fwd(u0, P["hgrn_lb_logits"], P["ev_gnorm_g"], "hgrn_fwd")
    cat0 = (a_out, b_out)
    mix0 = mm(cat0, P["ev_w_out"], mode="nn", name="ev_out")
    x1, xh1, rs1 = _ln_fwd(x, mix0, P["ln_mix_g"][0:1], P["ln_mix_b"][0:1], alpha, "ln_mix0")
    z0, hh0 = mm(x1, P["mlp_w1_0"], mode="nn", name="mlp_up0", out_dtypes=(BF16, BF16), epi=_relu2_fwd, b_blocked=True)
    m0 = mm(hh0, P["mlp_w2_0"], mode="nn", name="mlp_down0")
    x2, xh2, rs2 = _ln_fwd(x1, m0, P["ln_mlp_g"][0:1], P["ln_mlp_b"][0:1], alpha, "ln_mlp0")

    uc = mm(x2, P["od_w_c"], mode="nn", name="od_in_c")
    ud = mm(x2, P["od_w_d"], mode="nn", name="od_in_d")
    f_logit, F = _fgate_fwd(x2, P["od_w_f_t"], P["fox_b_f"].reshape(H, 1), "fgate_fwd")
    F3 = F.reshape(H, 1, S)
    c_out, c_out_b, c_lse, carried = _fox_fwd(uc, F3, H, "fox_fwd", carry=plan.carry("fox_fwd", g))
    plan.arrived("fox_fwd", carried, P, g)
    bias = _relbias_fwd(P["rel_bias"], "relbias_fwd")
    d_out, d_out_b, d_lse = _ca_fwd(ud, bias, H, "ca_fwd")
    cat1 = (c_out_b, d_out_b)
    mix1 = mm(cat1, P["od_w_out"], mode="nn", name="od_out")
    x3, xh3, rs3 = _ln_fwd(x2, mix1, P["ln_mix_g"][1:2], P["ln_mix_b"][1:2], alpha, "ln_mix1")
    z1, hh1 = mm(x3, P["mlp_w1_1"], mode="nn", name="mlp_up1", out_dtypes=(BF16, BF16), epi=_relu2_fwd, b_blocked=True)
    m1 = mm(hh1, P["mlp_w2_1"], mode="nn", name="mlp_down1")
    x4, xh4, rs4 = _ln_fwd(x3, m1, P["ln_mlp_g"][1:2], P["ln_mlp_b"][1:2], alpha, "ln_mlp1")

    dy, loss = _loss_head(x4, target, "loss_head")

    dzm1, dg_, db_ = _ln_bwd(dy, None, 0.0, xh4, rs4, P["ln_mlp_g"][1:2], "ln_mlp1_bwd")
    g["ln_mlp_g1"], g["ln_mlp_b1"] = dg_, db_
    g["mlp_w2_1"] = mm(hh1, dzm1, mode="tn", name="mlp_down1_dw", out_dtypes=(BF16,))
    dz1 = mm(dzm1, P["mlp_w2_1"], mode="nt", name="mlp_down1_dx", out_dtypes=(BF16,), extras=(z1,), epi=_relu2_bwd)
    g["mlp_w1_1"] = mm(x3, dz1, mode="tn", name="mlp_up1_dw", out_dtypes=(BF16,), out_blocked=True)
    dx3 = mm(dz1, P["mlp_w1_1"], mode="nt", name="mlp_up1_dx", extras=(dzm1,), epi=_add_scaled(alpha), b_blocked=True)
    dzx1, dg_, db_ = _ln_bwd(dx3, None, 0.0, xh3, rs3, P["ln_mix_g"][1:2], "ln_mix1_bwd")
    g["ln_mix_g1"], g["ln_mix_b1"] = dg_, db_
    g["od_w_out"] = mm(cat1, dzx1, mode="tn", name="od_out_dw", out_dtypes=(BF16,))
    dcat1 = mm(dzx1, P["od_w_out"], mode="nt", name="od_out_dx")
    dq_d, dk_d, dv_d, dbias = _ca_bwd(ud, bias, d_out, dcat1, H, d_lse, H, "ca_bwd")
    g["rel_bias"] = _relbias_bwd(dbias, "relbias_bwd")
    dud = jnp.concatenate([dq_d, dk_d, dv_d], axis=1)
    dq_c, dk_c, dv_c, dF3 = _fox_bwd(uc, F3, c_out, dcat1, 0, c_lse, H, "fox_bwd")
    duc = jnp.concatenate([dq_c, dk_c, dv_c], axis=1)
    dfl, dbf = _fgate_bwd(dF3.reshape(H, S), f_logit, "fgate_bwd")
    g["fox_b_f"] = dbf.reshape(1, H)
    g["od_w_c"] = mm(x2, duc, mode="tn", name="od_in_c_dw", out_dtypes=(BF16,))
    g["od_w_d"] = mm(x2, dud, mode="tn", name="od_in_d_dw", out_dtypes=(BF16,))
    g["od_w_f_t"] = mm(dfl, x2, mode="nn", name="od_in_f_dw", out_dtypes=(BF16,), exact_products=True)
    dx2 = mm(duc, P["od_w_c"], mode="nt", name="od_in_c_dx", extras=(dzx1,), epi=_add_scaled(alpha))
    dx2 = mm(dud, P["od_w_d"], mode="nt", name="od_in_d_dx", extras=(dx2,), epi=_add_scaled(1.0))
    dx2 = mm(dfl, P["od_w_f_t"], mode="tn", name="od_in_f_dx", extras=(dx2,), epi=_add_scaled(1.0), exact_products=True)

    dzm0, dg_, db_ = _ln_bwd(dx2, None, 0.0, xh2, rs2, P["ln_mlp_g"][0:1], "ln_mlp0_bwd")
    g["ln_mlp_g0"], g["ln_mlp_b0"] = dg_, db_
    g["mlp_w2_0"] = mm(hh0, dzm0, mode="tn", name="mlp_down0_dw", out_dtypes=(BF16,))
    dz0 = mm(dzm0, P["mlp_w2_0"], mode="nt", name="mlp_down0_dx", out_dtypes=(BF16,), extras=(z0,), epi=_relu2_bwd)
    g["mlp_w1_0"] = mm(x1, dz0, mode="tn", name="mlp_up0_dw", out_dtypes=(BF16,), out_blocked=True)
    dx1 = mm(dz0, P["mlp_w1_0"], mode="nt", name="mlp_up0_dx", extras=(dzm0,), epi=_add_scaled(alpha), b_blocked=True)
    dzx0, dg_, db_ = _ln_bwd(dx1, None, 0.0, xh1, rs1, P["ln_mix_g"][0:1], "ln_mix0_bwd")
    g["ln_mix_g0"], g["ln_mix_b0"] = dg_, db_
    g["ev_w_out"] = mm(cat0, dzx0, mode="tn", name="ev_out_dw", out_dtypes=(BF16,))
    dcat0 = mm(dzx0, P["ev_w_out"], mode="nt", name="ev_out_dx")
    dcv, g["ev_conv_ln_g"], g["ev_conv_ln_b"], g["ev_conv_b"] = _conv_bwd_norm(dcat0, 0, cv, P["ev_conv_ln_g"], P["ev_conv_ln_b"], "conv_bwd_norm")
    du_a, du_g, g["ev_conv_w"] = _conv_bwd_taps(dcv, u0, P["ev_conv_w"], "conv_bwd_taps")
    dhq, dhf, dhi, dhg, g["ev_gnorm_g"], g["hgrn_lb_logits"] = _hgrn_bwd(dcat0, 1, u0, h_raw, h_states, P["hgrn_lb_logits"], P["ev_gnorm_g"], "hgrn_bwd")
    du0 = jnp.concatenate([du_a, du_g, dhq, dhf, dhi, dhg], axis=1)
    g["ev_w_in"] = mm(x, du0, mode="tn", name="ev_in_dw", out_dtypes=(BF16,), out_blocked=True)
    dx0 = mm(du0, P["ev_w_in"], mode="nt", name="ev_in_dx", extras=(dzx0,), epi=_add_scaled(alpha), b_blocked=True)
    return loss, dx0, g


_NAMES = ['ev_w_in', 'ev_conv_w', 'ev_conv_b', 'ev_conv_ln_g', 'ev_conv_ln_b', 'hgrn_lb_logits', 'ev_gnorm_g', 'ev_w_out',
          'od_w_in', 'fox_b_f', 'rel_bias', 'od_w_out', 'ln_mix_g', 'ln_mix_b', 'mlp_w1', 'mlp_w2', 'ln_mlp_g', 'ln_mlp_b']
_SMALL = ['ev_conv_b', 'ev_conv_ln_g', 'ev_conv_ln_b', 'hgrn_lb_logits', 'ev_gnorm_g', 'fox_b_f', 'ln_mix_g', 'ln_mix_b',
          'ln_mlp_g', 'ln_mlp_b', 'ev_conv_w', 'rel_bias']
_PACK_COLS = 2048


def _cols_to_full(gathered):
    nd, K, n = gathered.shape
    return jnp.transpose(gathered, (1, 0, 2)).reshape(K, nd * n)


def _full_to_cols(full):
    K, N = full.shape
    return jnp.transpose(full.reshape(K, N_DEV, N // N_DEV), (1, 0, 2))


_ROW_SHARDED = ("ev_w_out", "od_w_out", "mlp_w2_0", "mlp_w2_1")
_BLOCKED = ("ev_w_in", "mlp_w1_0", "mlp_w1_1")


def _full_weights(name, gathered, heads):
    if name in _BLOCKED:
        return {name: gathered}
    if name in _ROW_SHARDED:
        return {name: gathered.reshape(-1, gathered.shape[-1])}
    full = _cols_to_full(gathered)
    if name != "od_w_in":
        return {name: full}
    w = heads * HEAD_DIM
    return {"od_w_c": full[:, :3 * w], "od_w_f_t": jnp.transpose(full[:, 3 * w:3 * w + heads]), "od_w_d": full[:, 3 * w + heads:]}


def _grad_blocks(name, g):
    if name in _BLOCKED:
        return g[name]
    if name in _ROW_SHARDED:
        return g[name].reshape(N_DEV, -1, g[name].shape[-1])
    if name == "od_w_in":
        return _full_to_cols(jnp.concatenate([g["od_w_c"], jnp.transpose(g["od_w_f_t"]), g["od_w_d"]], axis=1))
    return _full_to_cols(g[name])


class _Plan:
    GATHER = {"ev_in": ("mlp_w1_0",), "mlp_up0": ("mlp_w2_0",), "mlp_down0": ("od_w_in",), "od_in_c": ("od_w_out",),
              "fox_fwd": ("mlp_w1_1", "mlp_w2_1")}
    SCATTER = {"mlp_down1_dx": "mlp_w2_1", "mlp_up1_dx": "mlp_w1_1", "od_out_dx": "od_w_out", "od_in_c_dx": "od_w_in",
               "mlp_down0_dx": "mlp_w2_0", "mlp_up0_dx": "mlp_w1_0", "ev_out_dx": "ev_w_out", "ev_in_dx": "ev_w_in"}

    def __init__(self, shards, heads):
        self.shards, self.heads = shards, heads

    def carry(self, call, g):
        if call in self.GATHER:
            return [self.shards[n] for n in self.GATHER[call]], False
        if call in self.SCATTER:
            return [_grad_blocks(self.SCATTER[call], g)], True
        return None

    def arrived(self, call, carried, P, g):
        if call in self.GATHER:
            for n, gathered in zip(self.GATHER[call], carried):
                P.update(_full_weights(n, gathered, self.heads))
        elif call in self.SCATTER:
            g[self.SCATTER[call]] = carried[0]


def _pack(parts):
    flat = jnp.concatenate([p.reshape(-1).astype(F32) for p in parts])
    rows = -(-flat.shape[0] // (_PACK_COLS * 8)) * 8
    return jnp.pad(flat, (0, rows * _PACK_COLS - flat.shape[0])).reshape(rows, _PACK_COLS)


def _unpack(packed, shapes):
    flat = packed.reshape(-1)
    out, off = [], 0
    for s in shapes:
        n = math.prod(s)
        out.append(flat[off:off + n].reshape(s))
        off += n
    return out


def kernel(x, ev_w_in, ev_conv_w, ev_conv_b, ev_conv_ln_g, ev_conv_ln_b, hgrn_lb_logits, ev_gnorm_g, ev_w_out, od_w_in, fox_b_f, rel_bias, od_w_out, ln_mix_g, ln_mix_b, mlp_w1, mlp_w2, ln_mlp_g, ln_mlp_b, loss_target, m_ev_w_in, m_ev_conv_w, m_ev_conv_b, m_ev_conv_ln_g, m_ev_conv_ln_b, m_hgrn_lb_logits, m_ev_gnorm_g, m_ev_w_out, m_od_w_in, m_fox_b_f, m_rel_bias, m_od_w_out, m_ln_mix_g, m_ln_mix_b, m_mlp_w1, m_mlp_w2, m_ln_mlp_g, m_ln_mlp_b, v_ev_w_in, v_ev_conv_w, v_ev_conv_b, v_ev_conv_ln_g, v_ev_conv_ln_b, v_hgrn_lb_logits, v_ev_gnorm_g, v_ev_w_out, v_od_w_in, v_fox_b_f, v_rel_bias, v_od_w_out, v_ln_mix_g, v_ln_mix_b, v_mlp_w1, v_mlp_w2, v_ln_mlp_g, v_ln_mlp_b):
    args = locals()
    w = {n: args[n] for n in _NAMES}
    m = {n: args["m_" + n] for n in _NAMES}
    v = {n: args["v_" + n] for n in _NAMES}
    me = 4 * lax.axis_index("x") + 2 * lax.axis_index("y") + lax.axis_index("c")
    S, D = x.shape[1], x.shape[2]
    H = (D // 2) // HEAD_DIM
    W = H * HEAD_DIM
    n_layers = mlp_w1.shape[0]
    assert n_layers == 2 and ev_w_in.shape[0] == 1 and od_w_in.shape[0] == 1

    shards = {"ev_w_in": ev_w_in[0].astype(BF16), "ev_w_out": ev_w_out[0].astype(BF16),
              "od_w_in": od_w_in[0].astype(BF16), "od_w_out": od_w_out[0].astype(BF16)}
    for l in range(n_layers):
        shards["mlp_w1_%d" % l] = mlp_w1[l].astype(BF16)
        shards["mlp_w2_%d" % l] = mlp_w2[l].astype(BF16)
    first = ["ev_w_in", "ev_w_out", "ev_conv_w", "rel_bias"]
    G = _exchange([shards["ev_w_in"], shards["ev_w_out"], ev_conv_w[0], rel_bias[0]], False, "gather_first")
    P = {
        "ev_conv_b": ev_conv_b, "ev_conv_ln_g": ev_conv_ln_g, "ev_conv_ln_b": ev_conv_ln_b,
        "hgrn_lb_logits": hgrn_lb_logits, "ev_gnorm_g": ev_gnorm_g, "fox_b_f": fox_b_f,
        "ln_mix_g": ln_mix_g, "ln_mix_b": ln_mix_b, "ln_mlp_g": ln_mlp_g, "ln_mlp_b": ln_mlp_b,
    }
    for name, gathered in zip(first, G):
        P.update(_full_weights(name, gathered, H))

    loss, grad_x, g = _local_step(x[0], loss_target[0], P, _Plan(shards, H))
    recv = [g["ev_w_in"], g["ev_w_out"], g["od_w_in"], g["od_w_out"]]
    recv += [g["mlp_w1_%d" % l] for l in range(n_layers)] + [g["mlp_w2_%d" % l] for l in range(n_layers)]

    small = {
        "ev_conv_b": g["ev_conv_b"], "ev_conv_ln_g": g["ev_conv_ln_g"], "ev_conv_ln_b": g["ev_conv_ln_b"],
        "hgrn_lb_logits": g["hgrn_lb_logits"], "ev_gnorm_g": g["ev_gnorm_g"], "fox_b_f": g["fox_b_f"],
        "ln_mix_g": jnp.concatenate([g["ln_mix_g0"], g["ln_mix_g1"]]), "ln_mix_b": jnp.concatenate([g["ln_mix_b0"], g["ln_mix_b1"]]),
        "ln_mlp_g": jnp.concatenate([g["ln_mlp_g0"], g["ln_mlp_g1"]]), "ln_mlp_b": jnp.concatenate([g["ln_mlp_b0"], g["ln_mlp_b1"]]),
        "ev_conv_w": g["ev_conv_w"], "rel_bias": g["rel_bias"],
    }
    full_shapes = [small[n].shape for n in _SMALL]
    small_all = _exchange([_pack([small[n] for n in _SMALL])], False, "gather_small_grads")[0]
    small_sum = _unpack(_sum_parts(small_all, "sum_small_grads"), full_shapes)
    small_g = dict(zip(_SMALL, small_sum))
    cw = small_g["ev_conv_w"]
    small_g["ev_conv_w"] = lax.dynamic_slice_in_dim(cw, me * (cw.shape[1] // N_DEV), cw.shape[1] // N_DEV, axis=1)
    rb = small_g["rel_bias"]
    small_g["rel_bias"] = lax.dynamic_slice_in_dim(rb, me * (rb.shape[1] // N_DEV), rb.shape[1] // N_DEV, axis=1)

    out_g, out_d, out_m, out_v = {}, {}, {}, {}
    big = [("ev_w_in", recv[0:1]), ("ev_w_out", recv[1:2]), ("od_w_in", recv[2:3]), ("od_w_out", recv[3:4]),
           ("mlp_w1", recv[4:4 + n_layers]), ("mlp_w2", recv[4 + n_layers:4 + 2 * n_layers])]
    for name, parts in big:
        out_g[name], out_d[name], out_m[name], out_v[name] = _adamw(parts, w[name], m[name], v[name], "adamw_" + name)
    shapes = [w[n].shape for n in _SMALL]
    packed = _adamw([_pack([small_g[n] for n in _SMALL])[None]], _pack([w[n] for n in _SMALL])[None], _pack([m[n] for n in _SMALL])[None],
                    _pack([v[n] for n in _SMALL])[None], "adamw_small")
    for k, dst in enumerate((out_g, out_d, out_m, out_v)):
        for n, a in zip(_SMALL, _unpack(packed[k], shapes)):
            dst[n] = a

    loss = lax.psum(loss[0, 0], ("x", "y", "c"))
    return (loss, grad_x[None], *[out_g[n] for n in _NAMES], *[out_d[n] for n in _NAMES],
            *[out_m[n] for n in _NAMES], *[out_v[n] for n in _NAMES])
```

```python
import functools
import math

import jax
import jax.numpy as jnp
from jax import lax
from jax.experimental import pallas as pl
from jax.experimental.pallas import tpu as pltpu

F32 = jnp.float32
BF16 = jnp.bfloat16
HI = lax.Precision.HIGHEST
MESH_ID = pl.DeviceIdType.MESH

N_DEV = 8
LN_EPS = 1e-5
CHUNK = 64
HEAD_DIM = 128
CONV_WIDTH = 31
CONV_HALO = 32
CA_LEFT_CHUNKS = 8
CA_TILE = 256
CA_WIN = CA_TILE + CA_LEFT_CHUNKS * CHUNK
CA_SKEW = 1024
REL_CLIP = 256
REL_TABLE = (CHUNK - 1) + REL_CLIP + 1
NEG = -1e30
ADAM_LR = 0.001
ADAM_B1 = 0.9
ADAM_B2 = 0.999
ADAM_EPS = 1e-08
ADAM_WD = 0.01
ADAM_STEP = 10
VMEM_LIMIT_V7X = 56 * 1024 * 1024


def _cparams(*sem):
    return pltpu.CompilerParams(dimension_semantics=sem, vmem_limit_bytes=VMEM_LIMIT_V7X)


def _tile(n, t):
    if n <= t:
        return n
    for c in range(t - t % 128, 0, -128):
        if n % c == 0:
            return c
    return n


def _sigmoid(x):
    return 1.0 / (1.0 + jnp.exp(-x))


def _dot(a, b, dims, precision=None):
    return lax.dot_general(a, b, (dims, ((), ())), preferred_element_type=F32, precision=precision)


def _round_bf16(x):
    return x.astype(BF16).astype(F32)


NN = ((1,), (0,))
NT = ((1,), (1,))
TN = ((0,), (0,))


def _select(idx, loads):
    if len(loads) == 1:
        return loads[0]()
    mid = len(loads) // 2
    return lax.cond(idx < mid, lambda: _select(idx, loads[:mid]), lambda: _select(idx - mid, loads[mid:]))


def _mm(a, b, *, mode, name, out_dtypes=(F32,), extras=(), epi=None, exact_products=False, carry=None,
        b_blocked=False, out_blocked=False, tm=1024, tn=1024, tk=2048):
    dims = {"nn": NN, "nt": NT, "tn": TN}[mode]
    a_parts = a if isinstance(a, tuple) else (a,)
    n_a = len(a_parts)
    a_split_k = n_a > 1 and mode != "tn"
    a_split_m = n_a > 1 and mode == "tn"
    if mode == "tn":
        K, M = a_parts[0].shape[0], a_parts[0].shape[1] * n_a
    else:
        M, K = a_parts[0].shape[0], a_parts[0].shape[1] * n_a
    if b_blocked:
        nb, _, shard = b.shape
        N = b.shape[1] if mode == "nt" else nb * shard
    else:
        N = b.shape[0] if mode == "nt" else b.shape[1]
    tm, tn, tk = _tile(M, tm), _tile(N, tn), _tile(K, tk)
    if a_split_k:
        tk = K // n_a
    if a_split_m:
        tm = M // n_a
    per_step = 1
    if b_blocked and mode == "nt":
        per_step = max(s for s in range(1, nb + 1) if nb % s == 0 and s * shard <= max(tk, shard))
        tk = per_step * shard
    if b_blocked and mode == "nn":
        tn = shard
    if out_blocked:
        tn = N // N_DEV
    nk = K // tk
    n_ex, n_out = len(extras), len(out_dtypes)

    def body(*refs):
        a_refs, b_ref = refs[:n_a], refs[n_a]
        ex_refs = refs[n_a + 1:n_a + 1 + n_ex]
        o_refs = refs[n_a + 1 + n_ex:n_a + 1 + n_ex + n_out]
        cast = _round_bf16 if exact_products else (lambda t: t.astype(BF16))
        part = pl.program_id(0) if a_split_m else pl.program_id(2)
        av = _select(part, [functools.partial(lambda r: cast(r[...]), r) for r in a_refs])
        precision = HI if exact_products else None
        if per_step > 1:
            d = sum(_dot(av[:, q * shard:(q + 1) * shard], cast(b_ref[q]), dims, precision) for q in range(per_step))
        else:
            d = _dot(av, cast(b_ref[...]), dims, precision)

        def finish(acc):
            outs = (acc,) if epi is None else epi(acc, *[r[...] for r in ex_refs])
            for o_ref, o in zip(o_refs, outs):
                o_ref[...] = o.astype(o_ref.dtype)

        if nk == 1:
            finish(d)
        else:
            acc_ref = refs[-1]
            k = pl.program_id(2)

            @pl.when(k == 0)
            def _():
                acc_ref[...] = d

            @pl.when(k > 0)
            def _():
                acc_ref[...] += d

            @pl.when(k == nk - 1)
            def _():
                finish(acc_ref[...])

    if a_split_k:
        a_specs = [pl.BlockSpec((tm, tk), lambda i, j, k: (i, 0))] * n_a
    elif a_split_m:
        a_specs = [pl.BlockSpec((tk, tm), functools.partial(lambda p, i, j, k: (jnp.where(i == p, k, 0), 0), p)) for p in range(n_a)]
    elif mode == "tn":
        a_specs = [pl.BlockSpec((tk, tm), lambda i, j, k: (k, i))]
    else:
        a_specs = [pl.BlockSpec((tm, tk), lambda i, j, k: (i, k))]
    if b_blocked and mode == "nt" and per_step > 1:
        b_spec = pl.BlockSpec((per_step, tn, shard), lambda i, j, k: (k, j, 0))
    elif b_blocked:
        b_spec = pl.BlockSpec((None, tn, tk), lambda i, j, k: (k, j, 0)) if mode == "nt" else pl.BlockSpec((None, tk, tn), lambda i, j, k: (j, k, 0))
    else:
        b_spec = pl.BlockSpec((tn, tk), lambda i, j, k: (j, k)) if mode == "nt" else pl.BlockSpec((tk, tn), lambda i, j, k: (k, j))
    mn_spec = pl.BlockSpec((tm, tn), lambda i, j, k: (i, j))
    if out_blocked:
        out_specs = [pl.BlockSpec((None, tm, tn), lambda i, j, k: (j, i, 0))] * n_out
        out_shape = [jax.ShapeDtypeStruct((N_DEV, M, tn), dt) for dt in out_dtypes]
    else:
        out_specs = [mn_spec] * n_out
        out_shape = [jax.ShapeDtypeStruct((M, N), dt) for dt in out_dtypes]
    outs, carried = _call(
        body, name=name, grid=(M // tm, N // tn, nk),
        in_specs=a_specs + [b_spec] + [mn_spec] * n_ex,
        out_specs=out_specs, out_shape=out_shape,
        scratch_shapes=[pltpu.VMEM((tm, tn), F32)] if nk > 1 else [],
        sem=("parallel", "parallel", "arbitrary"), args=(*a_parts, b, *extras), carry=carry)
    return (outs[0] if n_out == 1 else outs), carried


def _ln_fwd(x, r, g, b, alpha, name, tr=256):
    S, D = x.shape
    tr = _tile(S, tr)

    def body(x_ref, r_ref, g_ref, b_ref, y_ref, yb_ref, xh_ref, rs_ref):
        z = alpha * x_ref[...] + r_ref[...]
        zc = z - jnp.mean(z, axis=-1, keepdims=True)
        rs = lax.rsqrt(jnp.mean(zc * zc, axis=-1, keepdims=True) + LN_EPS)
        xh = zc * rs
        xh_ref[...] = xh
        rs_ref[...] = rs
        y = xh * g_ref[...] + b_ref[...]
        y_ref[...] = y
        yb_ref[...] = y.astype(BF16)

    row = pl.BlockSpec((tr, D), lambda i: (i, 0))
    par = pl.BlockSpec((1, D), lambda i: (0, 0))
    return pl.pallas_call(
        body, name=name, grid=(S // tr,),
        in_specs=[row, row, par, par],
        out_specs=[row, row, row, pl.BlockSpec((tr, 1), lambda i: (i, 0))],
        out_shape=[jax.ShapeDtypeStruct((S, D), F32), jax.ShapeDtypeStruct((S, D), BF16), jax.ShapeDtypeStruct((S, D), F32),
                   jax.ShapeDtypeStruct((S, 1), F32)],
        compiler_params=_cparams("parallel"),
    )(x, r, g, b)


def _ln_bwd(dy, xh, rs, g, name, tr=256):
    S, D = dy.shape
    tr = _tile(S, tr)

    def body(dy_ref, xh_ref, rs_ref, g_ref, dz_ref, dzb_ref, dg_ref, db_ref):
        dyt = dy_ref[...]
        xh = xh_ref[...]
        dxh = dyt * g_ref[...]
        m1 = jnp.mean(dxh, axis=-1, keepdims=True)
        m2 = jnp.mean(dxh * xh, axis=-1, keepdims=True)
        dz = rs_ref[...] * (dxh - m1 - xh * m2)
        dz_ref[...] = dz
        dzb_ref[...] = dz.astype(BF16)

        @pl.when(pl.program_id(0) == 0)
        def _():
            dg_ref[...] = jnp.zeros_like(dg_ref)
            db_ref[...] = jnp.zeros_like(db_ref)

        dg_ref[...] += jnp.sum(dyt * xh, axis=0, keepdims=True)
        db_ref[...] += jnp.sum(dyt, axis=0, keepdims=True)

    row = pl.BlockSpec((tr, D), lambda i: (i, 0))
    par = pl.BlockSpec((1, D), lambda i: (0, 0))
    return pl.pallas_call(
        body, name=name, grid=(S // tr,),
        in_specs=[row, row, pl.BlockSpec((tr, 1), lambda i: (i, 0)), par],
        out_specs=[row, row, par, par],
        out_shape=[jax.ShapeDtypeStruct((S, D), F32), jax.ShapeDtypeStruct((S, D), BF16),
                   jax.ShapeDtypeStruct((1, D), F32), jax.ShapeDtypeStruct((1, D), F32)],
        compiler_params=_cparams("arbitrary"),
    )(dy, xh, rs, g)


def _loss_head(y, target, name, tr=256):
    S, D = y.shape
    tr = _tile(S, tr)

    def body(y_ref, t_ref, dy_ref, loss_ref):
        e = y_ref[...] - t_ref[...]
        dy_ref[...] = e * (1.0 / D)

        @pl.when(pl.program_id(0) == 0)
        def _():
            loss_ref[...] = jnp.zeros_like(loss_ref)

        loss_ref[...] += jnp.sum(jnp.sum(e * e, axis=-1, keepdims=True), axis=0, keepdims=True) * (0.5 / D)

    row = pl.BlockSpec((tr, D), lambda i: (i, 0))
    return pl.pallas_call(
        body, name=name, grid=(S // tr,),
        in_specs=[row, row],
        out_specs=[row, pl.BlockSpec((1, 1), lambda i: (0, 0))],
        out_shape=[jax.ShapeDtypeStruct((S, D), F32), jax.ShapeDtypeStruct((1, 1), F32)],
        compiler_params=_cparams("arbitrary"),
    )(y, target)


SUBLANES = 8


def _tap_reads(ext, shifted, tt, offset_of_tap):
    for r in range(SUBLANES):
        taps = [k for k in range(CONV_WIDTH) if offset_of_tap(k) % SUBLANES == r]
        if r == 0:
            src = ext
        else:
            shifted[...] = ext[pl.ds(r, tt + CONV_HALO - SUBLANES), :]
            src = shifted
        for k in taps:
            yield src, k, offset_of_tap(k) - r


def _conv_fwd(u, w, cb, lg, lb, name, tt=512):
    S = u.shape[0]
    C = w.shape[1]
    tt = _tile(S, tt)
    hpt = tt // CONV_HALO

    def body(a_ref, g_ref, ap_ref, gp_ref, w_ref, cb_ref, lg_ref, lb_ref, out_ref, cv_ref, hext, shifted):
        i = pl.program_id(0)
        hext[pl.ds(CONV_HALO, tt), :] = a_ref[...] * _sigmoid(g_ref[...])
        hp = ap_ref[...] * _sigmoid(gp_ref[...])
        hext[pl.ds(0, CONV_HALO), :] = jnp.where(i > 0, hp, 0.0)
        acc = jnp.zeros((tt, C), F32)
        for src, k, row0 in _tap_reads(hext, shifted, tt, lambda k: CONV_HALO - (CONV_WIDTH - 1) + k):
            acc = acc + w_ref[pl.ds(k, 1), :] * src[pl.ds(row0, tt), :]
        cv = acc + cb_ref[...]
        cv_ref[...] = cv
        zc = cv - jnp.mean(cv, axis=-1, keepdims=True)
        n = zc * lax.rsqrt(jnp.mean(zc * zc, axis=-1, keepdims=True) + LN_EPS) * lg_ref[...] + lb_ref[...]
        out_ref[...] = (n * _sigmoid(n)).astype(BF16)

    cur = lambda cb_: pl.BlockSpec((tt, C), lambda i: (i, cb_))
    prev = lambda cb_: pl.BlockSpec((CONV_HALO, C), lambda i: (jnp.maximum(i * hpt - 1, 0), cb_))
    par = pl.BlockSpec((1, C), lambda i: (0, 0))
    row = pl.BlockSpec((tt, C), lambda i: (i, 0))
    return pl.pallas_call(
        body, name=name, grid=(S // tt,),
        in_specs=[cur(0), cur(1), prev(0), prev(1), pl.BlockSpec((CONV_WIDTH, C), lambda i: (0, 0)), par, par, par],
        out_specs=[row, row],
        out_shape=[jax.ShapeDtypeStruct((S, C), BF16), jax.ShapeDtypeStruct((S, C), F32)],
        scratch_shapes=[pltpu.VMEM((tt + CONV_HALO, C), F32), pltpu.VMEM((tt + CONV_HALO - SUBLANES, C), F32)],
        compiler_params=_cparams("parallel"),
    )(u, u, u, u, w, cb, lg, lb)


def _conv_bwd_norm(da, da_col, cv, lg, lb, name, tt=512):
    S, C = cv.shape
    tt = _tile(S, tt)

    def body(da_ref, cv_ref, lg_ref, lb_ref, dcv_ref, dlg_ref, dlb_ref, dcb_ref):
        cv = cv_ref[...]
        zc = cv - jnp.mean(cv, axis=-1, keepdims=True)
        rs = lax.rsqrt(jnp.mean(zc * zc, axis=-1, keepdims=True) + LN_EPS)
        xh = zc * rs
        n = xh * lg_ref[...] + lb_ref[...]
        sg = _sigmoid(n)
        dn = da_ref[...] * sg * (1.0 + n * (1.0 - sg))
        dxh = dn * lg_ref[...]
        m1 = jnp.mean(dxh, axis=-1, keepdims=True)
        m2 = jnp.mean(dxh * xh, axis=-1, keepdims=True)
        dcv = rs * (dxh - m1 - xh * m2)
        dcv_ref[...] = dcv

        @pl.when(pl.program_id(0) == 0)
        def _():
            dlg_ref[...] = jnp.zeros_like(dlg_ref)
            dlb_ref[...] = jnp.zeros_like(dlb_ref)
            dcb_ref[...] = jnp.zeros_like(dcb_ref)

        dlg_ref[...] += jnp.sum(dn * xh, axis=0, keepdims=True)
        dlb_ref[...] += jnp.sum(dn, axis=0, keepdims=True)
        dcb_ref[...] += jnp.sum(dcv, axis=0, keepdims=True)

    row = pl.BlockSpec((tt, C), lambda i: (i, 0))
    par = pl.BlockSpec((1, C), lambda i: (0, 0))
    return pl.pallas_call(
        body, name=name, grid=(S // tt,),
        in_specs=[pl.BlockSpec((tt, C), lambda i: (i, da_col)), row, par, par],
        out_specs=[row, par, par, par],
        out_shape=[jax.ShapeDtypeStruct((S, C), F32)] + [jax.ShapeDtypeStruct((1, C), F32)] * 3,
        compiler_params=_cparams("arbitrary"),
    )(da, cv, lg, lb)


def _conv_bwd_taps(dcv, u, w, name, tt=512):
    S, C = dcv.shape
    tt = _tile(S, tt)
    hpt = tt // CONV_HALO
    nt = S // tt
    WPAD = 32

    def body(dc_ref, dn_ref, a_ref, g_ref, ap_ref, gp_ref, w_ref, dag_ref, dw_ref, hext, dext, shifted):
        i = pl.program_id(0)
        a = a_ref[...]
        sg = _sigmoid(g_ref[...])
        hext[pl.ds(CONV_HALO, tt), :] = a * sg
        hp = ap_ref[...] * _sigmoid(gp_ref[...])
        hext[pl.ds(0, CONV_HALO), :] = jnp.where(i > 0, hp, 0.0)
        dc = dc_ref[...]
        dext[pl.ds(0, tt), :] = dc
        dext[pl.ds(tt, CONV_HALO), :] = jnp.where(i < nt - 1, dn_ref[...], 0.0)

        @pl.when(i == 0)
        def _():
            dw_ref[...] = jnp.zeros_like(dw_ref)

        dh = jnp.zeros((tt, C), F32)
        for src, k, row0 in _tap_reads(dext, shifted, tt, lambda k: CONV_WIDTH - 1 - k):
            dh = dh + w_ref[pl.ds(k, 1), :] * src[pl.ds(row0, tt), :]
        for src, k, row0 in _tap_reads(hext, shifted, tt, lambda k: CONV_HALO - (CONV_WIDTH - 1) + k):
            dw_ref[pl.ds(k, 1), :] += jnp.sum(dc * src[pl.ds(row0, tt), :], axis=0, keepdims=True)
        dag_ref[:, :C] = dh * sg
        dag_ref[:, C:] = dh * a * sg * (1.0 - sg)

    row = pl.BlockSpec((tt, C), lambda i: (i, 0))
    nxt = pl.BlockSpec((CONV_HALO, C), lambda i: (jnp.minimum((i + 1) * hpt, S // CONV_HALO - 1), 0))
    cur = lambda cb_: pl.BlockSpec((tt, C), lambda i: (i, cb_))
    prev = lambda cb_: pl.BlockSpec((CONV_HALO, C), lambda i: (jnp.maximum(i * hpt - 1, 0), cb_))
    dag, dw = pl.pallas_call(
        body, name=name, grid=(nt,),
        in_specs=[row, nxt, cur(0), cur(1), prev(0), prev(1), pl.BlockSpec((CONV_WIDTH, C), lambda i: (0, 0))],
        out_specs=[pl.BlockSpec((tt, 2 * C), lambda i: (i, 0)), pl.BlockSpec((WPAD, C), lambda i: (0, 0))],
        out_shape=[jax.ShapeDtypeStruct((S, 2 * C), F32), jax.ShapeDtypeStruct((WPAD, C), F32)],
        scratch_shapes=[pltpu.VMEM((tt + CONV_HALO, C), F32)] * 2 + [pltpu.VMEM((tt + CONV_HALO - SUBLANES, C), F32)],
        compiler_params=_cparams("arbitrary"),
    )(dcv, dcv, u, u, u, u, w)
    return dag, dw[:CONV_WIDTH]


def _lower_bound(logits):
    e = jnp.exp(logits - jnp.max(logits, axis=0, keepdims=True))
    p = e / jnp.sum(e, axis=0, keepdims=True)
    return p[0:1, :], p


def _tri(n, lower):
    r = lax.broadcasted_iota(jnp.int32, (n, n), 0)
    c = lax.broadcasted_iota(jnp.int32, (n, n), 1)
    return ((c <= r) if lower else (c >= r)).astype(F32)


def _hgrn_gates(xq, xf, lb):
    sq = _sigmoid(xq)
    q = xq * sq
    sf = _sigmoid(xf)
    f = lb + (1.0 - lb) * sf
    logf = jnp.log(f)
    L = _dot(_tri(CHUNK, True), logf, NN, HI)
    Lend = L[CHUNK - 1:CHUNK, :]
    eL = jnp.exp(L)
    enL = jnp.exp(-L)
    eLe = jnp.exp(Lend - L)
    kk = 1.0 - f
    return dict(sq=sq, q=q, sf=sf, f=f, L=L, Lend=Lend, eL=eL, enL=enL, eLe=eLe, kk=kk,
                qe=q * eL, ke=kk * enL, kd=kk * eLe)


def _hgrn_fwd(u, lbl, gn, name, tb=256):
    S = u.shape[0]
    W = gn.shape[1]
    H = W // HEAD_DIM
    tb = _tile(S, tb)
    cpb = tb // CHUNK
    nc = S // CHUNK
    R = lbl.shape[0]

    def body(q_ref, f_ref, i_ref, g_ref, lbl_ref, gn_ref, out_ref, raw_ref, st_ref, state):
        @pl.when(pl.program_id(0) == 0)
        def _():
            state[...] = jnp.zeros_like(state)

        lb, _ = _lower_bound(lbl_ref[...])
        tril = _tri(CHUNK, True) > 0.5

        def chunk(c, carry):
            rows = pl.ds(pl.multiple_of(c * CHUNK, CHUNK), CHUNK)
            G = _hgrn_gates(q_ref[rows, :], f_ref[rows, :], lb)
            v = i_ref[rows, :]
            xg = g_ref[rows, :]
            outs = []
            for h in range(H):
                ln = slice(h * HEAD_DIM, (h + 1) * HEAD_DIM)
                qe, ke, kd, vh = G["qe"][:, ln].astype(BF16), G["ke"][:, ln].astype(BF16), G["kd"][:, ln].astype(BF16), v[:, ln].astype(BF16)
                st = state[h]
                st_ref[c, h] = st
                A = jnp.where(tril, _dot(qe, ke, NT), 0.0)
                o = _dot(A.astype(BF16), vh, NN) + _dot(qe, st.astype(BF16), NT)
                state[h] = jnp.exp(G["Lend"][:, ln]) * st + _dot(vh, kd, TN)
                outs.append(o)
            o = jnp.concatenate(outs, axis=1)
            raw_ref[rows, :] = o
            ns = []
            for h in range(H):
                oh = outs[h]
                ns.append(oh * lax.rsqrt(jnp.mean(oh * oh, axis=-1, keepdims=True) + LN_EPS))
            n = jnp.concatenate(ns, axis=1)
            out_ref[rows, :] = (n * gn_ref[...] * (xg * _sigmoid(xg))).astype(BF16)
            return carry

        lax.fori_loop(0, cpb, chunk, 0)

    col = lambda cb_: pl.BlockSpec((tb, W), lambda i: (i, cb_))
    row = pl.BlockSpec((tb, W), lambda i: (i, 0))
    return pl.pallas_call(
        body, name=name, grid=(S // tb,),
        in_specs=[col(2), col(3), col(4), col(5), pl.BlockSpec((R, W), lambda i: (0, 0)), pl.BlockSpec((1, W), lambda i: (0, 0))],
        out_specs=[row, row, pl.BlockSpec((cpb, H, HEAD_DIM, HEAD_DIM), lambda i: (i, 0, 0, 0))],
        out_shape=[jax.ShapeDtypeStruct((S, W), BF16), jax.ShapeDtypeStruct((S, W), F32),
                   jax.ShapeDtypeStruct((nc, H, HEAD_DIM, HEAD_DIM), F32)],
        scratch_shapes=[pltpu.VMEM((H, HEAD_DIM, HEAD_DIM), F32)],
        compiler_params=_cparams("arbitrary"),
    )(u, u, u, u, lbl, gn)


def _hgrn_bwd(dout, dout_col, dag, u, raw, states, lbl, gn, name, tb=256):
    S = u.shape[0]
    W = gn.shape[1]
    H = W // HEAD_DIM
    tb = _tile(S, tb)
    cpb = tb // CHUNK
    nb = S // tb
    R = lbl.shape[0]

    def body(do_ref, dag_ref, q_ref, f_ref, i_ref, g_ref, raw_ref, st_ref, lbl_ref, gn_ref,
             du_ref, dgn_ref, dlbl_ref, dstate, dlb_acc):
        @pl.when(pl.program_id(0) == 0)
        def _():
            dstate[...] = jnp.zeros_like(dstate)
            dlb_acc[...] = jnp.zeros_like(dlb_acc)
            dgn_ref[...] = jnp.zeros_like(dgn_ref)

        du_ref[:, pl.ds(0, 2 * W)] = dag_ref[...]
        dq_cols, df_cols, di_cols, dg_cols = (pl.ds(c * W, W) for c in (2, 3, 4, 5))

        lb, p = _lower_bound(lbl_ref[...])
        tril = _tri(CHUNK, True) > 0.5
        triu = _tri(CHUNK, False)
        gn_row = gn_ref[...]

        def chunk(cc, carry):
            c = cpb - 1 - cc
            rows = pl.ds(pl.multiple_of(c * CHUNK, CHUNK), CHUNK)
            xq, xf = q_ref[rows, :], f_ref[rows, :]
            G = _hgrn_gates(xq, xf, lb)
            v = i_ref[rows, :]
            xg = g_ref[rows, :]
            dy = do_ref[rows, :]
            o = raw_ref[rows, :]
            sgg = _sigmoid(xg)
            silu_g = xg * sgg
            do_parts, n_parts = [], []
            for h in range(H):
                ln = slice(h * HEAD_DIM, (h + 1) * HEAD_DIM)
                oh = o[:, ln]
                r = lax.rsqrt(jnp.mean(oh * oh, axis=-1, keepdims=True) + LN_EPS)
                nh = oh * r
                dn = dy[:, ln] * gn_row[:, ln] * silu_g[:, ln]
                do_parts.append(r * (dn - nh * jnp.mean(dn * nh, axis=-1, keepdims=True)))
                n_parts.append(nh)
            n = jnp.concatenate(n_parts, axis=1)
            dgn_ref[...] += jnp.sum(dy * n * silu_g, axis=0, keepdims=True)
            du_ref[rows, dg_cols] = dy * n * gn_row * sgg * (1.0 + xg * (1.0 - sgg))
            dqe_p, dke_p, dkd_p, dv_p, dle_p = [], [], [], [], []
            for h in range(H):
                ln = slice(h * HEAD_DIM, (h + 1) * HEAD_DIM)
                qe, ke, kd, vh = G["qe"][:, ln].astype(BF16), G["ke"][:, ln].astype(BF16), G["kd"][:, ln].astype(BF16), v[:, ln].astype(BF16)
                doh = do_parts[h].astype(BF16)
                st = st_ref[c, h]
                dst = dstate[h]
                dec = jnp.exp(G["Lend"][:, ln])
                A = jnp.where(tril, _dot(qe, ke, NT), 0.0).astype(BF16)
                dA = jnp.where(tril, _dot(doh, vh, NT), 0.0).astype(BF16)
                dv_p.append(_dot(A, doh, TN) + _dot(kd, dst.astype(BF16), NT))
                dqe_p.append(_dot(dA, ke, NN) + _dot(doh, st.astype(BF16), NN))
                dke_p.append(_dot(dA, qe, TN))
                dkd_p.append(_dot(vh, dst.astype(BF16), NN))
                dle_p.append(jnp.sum(dst * dec * st, axis=0, keepdims=True))
                dstate[h] = dec * dst + _dot(doh, qe, TN)
            dqe = jnp.concatenate(dqe_p, axis=1)
            dke = jnp.concatenate(dke_p, axis=1)
            dkd = jnp.concatenate(dkd_p, axis=1)
            qe_r, ke_r, kd_r = _round_bf16(G["qe"]), _round_bf16(G["ke"]), _round_bf16(G["kd"])
            dLend = jnp.concatenate(dle_p, axis=1) + jnp.sum(dkd * kd_r, axis=0, keepdims=True)
            dL = dqe * qe_r - dke * ke_r - dkd * kd_r
            dlogf = _dot(triu, dL, NN, HI) + dLend
            dkk = dke * G["enL"] + dkd * G["eLe"]
            dfv = dlogf / G["f"] - dkk
            sf = G["sf"]
            du_ref[rows, df_cols] = dfv * (1.0 - lb) * sf * (1.0 - sf)
            dlb_acc[...] += jnp.sum(dfv * (1.0 - sf), axis=0, keepdims=True)
            sq = G["sq"]
            du_ref[rows, dq_cols] = dqe * G["eL"] * sq * (1.0 + xq * (1.0 - sq))
            du_ref[rows, di_cols] = jnp.concatenate(dv_p, axis=1)
            return carry

        lax.fori_loop(0, cpb, chunk, 0)

        onehot0 = (lax.broadcasted_iota(jnp.int32, (R, W), 0) == 0).astype(F32)
        dlbl_ref[...] = p * (onehot0 - p[0:1, :]) * dlb_acc[...]

    rev = lambda i: nb - 1 - i
    col = lambda cb_: pl.BlockSpec((tb, W), lambda i: (rev(i), cb_))
    row = pl.BlockSpec((tb, W), lambda i: (rev(i), 0))
    par = pl.BlockSpec((1, W), lambda i: (0, 0))
    parR = pl.BlockSpec((R, W), lambda i: (0, 0))
    return pl.pallas_call(
        body, name=name, grid=(nb,),
        in_specs=[pl.BlockSpec((tb, W), lambda i: (rev(i), dout_col)), pl.BlockSpec((tb, 2 * W), lambda i: (rev(i), 0)),
                  col(2), col(3), col(4), col(5), row,
                  pl.BlockSpec((cpb, H, HEAD_DIM, HEAD_DIM), lambda i: (rev(i), 0, 0, 0)), parR, par],
        out_specs=[pl.BlockSpec((tb, 6 * W), lambda i: (rev(i), 0)), par, parR],
        out_shape=[jax.ShapeDtypeStruct((S, 6 * W), F32), jax.ShapeDtypeStruct((1, W), F32), jax.ShapeDtypeStruct((R, W), F32)],
        scratch_shapes=[pltpu.VMEM((H, HEAD_DIM, HEAD_DIM), F32), pltpu.VMEM((1, W), F32)],
        compiler_params=_cparams("arbitrary"),
    )(dout, dag, u, u, u, u, raw, states, lbl, gn)


def _fgate_fwd(xb, wft, bf, name, ts=512):
    S, D = xb.shape
    H = wft.shape[0]
    ts = _tile(S, ts)

    def body(x_ref, w_ref, b_ref, lg_ref, F_ref, carry):
        @pl.when(pl.program_id(0) == 0)
        def _():
            carry[...] = jnp.zeros_like(carry)

        lg = _dot(_round_bf16(w_ref[...]), _round_bf16(x_ref[...]), NT, HI) + b_ref[...]
        lg_ref[...] = lg
        ls = jnp.minimum(lg, 0.0) - jnp.log(1.0 + jnp.exp(-jnp.abs(lg)))
        F = _dot(ls, _tri(ts, False), NN, HI) + carry[...]
        F_ref[...] = F
        carry[...] = F[:, ts - 1:ts]

    return pl.pallas_call(
        body, name=name, grid=(S // ts,),
        in_specs=[pl.BlockSpec((ts, D), lambda i: (i, 0)), pl.BlockSpec((H, D), lambda i: (0, 0)), pl.BlockSpec((H, 1), lambda i: (0, 0))],
        out_specs=[pl.BlockSpec((H, ts), lambda i: (0, i))] * 2,
        out_shape=[jax.ShapeDtypeStruct((H, S), F32)] * 2,
        scratch_shapes=[pltpu.VMEM((H, 1), F32)],
        compiler_params=_cparams("arbitrary"),
    )(xb, wft, bf)


def _fgate_bwd(dF, lg, name, ts=512):
    H, S = dF.shape
    ts = _tile(S, ts)
    nb = S // ts

    def body(dF_ref, lg_ref, dl_ref, db_ref, carry):
        @pl.when(pl.program_id(0) == 0)
        def _():
            carry[...] = jnp.zeros_like(carry)
            db_ref[...] = jnp.zeros_like(db_ref)

        dls = _dot(dF_ref[...], _tri(ts, True), NN, HI) + carry[...]
        carry[...] = dls[:, 0:1]
        dl = dls * _sigmoid(-lg_ref[...])
        dl_ref[...] = dl
        db_ref[...] += jnp.sum(dl, axis=1, keepdims=True)

    blk = pl.BlockSpec((H, ts), lambda i: (0, nb - 1 - i))
    return pl.pallas_call(
        body, name=name, grid=(nb,),
        in_specs=[blk, blk],
        out_specs=[blk, pl.BlockSpec((H, 1), lambda i: (0, 0))],
        out_shape=[jax.ShapeDtypeStruct((H, S), F32), jax.ShapeDtypeStruct((H, 1), F32)],
        scratch_shapes=[pltpu.VMEM((H, 1), F32)],
        compiler_params=_cparams("arbitrary"),
    )(dF, lg)


def _causal_keep(i, j, tq, tk):
    rows = lax.broadcasted_iota(jnp.int32, (tq, tk), 0)
    cols = lax.broadcasted_iota(jnp.int32, (tq, tk), 1)
    return jnp.logical_or(j < i, cols <= rows)


def _causal_grid(S, t):
    n = S // t
    assert n % 2 == 0, (S, t)

    def by_query(r, c):
        second = c > r
        return jnp.where(second, n - 1 - r, r), jnp.where(second, c - r - 1, c)

    def by_key(r, c):
        second = c >= n - r
        return jnp.where(second, c - 1, r + c), jnp.where(second, n - 1 - r, r)

    return n, (n // 2, n + 1), by_query, by_key


def _fox_scores(q_ref, k_ref, fk_ref, i, j, t, scale):
    s = _dot(q_ref[...].astype(BF16), k_ref[...].astype(BF16), NT) * scale - fk_ref[...]
    return lax.cond(i == j, lambda x: jnp.where(_causal_keep(i, j, t, t), x, NEG), lambda x: x, s)


def _softmax_bwd(p, pb, dp, delta):
    return pb.astype(F32) * dp - p * delta


FOX_TILE = 1024


def _fox_tile(S):
    return _tile(S, min(FOX_TILE, S // 2))


def _fox_fwd(u, F3, H, name, carry=None):
    S = u.shape[0]
    W = H * HEAD_DIM
    t = _fox_tile(S)
    n, tri, ij, _ = _causal_grid(S, t)
    scale = HEAD_DIM ** -0.5

    def body(q_ref, k_ref, v_ref, fk_ref, o_ref, ob_ref, lse_ref, m_s, l_s, acc_s):
        i, j = ij(pl.program_id(1), pl.program_id(2))

        @pl.when(j == 0)
        def _():
            m_s[...] = jnp.full_like(m_s, NEG)
            l_s[...] = jnp.zeros_like(l_s)
            acc_s[...] = jnp.zeros_like(acc_s)

        s = _fox_scores(q_ref, k_ref, fk_ref, i, j, t, scale)
        m_new = jnp.maximum(m_s[...], jnp.max(s, axis=-1, keepdims=True))
        a = jnp.exp(m_s[...] - m_new)
        p = jnp.exp(s - m_new)
        hi = p.astype(BF16)
        lo = (p - hi.astype(F32)).astype(BF16)
        v = v_ref[...].astype(BF16)
        l_s[...] = a * l_s[...] + jnp.sum(p, axis=-1, keepdims=True)
        acc_s[...] = a * acc_s[...] + (_dot(hi, v, NN) + _dot(lo, v, NN))
        m_s[...] = m_new

        @pl.when(j == i)
        def _():
            o = acc_s[...] / l_s[...]
            o_ref[...] = o
            ob_ref[...] = o.astype(BF16)
            lse_ref[...] = m_s[...] + jnp.log(l_s[...])

    qblk = pl.BlockSpec((t, HEAD_DIM), lambda h, r, c: (ij(r, c)[0], h))
    kv = lambda off: pl.BlockSpec((t, HEAD_DIM), lambda h, r, c: (ij(r, c)[1], off * H + h))
    outs, carried = _call(
        body, name=name, grid=(H,) + tri,
        in_specs=[qblk, kv(1), kv(2), pl.BlockSpec((None, 1, t), lambda h, r, c: (h, 0, ij(r, c)[1]))],
        out_specs=[qblk, qblk, pl.BlockSpec((None, t, 1), lambda h, r, c: (h, ij(r, c)[0], 0))],
        out_shape=[jax.ShapeDtypeStruct((S, W), F32), jax.ShapeDtypeStruct((S, W), BF16), jax.ShapeDtypeStruct((H, S, 1), F32)],
        scratch_shapes=[pltpu.VMEM((t, 1), F32), pltpu.VMEM((t, 1), F32), pltpu.VMEM((t, HEAD_DIM), F32)],
        sem=("parallel", "parallel", "arbitrary"), args=(u, u, u, F3), carry=carry)
    return outs[0], outs[1], outs[2], carried


def _fox_bwd(u, F3, o, do, do_off, lse, H, name):
    S = u.shape[0]
    W = H * HEAD_DIM
    t = _fox_tile(S)
    n, tri, _, ij = _causal_grid(S, t)
    scale = HEAD_DIM ** -0.5

    def body(q_ref, k_ref, v_ref, fk_ref, o_ref, do_ref, lse_ref, dq_ref, dk_ref, dv_ref, dF_ref, dk_s, dv_s, dF_s):
        r, c = pl.program_id(1), pl.program_id(2)
        i, j = ij(r, c)

        @pl.when(jnp.logical_and(r == 0, c == 0))
        def _():
            dq_ref[...] = jnp.zeros_like(dq_ref)

        @pl.when(i == j)
        def _():
            dk_s[...] = jnp.zeros_like(dk_s)
            dv_s[...] = jnp.zeros_like(dv_s)
            dF_s[...] = jnp.zeros_like(dF_s)

        q = q_ref[...].astype(BF16)
        k = k_ref[...].astype(BF16)
        dob = do_ref[...].astype(BF16)
        p = jnp.exp(_fox_scores(q_ref, k_ref, fk_ref, i, j, t, scale) - lse_ref[...])
        dp = _dot(dob, v_ref[...].astype(BF16), NT)
        delta = jnp.sum(dob.astype(F32) * o_ref[...], axis=-1, keepdims=True)
        ds = p * (dp - delta)
        dsb = ds.astype(BF16)
        dv_s[...] += _dot(p.astype(BF16), dob, TN)
        dk_s[...] += _dot(dsb, q, TN)
        dF_s[...] -= jnp.sum(ds, axis=0, keepdims=True)
        rows = pl.ds(pl.multiple_of(i * t, t), t)
        dq_ref[rows, :] += _dot(dsb, k, NN) * scale

        @pl.when(i == n - 1)
        def _():
            dk_ref[...] = dk_s[...] * scale
            dv_ref[...] = dv_s[...]
            dF_ref[...] = dF_s[...]

    qblk = pl.BlockSpec((t, HEAD_DIM), lambda h, r, c: (ij(r, c)[0], h))
    kv = lambda off: pl.BlockSpec((t, HEAD_DIM), lambda h, r, c: (ij(r, c)[1], off * H + h))
    oblk = pl.BlockSpec((t, HEAD_DIM), lambda h, r, c: (ij(r, c)[1], h))
    fblk = pl.BlockSpec((None, 1, t), lambda h, r, c: (h, 0, ij(r, c)[1]))
    return pl.pallas_call(
        body, name=name, grid=(H,) + tri,
        in_specs=[qblk, kv(1), kv(2), fblk, qblk, pl.BlockSpec((t, HEAD_DIM), lambda h, r, c: (ij(r, c)[0], do_off + h)),
                  pl.BlockSpec((None, t, 1), lambda h, r, c: (h, ij(r, c)[0], 0))],
        out_specs=[pl.BlockSpec((S, HEAD_DIM), lambda h, r, c: (0, h)), oblk, oblk, fblk],
        out_shape=[jax.ShapeDtypeStruct((S, W), F32)] * 3 + [jax.ShapeDtypeStruct((H, 1, S), F32)],
        scratch_shapes=[pltpu.VMEM((t, HEAD_DIM), F32), pltpu.VMEM((t, HEAD_DIM), F32), pltpu.VMEM((1, t), F32)],
        compiler_params=_cparams("parallel", "arbitrary", "arbitrary"),
    )(u, u, u, F3, o, do, lse)


def _rel_index_matrix():
    a = lax.broadcasted_iota(jnp.int32, (REL_TABLE, CA_SKEW), 0)
    j = lax.broadcasted_iota(jnp.int32, (REL_TABLE, CA_SKEW), 1)
    rel = jnp.where(j < CA_WIN, CA_LEFT_CHUNKS * CHUNK - j, REL_CLIP)
    idx = jnp.clip(jnp.minimum(rel, REL_CLIP) + (CHUNK - 1), 0, REL_TABLE - 1)
    return (a == idx).astype(F32)


def _skew(x, sign):
    r = lax.broadcasted_iota(jnp.int32, x.shape, 0)
    for b in range(int(math.log2(CA_TILE))):
        sh = (1 << b) if sign > 0 else CA_SKEW - (1 << b)
        x = jnp.where((r >> b) & 1 == 1, pltpu.roll(x, sh, 1), x)
    return x


def _band_valid():
    shift = int(math.log2(CHUNK))
    r = lax.broadcasted_iota(jnp.int32, (CA_TILE, CA_WIN), 0) >> shift
    m = lax.broadcasted_iota(jnp.int32, (CA_TILE, CA_WIN), 1) >> shift
    return jnp.logical_and(m >= r, m <= r + CA_LEFT_CHUNKS)


def _relbias_fwd(table, name):
    H = table.shape[0]

    def body(t_ref, b_ref):
        rowv = _dot(t_ref[...], _rel_index_matrix(), NN, HI)
        valid = _band_valid()
        for h in range(H):
            x = _skew(jnp.broadcast_to(rowv[h:h + 1, :], (CA_TILE, CA_SKEW)), +1)
            b_ref[h] = jnp.where(valid, x[:, :CA_WIN], NEG)

    return pl.pallas_call(
        body, name=name,
        out_shape=jax.ShapeDtypeStruct((H, CA_TILE, CA_WIN), F32),
        compiler_params=pltpu.CompilerParams(vmem_limit_bytes=VMEM_LIMIT_V7X),
    )(table)


def _relbias_bwd(dB, name):
    H = dB.shape[0]
    HP = -(-H // 8) * 8

    def body(d_ref, dt_ref, rows):
        rows[...] = jnp.zeros_like(rows)
        for h in range(H):
            x = jnp.concatenate([d_ref[h], jnp.zeros((CA_TILE, CA_SKEW - CA_WIN), F32)], axis=1)
            rows[pl.ds(h, 1), :] = jnp.sum(_skew(x, -1), axis=0, keepdims=True)
        dt_ref[...] = _dot(rows[...], _rel_index_matrix(), NT, HI)[:H]

    return pl.pallas_call(
        body, name=name,
        out_shape=jax.ShapeDtypeStruct((H, REL_TABLE), F32),
        scratch_shapes=[pltpu.VMEM((HP, CA_SKEW), F32)],
        compiler_params=pltpu.CompilerParams(vmem_limit_bytes=VMEM_LIMIT_V7X),
    )(dB)


CA_PIECES = CA_WIN // CA_TILE


def _ca_fwd(u, bias, H, name):
    S = u.shape[0]
    W = H * HEAD_DIM
    T = CA_TILE
    n = S // T
    scale = HEAD_DIM ** -0.5

    def body(q_ref, k0, k1, k2, v0, v1, v2, b_ref, o_ref, ob_ref, lse_ref):
        i = pl.program_id(1)
        q = q_ref[...].astype(BF16)
        ss = []
        for pce, k_ref in enumerate((k0, k1, k2)):
            s = _dot(q, k_ref[...].astype(BF16), NT) * scale + b_ref[:, pce * T:(pce + 1) * T]
            ss.append(jnp.where(i + pce >= CA_PIECES - 1, s, NEG))
        m = jnp.maximum(jnp.maximum(jnp.max(ss[0], -1, keepdims=True), jnp.max(ss[1], -1, keepdims=True)), jnp.max(ss[2], -1, keepdims=True))
        ps = [jnp.exp(s - m) for s in ss]
        l = sum(jnp.sum(p, -1, keepdims=True) for p in ps)
        inv = 1.0 / l
        o = sum(_dot((p * inv).astype(BF16), v_ref[...].astype(BF16), NN) for p, v_ref in zip(ps, (v0, v1, v2)))
        o_ref[...] = o
        ob_ref[...] = o.astype(BF16)
        lse_ref[...] = m + jnp.log(l)

    qblk = pl.BlockSpec((T, HEAD_DIM), lambda h, i: (i, h))
    kv = lambda off, back: pl.BlockSpec((T, HEAD_DIM), lambda h, i: (jnp.maximum(i - back, 0), off * H + h))
    return pl.pallas_call(
        body, name=name, grid=(H, n),
        in_specs=[qblk, kv(1, 2), kv(1, 1), kv(1, 0), kv(2, 2), kv(2, 1), kv(2, 0),
                  pl.BlockSpec((None, T, CA_WIN), lambda h, i: (h, 0, 0))],
        out_specs=[qblk, qblk, pl.BlockSpec((None, T, 1), lambda h, i: (h, i, 0))],
        out_shape=[jax.ShapeDtypeStruct((S, W), F32), jax.ShapeDtypeStruct((S, W), BF16), jax.ShapeDtypeStruct((H, S, 1), F32)],
        compiler_params=_cparams("parallel", "arbitrary"),
    )(u, u, u, u, u, u, u, bias)


def _ca_bwd(u, bias, o, do, do_off, lse, H, name):
    S = u.shape[0]
    W = H * HEAD_DIM
    T = CA_TILE
    n = S // T
    scale = HEAD_DIM ** -0.5

    def body(q_ref, k0, k1, k2, v0, v1, v2, b_ref, o_ref, do_ref, lse_ref, dq_ref, dk_ref, dv_ref, db_ref):
        i = pl.program_id(1)

        @pl.when(i == 0)
        def _():
            db_ref[...] = jnp.zeros_like(db_ref)
            dk_ref[...] = jnp.zeros_like(dk_ref)
            dv_ref[...] = jnp.zeros_like(dv_ref)

        q = q_ref[...].astype(BF16)
        dob = do_ref[...].astype(BF16)
        delta = jnp.sum(dob.astype(F32) * o_ref[...], axis=-1, keepdims=True)
        dq = jnp.zeros((T, HEAD_DIM), F32)
        for pce, (k_ref, v_ref) in enumerate(((k0, v0), (k1, v1), (k2, v2))):
            k = k_ref[...].astype(BF16)
            s = _dot(q, k, NT) * scale + b_ref[:, pce * T:(pce + 1) * T]
            p = jnp.where(i + pce >= CA_PIECES - 1, jnp.exp(s - lse_ref[...]), 0.0)
            pb = p.astype(BF16)
            ds = _softmax_bwd(p, pb, _dot(dob, v_ref[...].astype(BF16), NT), delta)
            dsb = ds.astype(BF16)
            db_ref[:, pce * T:(pce + 1) * T] += ds
            dq = dq + _dot(dsb, k, NN)
            rows = pl.ds(pl.multiple_of(jnp.maximum(i - (CA_PIECES - 1) + pce, 0) * T, T), T)
            dk_ref[rows, :] += _dot(dsb, q, TN) * scale
            dv_ref[rows, :] += _dot(pb, dob, TN)
        dq_ref[...] = dq * scale

    qblk = pl.BlockSpec((T, HEAD_DIM), lambda h, i: (i, h))
    kv = lambda off, back: pl.BlockSpec((T, HEAD_DIM), lambda h, i: (jnp.maximum(i - back, 0), off * H + h))
    bblk = pl.BlockSpec((None, T, CA_WIN), lambda h, i: (h, 0, 0))
    head = pl.BlockSpec((S, HEAD_DIM), lambda h, i: (0, h))
    return pl.pallas_call(
        body, name=name, grid=(H, n),
        in_specs=[qblk, kv(1, 2), kv(1, 1), kv(1, 0), kv(2, 2), kv(2, 1), kv(2, 0), bblk,
                  qblk, pl.BlockSpec((T, HEAD_DIM), lambda h, i: (i, do_off + h)), pl.BlockSpec((None, T, 1), lambda h, i: (h, i, 0))],
        out_specs=[qblk, head, head, bblk],
        out_shape=[jax.ShapeDtypeStruct((S, W), F32)] * 3 + [jax.ShapeDtypeStruct((H, T, CA_WIN), F32)],
        compiler_params=_cparams("parallel", "arbitrary"),
    )(u, u, u, u, u, u, u, bias, o, do, lse)


def _sum_parts(parts, name):
    _, R, C = parts.shape

    def body(p_ref, o_ref):
        acc = p_ref[0].astype(F32)
        for d in range(1, N_DEV):
            acc = acc + p_ref[d].astype(F32)
        o_ref[...] = acc

    return pl.pallas_call(
        body, name=name, out_shape=jax.ShapeDtypeStruct((R, C), F32),
        compiler_params=pltpu.CompilerParams(vmem_limit_bytes=VMEM_LIMIT_V7X),
    )(parts)


def _adamw(parts, w, m, v, name, tr=128):
    L, R, C = w.shape
    P = parts[0].shape[0]
    tr = _tile(R, tr)
    c1 = 1.0 / (1.0 - ADAM_B1 ** ADAM_STEP)
    c2 = 1.0 / (1.0 - ADAM_B2 ** ADAM_STEP)

    def body(*refs):
        p_refs = refs[:L]
        w_ref, m_ref, v_ref, g_ref, d_ref, nm_ref, nv_ref = refs[L:]

        def total(p_ref):
            g = p_ref[0].astype(F32)
            for d in range(1, P):
                g = g + p_ref[d].astype(F32)
            return g

        g = _select(pl.program_id(0), [functools.partial(total, r) for r in p_refs])
        nm = ADAM_B1 * m_ref[...] + (1.0 - ADAM_B1) * g
        nv = ADAM_B2 * v_ref[...] + (1.0 - ADAM_B2) * (g * g)
        g_ref[...] = g
        nm_ref[...] = nm
        nv_ref[...] = nv
        d_ref[...] = -ADAM_LR * ((nm * c1) / (jnp.sqrt(nv * c2) + ADAM_EPS) + ADAM_WD * w_ref[...])

    blk = pl.BlockSpec((None, tr, C), lambda l, i: (l, i, 0))
    p_specs = [pl.BlockSpec((P, tr, C), functools.partial(lambda p, l, i: (0, jnp.where(l == p, i, 0), 0), p)) for p in range(L)]
    return pl.pallas_call(
        body, name=name, grid=(L, R // tr),
        in_specs=p_specs + [blk, blk, blk],
        out_specs=[blk] * 4,
        out_shape=[jax.ShapeDtypeStruct((L, R, C), F32)] * 4,
        compiler_params=_cparams("arbitrary", "arbitrary"),
    )(*parts, w, m, v)


def _peer(d):
    x, y, c = lax.axis_index("x"), lax.axis_index("y"), lax.axis_index("c")
    px = (1 - x) if d & 4 else x
    py = (1 - y) if d & 2 else y
    pc = (1 - c) if d & 1 else c
    return (px, py, pc), 4 * px + 2 * py + pc


N_PEER = N_DEV - 1


def _exchange_copies(ins, outs, sems, scatter):
    send_sems, recv_sems, local_sems = sems
    me = 4 * lax.axis_index("x") + 2 * lax.axis_index("y") + lax.axis_index("c")
    starts, waits = [], []
    for t in range(len(ins)):
        loc = pltpu.make_async_copy(ins[t].at[me] if scatter else ins[t], outs[t].at[me], local_sems.at[t])
        starts.append(loc.start)
        waits.append(loc.wait)
        for d in range(1, N_DEV):
            peer, pidx = _peer(d)
            src = ins[t].at[pidx] if scatter else ins[t]
            k = t * N_PEER + d - 1
            common = dict(src_ref=src, send_sem=send_sems.at[k], recv_sem=recv_sems.at[k], device_id=peer, device_id_type=MESH_ID)
            starts.append(pltpu.make_async_remote_copy(dst_ref=outs[t].at[me], **common).start)
            waits.append(pltpu.make_async_remote_copy(dst_ref=outs[t].at[pidx], **common).wait)
    return starts, waits


def _exchange_scratch(n):
    return [pltpu.SemaphoreType.DMA((n * N_PEER,)), pltpu.SemaphoreType.DMA((n * N_PEER,)), pltpu.SemaphoreType.DMA((n,))]


def _exchange_shapes(arrays, scatter):
    return [jax.ShapeDtypeStruct(a.shape if scatter else (N_DEV,) + a.shape, a.dtype) for a in arrays]


def _exchange(arrays, scatter, name):
    n = len(arrays)

    def body(*refs):
        starts, waits = _exchange_copies(refs[:n], refs[n:2 * n], refs[2 * n:], scatter)
        for f in starts:
            f()
        for f in waits:
            f()

    hbm = pl.BlockSpec(memory_space=pltpu.HBM)
    return pl.pallas_call(
        body, name=name,
        in_specs=[hbm] * n, out_specs=[hbm] * n,
        out_shape=_exchange_shapes(arrays, scatter),
        scratch_shapes=_exchange_scratch(n),
        compiler_params=pltpu.CompilerParams(has_side_effects=True),
    )(*arrays)


def _call(body, *, name, grid, in_specs, out_specs, out_shape, scratch_shapes, sem, args, carry=None):
    if carry is None:
        outs = pl.pallas_call(body, name=name, grid=grid, in_specs=in_specs, out_specs=out_specs, out_shape=out_shape,
                              scratch_shapes=scratch_shapes, compiler_params=_cparams(*sem))(*args)
        return list(outs), []
    arrays, scatter = carry
    nc, n_in, n_out, n_scr = len(arrays), len(in_specs), len(out_specs), len(scratch_shapes)

    def wrapped(*refs):
        ins, cins = refs[:n_in], refs[n_in:n_in + nc]
        outs, couts = refs[n_in + nc:n_in + nc + n_out], refs[n_in + nc + n_out:n_in + 2 * nc + n_out]
        scr = refs[n_in + 2 * nc + n_out:n_in + 2 * nc + n_out + n_scr]
        ids = [pl.program_id(k) for k in range(len(grid))]
        first = functools.reduce(jnp.logical_and, [i == 0 for i in ids])
        last = functools.reduce(jnp.logical_and, [i == g - 1 for i, g in zip(ids, grid)])
        starts, waits = _exchange_copies(cins, couts, refs[-3:], scatter)

        @pl.when(first)
        def _():
            for f in starts:
                f()

        body(*ins, *outs, *scr)

        @pl.when(last)
        def _():
            for f in waits:
                f()

    hbm = pl.BlockSpec(memory_space=pltpu.HBM)
    res = pl.pallas_call(
        wrapped, name=name, grid=grid,
        in_specs=list(in_specs) + [hbm] * nc, out_specs=list(out_specs) + [hbm] * nc,
        out_shape=list(out_shape) + _exchange_shapes(arrays, scatter),
        scratch_shapes=list(scratch_shapes) + _exchange_scratch(nc),
        compiler_params=_cparams(*(("arbitrary",) * len(grid))),
    )(*args, *arrays)
    return list(res[:n_out]), list(res[n_out:])


def _relu2_fwd(acc):
    r = jnp.maximum(acc, 0.0)
    return acc, r * r


def _relu2_bwd(acc, z):
    return (acc * 2.0 * jnp.maximum(z.astype(F32), 0.0),)


def _add_scaled(scale):
    def epi(acc, t):
        return (acc + scale * t,)
    return epi


def _local_step(x, target, P, plan):
    S, D = x.shape
    H = (D // 2) // HEAD_DIM
    W = H * HEAD_DIM
    alpha = (2 * 2) ** 0.25
    g = {}

    def mm(a, b, *, name, **kw):
        res, carried = _mm(a, b, name=name, carry=plan.carry(name, g), **kw)
        plan.arrived(name, carried, P, g)
        return res

    u0 = mm(x, P["ev_w_in"], mode="nn", name="ev_in", b_blocked=True)
    a_out, cv = _conv_fwd(u0, P["ev_conv_w"], P["ev_conv_b"], P["ev_conv_ln_g"], P["ev_conv_ln_b"], "conv_fwd")
    b_out, h_raw, h_states = _hgrn_fwd(u0, P["hgrn_lb_logits"], P["ev_gnorm_g"], "hgrn_fwd")
    cat0 = (a_out, b_out)
    mix0 = mm(cat0, P["ev_w_out"], mode="nn", name="ev_out")
    x1, x1b, xh1, rs1 = _ln_fwd(x, mix0, P["ln_mix_g"][0:1], P["ln_mix_b"][0:1], alpha, "ln_mix0")
    z0, hh0 = mm(x1b, P["mlp_w1_0"], mode="nn", name="mlp_up0", out_dtypes=(BF16, BF16), epi=_relu2_fwd, b_blocked=True)
    m0 = mm(hh0, P["mlp_w2_0"], mode="nn", name="mlp_down0")
    x2, x2b, xh2, rs2 = _ln_fwd(x1, m0, P["ln_mlp_g"][0:1], P["ln_mlp_b"][0:1], alpha, "ln_mlp0")

    uc = mm(x2b, P["od_w_c"], mode="nn", name="od_in_c")
    ud = mm(x2b, P["od_w_d"], mode="nn", name="od_in_d")
    f_logit, F = _fgate_fwd(x2b, P["od_w_f_t"], P["fox_b_f"].reshape(H, 1), "fgate_fwd")
    F3 = F.reshape(H, 1, S)
    c_out, c_out_b, c_lse, carried = _fox_fwd(uc, F3, H, "fox_fwd", carry=plan.carry("fox_fwd", g))
    plan.arrived("fox_fwd", carried, P, g)
    bias = _relbias_fwd(P["rel_bias"], "relbias_fwd")
    d_out, d_out_b, d_lse = _ca_fwd(ud, bias, H, "ca_fwd")
    cat1 = (c_out_b, d_out_b)
    mix1 = mm(cat1, P["od_w_out"], mode="nn", name="od_out")
    x3, x3b, xh3, rs3 = _ln_fwd(x2, mix1, P["ln_mix_g"][1:2], P["ln_mix_b"][1:2], alpha, "ln_mix1")
    z1, hh1 = mm(x3b, P["mlp_w1_1"], mode="nn", name="mlp_up1", out_dtypes=(BF16, BF16), epi=_relu2_fwd, b_blocked=True)
    m1 = mm(hh1, P["mlp_w2_1"], mode="nn", name="mlp_down1")
    x4, _, xh4, rs4 = _ln_fwd(x3, m1, P["ln_mlp_g"][1:2], P["ln_mlp_b"][1:2], alpha, "ln_mlp1")

    dy, loss = _loss_head(x4, target, "loss_head")

    dzm1, dzm1b, dg_, db_ = _ln_bwd(dy, xh4, rs4, P["ln_mlp_g"][1:2], "ln_mlp1_bwd")
    g["ln_mlp_g1"], g["ln_mlp_b1"] = dg_, db_
    g["mlp_w2_1"] = mm(hh1, dzm1b, mode="tn", name="mlp_down1_dw", out_dtypes=(BF16,))
    dz1 = mm(dzm1b, P["mlp_w2_1"], mode="nt", name="mlp_down1_dx", out_dtypes=(BF16,), extras=(z1,), epi=_relu2_bwd)
    g["mlp_w1_1"] = mm(x3b, dz1, mode="tn", name="mlp_up1_dw", out_dtypes=(BF16,), out_blocked=True)
    dx3 = mm(dz1, P["mlp_w1_1"], mode="nt", name="mlp_up1_dx", extras=(dzm1,), epi=_add_scaled(alpha), b_blocked=True)
    dzx1, dzx1b, dg_, db_ = _ln_bwd(dx3, xh3, rs3, P["ln_mix_g"][1:2], "ln_mix1_bwd")
    g["ln_mix_g1"], g["ln_mix_b1"] = dg_, db_
    g["od_w_out"] = mm(cat1, dzx1b, mode="tn", name="od_out_dw", out_dtypes=(BF16,))
    dcat1 = mm(dzx1b, P["od_w_out"], mode="nt", name="od_out_dx")
    dq_d, dk_d, dv_d, dbias = _ca_bwd(ud, bias, d_out, dcat1, H, d_lse, H, "ca_bwd")
    g["rel_bias"] = _relbias_bwd(dbias, "relbias_bwd")
    dud = jnp.concatenate([dq_d, dk_d, dv_d], axis=1)
    dq_c, dk_c, dv_c, dF3 = _fox_bwd(uc, F3, c_out, dcat1, 0, c_lse, H, "fox_bwd")
    duc = jnp.concatenate([dq_c, dk_c, dv_c], axis=1)
    dfl, dbf = _fgate_bwd(dF3.reshape(H, S), f_logit, "fgate_bwd")
    g["fox_b_f"] = dbf.reshape(1, H)
    g["od_w_c"] = mm(x2b, duc, mode="tn", name="od_in_c_dw", out_dtypes=(BF16,))
    g["od_w_d"] = mm(x2b, dud, mode="tn", name="od_in_d_dw", out_dtypes=(BF16,))
    g["od_w_f_t"] = mm(dfl, x2b, mode="nn", name="od_in_f_dw", out_dtypes=(BF16,), exact_products=True)
    dx2 = mm(duc, P["od_w_c"], mode="nt", name="od_in_c_dx", extras=(dzx1,), epi=_add_scaled(alpha))
    dx2 = mm(dud, P["od_w_d"], mode="nt", name="od_in_d_dx", extras=(dx2,), epi=_add_scaled(1.0))
    dx2 = mm(dfl, P["od_w_f_t"], mode="tn", name="od_in_f_dx", extras=(dx2,), epi=_add_scaled(1.0), exact_products=True)

    dzm0, dzm0b, dg_, db_ = _ln_bwd(dx2, xh2, rs2, P["ln_mlp_g"][0:1], "ln_mlp0_bwd")
    g["ln_mlp_g0"], g["ln_mlp_b0"] = dg_, db_
    g["mlp_w2_0"] = mm(hh0, dzm0b, mode="tn", name="mlp_down0_dw", out_dtypes=(BF16,))
    dz0 = mm(dzm0b, P["mlp_w2_0"], mode="nt", name="mlp_down0_dx", out_dtypes=(BF16,), extras=(z0,), epi=_relu2_bwd)
    g["mlp_w1_0"] = mm(x1b, dz0, mode="tn", name="mlp_up0_dw", out_dtypes=(BF16,), out_blocked=True)
    dx1 = mm(dz0, P["mlp_w1_0"], mode="nt", name="mlp_up0_dx", extras=(dzm0,), epi=_add_scaled(alpha), b_blocked=True)
    dzx0, dzx0b, dg_, db_ = _ln_bwd(dx1, xh1, rs1, P["ln_mix_g"][0:1], "ln_mix0_bwd")
    g["ln_mix_g0"], g["ln_mix_b0"] = dg_, db_
    g["ev_w_out"] = mm(cat0, dzx0b, mode="tn", name="ev_out_dw", out_dtypes=(BF16,))
    dcat0 = mm(dzx0b, P["ev_w_out"], mode="nt", name="ev_out_dx")
    dcv, g["ev_conv_ln_g"], g["ev_conv_ln_b"], g["ev_conv_b"] = _conv_bwd_norm(dcat0, 0, cv, P["ev_conv_ln_g"], P["ev_conv_ln_b"], "conv_bwd_norm")
    dag, g["ev_conv_w"] = _conv_bwd_taps(dcv, u0, P["ev_conv_w"], "conv_bwd_taps")
    du0, g["ev_gnorm_g"], g["hgrn_lb_logits"] = _hgrn_bwd(dcat0, 1, dag, u0, h_raw, h_states, P["hgrn_lb_logits"], P["ev_gnorm_g"], "hgrn_bwd")
    g["ev_w_in"] = mm(x, du0, mode="tn", name="ev_in_dw", out_dtypes=(BF16,), out_blocked=True)
    dx0 = mm(du0, P["ev_w_in"], mode="nt", name="ev_in_dx", extras=(dzx0,), epi=_add_scaled(alpha), b_blocked=True)
    return loss, dx0, g


_NAMES = ['ev_w_in', 'ev_conv_w', 'ev_conv_b', 'ev_conv_ln_g', 'ev_conv_ln_b', 'hgrn_lb_logits', 'ev_gnorm_g', 'ev_w_out',
          'od_w_in', 'fox_b_f', 'rel_bias', 'od_w_out', 'ln_mix_g', 'ln_mix_b', 'mlp_w1', 'mlp_w2', 'ln_mlp_g', 'ln_mlp_b']
_SMALL = ['ev_conv_b', 'ev_conv_ln_g', 'ev_conv_ln_b', 'hgrn_lb_logits', 'ev_gnorm_g', 'fox_b_f', 'ln_mix_g', 'ln_mix_b',
          'ln_mlp_g', 'ln_mlp_b', 'ev_conv_w', 'rel_bias']
_PACK_COLS = 2048


def _cols_to_full(gathered):
    nd, K, n = gathered.shape
    return jnp.transpose(gathered, (1, 0, 2)).reshape(K, nd * n)


def _full_to_cols(full):
    K, N = full.shape
    return jnp.transpose(full.reshape(K, N_DEV, N // N_DEV), (1, 0, 2))


_ROW_SHARDED = ("ev_w_out", "od_w_out", "mlp_w2_0", "mlp_w2_1")
_BLOCKED = ("ev_w_in", "mlp_w1_0", "mlp_w1_1")


def _full_weights(name, gathered, heads):
    if name in _BLOCKED:
        return {name: gathered}
    if name in _ROW_SHARDED:
        return {name: gathered.reshape(-1, gathered.shape[-1])}
    full = _cols_to_full(gathered)
    if name != "od_w_in":
        return {name: full}
    w = heads * HEAD_DIM
    return {"od_w_c": full[:, :3 * w], "od_w_f_t": jnp.transpose(full[:, 3 * w:3 * w + heads]), "od_w_d": full[:, 3 * w + heads:]}


def _grad_blocks(name, g):
    if name in _BLOCKED:
        return g[name]
    if name in _ROW_SHARDED:
        return g[name].reshape(N_DEV, -1, g[name].shape[-1])
    if name == "od_w_in":
        return _full_to_cols(jnp.concatenate([g["od_w_c"], jnp.transpose(g["od_w_f_t"]), g["od_w_d"]], axis=1))
    return _full_to_cols(g[name])


class _Plan:
    GATHER = {"ev_in": ("mlp_w1_0",), "mlp_up0": ("mlp_w2_0",), "mlp_down0": ("od_w_in",), "od_in_c": ("od_w_out",),
              "fox_fwd": ("mlp_w1_1", "mlp_w2_1")}
    SCATTER = {"mlp_down1_dx": "mlp_w2_1", "mlp_up1_dx": "mlp_w1_1", "od_out_dx": "od_w_out", "od_in_c_dx": "od_w_in",
               "mlp_down0_dx": "mlp_w2_0", "mlp_up0_dx": "mlp_w1_0", "ev_out_dx": "ev_w_out", "ev_in_dx": "ev_w_in"}

    def __init__(self, shards, heads):
        self.shards, self.heads = shards, heads

    def carry(self, call, g):
        if call in self.GATHER:
            return [self.shards[n] for n in self.GATHER[call]], False
        if call in self.SCATTER:
            return [_grad_blocks(self.SCATTER[call], g)], True
        return None

    def arrived(self, call, carried, P, g):
        if call in self.GATHER:
            for n, gathered in zip(self.GATHER[call], carried):
                P.update(_full_weights(n, gathered, self.heads))
        elif call in self.SCATTER:
            g[self.SCATTER[call]] = carried[0]


def _pack(parts):
    flat = jnp.concatenate([p.reshape(-1).astype(F32) for p in parts])
    rows = -(-flat.shape[0] // (_PACK_COLS * 8)) * 8
    return jnp.pad(flat, (0, rows * _PACK_COLS - flat.shape[0])).reshape(rows, _PACK_COLS)


def _unpack(packed, shapes):
    flat = packed.reshape(-1)
    out, off = [], 0
    for s in shapes:
        n = math.prod(s)
        out.append(flat[off:off + n].reshape(s))
        off += n
    return out


def kernel(x, ev_w_in, ev_conv_w, ev_conv_b, ev_conv_ln_g, ev_conv_ln_b, hgrn_lb_logits, ev_gnorm_g, ev_w_out, od_w_in, fox_b_f, rel_bias, od_w_out, ln_mix_g, ln_mix_b, mlp_w1, mlp_w2, ln_mlp_g, ln_mlp_b, loss_target, m_ev_w_in, m_ev_conv_w, m_ev_conv_b, m_ev_conv_ln_g, m_ev_conv_ln_b, m_hgrn_lb_logits, m_ev_gnorm_g, m_ev_w_out, m_od_w_in, m_fox_b_f, m_rel_bias, m_od_w_out, m_ln_mix_g, m_ln_mix_b, m_mlp_w1, m_mlp_w2, m_ln_mlp_g, m_ln_mlp_b, v_ev_w_in, v_ev_conv_w, v_ev_conv_b, v_ev_conv_ln_g, v_ev_conv_ln_b, v_hgrn_lb_logits, v_ev_gnorm_g, v_ev_w_out, v_od_w_in, v_fox_b_f, v_rel_bias, v_od_w_out, v_ln_mix_g, v_ln_mix_b, v_mlp_w1, v_mlp_w2, v_ln_mlp_g, v_ln_mlp_b):
    args = locals()
    w = {n: args[n] for n in _NAMES}
    m = {n: args["m_" + n] for n in _NAMES}
    v = {n: args["v_" + n] for n in _NAMES}
    me = 4 * lax.axis_index("x") + 2 * lax.axis_index("y") + lax.axis_index("c")
    S, D = x.shape[1], x.shape[2]
    H = (D // 2) // HEAD_DIM
    W = H * HEAD_DIM
    n_layers = mlp_w1.shape[0]
    assert n_layers == 2 and ev_w_in.shape[0] == 1 and od_w_in.shape[0] == 1

    shards = {"ev_w_in": ev_w_in[0].astype(BF16), "ev_w_out": ev_w_out[0].astype(BF16),
              "od_w_in": od_w_in[0].astype(BF16), "od_w_out": od_w_out[0].astype(BF16)}
    for l in range(n_layers):
        shards["mlp_w1_%d" % l] = mlp_w1[l].astype(BF16)
        shards["mlp_w2_%d" % l] = mlp_w2[l].astype(BF16)
    first = ["ev_w_in", "ev_w_out", "ev_conv_w", "rel_bias"]
    G = _exchange([shards["ev_w_in"], shards["ev_w_out"], ev_conv_w[0], rel_bias[0]], False, "gather_first")
    P = {
        "ev_conv_b": ev_conv_b, "ev_conv_ln_g": ev_conv_ln_g, "ev_conv_ln_b": ev_conv_ln_b,
        "hgrn_lb_logits": hgrn_lb_logits, "ev_gnorm_g": ev_gnorm_g, "fox_b_f": fox_b_f,
        "ln_mix_g": ln_mix_g, "ln_mix_b": ln_mix_b, "ln_mlp_g": ln_mlp_g, "ln_mlp_b": ln_mlp_b,
    }
    for name, gathered in zip(first, G):
        P.update(_full_weights(name, gathered, H))

    loss, grad_x, g = _local_step(x[0], loss_target[0], P, _Plan(shards, H))
    recv = [g["ev_w_in"], g["ev_w_out"], g["od_w_in"], g["od_w_out"]]
    recv += [g["mlp_w1_%d" % l] for l in range(n_layers)] + [g["mlp_w2_%d" % l] for l in range(n_layers)]

    small = {
        "ev_conv_b": g["ev_conv_b"], "ev_conv_ln_g": g["ev_conv_ln_g"], "ev_conv_ln_b": g["ev_conv_ln_b"],
        "hgrn_lb_logits": g["hgrn_lb_logits"], "ev_gnorm_g": g["ev_gnorm_g"], "fox_b_f": g["fox_b_f"],
        "ln_mix_g": jnp.concatenate([g["ln_mix_g0"], g["ln_mix_g1"]]), "ln_mix_b": jnp.concatenate([g["ln_mix_b0"], g["ln_mix_b1"]]),
        "ln_mlp_g": jnp.concatenate([g["ln_mlp_g0"], g["ln_mlp_g1"]]), "ln_mlp_b": jnp.concatenate([g["ln_mlp_b0"], g["ln_mlp_b1"]]),
        "ev_conv_w": g["ev_conv_w"], "rel_bias": g["rel_bias"],
    }
    full_shapes = [small[n].shape for n in _SMALL]
    small_all = _exchange([_pack([small[n] for n in _SMALL])], False, "gather_small_grads")[0]
    small_sum = _unpack(_sum_parts(small_all, "sum_small_grads"), full_shapes)
    small_g = dict(zip(_SMALL, small_sum))
    cw = small_g["ev_conv_w"]
    small_g["ev_conv_w"] = lax.dynamic_slice_in_dim(cw, me * (cw.shape[1] // N_DEV), cw.shape[1] // N_DEV, axis=1)
    rb = small_g["rel_bias"]
    small_g["rel_bias"] = lax.dynamic_slice_in_dim(rb, me * (rb.shape[1] // N_DEV), rb.shape[1] // N_DEV, axis=1)

    out_g, out_d, out_m, out_v = {}, {}, {}, {}
    big = [("ev_w_in", recv[0:1]), ("ev_w_out", recv[1:2]), ("od_w_in", recv[2:3]), ("od_w_out", recv[3:4]),
           ("mlp_w1", recv[4:4 + n_layers]), ("mlp_w2", recv[4 + n_layers:4 + 2 * n_layers])]
    for name, parts in big:
        out_g[name], out_d[name], out_m[name], out_v[name] = _adamw(parts, w[name], m[name], v[name], "adamw_" + name)
    shapes = [w[n].shape for n in _SMALL]
    packed = _adamw([_pack([small_g[n] for n in _SMALL])[None]], _pack([w[n] for n in _SMALL])[None], _pack([m[n] for n in _SMALL])[None],
                    _pack([v[n] for n in _SMALL])[None], "adamw_small")
    for k, dst in enumerate((out_g, out_d, out_m, out_v)):
        for n, a in zip(_SMALL, _unpack(packed[k], shapes)):
            dst[n] = a

    loss = lax.psum(loss[0, 0], ("x", "y", "c"))
    return (loss, grad_x[None], *[out_g[n] for n in _NAMES], *[out_d[n] for n in _NAMES],
            *[out_m[n] for n in _NAMES], *[out_v[n] for n in _NAMES])
```

```python
import functools
import math

import jax
import jax.numpy as jnp
from jax import lax
from jax.experimental import pallas as pl
from jax.experimental.pallas import tpu as pltpu

F32 = jnp.float32
BF16 = jnp.bfloat16
HI = lax.Precision.HIGHEST
MESH_ID = pl.DeviceIdType.MESH

N_DEV = 8
LN_EPS = 1e-5
CHUNK = 64
HEAD_DIM = 128
CONV_WIDTH = 31
CONV_HALO = 32
CA_LEFT_CHUNKS = 8
CA_TILE = 256
CA_WIN = CA_TILE + CA_LEFT_CHUNKS * CHUNK
CA_SKEW = 1024
REL_CLIP = 256
REL_TABLE = (CHUNK - 1) + REL_CLIP + 1
NEG = -1e30
ADAM_LR = 0.001
ADAM_B1 = 0.9
ADAM_B2 = 0.999
ADAM_EPS = 1e-08
ADAM_WD = 0.01
ADAM_STEP = 10
VMEM_LIMIT_V7X = 56 * 1024 * 1024


def _cparams(*sem):
    return pltpu.CompilerParams(dimension_semantics=sem, vmem_limit_bytes=VMEM_LIMIT_V7X)


def _tile(n, t):
    if n <= t:
        return n
    for c in range(t - t % 128, 0, -128):
        if n % c == 0:
            return c
    return n


def _sigmoid(x):
    return 1.0 / (1.0 + jnp.exp(-x))


def _dot(a, b, dims, precision=None):
    return lax.dot_general(a, b, (dims, ((), ())), preferred_element_type=F32, precision=precision)


def _round_bf16(x):
    return x.astype(BF16).astype(F32)


NN = ((1,), (0,))
NT = ((1,), (1,))
TN = ((0,), (0,))


def _select(idx, loads):
    if len(loads) == 1:
        return loads[0]()
    mid = len(loads) // 2
    return lax.cond(idx < mid, lambda: _select(idx, loads[:mid]), lambda: _select(idx - mid, loads[mid:]))


def _mm(a, b, *, mode, name, out_dtypes=(F32,), extras=(), epi=None, exact_products=False, carry=None,
        b_blocked=False, out_blocked=False, tm=1024, tn=1024, tk=2048):
    dims = {"nn": NN, "nt": NT, "tn": TN}[mode]
    a_parts = a if isinstance(a, tuple) else (a,)
    n_a = len(a_parts)
    a_split_k = n_a > 1 and mode != "tn"
    a_split_m = n_a > 1 and mode == "tn"
    if mode == "tn":
        K, M = a_parts[0].shape[0], a_parts[0].shape[1] * n_a
    else:
        M, K = a_parts[0].shape[0], a_parts[0].shape[1] * n_a
    if b_blocked:
        nb, _, shard = b.shape
        N = b.shape[1] if mode == "nt" else nb * shard
    else:
        N = b.shape[0] if mode == "nt" else b.shape[1]
    tm, tn, tk = _tile(M, tm), _tile(N, tn), _tile(K, tk)
    if a_split_k:
        tk = K // n_a
    if a_split_m:
        tm = M // n_a
    per_step = 1
    if b_blocked and mode == "nt":
        per_step = max(s for s in range(1, nb + 1) if nb % s == 0 and s * shard <= max(tk, shard))
        tk = per_step * shard
    if b_blocked and mode == "nn":
        tn = shard
    if out_blocked:
        tn = N // N_DEV
    nk = K // tk
    n_ex, n_out = len(extras), len(out_dtypes)

    def body(*refs):
        a_refs, b_ref = refs[:n_a], refs[n_a]
        ex_refs = refs[n_a + 1:n_a + 1 + n_ex]
        o_refs = refs[n_a + 1 + n_ex:n_a + 1 + n_ex + n_out]
        cast = _round_bf16 if exact_products else (lambda t: t.astype(BF16))
        part = pl.program_id(0) if a_split_m else pl.program_id(2)
        av = _select(part, [functools.partial(lambda r: cast(r[...]), r) for r in a_refs])
        precision = HI if exact_products else None
        if per_step > 1:
            d = sum(_dot(av[:, q * shard:(q + 1) * shard], cast(b_ref[q]), dims, precision) for q in range(per_step))
        else:
            d = _dot(av, cast(b_ref[...]), dims, precision)

        def finish(acc):
            outs = (acc,) if epi is None else epi(acc, *[r[...] for r in ex_refs])
            for o_ref, o in zip(o_refs, outs):
                o_ref[...] = o.astype(o_ref.dtype)

        if nk == 1:
            finish(d)
        else:
            acc_ref = refs[-1]
            k = pl.program_id(2)

            @pl.when(k == 0)
            def _():
                acc_ref[...] = d

            @pl.when(k > 0)
            def _():
                acc_ref[...] += d

            @pl.when(k == nk - 1)
            def _():
                finish(acc_ref[...])

    if a_split_k:
        a_specs = [pl.BlockSpec((tm, tk), lambda i, j, k: (i, 0))] * n_a
    elif a_split_m:
        a_specs = [pl.BlockSpec((tk, tm), functools.partial(lambda p, i, j, k: (jnp.where(i == p, k, 0), 0), p)) for p in range(n_a)]
    elif mode == "tn":
        a_specs = [pl.BlockSpec((tk, tm), lambda i, j, k: (k, i))]
    else:
        a_specs = [pl.BlockSpec((tm, tk), lambda i, j, k: (i, k))]
    if b_blocked and mode == "nt" and per_step > 1:
        b_spec = pl.BlockSpec((per_step, tn, shard), lambda i, j, k: (k, j, 0))
    elif b_blocked:
        b_spec = pl.BlockSpec((None, tn, tk), lambda i, j, k: (k, j, 0)) if mode == "nt" else pl.BlockSpec((None, tk, tn), lambda i, j, k: (j, k, 0))
    else:
        b_spec = pl.BlockSpec((tn, tk), lambda i, j, k: (j, k)) if mode == "nt" else pl.BlockSpec((tk, tn), lambda i, j, k: (k, j))
    mn_spec = pl.BlockSpec((tm, tn), lambda i, j, k: (i, j))
    if out_blocked:
        out_specs = [pl.BlockSpec((None, tm, tn), lambda i, j, k: (j, i, 0))] * n_out
        out_shape = [jax.ShapeDtypeStruct((N_DEV, M, tn), dt) for dt in out_dtypes]
    else:
        out_specs = [mn_spec] * n_out
        out_shape = [jax.ShapeDtypeStruct((M, N), dt) for dt in out_dtypes]
    outs, carried = _call(
        body, name=name, grid=(M // tm, N // tn, nk),
        in_specs=a_specs + [b_spec] + [mn_spec] * n_ex,
        out_specs=out_specs, out_shape=out_shape,
        scratch_shapes=[pltpu.VMEM((tm, tn), F32)] if nk > 1 else [],
        sem=("parallel", "parallel", "arbitrary"), args=(*a_parts, b, *extras), carry=carry)
    return (outs[0] if n_out == 1 else outs), carried


def _ln_fwd(x, r, g, b, alpha, name, tr=256):
    S, D = x.shape
    tr = _tile(S, tr)

    def body(x_ref, r_ref, g_ref, b_ref, y_ref, yb_ref, xh_ref, rs_ref):
        z = alpha * x_ref[...] + r_ref[...]
        zc = z - jnp.mean(z, axis=-1, keepdims=True)
        rs = lax.rsqrt(jnp.mean(zc * zc, axis=-1, keepdims=True) + LN_EPS)
        xh = zc * rs
        xh_ref[...] = xh
        rs_ref[...] = rs
        y = xh * g_ref[...] + b_ref[...]
        y_ref[...] = y
        yb_ref[...] = y.astype(BF16)

    row = pl.BlockSpec((tr, D), lambda i: (i, 0))
    par = pl.BlockSpec((1, D), lambda i: (0, 0))
    return pl.pallas_call(
        body, name=name, grid=(S // tr,),
        in_specs=[row, row, par, par],
        out_specs=[row, row, row, pl.BlockSpec((tr, 1), lambda i: (i, 0))],
        out_shape=[jax.ShapeDtypeStruct((S, D), F32), jax.ShapeDtypeStruct((S, D), BF16), jax.ShapeDtypeStruct((S, D), F32),
                   jax.ShapeDtypeStruct((S, 1), F32)],
        compiler_params=_cparams("parallel"),
    )(x, r, g, b)


def _ln_loss(x, r, g, b, alpha, target, name, tr=256):
    S, D = x.shape
    tr = _tile(S, tr)

    def body(x_ref, r_ref, g_ref, b_ref, t_ref, dy_ref, xh_ref, rs_ref, loss_ref):
        z = alpha * x_ref[...] + r_ref[...]
        zc = z - jnp.mean(z, axis=-1, keepdims=True)
        rs = lax.rsqrt(jnp.mean(zc * zc, axis=-1, keepdims=True) + LN_EPS)
        xh = zc * rs
        xh_ref[...] = xh
        rs_ref[...] = rs
        e = (xh * g_ref[...] + b_ref[...]) - t_ref[...]
        dy_ref[...] = e * (1.0 / D)

        @pl.when(pl.program_id(0) == 0)
        def _():
            loss_ref[...] = jnp.zeros_like(loss_ref)

        loss_ref[...] += jnp.sum(jnp.sum(e * e, axis=-1, keepdims=True), axis=0, keepdims=True) * (0.5 / D)

    row = pl.BlockSpec((tr, D), lambda i: (i, 0))
    par = pl.BlockSpec((1, D), lambda i: (0, 0))
    return pl.pallas_call(
        body, name=name, grid=(S // tr,),
        in_specs=[row, row, par, par, row],
        out_specs=[row, row, pl.BlockSpec((tr, 1), lambda i: (i, 0)), pl.BlockSpec((1, 1), lambda i: (0, 0))],
        out_shape=[jax.ShapeDtypeStruct((S, D), F32), jax.ShapeDtypeStruct((S, D), F32), jax.ShapeDtypeStruct((S, 1), F32),
                   jax.ShapeDtypeStruct((1, 1), F32)],
        compiler_params=_cparams("arbitrary"),
    )(x, r, g, b, target)


def _ln_bwd(dy, xh, rs, g, name, tr=256):
    S, D = dy.shape
    tr = _tile(S, tr)

    def body(dy_ref, xh_ref, rs_ref, g_ref, dz_ref, dzb_ref, dg_ref, db_ref):
        dyt = dy_ref[...]
        xh = xh_ref[...]
        dxh = dyt * g_ref[...]
        m1 = jnp.mean(dxh, axis=-1, keepdims=True)
        m2 = jnp.mean(dxh * xh, axis=-1, keepdims=True)
        dz = rs_ref[...] * (dxh - m1 - xh * m2)
        dz_ref[...] = dz
        dzb_ref[...] = dz.astype(BF16)

        @pl.when(pl.program_id(0) == 0)
        def _():
            dg_ref[...] = jnp.zeros_like(dg_ref)
            db_ref[...] = jnp.zeros_like(db_ref)

        dg_ref[...] += jnp.sum(dyt * xh, axis=0, keepdims=True)
        db_ref[...] += jnp.sum(dyt, axis=0, keepdims=True)

    row = pl.BlockSpec((tr, D), lambda i: (i, 0))
    par = pl.BlockSpec((1, D), lambda i: (0, 0))
    return pl.pallas_call(
        body, name=name, grid=(S // tr,),
        in_specs=[row, row, pl.BlockSpec((tr, 1), lambda i: (i, 0)), par],
        out_specs=[row, row, par, par],
        out_shape=[jax.ShapeDtypeStruct((S, D), F32), jax.ShapeDtypeStruct((S, D), BF16),
                   jax.ShapeDtypeStruct((1, D), F32), jax.ShapeDtypeStruct((1, D), F32)],
        compiler_params=_cparams("arbitrary"),
    )(dy, xh, rs, g)


SUBLANES = 8


def _tap_reads(ext, shifted, tt, offset_of_tap):
    for r in range(SUBLANES):
        taps = [k for k in range(CONV_WIDTH) if offset_of_tap(k) % SUBLANES == r]
        if r == 0:
            src = ext
        else:
            shifted[...] = ext[pl.ds(r, tt + CONV_HALO - SUBLANES), :]
            src = shifted
        for k in taps:
            yield src, k, offset_of_tap(k) - r


def _conv_fwd(u, w, cb, lg, lb, name, carry=None, tt=512):
    S = u.shape[0]
    C = w.shape[1]
    tt = _tile(S, tt)
    hpt = tt // CONV_HALO

    def body(a_ref, g_ref, ap_ref, gp_ref, w_ref, cb_ref, lg_ref, lb_ref, out_ref, cv_ref, hext, shifted):
        i = pl.program_id(0)
        hext[pl.ds(CONV_HALO, tt), :] = a_ref[...] * _sigmoid(g_ref[...])
        hp = ap_ref[...] * _sigmoid(gp_ref[...])
        hext[pl.ds(0, CONV_HALO), :] = jnp.where(i > 0, hp, 0.0)
        acc = jnp.zeros((tt, C), F32)
        for src, k, row0 in _tap_reads(hext, shifted, tt, lambda k: CONV_HALO - (CONV_WIDTH - 1) + k):
            acc = acc + w_ref[pl.ds(k, 1), :] * src[pl.ds(row0, tt), :]
        cv = acc + cb_ref[...]
        cv_ref[...] = cv
        zc = cv - jnp.mean(cv, axis=-1, keepdims=True)
        n = zc * lax.rsqrt(jnp.mean(zc * zc, axis=-1, keepdims=True) + LN_EPS) * lg_ref[...] + lb_ref[...]
        out_ref[...] = (n * _sigmoid(n)).astype(BF16)

    cur = lambda cb_: pl.BlockSpec((tt, C), lambda i: (i, cb_))
    prev = lambda cb_: pl.BlockSpec((CONV_HALO, C), lambda i: (jnp.maximum(i * hpt - 1, 0), cb_))
    par = pl.BlockSpec((1, C), lambda i: (0, 0))
    row = pl.BlockSpec((tt, C), lambda i: (i, 0))
    return _call(
        body, name=name, grid=(S // tt,),
        in_specs=[cur(0), cur(1), prev(0), prev(1), pl.BlockSpec((CONV_WIDTH, C), lambda i: (0, 0)), par, par, par],
        out_specs=[row, row],
        out_shape=[jax.ShapeDtypeStruct((S, C), BF16), jax.ShapeDtypeStruct((S, C), F32)],
        scratch_shapes=[pltpu.VMEM((tt + CONV_HALO, C), F32), pltpu.VMEM((tt + CONV_HALO - SUBLANES, C), F32)],
        sem=("parallel",), args=(u, u, u, u, w, cb, lg, lb), carry=carry)


def _conv_bwd_norm(da, da_col, cv, lg, lb, name, tt=512):
    S, C = cv.shape
    tt = _tile(S, tt)

    def body(da_ref, cv_ref, lg_ref, lb_ref, dcv_ref, dlg_ref, dlb_ref, dcb_ref):
        cv = cv_ref[...]
        zc = cv - jnp.mean(cv, axis=-1, keepdims=True)
        rs = lax.rsqrt(jnp.mean(zc * zc, axis=-1, keepdims=True) + LN_EPS)
        xh = zc * rs
        n = xh * lg_ref[...] + lb_ref[...]
        sg = _sigmoid(n)
        dn = da_ref[...] * sg * (1.0 + n * (1.0 - sg))
        dxh = dn * lg_ref[...]
        m1 = jnp.mean(dxh, axis=-1, keepdims=True)
        m2 = jnp.mean(dxh * xh, axis=-1, keepdims=True)
        dcv = rs * (dxh - m1 - xh * m2)
        dcv_ref[...] = dcv

        @pl.when(pl.program_id(0) == 0)
        def _():
            dlg_ref[...] = jnp.zeros_like(dlg_ref)
            dlb_ref[...] = jnp.zeros_like(dlb_ref)
            dcb_ref[...] = jnp.zeros_like(dcb_ref)

        dlg_ref[...] += jnp.sum(dn * xh, axis=0, keepdims=True)
        dlb_ref[...] += jnp.sum(dn, axis=0, keepdims=True)
        dcb_ref[...] += jnp.sum(dcv, axis=0, keepdims=True)

    row = pl.BlockSpec((tt, C), lambda i: (i, 0))
    par = pl.BlockSpec((1, C), lambda i: (0, 0))
    return pl.pallas_call(
        body, name=name, grid=(S // tt,),
        in_specs=[pl.BlockSpec((tt, C), lambda i: (i, da_col)), row, par, par],
        out_specs=[row, par, par, par],
        out_shape=[jax.ShapeDtypeStruct((S, C), F32)] + [jax.ShapeDtypeStruct((1, C), F32)] * 3,
        compiler_params=_cparams("arbitrary"),
    )(da, cv, lg, lb)


def _conv_bwd_taps(dcv, u, w, name, tt=512):
    S, C = dcv.shape
    tt = _tile(S, tt)
    hpt = tt // CONV_HALO
    nt = S // tt
    WPAD = 32

    def body(dc_ref, dn_ref, a_ref, g_ref, ap_ref, gp_ref, w_ref, dag_ref, dw_ref, hext, dext, shifted):
        i = pl.program_id(0)
        a = a_ref[...]
        sg = _sigmoid(g_ref[...])
        hext[pl.ds(CONV_HALO, tt), :] = a * sg
        hp = ap_ref[...] * _sigmoid(gp_ref[...])
        hext[pl.ds(0, CONV_HALO), :] = jnp.where(i > 0, hp, 0.0)
        dc = dc_ref[...]
        dext[pl.ds(0, tt), :] = dc
        dext[pl.ds(tt, CONV_HALO), :] = jnp.where(i < nt - 1, dn_ref[...], 0.0)

        @pl.when(i == 0)
        def _():
            dw_ref[...] = jnp.zeros_like(dw_ref)

        dh = jnp.zeros((tt, C), F32)
        for src, k, row0 in _tap_reads(dext, shifted, tt, lambda k: CONV_WIDTH - 1 - k):
            dh = dh + w_ref[pl.ds(k, 1), :] * src[pl.ds(row0, tt), :]
        for src, k, row0 in _tap_reads(hext, shifted, tt, lambda k: CONV_HALO - (CONV_WIDTH - 1) + k):
            dw_ref[pl.ds(k, 1), :] += jnp.sum(dc * src[pl.ds(row0, tt), :], axis=0, keepdims=True)
        dag_ref[:, :C] = dh * sg
        dag_ref[:, C:] = dh * a * sg * (1.0 - sg)

    row = pl.BlockSpec((tt, C), lambda i: (i, 0))
    nxt = pl.BlockSpec((CONV_HALO, C), lambda i: (jnp.minimum((i + 1) * hpt, S // CONV_HALO - 1), 0))
    cur = lambda cb_: pl.BlockSpec((tt, C), lambda i: (i, cb_))
    prev = lambda cb_: pl.BlockSpec((CONV_HALO, C), lambda i: (jnp.maximum(i * hpt - 1, 0), cb_))
    dag, dw = pl.pallas_call(
        body, name=name, grid=(nt,),
        in_specs=[row, nxt, cur(0), cur(1), prev(0), prev(1), pl.BlockSpec((CONV_WIDTH, C), lambda i: (0, 0))],
        out_specs=[pl.BlockSpec((tt, 2 * C), lambda i: (i, 0)), pl.BlockSpec((WPAD, C), lambda i: (0, 0))],
        out_shape=[jax.ShapeDtypeStruct((S, 2 * C), F32), jax.ShapeDtypeStruct((WPAD, C), F32)],
        scratch_shapes=[pltpu.VMEM((tt + CONV_HALO, C), F32)] * 2 + [pltpu.VMEM((tt + CONV_HALO - SUBLANES, C), F32)],
        compiler_params=_cparams("arbitrary"),
    )(dcv, dcv, u, u, u, u, w)
    return dag, dw[:CONV_WIDTH]


def _lower_bound(logits):
    e = jnp.exp(logits - jnp.max(logits, axis=0, keepdims=True))
    p = e / jnp.sum(e, axis=0, keepdims=True)
    return p[0:1, :], p


def _tri(n, lower):
    r = lax.broadcasted_iota(jnp.int32, (n, n), 0)
    c = lax.broadcasted_iota(jnp.int32, (n, n), 1)
    return ((c <= r) if lower else (c >= r)).astype(F32)


def _hgrn_gates(xq, xf, lb):
    sq = _sigmoid(xq)
    q = xq * sq
    sf = _sigmoid(xf)
    f = lb + (1.0 - lb) * sf
    logf = jnp.log(f)
    L = _dot(_tri(CHUNK, True), logf, NN, HI)
    Lend = L[CHUNK - 1:CHUNK, :]
    eL = jnp.exp(L)
    enL = jnp.exp(-L)
    eLe = jnp.exp(Lend - L)
    kk = 1.0 - f
    return dict(sq=sq, q=q, sf=sf, f=f, L=L, Lend=Lend, eL=eL, enL=enL, eLe=eLe, kk=kk,
                qe=q * eL, ke=kk * enL, kd=kk * eLe)


def _hgrn_fwd(u, lbl, gn, name, tb=256):
    S = u.shape[0]
    W = gn.shape[1]
    H = W // HEAD_DIM
    tb = _tile(S, tb)
    cpb = tb // CHUNK
    nc = S // CHUNK
    R = lbl.shape[0]

    def body(q_ref, f_ref, i_ref, g_ref, lbl_ref, gn_ref, out_ref, raw_ref, st_ref, state):
        @pl.when(pl.program_id(0) == 0)
        def _():
            state[...] = jnp.zeros_like(state)

        lb, _ = _lower_bound(lbl_ref[...])
        tril = _tri(CHUNK, True) > 0.5

        def chunk(c, carry):
            rows = pl.ds(pl.multiple_of(c * CHUNK, CHUNK), CHUNK)
            G = _hgrn_gates(q_ref[rows, :], f_ref[rows, :], lb)
            v = i_ref[rows, :]
            xg = g_ref[rows, :]
            outs = []
            for h in range(H):
                ln = slice(h * HEAD_DIM, (h + 1) * HEAD_DIM)
                qe, ke, kd, vh = G["qe"][:, ln].astype(BF16), G["ke"][:, ln].astype(BF16), G["kd"][:, ln].astype(BF16), v[:, ln].astype(BF16)
                st = state[h]
                st_ref[c, h] = st
                A = jnp.where(tril, _dot(qe, ke, NT), 0.0)
                o = _dot(A.astype(BF16), vh, NN) + _dot(qe, st.astype(BF16), NT)
                state[h] = jnp.exp(G["Lend"][:, ln]) * st + _dot(vh, kd, TN)
                outs.append(o)
            o = jnp.concatenate(outs, axis=1)
            raw_ref[rows, :] = o
            ns = []
            for h in range(H):
                oh = outs[h]
                ns.append(oh * lax.rsqrt(jnp.mean(oh * oh, axis=-1, keepdims=True) + LN_EPS))
            n = jnp.concatenate(ns, axis=1)
            out_ref[rows, :] = (n * gn_ref[...] * (xg * _sigmoid(xg))).astype(BF16)
            return carry

        lax.fori_loop(0, cpb, chunk, 0)

    col = lambda cb_: pl.BlockSpec((tb, W), lambda i: (i, cb_))
    row = pl.BlockSpec((tb, W), lambda i: (i, 0))
    return pl.pallas_call(
        body, name=name, grid=(S // tb,),
        in_specs=[col(2), col(3), col(4), col(5), pl.BlockSpec((R, W), lambda i: (0, 0)), pl.BlockSpec((1, W), lambda i: (0, 0))],
        out_specs=[row, row, pl.BlockSpec((cpb, H, HEAD_DIM, HEAD_DIM), lambda i: (i, 0, 0, 0))],
        out_shape=[jax.ShapeDtypeStruct((S, W), BF16), jax.ShapeDtypeStruct((S, W), F32),
                   jax.ShapeDtypeStruct((nc, H, HEAD_DIM, HEAD_DIM), F32)],
        scratch_shapes=[pltpu.VMEM((H, HEAD_DIM, HEAD_DIM), F32)],
        compiler_params=_cparams("arbitrary"),
    )(u, u, u, u, lbl, gn)


def _hgrn_bwd(dout, dout_col, dag, u, raw, states, lbl, gn, name, tb=256):
    S = u.shape[0]
    W = gn.shape[1]
    H = W // HEAD_DIM
    tb = _tile(S, tb)
    cpb = tb // CHUNK
    nb = S // tb
    R = lbl.shape[0]

    def body(do_ref, dag_ref, q_ref, f_ref, i_ref, g_ref, raw_ref, st_ref, lbl_ref, gn_ref,
             du_ref, dgn_ref, dlbl_ref, dstate, dlb_acc):
        @pl.when(pl.program_id(0) == 0)
        def _():
            dstate[...] = jnp.zeros_like(dstate)
            dlb_acc[...] = jnp.zeros_like(dlb_acc)
            dgn_ref[...] = jnp.zeros_like(dgn_ref)

        du_ref[:, pl.ds(0, 2 * W)] = dag_ref[...]
        dq_cols, df_cols, di_cols, dg_cols = (pl.ds(c * W, W) for c in (2, 3, 4, 5))

        lb, p = _lower_bound(lbl_ref[...])
        tril = _tri(CHUNK, True) > 0.5
        triu = _tri(CHUNK, False)
        gn_row = gn_ref[...]

        def chunk(cc, carry):
            c = cpb - 1 - cc
            rows = pl.ds(pl.multiple_of(c * CHUNK, CHUNK), CHUNK)
            xq, xf = q_ref[rows, :], f_ref[rows, :]
            G = _hgrn_gates(xq, xf, lb)
            v = i_ref[rows, :]
            xg = g_ref[rows, :]
            dy = do_ref[rows, :]
            o = raw_ref[rows, :]
            sgg = _sigmoid(xg)
            silu_g = xg * sgg
            do_parts, n_parts = [], []
            for h in range(H):
                ln = slice(h * HEAD_DIM, (h + 1) * HEAD_DIM)
                oh = o[:, ln]
                r = lax.rsqrt(jnp.mean(oh * oh, axis=-1, keepdims=True) + LN_EPS)
                nh = oh * r
                dn = dy[:, ln] * gn_row[:, ln] * silu_g[:, ln]
                do_parts.append(r * (dn - nh * jnp.mean(dn * nh, axis=-1, keepdims=True)))
                n_parts.append(nh)
            n = jnp.concatenate(n_parts, axis=1)
            dgn_ref[...] += jnp.sum(dy * n * silu_g, axis=0, keepdims=True)
            du_ref[rows, dg_cols] = dy * n * gn_row * sgg * (1.0 + xg * (1.0 - sgg))
            dqe_p, dke_p, dkd_p, dv_p, dle_p = [], [], [], [], []
            for h in range(H):
                ln = slice(h * HEAD_DIM, (h + 1) * HEAD_DIM)
                qe, ke, kd, vh = G["qe"][:, ln].astype(BF16), G["ke"][:, ln].astype(BF16), G["kd"][:, ln].astype(BF16), v[:, ln].astype(BF16)
                doh = do_parts[h].astype(BF16)
                st = st_ref[c, h]
                dst = dstate[h]
                dec = jnp.exp(G["Lend"][:, ln])
                A = jnp.where(tril, _dot(qe, ke, NT), 0.0).astype(BF16)
                dA = jnp.where(tril, _dot(doh, vh, NT), 0.0).astype(BF16)
                dv_p.append(_dot(A, doh, TN) + _dot(kd, dst.astype(BF16), NT))
                dqe_p.append(_dot(dA, ke, NN) + _dot(doh, st.astype(BF16), NN))
                dke_p.append(_dot(dA, qe, TN))
                dkd_p.append(_dot(vh, dst.astype(BF16), NN))
                dle_p.append(jnp.sum(dst * dec * st, axis=0, keepdims=True))
                dstate[h] = dec * dst + _dot(doh, qe, TN)
            dqe = jnp.concatenate(dqe_p, axis=1)
            dke = jnp.concatenate(dke_p, axis=1)
            dkd = jnp.concatenate(dkd_p, axis=1)
            qe_r, ke_r, kd_r = _round_bf16(G["qe"]), _round_bf16(G["ke"]), _round_bf16(G["kd"])
            dLend = jnp.concatenate(dle_p, axis=1) + jnp.sum(dkd * kd_r, axis=0, keepdims=True)
            dL = dqe * qe_r - dke * ke_r - dkd * kd_r
            dlogf = _dot(triu, dL, NN, HI) + dLend
            dkk = dke * G["enL"] + dkd * G["eLe"]
            dfv = dlogf / G["f"] - dkk
            sf = G["sf"]
            du_ref[rows, df_cols] = dfv * (1.0 - lb) * sf * (1.0 - sf)
            dlb_acc[...] += jnp.sum(dfv * (1.0 - sf), axis=0, keepdims=True)
            sq = G["sq"]
            du_ref[rows, dq_cols] = dqe * G["eL"] * sq * (1.0 + xq * (1.0 - sq))
            du_ref[rows, di_cols] = jnp.concatenate(dv_p, axis=1)
            return carry

        lax.fori_loop(0, cpb, chunk, 0)

        onehot0 = (lax.broadcasted_iota(jnp.int32, (R, W), 0) == 0).astype(F32)
        dlbl_ref[...] = p * (onehot0 - p[0:1, :]) * dlb_acc[...]

    rev = lambda i: nb - 1 - i
    col = lambda cb_: pl.BlockSpec((tb, W), lambda i: (rev(i), cb_))
    row = pl.BlockSpec((tb, W), lambda i: (rev(i), 0))
    par = pl.BlockSpec((1, W), lambda i: (0, 0))
    parR = pl.BlockSpec((R, W), lambda i: (0, 0))
    return pl.pallas_call(
        body, name=name, grid=(nb,),
        in_specs=[pl.BlockSpec((tb, W), lambda i: (rev(i), dout_col)), pl.BlockSpec((tb, 2 * W), lambda i: (rev(i), 0)),
                  col(2), col(3), col(4), col(5), row,
                  pl.BlockSpec((cpb, H, HEAD_DIM, HEAD_DIM), lambda i: (rev(i), 0, 0, 0)), parR, par],
        out_specs=[pl.BlockSpec((tb, 6 * W), lambda i: (rev(i), 0)), par, parR],
        out_shape=[jax.ShapeDtypeStruct((S, 6 * W), F32), jax.ShapeDtypeStruct((1, W), F32), jax.ShapeDtypeStruct((R, W), F32)],
        scratch_shapes=[pltpu.VMEM((H, HEAD_DIM, HEAD_DIM), F32), pltpu.VMEM((1, W), F32)],
        compiler_params=_cparams("arbitrary"),
    )(dout, dag, u, u, u, u, raw, states, lbl, gn)


def _fgate_fwd(xb, wft, bf, name, ts=512):
    S, D = xb.shape
    H = wft.shape[0]
    ts = _tile(S, ts)

    def body(x_ref, w_ref, b_ref, lg_ref, F_ref, carry):
        @pl.when(pl.program_id(0) == 0)
        def _():
            carry[...] = jnp.zeros_like(carry)

        lg = _dot(_round_bf16(w_ref[...]), _round_bf16(x_ref[...]), NT, HI) + b_ref[...]
        lg_ref[...] = lg
        ls = jnp.minimum(lg, 0.0) - jnp.log(1.0 + jnp.exp(-jnp.abs(lg)))
        F = _dot(ls, _tri(ts, False), NN, HI) + carry[...]
        F_ref[...] = F
        carry[...] = F[:, ts - 1:ts]

    return pl.pallas_call(
        body, name=name, grid=(S // ts,),
        in_specs=[pl.BlockSpec((ts, D), lambda i: (i, 0)), pl.BlockSpec((H, D), lambda i: (0, 0)), pl.BlockSpec((H, 1), lambda i: (0, 0))],
        out_specs=[pl.BlockSpec((H, ts), lambda i: (0, i))] * 2,
        out_shape=[jax.ShapeDtypeStruct((H, S), F32)] * 2,
        scratch_shapes=[pltpu.VMEM((H, 1), F32)],
        compiler_params=_cparams("arbitrary"),
    )(xb, wft, bf)


def _fgate_bwd(dF, lg, name, ts=512):
    H, S = dF.shape
    ts = _tile(S, ts)
    nb = S // ts

    def body(dF_ref, lg_ref, dl_ref, db_ref, carry):
        @pl.when(pl.program_id(0) == 0)
        def _():
            carry[...] = jnp.zeros_like(carry)
            db_ref[...] = jnp.zeros_like(db_ref)

        dls = _dot(dF_ref[...], _tri(ts, True), NN, HI) + carry[...]
        carry[...] = dls[:, 0:1]
        dl = dls * _sigmoid(-lg_ref[...])
        dl_ref[...] = dl
        db_ref[...] += jnp.sum(dl, axis=1, keepdims=True)

    blk = pl.BlockSpec((H, ts), lambda i: (0, nb - 1 - i))
    return pl.pallas_call(
        body, name=name, grid=(nb,),
        in_specs=[blk, blk],
        out_specs=[blk, pl.BlockSpec((H, 1), lambda i: (0, 0))],
        out_shape=[jax.ShapeDtypeStruct((H, S), F32), jax.ShapeDtypeStruct((H, 1), F32)],
        scratch_shapes=[pltpu.VMEM((H, 1), F32)],
        compiler_params=_cparams("arbitrary"),
    )(dF, lg)


def _causal_keep(i, j, tq, tk):
    rows = lax.broadcasted_iota(jnp.int32, (tq, tk), 0)
    cols = lax.broadcasted_iota(jnp.int32, (tq, tk), 1)
    return jnp.logical_or(j < i, cols <= rows)


def _causal_grid(S, t):
    n = S // t
    assert n % 2 == 0, (S, t)

    def by_query(r, c):
        second = c > r
        return jnp.where(second, n - 1 - r, r), jnp.where(second, c - r - 1, c)

    def by_key(r, c):
        second = c >= n - r
        return jnp.where(second, c - 1, r + c), jnp.where(second, n - 1 - r, r)

    return n, (n // 2, n + 1), by_query, by_key


def _fox_scores(q_ref, k_ref, fk_ref, i, j, t, scale):
    s = _dot(q_ref[...].astype(BF16), k_ref[...].astype(BF16), NT) * scale - fk_ref[...]
    return lax.cond(i == j, lambda x: jnp.where(_causal_keep(i, j, t, t), x, NEG), lambda x: x, s)


def _softmax_bwd(p, pb, dp, delta):
    return pb.astype(F32) * dp - p * delta


FOX_TILE = 1024


def _fox_tile(S):
    return _tile(S, min(FOX_TILE, S // 2))


def _fox_fwd(u, F3, H, name, carry=None):
    S = u.shape[0]
    W = H * HEAD_DIM
    t = _fox_tile(S)
    n, tri, ij, _ = _causal_grid(S, t)
    scale = HEAD_DIM ** -0.5

    def body(q_ref, k_ref, v_ref, fk_ref, o_ref, ob_ref, lse_ref, m_s, l_s, acc_s):
        i, j = ij(pl.program_id(1), pl.program_id(2))

        @pl.when(j == 0)
        def _():
            m_s[...] = jnp.full_like(m_s, NEG)
            l_s[...] = jnp.zeros_like(l_s)
            acc_s[...] = jnp.zeros_like(acc_s)

        s = _fox_scores(q_ref, k_ref, fk_ref, i, j, t, scale)
        m_new = jnp.maximum(m_s[...], jnp.max(s, axis=-1, keepdims=True))
        a = jnp.exp(m_s[...] - m_new)
        p = jnp.exp(s - m_new)
        hi = p.astype(BF16)
        lo = (p - hi.astype(F32)).astype(BF16)
        v = v_ref[...].astype(BF16)
        l_s[...] = a * l_s[...] + jnp.sum(p, axis=-1, keepdims=True)
        acc_s[...] = a * acc_s[...] + (_dot(hi, v, NN) + _dot(lo, v, NN))
        m_s[...] = m_new

        @pl.when(j == i)
        def _():
            o = acc_s[...] / l_s[...]
            o_ref[...] = o
            ob_ref[...] = o.astype(BF16)
            lse_ref[...] = m_s[...] + jnp.log(l_s[...])

    qblk = pl.BlockSpec((t, HEAD_DIM), lambda h, r, c: (ij(r, c)[0], h))
    kv = lambda off: pl.BlockSpec((t, HEAD_DIM), lambda h, r, c: (ij(r, c)[1], off * H + h))
    outs, carried = _call(
        body, name=name, grid=(H,) + tri,
        in_specs=[qblk, kv(1), kv(2), pl.BlockSpec((None, 1, t), lambda h, r, c: (h, 0, ij(r, c)[1]))],
        out_specs=[qblk, qblk, pl.BlockSpec((None, t, 1), lambda h, r, c: (h, ij(r, c)[0], 0))],
        out_shape=[jax.ShapeDtypeStruct((S, W), F32), jax.ShapeDtypeStruct((S, W), BF16), jax.ShapeDtypeStruct((H, S, 1), F32)],
        scratch_shapes=[pltpu.VMEM((t, 1), F32), pltpu.VMEM((t, 1), F32), pltpu.VMEM((t, HEAD_DIM), F32)],
        sem=("parallel", "parallel", "arbitrary"), args=(u, u, u, F3), carry=carry)
    return outs[0], outs[1], outs[2], carried


def _fox_bwd(u, F3, o, do, do_off, lse, H, name):
    S = u.shape[0]
    W = H * HEAD_DIM
    t = _fox_tile(S)
    n, tri, _, ij = _causal_grid(S, t)
    scale = HEAD_DIM ** -0.5

    def body(q_ref, k_ref, v_ref, fk_ref, o_ref, do_ref, lse_ref, dq_ref, dk_ref, dv_ref, dF_ref, dk_s, dv_s, dF_s):
        r, c = pl.program_id(1), pl.program_id(2)
        i, j = ij(r, c)

        @pl.when(jnp.logical_and(r == 0, c == 0))
        def _():
            dq_ref[...] = jnp.zeros_like(dq_ref)

        @pl.when(i == j)
        def _():
            dk_s[...] = jnp.zeros_like(dk_s)
            dv_s[...] = jnp.zeros_like(dv_s)
            dF_s[...] = jnp.zeros_like(dF_s)

        q = q_ref[...].astype(BF16)
        k = k_ref[...].astype(BF16)
        dob = do_ref[...].astype(BF16)
        p = jnp.exp(_fox_scores(q_ref, k_ref, fk_ref, i, j, t, scale) - lse_ref[...])
        dp = _dot(dob, v_ref[...].astype(BF16), NT)
        delta = jnp.sum(dob.astype(F32) * o_ref[...], axis=-1, keepdims=True)
        ds = p * (dp - delta)
        dsb = ds.astype(BF16)
        dv_s[...] += _dot(p.astype(BF16), dob, TN)
        dk_s[...] += _dot(dsb, q, TN)
        dF_s[...] -= jnp.sum(ds, axis=0, keepdims=True)
        rows = pl.ds(pl.multiple_of(i * t, t), t)
        dq_ref[rows, :] += _dot(dsb, k, NN) * scale

        @pl.when(i == n - 1)
        def _():
            dk_ref[...] = dk_s[...] * scale
            dv_ref[...] = dv_s[...]
            dF_ref[...] = dF_s[...]

    qblk = pl.BlockSpec((t, HEAD_DIM), lambda h, r, c: (ij(r, c)[0], h))
    kv = lambda off: pl.BlockSpec((t, HEAD_DIM), lambda h, r, c: (ij(r, c)[1], off * H + h))
    oblk = pl.BlockSpec((t, HEAD_DIM), lambda h, r, c: (ij(r, c)[1], h))
    fblk = pl.BlockSpec((None, 1, t), lambda h, r, c: (h, 0, ij(r, c)[1]))
    return pl.pallas_call(
        body, name=name, grid=(H,) + tri,
        in_specs=[qblk, kv(1), kv(2), fblk, qblk, pl.BlockSpec((t, HEAD_DIM), lambda h, r, c: (ij(r, c)[0], do_off + h)),
                  pl.BlockSpec((None, t, 1), lambda h, r, c: (h, ij(r, c)[0], 0))],
        out_specs=[pl.BlockSpec((S, HEAD_DIM), lambda h, r, c: (0, h)), oblk, oblk, fblk],
        out_shape=[jax.ShapeDtypeStruct((S, W), F32)] * 3 + [jax.ShapeDtypeStruct((H, 1, S), F32)],
        scratch_shapes=[pltpu.VMEM((t, HEAD_DIM), F32), pltpu.VMEM((t, HEAD_DIM), F32), pltpu.VMEM((1, t), F32)],
        compiler_params=_cparams("parallel", "arbitrary", "arbitrary"),
    )(u, u, u, F3, o, do, lse)


def _rel_index_matrix():
    a = lax.broadcasted_iota(jnp.int32, (REL_TABLE, CA_SKEW), 0)
    j = lax.broadcasted_iota(jnp.int32, (REL_TABLE, CA_SKEW), 1)
    rel = jnp.where(j < CA_WIN, CA_LEFT_CHUNKS * CHUNK - j, REL_CLIP)
    idx = jnp.clip(jnp.minimum(rel, REL_CLIP) + (CHUNK - 1), 0, REL_TABLE - 1)
    return (a == idx).astype(F32)


def _skew(x, sign):
    r = lax.broadcasted_iota(jnp.int32, x.shape, 0)
    for b in range(int(math.log2(CA_TILE))):
        sh = (1 << b) if sign > 0 else CA_SKEW - (1 << b)
        x = jnp.where((r >> b) & 1 == 1, pltpu.roll(x, sh, 1), x)
    return x


def _band_valid():
    shift = int(math.log2(CHUNK))
    r = lax.broadcasted_iota(jnp.int32, (CA_TILE, CA_WIN), 0) >> shift
    m = lax.broadcasted_iota(jnp.int32, (CA_TILE, CA_WIN), 1) >> shift
    return jnp.logical_and(m >= r, m <= r + CA_LEFT_CHUNKS)


def _relbias_fwd(table, name):
    H = table.shape[0]

    def body(t_ref, b_ref):
        rowv = _dot(t_ref[...], _rel_index_matrix(), NN, HI)
        valid = _band_valid()
        for h in range(H):
            x = _skew(jnp.broadcast_to(rowv[h:h + 1, :], (CA_TILE, CA_SKEW)), +1)
            b_ref[h] = jnp.where(valid, x[:, :CA_WIN], NEG)

    return pl.pallas_call(
        body, name=name,
        out_shape=jax.ShapeDtypeStruct((H, CA_TILE, CA_WIN), F32),
        compiler_params=pltpu.CompilerParams(vmem_limit_bytes=VMEM_LIMIT_V7X),
    )(table)


def _relbias_bwd(dB, name):
    H = dB.shape[0]
    HP = -(-H // 8) * 8

    def body(d_ref, dt_ref, rows):
        rows[...] = jnp.zeros_like(rows)
        for h in range(H):
            x = jnp.concatenate([d_ref[h], jnp.zeros((CA_TILE, CA_SKEW - CA_WIN), F32)], axis=1)
            rows[pl.ds(h, 1), :] = jnp.sum(_skew(x, -1), axis=0, keepdims=True)
        dt_ref[...] = _dot(rows[...], _rel_index_matrix(), NT, HI)[:H]

    return pl.pallas_call(
        body, name=name,
        out_shape=jax.ShapeDtypeStruct((H, REL_TABLE), F32),
        scratch_shapes=[pltpu.VMEM((HP, CA_SKEW), F32)],
        compiler_params=pltpu.CompilerParams(vmem_limit_bytes=VMEM_LIMIT_V7X),
    )(dB)


CA_PIECES = CA_WIN // CA_TILE


def _ca_fwd(u, bias, H, name):
    S = u.shape[0]
    W = H * HEAD_DIM
    T = CA_TILE
    n = S // T
    scale = HEAD_DIM ** -0.5

    def body(q_ref, k0, k1, k2, v0, v1, v2, b_ref, o_ref, ob_ref, lse_ref):
        i = pl.program_id(1)
        q = q_ref[...].astype(BF16)
        ss = []
        for pce, k_ref in enumerate((k0, k1, k2)):
            s = _dot(q, k_ref[...].astype(BF16), NT) * scale + b_ref[:, pce * T:(pce + 1) * T]
            ss.append(jnp.where(i + pce >= CA_PIECES - 1, s, NEG))
        m = jnp.maximum(jnp.maximum(jnp.max(ss[0], -1, keepdims=True), jnp.max(ss[1], -1, keepdims=True)), jnp.max(ss[2], -1, keepdims=True))
        ps = [jnp.exp(s - m) for s in ss]
        l = sum(jnp.sum(p, -1, keepdims=True) for p in ps)
        inv = 1.0 / l
        o = sum(_dot((p * inv).astype(BF16), v_ref[...].astype(BF16), NN) for p, v_ref in zip(ps, (v0, v1, v2)))
        o_ref[...] = o
        ob_ref[...] = o.astype(BF16)
        lse_ref[...] = m + jnp.log(l)

    qblk = pl.BlockSpec((T, HEAD_DIM), lambda h, i: (i, h))
    kv = lambda off, back: pl.BlockSpec((T, HEAD_DIM), lambda h, i: (jnp.maximum(i - back, 0), off * H + h))
    return pl.pallas_call(
        body, name=name, grid=(H, n),
        in_specs=[qblk, kv(1, 2), kv(1, 1), kv(1, 0), kv(2, 2), kv(2, 1), kv(2, 0),
                  pl.BlockSpec((None, T, CA_WIN), lambda h, i: (h, 0, 0))],
        out_specs=[qblk, qblk, pl.BlockSpec((None, T, 1), lambda h, i: (h, i, 0))],
        out_shape=[jax.ShapeDtypeStruct((S, W), F32), jax.ShapeDtypeStruct((S, W), BF16), jax.ShapeDtypeStruct((H, S, 1), F32)],
        compiler_params=_cparams("parallel", "arbitrary"),
    )(u, u, u, u, u, u, u, bias)


def _ca_bwd(u, bias, o, do, do_off, lse, H, name):
    S = u.shape[0]
    W = H * HEAD_DIM
    T = CA_TILE
    n = S // T
    scale = HEAD_DIM ** -0.5

    def body(q_ref, k0, k1, k2, v0, v1, v2, b_ref, o_ref, do_ref, lse_ref, dq_ref, dk_ref, dv_ref, db_ref):
        i = pl.program_id(1)

        @pl.when(i == 0)
        def _():
            db_ref[...] = jnp.zeros_like(db_ref)
            dk_ref[...] = jnp.zeros_like(dk_ref)
            dv_ref[...] = jnp.zeros_like(dv_ref)

        q = q_ref[...].astype(BF16)
        dob = do_ref[...].astype(BF16)
        delta = jnp.sum(dob.astype(F32) * o_ref[...], axis=-1, keepdims=True)
        dq = jnp.zeros((T, HEAD_DIM), F32)
        for pce, (k_ref, v_ref) in enumerate(((k0, v0), (k1, v1), (k2, v2))):
            k = k_ref[...].astype(BF16)
            s = _dot(q, k, NT) * scale + b_ref[:, pce * T:(pce + 1) * T]
            p = jnp.where(i + pce >= CA_PIECES - 1, jnp.exp(s - lse_ref[...]), 0.0)
            pb = p.astype(BF16)
            ds = _softmax_bwd(p, pb, _dot(dob, v_ref[...].astype(BF16), NT), delta)
            dsb = ds.astype(BF16)
            db_ref[:, pce * T:(pce + 1) * T] += ds
            dq = dq + _dot(dsb, k, NN)
            rows = pl.ds(pl.multiple_of(jnp.maximum(i - (CA_PIECES - 1) + pce, 0) * T, T), T)
            dk_ref[rows, :] += _dot(dsb, q, TN) * scale
            dv_ref[rows, :] += _dot(pb, dob, TN)
        dq_ref[...] = dq * scale

    qblk = pl.BlockSpec((T, HEAD_DIM), lambda h, i: (i, h))
    kv = lambda off, back: pl.BlockSpec((T, HEAD_DIM), lambda h, i: (jnp.maximum(i - back, 0), off * H + h))
    bblk = pl.BlockSpec((None, T, CA_WIN), lambda h, i: (h, 0, 0))
    head = pl.BlockSpec((S, HEAD_DIM), lambda h, i: (0, h))
    return pl.pallas_call(
        body, name=name, grid=(H, n),
        in_specs=[qblk, kv(1, 2), kv(1, 1), kv(1, 0), kv(2, 2), kv(2, 1), kv(2, 0), bblk,
                  qblk, pl.BlockSpec((T, HEAD_DIM), lambda h, i: (i, do_off + h)), pl.BlockSpec((None, T, 1), lambda h, i: (h, i, 0))],
        out_specs=[qblk, head, head, bblk],
        out_shape=[jax.ShapeDtypeStruct((S, W), F32)] * 3 + [jax.ShapeDtypeStruct((H, T, CA_WIN), F32)],
        compiler_params=_cparams("parallel", "arbitrary"),
    )(u, u, u, u, u, u, u, bias, o, do, lse)


def _sum_parts(parts, name):
    _, R, C = parts.shape

    def body(p_ref, o_ref):
        acc = p_ref[0].astype(F32)
        for d in range(1, N_DEV):
            acc = acc + p_ref[d].astype(F32)
        o_ref[...] = acc

    return pl.pallas_call(
        body, name=name, out_shape=jax.ShapeDtypeStruct((R, C), F32),
        compiler_params=pltpu.CompilerParams(vmem_limit_bytes=VMEM_LIMIT_V7X),
    )(parts)


def _adamw(parts, w, m, v, name, tr=128):
    L, R, C = w.shape
    P = parts[0].shape[0]
    tr = _tile(R, tr)
    c1 = 1.0 / (1.0 - ADAM_B1 ** ADAM_STEP)
    c2 = 1.0 / (1.0 - ADAM_B2 ** ADAM_STEP)

    def body(*refs):
        p_refs = refs[:L]
        w_ref, m_ref, v_ref, g_ref, d_ref, nm_ref, nv_ref = refs[L:]

        def total(p_ref):
            g = p_ref[0].astype(F32)
            for d in range(1, P):
                g = g + p_ref[d].astype(F32)
            return g

        g = _select(pl.program_id(0), [functools.partial(total, r) for r in p_refs])
        nm = ADAM_B1 * m_ref[...] + (1.0 - ADAM_B1) * g
        nv = ADAM_B2 * v_ref[...] + (1.0 - ADAM_B2) * (g * g)
        g_ref[...] = g
        nm_ref[...] = nm
        nv_ref[...] = nv
        d_ref[...] = -ADAM_LR * ((nm * c1) / (jnp.sqrt(nv * c2) + ADAM_EPS) + ADAM_WD * w_ref[...])

    blk = pl.BlockSpec((None, tr, C), lambda l, i: (l, i, 0))
    p_specs = [pl.BlockSpec((P, tr, C), functools.partial(lambda p, l, i: (0, jnp.where(l == p, i, 0), 0), p)) for p in range(L)]
    return pl.pallas_call(
        body, name=name, grid=(L, R // tr),
        in_specs=p_specs + [blk, blk, blk],
        out_specs=[blk] * 4,
        out_shape=[jax.ShapeDtypeStruct((L, R, C), F32)] * 4,
        compiler_params=_cparams("arbitrary", "arbitrary"),
    )(*parts, w, m, v)


def _peer(d):
    x, y, c = lax.axis_index("x"), lax.axis_index("y"), lax.axis_index("c")
    px = (1 - x) if d & 4 else x
    py = (1 - y) if d & 2 else y
    pc = (1 - c) if d & 1 else c
    return (px, py, pc), 4 * px + 2 * py + pc


N_PEER = N_DEV - 1


def _exchange_copies(ins, outs, sems, scatter):
    send_sems, recv_sems, local_sems = sems
    me = 4 * lax.axis_index("x") + 2 * lax.axis_index("y") + lax.axis_index("c")
    starts, waits = [], []
    for t in range(len(ins)):
        loc = pltpu.make_async_copy(ins[t].at[me] if scatter else ins[t], outs[t].at[me], local_sems.at[t])
        starts.append(loc.start)
        waits.append(loc.wait)
        for d in range(1, N_DEV):
            peer, pidx = _peer(d)
            src = ins[t].at[pidx] if scatter else ins[t]
            k = t * N_PEER + d - 1
            common = dict(src_ref=src, send_sem=send_sems.at[k], recv_sem=recv_sems.at[k], device_id=peer, device_id_type=MESH_ID)
            starts.append(pltpu.make_async_remote_copy(dst_ref=outs[t].at[me], **common).start)
            waits.append(pltpu.make_async_remote_copy(dst_ref=outs[t].at[pidx], **common).wait)
    return starts, waits


def _exchange_scratch(n):
    return [pltpu.SemaphoreType.DMA((n * N_PEER,)), pltpu.SemaphoreType.DMA((n * N_PEER,)), pltpu.SemaphoreType.DMA((n,))]


def _exchange_shapes(arrays, scatter):
    return [jax.ShapeDtypeStruct(a.shape if scatter else (N_DEV,) + a.shape, a.dtype) for a in arrays]


def _exchange(arrays, scatter, name):
    n = len(arrays)

    def body(*refs):
        starts, waits = _exchange_copies(refs[:n], refs[n:2 * n], refs[2 * n:], scatter)
        for f in starts:
            f()
        for f in waits:
            f()

    hbm = pl.BlockSpec(memory_space=pltpu.HBM)
    return pl.pallas_call(
        body, name=name,
        in_specs=[hbm] * n, out_specs=[hbm] * n,
        out_shape=_exchange_shapes(arrays, scatter),
        scratch_shapes=_exchange_scratch(n),
        compiler_params=pltpu.CompilerParams(has_side_effects=True),
    )(*arrays)


def _call(body, *, name, grid, in_specs, out_specs, out_shape, scratch_shapes, sem, args, carry=None):
    if carry is None:
        outs = pl.pallas_call(body, name=name, grid=grid, in_specs=in_specs, out_specs=out_specs, out_shape=out_shape,
                              scratch_shapes=scratch_shapes, compiler_params=_cparams(*sem))(*args)
        return list(outs), []
    arrays, scatter = carry
    nc, n_in, n_out, n_scr = len(arrays), len(in_specs), len(out_specs), len(scratch_shapes)

    def wrapped(*refs):
        ins, cins = refs[:n_in], refs[n_in:n_in + nc]
        outs, couts = refs[n_in + nc:n_in + nc + n_out], refs[n_in + nc + n_out:n_in + 2 * nc + n_out]
        scr = refs[n_in + 2 * nc + n_out:n_in + 2 * nc + n_out + n_scr]
        ids = [pl.program_id(k) for k in range(len(grid))]
        first = functools.reduce(jnp.logical_and, [i == 0 for i in ids])
        last = functools.reduce(jnp.logical_and, [i == g - 1 for i, g in zip(ids, grid)])
        starts, waits = _exchange_copies(cins, couts, refs[-3:], scatter)

        @pl.when(first)
        def _():
            for f in starts:
                f()

        body(*ins, *outs, *scr)

        @pl.when(last)
        def _():
            for f in waits:
                f()

    hbm = pl.BlockSpec(memory_space=pltpu.HBM)
    res = pl.pallas_call(
        wrapped, name=name, grid=grid,
        in_specs=list(in_specs) + [hbm] * nc, out_specs=list(out_specs) + [hbm] * nc,
        out_shape=list(out_shape) + _exchange_shapes(arrays, scatter),
        scratch_shapes=list(scratch_shapes) + _exchange_scratch(nc),
        compiler_params=_cparams(*(("arbitrary",) * len(grid))),
    )(*args, *arrays)
    return list(res[:n_out]), list(res[n_out:])


def _relu2_fwd(acc):
    r = jnp.maximum(acc, 0.0)
    return acc, r * r


def _relu2_bwd(acc, z):
    return (acc * 2.0 * jnp.maximum(z.astype(F32), 0.0),)


def _add_scaled(scale):
    def epi(acc, t):
        return (acc + scale * t,)
    return epi


def _local_step(x, target, P, plan):
    S, D = x.shape
    H = (D // 2) // HEAD_DIM
    W = H * HEAD_DIM
    alpha = (2 * 2) ** 0.25
    g = {}

    def mm(a, b, *, name, **kw):
        res, carried = _mm(a, b, name=name, carry=plan.carry(name, g), **kw)
        plan.arrived(name, carried, P, g)
        return res

    u0 = mm(x, P["ev_w_in"], mode="nn", name="ev_in", b_blocked=True)
    (a_out, cv), carried = _conv_fwd(u0, P["ev_conv_w"], P["ev_conv_b"], P["ev_conv_ln_g"], P["ev_conv_ln_b"], "conv_fwd",
                                     carry=plan.carry("conv_fwd", g))
    plan.arrived("conv_fwd", carried, P, g)
    b_out, h_raw, h_states = _hgrn_fwd(u0, P["hgrn_lb_logits"], P["ev_gnorm_g"], "hgrn_fwd")
    cat0 = (a_out, b_out)
    mix0 = mm(cat0, P["ev_w_out"], mode="nn", name="ev_out")
    x1, x1b, xh1, rs1 = _ln_fwd(x, mix0, P["ln_mix_g"][0:1], P["ln_mix_b"][0:1], alpha, "ln_mix0")
    z0, hh0 = mm(x1b, P["mlp_w1_0"], mode="nn", name="mlp_up0", out_dtypes=(BF16, BF16), epi=_relu2_fwd, b_blocked=True)
    m0 = mm(hh0, P["mlp_w2_0"], mode="nn", name="mlp_down0")
    x2, x2b, xh2, rs2 = _ln_fwd(x1, m0, P["ln_mlp_g"][0:1], P["ln_mlp_b"][0:1], alpha, "ln_mlp0")

    uc = mm(x2b, P["od_w_c"], mode="nn", name="od_in_c")
    ud = mm(x2b, P["od_w_d"], mode="nn", name="od_in_d")
    f_logit, F = _fgate_fwd(x2b, P["od_w_f_t"], P["fox_b_f"].reshape(H, 1), "fgate_fwd")
    F3 = F.reshape(H, 1, S)
    c_out, c_out_b, c_lse, carried = _fox_fwd(uc, F3, H, "fox_fwd", carry=plan.carry("fox_fwd", g))
    plan.arrived("fox_fwd", carried, P, g)
    bias = _relbias_fwd(P["rel_bias"], "relbias_fwd")
    d_out, d_out_b, d_lse = _ca_fwd(ud, bias, H, "ca_fwd")
    cat1 = (c_out_b, d_out_b)
    mix1 = mm(cat1, P["od_w_out"], mode="nn", name="od_out")
    x3, x3b, xh3, rs3 = _ln_fwd(x2, mix1, P["ln_mix_g"][1:2], P["ln_mix_b"][1:2], alpha, "ln_mix1")
    z1, hh1 = mm(x3b, P["mlp_w1_1"], mode="nn", name="mlp_up1", out_dtypes=(BF16, BF16), epi=_relu2_fwd, b_blocked=True)
    m1 = mm(hh1, P["mlp_w2_1"], mode="nn", name="mlp_down1")
    dy, xh4, rs4, loss = _ln_loss(x3, m1, P["ln_mlp_g"][1:2], P["ln_mlp_b"][1:2], alpha, target, "ln_mlp1_loss")

    dzm1, dzm1b, dg_, db_ = _ln_bwd(dy, xh4, rs4, P["ln_mlp_g"][1:2], "ln_mlp1_bwd")
    g["ln_mlp_g1"], g["ln_mlp_b1"] = dg_, db_
    g["mlp_w2_1"] = mm(hh1, dzm1b, mode="tn", name="mlp_down1_dw", out_dtypes=(BF16,))
    dz1 = mm(dzm1b, P["mlp_w2_1"], mode="nt", name="mlp_down1_dx", out_dtypes=(BF16,), extras=(z1,), epi=_relu2_bwd)
    g["mlp_w1_1"] = mm(x3b, dz1, mode="tn", name="mlp_up1_dw", out_dtypes=(BF16,), out_blocked=True)
    dx3 = mm(dz1, P["mlp_w1_1"], mode="nt", name="mlp_up1_dx", extras=(dzm1,), epi=_add_scaled(alpha), b_blocked=True)
    dzx1, dzx1b, dg_, db_ = _ln_bwd(dx3, xh3, rs3, P["ln_mix_g"][1:2], "ln_mix1_bwd")
    g["ln_mix_g1"], g["ln_mix_b1"] = dg_, db_
    g["od_w_out"] = mm(cat1, dzx1b, mode="tn", name="od_out_dw", out_dtypes=(BF16,))
    dcat1 = mm(dzx1b, P["od_w_out"], mode="nt", name="od_out_dx")
    dq_d, dk_d, dv_d, dbias = _ca_bwd(ud, bias, d_out, dcat1, H, d_lse, H, "ca_bwd")
    g["rel_bias"] = _relbias_bwd(dbias, "relbias_bwd")
    dud = jnp.concatenate([dq_d, dk_d, dv_d], axis=1)
    dq_c, dk_c, dv_c, dF3 = _fox_bwd(uc, F3, c_out, dcat1, 0, c_lse, H, "fox_bwd")
    duc = jnp.concatenate([dq_c, dk_c, dv_c], axis=1)
    dfl, dbf = _fgate_bwd(dF3.reshape(H, S), f_logit, "fgate_bwd")
    g["fox_b_f"] = dbf.reshape(1, H)
    g["od_w_c"] = mm(x2b, duc, mode="tn", name="od_in_c_dw", out_dtypes=(BF16,))
    g["od_w_d"] = mm(x2b, dud, mode="tn", name="od_in_d_dw", out_dtypes=(BF16,))
    g["od_w_f_t"] = mm(dfl, x2b, mode="nn", name="od_in_f_dw", out_dtypes=(BF16,), exact_products=True)
    dx2 = mm(duc, P["od_w_c"], mode="nt", name="od_in_c_dx", extras=(dzx1,), epi=_add_scaled(alpha))
    dx2 = mm(dud, P["od_w_d"], mode="nt", name="od_in_d_dx", extras=(dx2,), epi=_add_scaled(1.0))
    dx2 = mm(dfl, P["od_w_f_t"], mode="tn", name="od_in_f_dx", extras=(dx2,), epi=_add_scaled(1.0), exact_products=True)

    dzm0, dzm0b, dg_, db_ = _ln_bwd(dx2, xh2, rs2, P["ln_mlp_g"][0:1], "ln_mlp0_bwd")
    g["ln_mlp_g0"], g["ln_mlp_b0"] = dg_, db_
    g["mlp_w2_0"] = mm(hh0, dzm0b, mode="tn", name="mlp_down0_dw", out_dtypes=(BF16,))
    dz0 = mm(dzm0b, P["mlp_w2_0"], mode="nt", name="mlp_down0_dx", out_dtypes=(BF16,), extras=(z0,), epi=_relu2_bwd)
    g["mlp_w1_0"] = mm(x1b, dz0, mode="tn", name="mlp_up0_dw", out_dtypes=(BF16,), out_blocked=True)
    dx1 = mm(dz0, P["mlp_w1_0"], mode="nt", name="mlp_up0_dx", extras=(dzm0,), epi=_add_scaled(alpha), b_blocked=True)
    dzx0, dzx0b, dg_, db_ = _ln_bwd(dx1, xh1, rs1, P["ln_mix_g"][0:1], "ln_mix0_bwd")
    g["ln_mix_g0"], g["ln_mix_b0"] = dg_, db_
    g["ev_w_out"] = mm(cat0, dzx0b, mode="tn", name="ev_out_dw", out_dtypes=(BF16,))
    dcat0 = mm(dzx0b, P["ev_w_out"], mode="nt", name="ev_out_dx")
    dcv, g["ev_conv_ln_g"], g["ev_conv_ln_b"], g["ev_conv_b"] = _conv_bwd_norm(dcat0, 0, cv, P["ev_conv_ln_g"], P["ev_conv_ln_b"], "conv_bwd_norm")
    dag, g["ev_conv_w"] = _conv_bwd_taps(dcv, u0, P["ev_conv_w"], "conv_bwd_taps")
    du0, g["ev_gnorm_g"], g["hgrn_lb_logits"] = _hgrn_bwd(dcat0, 1, dag, u0, h_raw, h_states, P["hgrn_lb_logits"], P["ev_gnorm_g"], "hgrn_bwd")
    g["ev_w_in"] = mm(x, du0, mode="tn", name="ev_in_dw", out_dtypes=(BF16,), out_blocked=True)
    dx0 = mm(du0, P["ev_w_in"], mode="nt", name="ev_in_dx", extras=(dzx0,), epi=_add_scaled(alpha), b_blocked=True)
    return loss, dx0, g


_NAMES = ['ev_w_in', 'ev_conv_w', 'ev_conv_b', 'ev_conv_ln_g', 'ev_conv_ln_b', 'hgrn_lb_logits', 'ev_gnorm_g', 'ev_w_out',
          'od_w_in', 'fox_b_f', 'rel_bias', 'od_w_out', 'ln_mix_g', 'ln_mix_b', 'mlp_w1', 'mlp_w2', 'ln_mlp_g', 'ln_mlp_b']
_SMALL = ['ev_conv_b', 'ev_conv_ln_g', 'ev_conv_ln_b', 'hgrn_lb_logits', 'ev_gnorm_g', 'fox_b_f', 'ln_mix_g', 'ln_mix_b',
          'ln_mlp_g', 'ln_mlp_b', 'ev_conv_w', 'rel_bias']
_PACK_COLS = 2048


def _cols_to_full(gathered):
    nd, K, n = gathered.shape
    return jnp.transpose(gathered, (1, 0, 2)).reshape(K, nd * n)


def _full_to_cols(full):
    K, N = full.shape
    return jnp.transpose(full.reshape(K, N_DEV, N // N_DEV), (1, 0, 2))


_ROW_SHARDED = ("ev_w_out", "od_w_out", "mlp_w2_0", "mlp_w2_1")
_BLOCKED = ("ev_w_in", "mlp_w1_0", "mlp_w1_1")


def _full_weights(name, gathered, heads):
    if name in _BLOCKED:
        return {name: gathered}
    if name in _ROW_SHARDED:
        return {name: gathered.reshape(-1, gathered.shape[-1])}
    full = _cols_to_full(gathered)
    if name != "od_w_in":
        return {name: full}
    w = heads * HEAD_DIM
    return {"od_w_c": full[:, :3 * w], "od_w_f_t": jnp.transpose(full[:, 3 * w:3 * w + heads]), "od_w_d": full[:, 3 * w + heads:]}


def _grad_blocks(name, g):
    if name in _BLOCKED:
        return g[name]
    if name in _ROW_SHARDED:
        return g[name].reshape(N_DEV, -1, g[name].shape[-1])
    if name == "od_w_in":
        return _full_to_cols(jnp.concatenate([g["od_w_c"], jnp.transpose(g["od_w_f_t"]), g["od_w_d"]], axis=1))
    return _full_to_cols(g[name])


class _Plan:
    GATHER = {"ev_in": ("mlp_w1_0",), "conv_fwd": ("ev_w_out",), "mlp_up0": ("mlp_w2_0",), "mlp_down0": ("od_w_in",), "od_in_c": ("od_w_out",),
              "fox_fwd": ("mlp_w1_1", "mlp_w2_1")}
    SCATTER = {"mlp_down1_dx": "mlp_w2_1", "mlp_up1_dx": "mlp_w1_1", "od_out_dx": "od_w_out", "od_in_c_dx": "od_w_in",
               "mlp_down0_dx": "mlp_w2_0", "mlp_up0_dx": "mlp_w1_0", "ev_out_dx": "ev_w_out", "ev_in_dx": "ev_w_in"}

    def __init__(self, shards, heads):
        self.shards, self.heads = shards, heads

    def carry(self, call, g):
        if call in self.GATHER:
            return [self.shards[n] for n in self.GATHER[call]], False
        if call in self.SCATTER:
            return [_grad_blocks(self.SCATTER[call], g)], True
        return None

    def arrived(self, call, carried, P, g):
        if call in self.GATHER:
            for n, gathered in zip(self.GATHER[call], carried):
                P.update(_full_weights(n, gathered, self.heads))
        elif call in self.SCATTER:
            g[self.SCATTER[call]] = carried[0]


def _pack(parts):
    flat = jnp.concatenate([p.reshape(-1).astype(F32) for p in parts])
    rows = -(-flat.shape[0] // (_PACK_COLS * 8)) * 8
    return jnp.pad(flat, (0, rows * _PACK_COLS - flat.shape[0])).reshape(rows, _PACK_COLS)


def _unpack(packed, shapes):
    flat = packed.reshape(-1)
    out, off = [], 0
    for s in shapes:
        n = math.prod(s)
        out.append(flat[off:off + n].reshape(s))
        off += n
    return out


def kernel(x, ev_w_in, ev_conv_w, ev_conv_b, ev_conv_ln_g, ev_conv_ln_b, hgrn_lb_logits, ev_gnorm_g, ev_w_out, od_w_in, fox_b_f, rel_bias, od_w_out, ln_mix_g, ln_mix_b, mlp_w1, mlp_w2, ln_mlp_g, ln_mlp_b, loss_target, m_ev_w_in, m_ev_conv_w, m_ev_conv_b, m_ev_conv_ln_g, m_ev_conv_ln_b, m_hgrn_lb_logits, m_ev_gnorm_g, m_ev_w_out, m_od_w_in, m_fox_b_f, m_rel_bias, m_od_w_out, m_ln_mix_g, m_ln_mix_b, m_mlp_w1, m_mlp_w2, m_ln_mlp_g, m_ln_mlp_b, v_ev_w_in, v_ev_conv_w, v_ev_conv_b, v_ev_conv_ln_g, v_ev_conv_ln_b, v_hgrn_lb_logits, v_ev_gnorm_g, v_ev_w_out, v_od_w_in, v_fox_b_f, v_rel_bias, v_od_w_out, v_ln_mix_g, v_ln_mix_b, v_mlp_w1, v_mlp_w2, v_ln_mlp_g, v_ln_mlp_b):
    args = locals()
    w = {n: args[n] for n in _NAMES}
    m = {n: args["m_" + n] for n in _NAMES}
    v = {n: args["v_" + n] for n in _NAMES}
    me = 4 * lax.axis_index("x") + 2 * lax.axis_index("y") + lax.axis_index("c")
    S, D = x.shape[1], x.shape[2]
    H = (D // 2) // HEAD_DIM
    W = H * HEAD_DIM
    n_layers = mlp_w1.shape[0]
    assert n_layers == 2 and ev_w_in.shape[0] == 1 and od_w_in.shape[0] == 1

    shards = {"ev_w_in": ev_w_in[0].astype(BF16), "ev_w_out": ev_w_out[0].astype(BF16),
              "od_w_in": od_w_in[0].astype(BF16), "od_w_out": od_w_out[0].astype(BF16)}
    for l in range(n_layers):
        shards["mlp_w1_%d" % l] = mlp_w1[l].astype(BF16)
        shards["mlp_w2_%d" % l] = mlp_w2[l].astype(BF16)
    first = ["ev_w_in", "ev_conv_w", "rel_bias"]
    G = _exchange([shards["ev_w_in"], ev_conv_w[0], rel_bias[0]], False, "gather_first")
    P = {
        "ev_conv_b": ev_conv_b, "ev_conv_ln_g": ev_conv_ln_g, "ev_conv_ln_b": ev_conv_ln_b,
        "hgrn_lb_logits": hgrn_lb_logits, "ev_gnorm_g": ev_gnorm_g, "fox_b_f": fox_b_f,
        "ln_mix_g": ln_mix_g, "ln_mix_b": ln_mix_b, "ln_mlp_g": ln_mlp_g, "ln_mlp_b": ln_mlp_b,
    }
    for name, gathered in zip(first, G):
        P.update(_full_weights(name, gathered, H))

    loss, grad_x, g = _local_step(x[0], loss_target[0], P, _Plan(shards, H))
    recv = [g["ev_w_in"], g["ev_w_out"], g["od_w_in"], g["od_w_out"]]
    recv += [g["mlp_w1_%d" % l] for l in range(n_layers)] + [g["mlp_w2_%d" % l] for l in range(n_layers)]

    small = {
        "ev_conv_b": g["ev_conv_b"], "ev_conv_ln_g": g["ev_conv_ln_g"], "ev_conv_ln_b": g["ev_conv_ln_b"],
        "hgrn_lb_logits": g["hgrn_lb_logits"], "ev_gnorm_g": g["ev_gnorm_g"], "fox_b_f": g["fox_b_f"],
        "ln_mix_g": jnp.concatenate([g["ln_mix_g0"], g["ln_mix_g1"]]), "ln_mix_b": jnp.concatenate([g["ln_mix_b0"], g["ln_mix_b1"]]),
        "ln_mlp_g": jnp.concatenate([g["ln_mlp_g0"], g["ln_mlp_g1"]]), "ln_mlp_b": jnp.concatenate([g["ln_mlp_b0"], g["ln_mlp_b1"]]),
        "ev_conv_w": g["ev_conv_w"], "rel_bias": g["rel_bias"],
    }
    full_shapes = [small[n].shape for n in _SMALL]
    small_all = _exchange([_pack([small[n] for n in _SMALL])], False, "gather_small_grads")[0]
    small_sum = _unpack(_sum_parts(small_all, "sum_small_grads"), full_shapes)
    small_g = dict(zip(_SMALL, small_sum))
    cw = small_g["ev_conv_w"]
    small_g["ev_conv_w"] = lax.dynamic_slice_in_dim(cw, me * (cw.shape[1] // N_DEV), cw.shape[1] // N_DEV, axis=1)
    rb = small_g["rel_bias"]
    small_g["rel_bias"] = lax.dynamic_slice_in_dim(rb, me * (rb.shape[1] // N_DEV), rb.shape[1] // N_DEV, axis=1)

    out_g, out_d, out_m, out_v = {}, {}, {}, {}
    big = [("ev_w_in", recv[0:1]), ("ev_w_out", recv[1:2]), ("od_w_in", recv[2:3]), ("od_w_out", recv[3:4]),
           ("mlp_w1", recv[4:4 + n_layers]), ("mlp_w2", recv[4 + n_layers:4 + 2 * n_layers])]
    for name, parts in big:
        out_g[name], out_d[name], out_m[name], out_v[name] = _adamw(parts, w[name], m[name], v[name], "adamw_" + name)
    shapes = [w[n].shape for n in _SMALL]
    packed = _adamw([_pack([small_g[n] for n in _SMALL])[None]], _pack([w[n] for n in _SMALL])[None], _pack([m[n] for n in _SMALL])[None],
                    _pack([v[n] for n in _SMALL])[None], "adamw_small")
    for k, dst in enumerate((out_g, out_d, out_m, out_v)):
        for n, a in zip(_SMALL, _unpack(packed[k], shapes)):
            dst[n] = a

    loss = lax.psum(loss[0, 0], ("x", "y", "c"))
    return (loss, grad_x[None], *[out_g[n] for n in _NAMES], *[out_d[n] for n in _NAMES],
            *[out_m[n] for n in _NAMES], *[out_v[n] for n in _NAMES])
```

```python
import functools
import math

import jax
import jax.numpy as jnp
from jax import lax
from jax.experimental import pallas as pl
from jax.experimental.pallas import tpu as pltpu

F32 = jnp.float32
BF16 = jnp.bfloat16
HI = lax.Precision.HIGHEST
MESH_ID = pl.DeviceIdType.MESH

N_DEV = 8
LN_EPS = 1e-5
CHUNK = 64
HEAD_DIM = 128
CONV_WIDTH = 31
CONV_HALO = 32
CA_LEFT_CHUNKS = 8
CA_TILE = 256
CA_WIN = CA_TILE + CA_LEFT_CHUNKS * CHUNK
CA_SKEW = 1024
REL_CLIP = 256
REL_TABLE = (CHUNK - 1) + REL_CLIP + 1
NEG = -1e30
ADAM_LR = 0.001
ADAM_B1 = 0.9
ADAM_B2 = 0.999
ADAM_EPS = 1e-08
ADAM_WD = 0.01
ADAM_STEP = 10
VMEM_LIMIT_V7X = 56 * 1024 * 1024


def _cparams(*sem):
    return pltpu.CompilerParams(dimension_semantics=sem, vmem_limit_bytes=VMEM_LIMIT_V7X)


def _tile(n, t):
    if n <= t:
        return n
    for c in range(t - t % 128, 0, -128):
        if n % c == 0:
            return c
    return n


def _sigmoid(x):
    return 1.0 / (1.0 + jnp.exp(-x))


def _dot(a, b, dims, precision=None):
    return lax.dot_general(a, b, (dims, ((), ())), preferred_element_type=F32, precision=precision)


def _round_bf16(x):
    return x.astype(BF16).astype(F32)


NN = ((1,), (0,))
NT = ((1,), (1,))
TN = ((0,), (0,))


def _select(idx, loads):
    if len(loads) == 1:
        return loads[0]()
    mid = len(loads) // 2
    return lax.cond(idx < mid, lambda: _select(idx, loads[:mid]), lambda: _select(idx - mid, loads[mid:]))


def _mm(a, b, *, mode, name, out_dtypes=(F32,), extras=(), epi=None, exact_products=False, carry=None,
        b_blocked=False, out_blocked=False, tm=1024, tn=1024, tk=2048):
    dims = {"nn": NN, "nt": NT, "tn": TN}[mode]
    a_parts = a if isinstance(a, tuple) else (a,)
    n_a = len(a_parts)
    a_split_k = n_a > 1 and mode != "tn"
    a_split_m = n_a > 1 and mode == "tn"
    if mode == "tn":
        K, M = a_parts[0].shape[0], a_parts[0].shape[1] * n_a
    else:
        M, K = a_parts[0].shape[0], a_parts[0].shape[1] * n_a
    if b_blocked:
        nb, _, shard = b.shape
        N = b.shape[1] if mode == "nt" else nb * shard
    else:
        N = b.shape[0] if mode == "nt" else b.shape[1]
    tm, tn, tk = _tile(M, tm), _tile(N, tn), _tile(K, tk)
    if a_split_k:
        tk = K // n_a
    if a_split_m:
        tm = M // n_a
    per_step = 1
    if b_blocked and mode == "nt":
        per_step = max(s for s in range(1, nb + 1) if nb % s == 0 and s * shard <= max(tk, shard))
        tk = per_step * shard
    if b_blocked and mode == "nn":
        tn = shard
    if out_blocked:
        tn = N // N_DEV
    nk = K // tk
    n_ex, n_out = len(extras), len(out_dtypes)

    def body(*refs):
        a_refs, b_ref = refs[:n_a], refs[n_a]
        ex_refs = refs[n_a + 1:n_a + 1 + n_ex]
        o_refs = refs[n_a + 1 + n_ex:n_a + 1 + n_ex + n_out]
        cast = _round_bf16 if exact_products else (lambda t: t.astype(BF16))
        part = pl.program_id(0) if a_split_m else pl.program_id(2)
        av = _select(part, [functools.partial(lambda r: cast(r[...]), r) for r in a_refs])
        precision = HI if exact_products else None
        if per_step > 1:
            d = sum(_dot(av[:, q * shard:(q + 1) * shard], cast(b_ref[q]), dims, precision) for q in range(per_step))
        else:
            d = _dot(av, cast(b_ref[...]), dims, precision)

        def finish(acc):
            outs = (acc,) if epi is None else epi(acc, *[r[...] for r in ex_refs])
            for o_ref, o in zip(o_refs, outs):
                o_ref[...] = o.astype(o_ref.dtype)

        if nk == 1:
            finish(d)
        else:
            acc_ref = refs[-1]
            k = pl.program_id(2)

            @pl.when(k == 0)
            def _():
                acc_ref[...] = d

            @pl.when(k > 0)
            def _():
                acc_ref[...] += d

            @pl.when(k == nk - 1)
            def _():
                finish(acc_ref[...])

    if a_split_k:
        a_specs = [pl.BlockSpec((tm, tk), lambda i, j, k: (i, 0))] * n_a
    elif a_split_m:
        a_specs = [pl.BlockSpec((tk, tm), functools.partial(lambda p, i, j, k: (jnp.where(i == p, k, 0), 0), p)) for p in range(n_a)]
    elif mode == "tn":
        a_specs = [pl.BlockSpec((tk, tm), lambda i, j, k: (k, i))]
    else:
        a_specs = [pl.BlockSpec((tm, tk), lambda i, j, k: (i, k))]
    if b_blocked and mode == "nt" and per_step > 1:
        b_spec = pl.BlockSpec((per_step, tn, shard), lambda i, j, k: (k, j, 0))
    elif b_blocked:
        b_spec = pl.BlockSpec((None, tn, tk), lambda i, j, k: (k, j, 0)) if mode == "nt" else pl.BlockSpec((None, tk, tn), lambda i, j, k: (j, k, 0))
    else:
        b_spec = pl.BlockSpec((tn, tk), lambda i, j, k: (j, k)) if mode == "nt" else pl.BlockSpec((tk, tn), lambda i, j, k: (k, j))
    mn_spec = pl.BlockSpec((tm, tn), lambda i, j, k: (i, j))
    if out_blocked:
        out_specs = [pl.BlockSpec((None, tm, tn), lambda i, j, k: (j, i, 0))] * n_out
        out_shape = [jax.ShapeDtypeStruct((N_DEV, M, tn), dt) for dt in out_dtypes]
    else:
        out_specs = [mn_spec] * n_out
        out_shape = [jax.ShapeDtypeStruct((M, N), dt) for dt in out_dtypes]
    outs, carried = _call(
        body, name=name, grid=(M // tm, N // tn, nk),
        in_specs=a_specs + [b_spec] + [mn_spec] * n_ex,
        out_specs=out_specs, out_shape=out_shape,
        scratch_shapes=[pltpu.VMEM((tm, tn), F32)] if nk > 1 else [],
        sem=("parallel", "parallel", "arbitrary"), args=(*a_parts, b, *extras), carry=carry)
    return (outs[0] if n_out == 1 else outs), carried


def _ln_fwd(x, r, g, b, alpha, name, tr=256):
    S, D = x.shape
    tr = _tile(S, tr)

    def body(x_ref, r_ref, g_ref, b_ref, y_ref, yb_ref, xh_ref, rs_ref):
        z = alpha * x_ref[...] + r_ref[...]
        zc = z - jnp.mean(z, axis=-1, keepdims=True)
        rs = lax.rsqrt(jnp.mean(zc * zc, axis=-1, keepdims=True) + LN_EPS)
        xh = zc * rs
        xh_ref[...] = xh
        rs_ref[...] = rs
        y = xh * g_ref[...] + b_ref[...]
        y_ref[...] = y
        yb_ref[...] = y.astype(BF16)

    row = pl.BlockSpec((tr, D), lambda i: (i, 0))
    par = pl.BlockSpec((1, D), lambda i: (0, 0))
    return pl.pallas_call(
        body, name=name, grid=(S // tr,),
        in_specs=[row, row, par, par],
        out_specs=[row, row, row, pl.BlockSpec((tr, 1), lambda i: (i, 0))],
        out_shape=[jax.ShapeDtypeStruct((S, D), F32), jax.ShapeDtypeStruct((S, D), BF16), jax.ShapeDtypeStruct((S, D), F32),
                   jax.ShapeDtypeStruct((S, 1), F32)],
        compiler_params=_cparams("parallel"),
    )(x, r, g, b)


def _ln_loss(x, r, g, b, alpha, target, name, tr=256):
    S, D = x.shape
    tr = _tile(S, tr)

    def body(x_ref, r_ref, g_ref, b_ref, t_ref, dy_ref, xh_ref, rs_ref, loss_ref):
        z = alpha * x_ref[...] + r_ref[...]
        zc = z - jnp.mean(z, axis=-1, keepdims=True)
        rs = lax.rsqrt(jnp.mean(zc * zc, axis=-1, keepdims=True) + LN_EPS)
        xh = zc * rs
        xh_ref[...] = xh
        rs_ref[...] = rs
        e = (xh * g_ref[...] + b_ref[...]) - t_ref[...]
        dy_ref[...] = e * (1.0 / D)

        @pl.when(pl.program_id(0) == 0)
        def _():
            loss_ref[...] = jnp.zeros_like(loss_ref)

        loss_ref[...] += jnp.sum(jnp.sum(e * e, axis=-1, keepdims=True), axis=0, keepdims=True) * (0.5 / D)

    row = pl.BlockSpec((tr, D), lambda i: (i, 0))
    par = pl.BlockSpec((1, D), lambda i: (0, 0))
    return pl.pallas_call(
        body, name=name, grid=(S // tr,),
        in_specs=[row, row, par, par, row],
        out_specs=[row, row, pl.BlockSpec((tr, 1), lambda i: (i, 0)), pl.BlockSpec((1, 1), lambda i: (0, 0))],
        out_shape=[jax.ShapeDtypeStruct((S, D), F32), jax.ShapeDtypeStruct((S, D), F32), jax.ShapeDtypeStruct((S, 1), F32),
                   jax.ShapeDtypeStruct((1, 1), F32)],
        compiler_params=_cparams("arbitrary"),
    )(x, r, g, b, target)


def _ln_bwd(dy, xh, rs, g, name, tr=256):
    S, D = dy.shape
    tr = _tile(S, tr)

    def body(dy_ref, xh_ref, rs_ref, g_ref, dz_ref, dzb_ref, dg_ref, db_ref):
        dyt = dy_ref[...]
        xh = xh_ref[...]
        dxh = dyt * g_ref[...]
        m1 = jnp.mean(dxh, axis=-1, keepdims=True)
        m2 = jnp.mean(dxh * xh, axis=-1, keepdims=True)
        dz = rs_ref[...] * (dxh - m1 - xh * m2)
        dz_ref[...] = dz
        dzb_ref[...] = dz.astype(BF16)

        @pl.when(pl.program_id(0) == 0)
        def _():
            dg_ref[...] = jnp.zeros_like(dg_ref)
            db_ref[...] = jnp.zeros_like(db_ref)

        dg_ref[...] += jnp.sum(dyt * xh, axis=0, keepdims=True)
        db_ref[...] += jnp.sum(dyt, axis=0, keepdims=True)

    row = pl.BlockSpec((tr, D), lambda i: (i, 0))
    par = pl.BlockSpec((1, D), lambda i: (0, 0))
    return pl.pallas_call(
        body, name=name, grid=(S // tr,),
        in_specs=[row, row, pl.BlockSpec((tr, 1), lambda i: (i, 0)), par],
        out_specs=[row, row, par, par],
        out_shape=[jax.ShapeDtypeStruct((S, D), F32), jax.ShapeDtypeStruct((S, D), BF16),
                   jax.ShapeDtypeStruct((1, D), F32), jax.ShapeDtypeStruct((1, D), F32)],
        compiler_params=_cparams("arbitrary"),
    )(dy, xh, rs, g)


SUBLANES = 8


def _tap_reads(ext, shifted, tt, offset_of_tap):
    for r in range(SUBLANES):
        taps = [k for k in range(CONV_WIDTH) if offset_of_tap(k) % SUBLANES == r]
        if r == 0:
            src = ext
        else:
            shifted[...] = ext[pl.ds(r, tt + CONV_HALO - SUBLANES), :]
            src = shifted
        for k in taps:
            yield src, k, offset_of_tap(k) - r


def _conv_fwd(u, w, cb, lg, lb, name, carry=None, tt=512):
    S = u.shape[0]
    C = w.shape[1]
    tt = _tile(S, tt)
    hpt = tt // CONV_HALO

    def body(a_ref, g_ref, ap_ref, gp_ref, w_ref, cb_ref, lg_ref, lb_ref, out_ref, cv_ref, hext, shifted):
        i = pl.program_id(0)
        hext[pl.ds(CONV_HALO, tt), :] = a_ref[...] * _sigmoid(g_ref[...])
        hp = ap_ref[...] * _sigmoid(gp_ref[...])
        hext[pl.ds(0, CONV_HALO), :] = jnp.where(i > 0, hp, 0.0)
        acc = jnp.zeros((tt, C), F32)
        for src, k, row0 in _tap_reads(hext, shifted, tt, lambda k: CONV_HALO - (CONV_WIDTH - 1) + k):
            acc = acc + w_ref[pl.ds(k, 1), :] * src[pl.ds(row0, tt), :]
        cv = acc + cb_ref[...]
        cv_ref[...] = cv
        zc = cv - jnp.mean(cv, axis=-1, keepdims=True)
        n = zc * lax.rsqrt(jnp.mean(zc * zc, axis=-1, keepdims=True) + LN_EPS) * lg_ref[...] + lb_ref[...]
        out_ref[...] = (n * _sigmoid(n)).astype(BF16)

    cur = lambda cb_: pl.BlockSpec((tt, C), lambda i: (i, cb_))
    prev = lambda cb_: pl.BlockSpec((CONV_HALO, C), lambda i: (jnp.maximum(i * hpt - 1, 0), cb_))
    par = pl.BlockSpec((1, C), lambda i: (0, 0))
    row = pl.BlockSpec((tt, C), lambda i: (i, 0))
    return _call(
        body, name=name, grid=(S // tt,),
        in_specs=[cur(0), cur(1), prev(0), prev(1), pl.BlockSpec((CONV_WIDTH, C), lambda i: (0, 0)), par, par, par],
        out_specs=[row, row],
        out_shape=[jax.ShapeDtypeStruct((S, C), BF16), jax.ShapeDtypeStruct((S, C), F32)],
        scratch_shapes=[pltpu.VMEM((tt + CONV_HALO, C), F32), pltpu.VMEM((tt + CONV_HALO - SUBLANES, C), F32)],
        sem=("parallel",), args=(u, u, u, u, w, cb, lg, lb), carry=carry)


def _conv_bwd_norm(da, da_col, cv, lg, lb, name, tt=512):
    S, C = cv.shape
    tt = _tile(S, tt)

    def body(da_ref, cv_ref, lg_ref, lb_ref, dcv_ref, dlg_ref, dlb_ref, dcb_ref):
        cv = cv_ref[...]
        zc = cv - jnp.mean(cv, axis=-1, keepdims=True)
        rs = lax.rsqrt(jnp.mean(zc * zc, axis=-1, keepdims=True) + LN_EPS)
        xh = zc * rs
        n = xh * lg_ref[...] + lb_ref[...]
        sg = _sigmoid(n)
        dn = da_ref[...] * sg * (1.0 + n * (1.0 - sg))
        dxh = dn * lg_ref[...]
        m1 = jnp.mean(dxh, axis=-1, keepdims=True)
        m2 = jnp.mean(dxh * xh, axis=-1, keepdims=True)
        dcv = rs * (dxh - m1 - xh * m2)
        dcv_ref[...] = dcv

        @pl.when(pl.program_id(0) == 0)
        def _():
            dlg_ref[...] = jnp.zeros_like(dlg_ref)
            dlb_ref[...] = jnp.zeros_like(dlb_ref)
            dcb_ref[...] = jnp.zeros_like(dcb_ref)

        dlg_ref[...] += jnp.sum(dn * xh, axis=0, keepdims=True)
        dlb_ref[...] += jnp.sum(dn, axis=0, keepdims=True)
        dcb_ref[...] += jnp.sum(dcv, axis=0, keepdims=True)

    row = pl.BlockSpec((tt, C), lambda i: (i, 0))
    par = pl.BlockSpec((1, C), lambda i: (0, 0))
    return pl.pallas_call(
        body, name=name, grid=(S // tt,),
        in_specs=[pl.BlockSpec((tt, C), lambda i: (i, da_col)), row, par, par],
        out_specs=[row, par, par, par],
        out_shape=[jax.ShapeDtypeStruct((S, C), F32)] + [jax.ShapeDtypeStruct((1, C), F32)] * 3,
        compiler_params=_cparams("arbitrary"),
    )(da, cv, lg, lb)


def _conv_bwd_taps(dcv, u, w, name, tt=512):
    S, C = dcv.shape
    tt = _tile(S, tt)
    hpt = tt // CONV_HALO
    nt = S // tt
    WPAD = 32

    def body(dc_ref, dn_ref, a_ref, g_ref, ap_ref, gp_ref, w_ref, dag_ref, dw_ref, hext, dext, shifted):
        i = pl.program_id(0)
        a = a_ref[...]
        sg = _sigmoid(g_ref[...])
        hext[pl.ds(CONV_HALO, tt), :] = a * sg
        hp = ap_ref[...] * _sigmoid(gp_ref[...])
        hext[pl.ds(0, CONV_HALO), :] = jnp.where(i > 0, hp, 0.0)
        dc = dc_ref[...]
        dext[pl.ds(0, tt), :] = dc
        dext[pl.ds(tt, CONV_HALO), :] = jnp.where(i < nt - 1, dn_ref[...], 0.0)

        @pl.when(i == 0)
        def _():
            dw_ref[...] = jnp.zeros_like(dw_ref)

        dh = jnp.zeros((tt, C), F32)
        for src, k, row0 in _tap_reads(dext, shifted, tt, lambda k: CONV_WIDTH - 1 - k):
            dh = dh + w_ref[pl.ds(k, 1), :] * src[pl.ds(row0, tt), :]
        for src, k, row0 in _tap_reads(hext, shifted, tt, lambda k: CONV_HALO - (CONV_WIDTH - 1) + k):
            dw_ref[pl.ds(k, 1), :] += jnp.sum(dc * src[pl.ds(row0, tt), :], axis=0, keepdims=True)
        dag_ref[:, :C] = dh * sg
        dag_ref[:, C:] = dh * a * sg * (1.0 - sg)

    row = pl.BlockSpec((tt, C), lambda i: (i, 0))
    nxt = pl.BlockSpec((CONV_HALO, C), lambda i: (jnp.minimum((i + 1) * hpt, S // CONV_HALO - 1), 0))
    cur = lambda cb_: pl.BlockSpec((tt, C), lambda i: (i, cb_))
    prev = lambda cb_: pl.BlockSpec((CONV_HALO, C), lambda i: (jnp.maximum(i * hpt - 1, 0), cb_))
    dag, dw = pl.pallas_call(
        body, name=name, grid=(nt,),
        in_specs=[row, nxt, cur(0), cur(1), prev(0), prev(1), pl.BlockSpec((CONV_WIDTH, C), lambda i: (0, 0))],
        out_specs=[pl.BlockSpec((tt, 2 * C), lambda i: (i, 0)), pl.BlockSpec((WPAD, C), lambda i: (0, 0))],
        out_shape=[jax.ShapeDtypeStruct((S, 2 * C), F32), jax.ShapeDtypeStruct((WPAD, C), F32)],
        scratch_shapes=[pltpu.VMEM((tt + CONV_HALO, C), F32)] * 2 + [pltpu.VMEM((tt + CONV_HALO - SUBLANES, C), F32)],
        compiler_params=_cparams("arbitrary"),
    )(dcv, dcv, u, u, u, u, w)
    return dag, dw[:CONV_WIDTH]


def _lower_bound(logits):
    e = jnp.exp(logits - jnp.max(logits, axis=0, keepdims=True))
    p = e / jnp.sum(e, axis=0, keepdims=True)
    return p[0:1, :], p


def _tri(n, lower):
    r = lax.broadcasted_iota(jnp.int32, (n, n), 0)
    c = lax.broadcasted_iota(jnp.int32, (n, n), 1)
    return ((c <= r) if lower else (c >= r)).astype(F32)


def _hgrn_gates(xq, xf, lb):
    sq = _sigmoid(xq)
    q = xq * sq
    sf = _sigmoid(xf)
    f = lb + (1.0 - lb) * sf
    logf = jnp.log(f)
    L = _dot(_tri(CHUNK, True), logf, NN, HI)
    Lend = L[CHUNK - 1:CHUNK, :]
    eL = jnp.exp(L)
    enL = jnp.exp(-L)
    eLe = jnp.exp(Lend - L)
    kk = 1.0 - f
    return dict(sq=sq, q=q, sf=sf, f=f, L=L, Lend=Lend, eL=eL, enL=enL, eLe=eLe, kk=kk,
                qe=q * eL, ke=kk * enL, kd=kk * eLe)


def _hgrn_fwd(u, lbl, gn, name, tb=256):
    S = u.shape[0]
    W = gn.shape[1]
    H = W // HEAD_DIM
    tb = _tile(S, tb)
    cpb = tb // CHUNK
    nc = S // CHUNK
    R = lbl.shape[0]

    def body(q_ref, f_ref, i_ref, g_ref, lbl_ref, gn_ref, out_ref, raw_ref, st_ref, state):
        @pl.when(pl.program_id(0) == 0)
        def _():
            state[...] = jnp.zeros_like(state)

        lb, _ = _lower_bound(lbl_ref[...])
        tril = _tri(CHUNK, True) > 0.5

        def chunk(c, carry):
            rows = pl.ds(pl.multiple_of(c * CHUNK, CHUNK), CHUNK)
            G = _hgrn_gates(q_ref[rows, :], f_ref[rows, :], lb)
            v = i_ref[rows, :]
            xg = g_ref[rows, :]
            outs = []
            for h in range(H):
                ln = slice(h * HEAD_DIM, (h + 1) * HEAD_DIM)
                qe, ke, kd, vh = G["qe"][:, ln].astype(BF16), G["ke"][:, ln].astype(BF16), G["kd"][:, ln].astype(BF16), v[:, ln].astype(BF16)
                st = state[h]
                st_ref[c, h] = st
                A = jnp.where(tril, _dot(qe, ke, NT), 0.0)
                o = _dot(A.astype(BF16), vh, NN) + _dot(qe, st.astype(BF16), NT)
                state[h] = jnp.exp(G["Lend"][:, ln]) * st + _dot(vh, kd, TN)
                outs.append(o)
            o = jnp.concatenate(outs, axis=1)
            raw_ref[rows, :] = o
            ns = []
            for h in range(H):
                oh = outs[h]
                ns.append(oh * lax.rsqrt(jnp.mean(oh * oh, axis=-1, keepdims=True) + LN_EPS))
            n = jnp.concatenate(ns, axis=1)
            out_ref[rows, :] = (n * gn_ref[...] * (xg * _sigmoid(xg))).astype(BF16)
            return carry

        lax.fori_loop(0, cpb, chunk, 0)

    col = lambda cb_: pl.BlockSpec((tb, W), lambda i: (i, cb_))
    row = pl.BlockSpec((tb, W), lambda i: (i, 0))
    return pl.pallas_call(
        body, name=name, grid=(S // tb,),
        in_specs=[col(2), col(3), col(4), col(5), pl.BlockSpec((R, W), lambda i: (0, 0)), pl.BlockSpec((1, W), lambda i: (0, 0))],
        out_specs=[row, row, pl.BlockSpec((cpb, H, HEAD_DIM, HEAD_DIM), lambda i: (i, 0, 0, 0))],
        out_shape=[jax.ShapeDtypeStruct((S, W), BF16), jax.ShapeDtypeStruct((S, W), F32),
                   jax.ShapeDtypeStruct((nc, H, HEAD_DIM, HEAD_DIM), F32)],
        scratch_shapes=[pltpu.VMEM((H, HEAD_DIM, HEAD_DIM), F32)],
        compiler_params=_cparams("arbitrary"),
    )(u, u, u, u, lbl, gn)


def _hgrn_bwd(dout, dout_col, dag, u, raw, states, lbl, gn, name, tb=256):
    S = u.shape[0]
    W = gn.shape[1]
    H = W // HEAD_DIM
    tb = _tile(S, tb)
    cpb = tb // CHUNK
    nb = S // tb
    R = lbl.shape[0]

    def body(do_ref, dag_ref, q_ref, f_ref, i_ref, g_ref, raw_ref, st_ref, lbl_ref, gn_ref,
             du_ref, dgn_ref, dlbl_ref, dstate, dlb_acc):
        @pl.when(pl.program_id(0) == 0)
        def _():
            dstate[...] = jnp.zeros_like(dstate)
            dlb_acc[...] = jnp.zeros_like(dlb_acc)
            dgn_ref[...] = jnp.zeros_like(dgn_ref)

        du_ref[:, pl.ds(0, 2 * W)] = dag_ref[...]
        dq_cols, df_cols, di_cols, dg_cols = (pl.ds(c * W, W) for c in (2, 3, 4, 5))

        lb, p = _lower_bound(lbl_ref[...])
        tril = _tri(CHUNK, True) > 0.5
        triu = _tri(CHUNK, False)
        gn_row = gn_ref[...]

        def chunk(cc, carry):
            c = cpb - 1 - cc
            rows = pl.ds(pl.multiple_of(c * CHUNK, CHUNK), CHUNK)
            xq, xf = q_ref[rows, :], f_ref[rows, :]
            G = _hgrn_gates(xq, xf, lb)
            v = i_ref[rows, :]
            xg = g_ref[rows, :]
            dy = do_ref[rows, :]
            o = raw_ref[rows, :]
            sgg = _sigmoid(xg)
            silu_g = xg * sgg
            do_parts, n_parts = [], []
            for h in range(H):
                ln = slice(h * HEAD_DIM, (h + 1) * HEAD_DIM)
                oh = o[:, ln]
                r = lax.rsqrt(jnp.mean(oh * oh, axis=-1, keepdims=True) + LN_EPS)
                nh = oh * r
                dn = dy[:, ln] * gn_row[:, ln] * silu_g[:, ln]
                do_parts.append(r * (dn - nh * jnp.mean(dn * nh, axis=-1, keepdims=True)))
                n_parts.append(nh)
            n = jnp.concatenate(n_parts, axis=1)
            dgn_ref[...] += jnp.sum(dy * n * silu_g, axis=0, keepdims=True)
            du_ref[rows, dg_cols] = dy * n * gn_row * sgg * (1.0 + xg * (1.0 - sgg))
            dqe_p, dke_p, dkd_p, dv_p, dle_p = [], [], [], [], []
            for h in range(H):
                ln = slice(h * HEAD_DIM, (h + 1) * HEAD_DIM)
                qe, ke, kd, vh = G["qe"][:, ln].astype(BF16), G["ke"][:, ln].astype(BF16), G["kd"][:, ln].astype(BF16), v[:, ln].astype(BF16)
                doh = do_parts[h].astype(BF16)
                st = st_ref[c, h]
                dst = dstate[h]
                dec = jnp.exp(G["Lend"][:, ln])
                A = jnp.where(tril, _dot(qe, ke, NT), 0.0).astype(BF16)
                dA = jnp.where(tril, _dot(doh, vh, NT), 0.0).astype(BF16)
                dv_p.append(_dot(A, doh, TN) + _dot(kd, dst.astype(BF16), NT))
                dqe_p.append(_dot(dA, ke, NN) + _dot(doh, st.astype(BF16), NN))
                dke_p.append(_dot(dA, qe, TN))
                dkd_p.append(_dot(vh, dst.astype(BF16), NN))
                dle_p.append(jnp.sum(dst * dec * st, axis=0, keepdims=True))
                dstate[h] = dec * dst + _dot(doh, qe, TN)
            dqe = jnp.concatenate(dqe_p, axis=1)
            dke = jnp.concatenate(dke_p, axis=1)
            dkd = jnp.concatenate(dkd_p, axis=1)
            qe_r, ke_r, kd_r = _round_bf16(G["qe"]), _round_bf16(G["ke"]), _round_bf16(G["kd"])
            dLend = jnp.concatenate(dle_p, axis=1) + jnp.sum(dkd * kd_r, axis=0, keepdims=True)
            dL = dqe * qe_r - dke * ke_r - dkd * kd_r
            dlogf = _dot(triu, dL, NN, HI) + dLend
            dkk = dke * G["enL"] + dkd * G["eLe"]
            dfv = dlogf / G["f"] - dkk
            sf = G["sf"]
            du_ref[rows, df_cols] = dfv * (1.0 - lb) * sf * (1.0 - sf)
            dlb_acc[...] += jnp.sum(dfv * (1.0 - sf), axis=0, keepdims=True)
            sq = G["sq"]
            du_ref[rows, dq_cols] = dqe * G["eL"] * sq * (1.0 + xq * (1.0 - sq))
            du_ref[rows, di_cols] = jnp.concatenate(dv_p, axis=1)
            return carry

        lax.fori_loop(0, cpb, chunk, 0)

        onehot0 = (lax.broadcasted_iota(jnp.int32, (R, W), 0) == 0).astype(F32)
        dlbl_ref[...] = p * (onehot0 - p[0:1, :]) * dlb_acc[...]

    rev = lambda i: nb - 1 - i
    col = lambda cb_: pl.BlockSpec((tb, W), lambda i: (rev(i), cb_))
    row = pl.BlockSpec((tb, W), lambda i: (rev(i), 0))
    par = pl.BlockSpec((1, W), lambda i: (0, 0))
    parR = pl.BlockSpec((R, W), lambda i: (0, 0))
    return pl.pallas_call(
        body, name=name, grid=(nb,),
        in_specs=[pl.BlockSpec((tb, W), lambda i: (rev(i), dout_col)), pl.BlockSpec((tb, 2 * W), lambda i: (rev(i), 0)),
                  col(2), col(3), col(4), col(5), row,
                  pl.BlockSpec((cpb, H, HEAD_DIM, HEAD_DIM), lambda i: (rev(i), 0, 0, 0)), parR, par],
        out_specs=[pl.BlockSpec((tb, 6 * W), lambda i: (rev(i), 0)), par, parR],
        out_shape=[jax.ShapeDtypeStruct((S, 6 * W), F32), jax.ShapeDtypeStruct((1, W), F32), jax.ShapeDtypeStruct((R, W), F32)],
        scratch_shapes=[pltpu.VMEM((H, HEAD_DIM, HEAD_DIM), F32), pltpu.VMEM((1, W), F32)],
        compiler_params=_cparams("arbitrary"),
    )(dout, dag, u, u, u, u, raw, states, lbl, gn)


def _fgate_fwd(xb, wft, bf, name, ts=512):
    S, D = xb.shape
    H = wft.shape[0]
    ts = _tile(S, ts)

    def body(x_ref, w_ref, b_ref, lg_ref, F_ref, carry):
        @pl.when(pl.program_id(0) == 0)
        def _():
            carry[...] = jnp.zeros_like(carry)

        lg = _dot(_round_bf16(w_ref[...]), _round_bf16(x_ref[...]), NT, HI) + b_ref[...]
        lg_ref[...] = lg
        ls = jnp.minimum(lg, 0.0) - jnp.log(1.0 + jnp.exp(-jnp.abs(lg)))
        F = _dot(ls, _tri(ts, False), NN, HI) + carry[...]
        F_ref[...] = F
        carry[...] = F[:, ts - 1:ts]

    return pl.pallas_call(
        body, name=name, grid=(S // ts,),
        in_specs=[pl.BlockSpec((ts, D), lambda i: (i, 0)), pl.BlockSpec((H, D), lambda i: (0, 0)), pl.BlockSpec((H, 1), lambda i: (0, 0))],
        out_specs=[pl.BlockSpec((H, ts), lambda i: (0, i))] * 2,
        out_shape=[jax.ShapeDtypeStruct((H, S), F32)] * 2,
        scratch_shapes=[pltpu.VMEM((H, 1), F32)],
        compiler_params=_cparams("arbitrary"),
    )(xb, wft, bf)


def _fgate_bwd(dF, lg, name, ts=512):
    H, S = dF.shape
    ts = _tile(S, ts)
    nb = S // ts

    def body(dF_ref, lg_ref, dl_ref, db_ref, carry):
        @pl.when(pl.program_id(0) == 0)
        def _():
            carry[...] = jnp.zeros_like(carry)
            db_ref[...] = jnp.zeros_like(db_ref)

        dls = _dot(dF_ref[...], _tri(ts, True), NN, HI) + carry[...]
        carry[...] = dls[:, 0:1]
        dl = dls * _sigmoid(-lg_ref[...])
        dl_ref[...] = dl
        db_ref[...] += jnp.sum(dl, axis=1, keepdims=True)

    blk = pl.BlockSpec((H, ts), lambda i: (0, nb - 1 - i))
    return pl.pallas_call(
        body, name=name, grid=(nb,),
        in_specs=[blk, blk],
        out_specs=[blk, pl.BlockSpec((H, 1), lambda i: (0, 0))],
        out_shape=[jax.ShapeDtypeStruct((H, S), F32), jax.ShapeDtypeStruct((H, 1), F32)],
        scratch_shapes=[pltpu.VMEM((H, 1), F32)],
        compiler_params=_cparams("arbitrary"),
    )(dF, lg)


def _causal_keep(i, j, tq, tk):
    rows = lax.broadcasted_iota(jnp.int32, (tq, tk), 0)
    cols = lax.broadcasted_iota(jnp.int32, (tq, tk), 1)
    return jnp.logical_or(j < i, cols <= rows)


def _causal_grid(S, t):
    n = S // t
    assert n % 2 == 0, (S, t)

    def by_query(r, c):
        second = c > r
        return jnp.where(second, n - 1 - r, r), jnp.where(second, c - r - 1, c)

    def by_key(r, c):
        second = c >= n - r
        return jnp.where(second, c - 1, r + c), jnp.where(second, n - 1 - r, r)

    return n, (n // 2, n + 1), by_query, by_key


def _fox_scores(q_ref, k_ref, fk_ref, i, j, t, scale):
    s = _dot(q_ref[...].astype(BF16), k_ref[...].astype(BF16), NT) * scale - fk_ref[...]
    return lax.cond(i == j, lambda x: jnp.where(_causal_keep(i, j, t, t), x, NEG), lambda x: x, s)


def _softmax_bwd(p, pb, dp, delta):
    return pb.astype(F32) * dp - p * delta


FOX_TILE = 1024


def _fox_tile(S):
    return _tile(S, min(FOX_TILE, S // 2))


def _fox_fwd(u, F3, H, name, carry=None):
    S = u.shape[0]
    W = H * HEAD_DIM
    t = _fox_tile(S)
    n, tri, ij, _ = _causal_grid(S, t)
    scale = HEAD_DIM ** -0.5

    def body(q_ref, k_ref, v_ref, fk_ref, o_ref, ob_ref, lse_ref, m_s, l_s, acc_s):
        i, j = ij(pl.program_id(1), pl.program_id(2))

        @pl.when(j == 0)
        def _():
            m_s[...] = jnp.full_like(m_s, NEG)
            l_s[...] = jnp.zeros_like(l_s)
            acc_s[...] = jnp.zeros_like(acc_s)

        s = _fox_scores(q_ref, k_ref, fk_ref, i, j, t, scale)
        m_new = jnp.maximum(m_s[...], jnp.max(s, axis=-1, keepdims=True))
        a = jnp.exp(m_s[...] - m_new)
        p = jnp.exp(s - m_new)
        hi = p.astype(BF16)
        lo = (p - hi.astype(F32)).astype(BF16)
        v = v_ref[...].astype(BF16)
        l_s[...] = a * l_s[...] + jnp.sum(p, axis=-1, keepdims=True)
        acc_s[...] = a * acc_s[...] + (_dot(hi, v, NN) + _dot(lo, v, NN))
        m_s[...] = m_new

        @pl.when(j == i)
        def _():
            o = acc_s[...] / l_s[...]
            o_ref[...] = o
            ob_ref[...] = o.astype(BF16)
            lse_ref[...] = m_s[...] + jnp.log(l_s[...])

    qblk = pl.BlockSpec((t, HEAD_DIM), lambda h, r, c: (ij(r, c)[0], h))
    kv = lambda off: pl.BlockSpec((t, HEAD_DIM), lambda h, r, c: (ij(r, c)[1], off * H + h))
    outs, carried = _call(
        body, name=name, grid=(H,) + tri,
        in_specs=[qblk, kv(1), kv(2), pl.BlockSpec((None, 1, t), lambda h, r, c: (h, 0, ij(r, c)[1]))],
        out_specs=[qblk, qblk, pl.BlockSpec((None, t, 1), lambda h, r, c: (h, ij(r, c)[0], 0))],
        out_shape=[jax.ShapeDtypeStruct((S, W), F32), jax.ShapeDtypeStruct((S, W), BF16), jax.ShapeDtypeStruct((H, S, 1), F32)],
        scratch_shapes=[pltpu.VMEM((t, 1), F32), pltpu.VMEM((t, 1), F32), pltpu.VMEM((t, HEAD_DIM), F32)],
        sem=("parallel", "parallel", "arbitrary"), args=(u, u, u, F3), carry=carry)
    return outs[0], outs[1], outs[2], carried


def _fox_bwd(u, F3, o, do, do_off, lse, H, name, carry=None):
    S = u.shape[0]
    W = H * HEAD_DIM
    t = _fox_tile(S)
    n, tri, _, ij = _causal_grid(S, t)
    scale = HEAD_DIM ** -0.5

    def body(q_ref, k_ref, v_ref, fk_ref, o_ref, do_ref, lse_ref, dq_ref, dk_ref, dv_ref, dF_ref, dk_s, dv_s, dF_s):
        r, c = pl.program_id(1), pl.program_id(2)
        i, j = ij(r, c)

        @pl.when(jnp.logical_and(r == 0, c == 0))
        def _():
            dq_ref[...] = jnp.zeros_like(dq_ref)

        @pl.when(i == j)
        def _():
            dk_s[...] = jnp.zeros_like(dk_s)
            dv_s[...] = jnp.zeros_like(dv_s)
            dF_s[...] = jnp.zeros_like(dF_s)

        q = q_ref[...].astype(BF16)
        k = k_ref[...].astype(BF16)
        dob = do_ref[...].astype(BF16)
        p = jnp.exp(_fox_scores(q_ref, k_ref, fk_ref, i, j, t, scale) - lse_ref[...])
        dp = _dot(dob, v_ref[...].astype(BF16), NT)
        delta = jnp.sum(dob.astype(F32) * o_ref[...], axis=-1, keepdims=True)
        ds = p * (dp - delta)
        dsb = ds.astype(BF16)
        dv_s[...] += _dot(p.astype(BF16), dob, TN)
        dk_s[...] += _dot(dsb, q, TN)
        dF_s[...] -= jnp.sum(ds, axis=0, keepdims=True)
        rows = pl.ds(pl.multiple_of(i * t, t), t)
        dq_ref[rows, :] += _dot(dsb, k, NN) * scale

        @pl.when(i == n - 1)
        def _():
            dk_ref[...] = dk_s[...] * scale
            dv_ref[...] = dv_s[...]
            dF_ref[...] = dF_s[...]

    qblk = pl.BlockSpec((t, HEAD_DIM), lambda h, r, c: (ij(r, c)[0], h))
    kv = lambda off: pl.BlockSpec((t, HEAD_DIM), lambda h, r, c: (ij(r, c)[1], off * H + h))
    oblk = pl.BlockSpec((t, HEAD_DIM), lambda h, r, c: (ij(r, c)[1], h))
    fblk = pl.BlockSpec((None, 1, t), lambda h, r, c: (h, 0, ij(r, c)[1]))
    return _call(
        body, name=name, grid=(H,) + tri,
        in_specs=[qblk, kv(1), kv(2), fblk, qblk, pl.BlockSpec((t, HEAD_DIM), lambda h, r, c: (ij(r, c)[0], do_off + h)),
                  pl.BlockSpec((None, t, 1), lambda h, r, c: (h, ij(r, c)[0], 0))],
        out_specs=[pl.BlockSpec((S, HEAD_DIM), lambda h, r, c: (0, h)), oblk, oblk, fblk],
        out_shape=[jax.ShapeDtypeStruct((S, W), F32)] * 3 + [jax.ShapeDtypeStruct((H, 1, S), F32)],
        scratch_shapes=[pltpu.VMEM((t, HEAD_DIM), F32), pltpu.VMEM((t, HEAD_DIM), F32), pltpu.VMEM((1, t), F32)],
        sem=("parallel", "arbitrary", "arbitrary"), args=(u, u, u, F3, o, do, lse), carry=carry)


def _rel_index_matrix():
    a = lax.broadcasted_iota(jnp.int32, (REL_TABLE, CA_SKEW), 0)
    j = lax.broadcasted_iota(jnp.int32, (REL_TABLE, CA_SKEW), 1)
    rel = jnp.where(j < CA_WIN, CA_LEFT_CHUNKS * CHUNK - j, REL_CLIP)
    idx = jnp.clip(jnp.minimum(rel, REL_CLIP) + (CHUNK - 1), 0, REL_TABLE - 1)
    return (a == idx).astype(F32)


def _skew(x, sign):
    r = lax.broadcasted_iota(jnp.int32, x.shape, 0)
    for b in range(int(math.log2(CA_TILE))):
        sh = (1 << b) if sign > 0 else CA_SKEW - (1 << b)
        x = jnp.where((r >> b) & 1 == 1, pltpu.roll(x, sh, 1), x)
    return x


def _band_valid():
    shift = int(math.log2(CHUNK))
    r = lax.broadcasted_iota(jnp.int32, (CA_TILE, CA_WIN), 0) >> shift
    m = lax.broadcasted_iota(jnp.int32, (CA_TILE, CA_WIN), 1) >> shift
    return jnp.logical_and(m >= r, m <= r + CA_LEFT_CHUNKS)


def _relbias_fwd(table, name):
    H = table.shape[0]

    def body(t_ref, b_ref):
        rowv = _dot(t_ref[...], _rel_index_matrix(), NN, HI)
        valid = _band_valid()
        for h in range(H):
            x = _skew(jnp.broadcast_to(rowv[h:h + 1, :], (CA_TILE, CA_SKEW)), +1)
            b_ref[h] = jnp.where(valid, x[:, :CA_WIN], NEG)

    return pl.pallas_call(
        body, name=name,
        out_shape=jax.ShapeDtypeStruct((H, CA_TILE, CA_WIN), F32),
        compiler_params=pltpu.CompilerParams(vmem_limit_bytes=VMEM_LIMIT_V7X),
    )(table)


def _relbias_bwd(dB, name):
    H = dB.shape[0]
    HP = -(-H // 8) * 8

    def body(d_ref, dt_ref, rows):
        rows[...] = jnp.zeros_like(rows)
        for h in range(H):
            x = jnp.concatenate([d_ref[h], jnp.zeros((CA_TILE, CA_SKEW - CA_WIN), F32)], axis=1)
            rows[pl.ds(h, 1), :] = jnp.sum(_skew(x, -1), axis=0, keepdims=True)
        dt_ref[...] = _dot(rows[...], _rel_index_matrix(), NT, HI)[:H]

    return pl.pallas_call(
        body, name=name,
        out_shape=jax.ShapeDtypeStruct((H, REL_TABLE), F32),
        scratch_shapes=[pltpu.VMEM((HP, CA_SKEW), F32)],
        compiler_params=pltpu.CompilerParams(vmem_limit_bytes=VMEM_LIMIT_V7X),
    )(dB)


CA_PIECES = CA_WIN // CA_TILE


def _ca_fwd(u, bias, H, name):
    S = u.shape[0]
    W = H * HEAD_DIM
    T = CA_TILE
    n = S // T
    scale = HEAD_DIM ** -0.5

    def body(q_ref, k0, k1, k2, v0, v1, v2, b_ref, o_ref, ob_ref, lse_ref):
        i = pl.program_id(1)
        q = q_ref[...].astype(BF16)
        ss = []
        for pce, k_ref in enumerate((k0, k1, k2)):
            s = _dot(q, k_ref[...].astype(BF16), NT) * scale + b_ref[:, pce * T:(pce + 1) * T]
            ss.append(jnp.where(i + pce >= CA_PIECES - 1, s, NEG))
        m = jnp.maximum(jnp.maximum(jnp.max(ss[0], -1, keepdims=True), jnp.max(ss[1], -1, keepdims=True)), jnp.max(ss[2], -1, keepdims=True))
        ps = [jnp.exp(s - m) for s in ss]
        l = sum(jnp.sum(p, -1, keepdims=True) for p in ps)
        inv = 1.0 / l
        o = sum(_dot((p * inv).astype(BF16), v_ref[...].astype(BF16), NN) for p, v_ref in zip(ps, (v0, v1, v2)))
        o_ref[...] = o
        ob_ref[...] = o.astype(BF16)
        lse_ref[...] = m + jnp.log(l)

    qblk = pl.BlockSpec((T, HEAD_DIM), lambda h, i: (i, h))
    kv = lambda off, back: pl.BlockSpec((T, HEAD_DIM), lambda h, i: (jnp.maximum(i - back, 0), off * H + h))
    return pl.pallas_call(
        body, name=name, grid=(H, n),
        in_specs=[qblk, kv(1, 2), kv(1, 1), kv(1, 0), kv(2, 2), kv(2, 1), kv(2, 0),
                  pl.BlockSpec((None, T, CA_WIN), lambda h, i: (h, 0, 0))],
        out_specs=[qblk, qblk, pl.BlockSpec((None, T, 1), lambda h, i: (h, i, 0))],
        out_shape=[jax.ShapeDtypeStruct((S, W), F32), jax.ShapeDtypeStruct((S, W), BF16), jax.ShapeDtypeStruct((H, S, 1), F32)],
        compiler_params=_cparams("parallel", "arbitrary"),
    )(u, u, u, u, u, u, u, bias)


def _ca_bwd(u, bias, o, do, do_off, lse, H, name):
    S = u.shape[0]
    W = H * HEAD_DIM
    T = CA_TILE
    n = S // T
    scale = HEAD_DIM ** -0.5

    def body(q_ref, k0, k1, k2, v0, v1, v2, b_ref, o_ref, do_ref, lse_ref, dq_ref, dk_ref, dv_ref, db_ref):
        i = pl.program_id(1)

        @pl.when(i == 0)
        def _():
            db_ref[...] = jnp.zeros_like(db_ref)
            dk_ref[...] = jnp.zeros_like(dk_ref)
            dv_ref[...] = jnp.zeros_like(dv_ref)

        q = q_ref[...].astype(BF16)
        dob = do_ref[...].astype(BF16)
        delta = jnp.sum(dob.astype(F32) * o_ref[...], axis=-1, keepdims=True)
        dq = jnp.zeros((T, HEAD_DIM), F32)
        for pce, (k_ref, v_ref) in enumerate(((k0, v0), (k1, v1), (k2, v2))):
            k = k_ref[...].astype(BF16)
            s = _dot(q, k, NT) * scale + b_ref[:, pce * T:(pce + 1) * T]
            p = jnp.where(i + pce >= CA_PIECES - 1, jnp.exp(s - lse_ref[...]), 0.0)
            pb = p.astype(BF16)
            ds = _softmax_bwd(p, pb, _dot(dob, v_ref[...].astype(BF16), NT), delta)
            dsb = ds.astype(BF16)
            db_ref[:, pce * T:(pce + 1) * T] += ds
            dq = dq + _dot(dsb, k, NN)
            rows = pl.ds(pl.multiple_of(jnp.maximum(i - (CA_PIECES - 1) + pce, 0) * T, T), T)
            dk_ref[rows, :] += _dot(dsb, q, TN) * scale
            dv_ref[rows, :] += _dot(pb, dob, TN)
        dq_ref[...] = dq * scale

    qblk = pl.BlockSpec((T, HEAD_DIM), lambda h, i: (i, h))
    kv = lambda off, back: pl.BlockSpec((T, HEAD_DIM), lambda h, i: (jnp.maximum(i - back, 0), off * H + h))
    bblk = pl.BlockSpec((None, T, CA_WIN), lambda h, i: (h, 0, 0))
    head = pl.BlockSpec((S, HEAD_DIM), lambda h, i: (0, h))
    return pl.pallas_call(
        body, name=name, grid=(H, n),
        in_specs=[qblk, kv(1, 2), kv(1, 1), kv(1, 0), kv(2, 2), kv(2, 1), kv(2, 0), bblk,
                  qblk, pl.BlockSpec((T, HEAD_DIM), lambda h, i: (i, do_off + h)), pl.BlockSpec((None, T, 1), lambda h, i: (h, i, 0))],
        out_specs=[qblk, head, head, bblk],
        out_shape=[jax.ShapeDtypeStruct((S, W), F32)] * 3 + [jax.ShapeDtypeStruct((H, T, CA_WIN), F32)],
        compiler_params=_cparams("parallel", "arbitrary"),
    )(u, u, u, u, u, u, u, bias, o, do, lse)


def _sum_parts(parts, name):
    _, R, C = parts.shape

    def body(p_ref, o_ref):
        acc = p_ref[0].astype(F32)
        for d in range(1, N_DEV):
            acc = acc + p_ref[d].astype(F32)
        o_ref[...] = acc

    return pl.pallas_call(
        body, name=name, out_shape=jax.ShapeDtypeStruct((R, C), F32),
        compiler_params=pltpu.CompilerParams(vmem_limit_bytes=VMEM_LIMIT_V7X),
    )(parts)


def _adamw(parts, w, m, v, name, tr=128):
    L, R, C = w.shape
    P = parts[0].shape[0]
    tr = _tile(R, tr)
    c1 = 1.0 / (1.0 - ADAM_B1 ** ADAM_STEP)
    c2 = 1.0 / (1.0 - ADAM_B2 ** ADAM_STEP)

    def body(*refs):
        p_refs = refs[:L]
        w_ref, m_ref, v_ref, g_ref, d_ref, nm_ref, nv_ref = refs[L:]

        def total(p_ref):
            g = p_ref[0].astype(F32)
            for d in range(1, P):
                g = g + p_ref[d].astype(F32)
            return g

        g = _select(pl.program_id(0), [functools.partial(total, r) for r in p_refs])
        nm = ADAM_B1 * m_ref[...] + (1.0 - ADAM_B1) * g
        nv = ADAM_B2 * v_ref[...] + (1.0 - ADAM_B2) * (g * g)
        g_ref[...] = g
        nm_ref[...] = nm
        nv_ref[...] = nv
        d_ref[...] = -ADAM_LR * ((nm * c1) / (jnp.sqrt(nv * c2) + ADAM_EPS) + ADAM_WD * w_ref[...])

    blk = pl.BlockSpec((None, tr, C), lambda l, i: (l, i, 0))
    p_specs = [pl.BlockSpec((P, tr, C), functools.partial(lambda p, l, i: (0, jnp.where(l == p, i, 0), 0), p)) for p in range(L)]
    return pl.pallas_call(
        body, name=name, grid=(L, R // tr),
        in_specs=p_specs + [blk, blk, blk],
        out_specs=[blk] * 4,
        out_shape=[jax.ShapeDtypeStruct((L, R, C), F32)] * 4,
        compiler_params=_cparams("arbitrary", "arbitrary"),
    )(*parts, w, m, v)


def _peer(d):
    x, y, c = lax.axis_index("x"), lax.axis_index("y"), lax.axis_index("c")
    px = (1 - x) if d & 4 else x
    py = (1 - y) if d & 2 else y
    pc = (1 - c) if d & 1 else c
    return (px, py, pc), 4 * px + 2 * py + pc


N_PEER = N_DEV - 1


def _exchange_copies(ins, outs, sems, scatter):
    send_sems, recv_sems, local_sems = sems
    me = 4 * lax.axis_index("x") + 2 * lax.axis_index("y") + lax.axis_index("c")
    starts, waits = [], []
    for t in range(len(ins)):
        loc = pltpu.make_async_copy(ins[t].at[me] if scatter else ins[t], outs[t].at[me], local_sems.at[t])
        starts.append(loc.start)
        waits.append(loc.wait)
        for d in range(1, N_DEV):
            peer, pidx = _peer(d)
            src = ins[t].at[pidx] if scatter else ins[t]
            k = t * N_PEER + d - 1
            common = dict(src_ref=src, send_sem=send_sems.at[k], recv_sem=recv_sems.at[k], device_id=peer, device_id_type=MESH_ID)
            starts.append(pltpu.make_async_remote_copy(dst_ref=outs[t].at[me], **common).start)
            waits.append(pltpu.make_async_remote_copy(dst_ref=outs[t].at[pidx], **common).wait)
    return starts, waits


def _exchange_scratch(n):
    return [pltpu.SemaphoreType.DMA((n * N_PEER,)), pltpu.SemaphoreType.DMA((n * N_PEER,)), pltpu.SemaphoreType.DMA((n,))]


def _exchange_shapes(arrays, scatter):
    return [jax.ShapeDtypeStruct(a.shape if scatter else (N_DEV,) + a.shape, a.dtype) for a in arrays]


def _exchange(arrays, scatter, name):
    n = len(arrays)

    def body(*refs):
        starts, waits = _exchange_copies(refs[:n], refs[n:2 * n], refs[2 * n:], scatter)
        for f in starts:
            f()
        for f in waits:
            f()

    hbm = pl.BlockSpec(memory_space=pltpu.HBM)
    return pl.pallas_call(
        body, name=name,
        in_specs=[hbm] * n, out_specs=[hbm] * n,
        out_shape=_exchange_shapes(arrays, scatter),
        scratch_shapes=_exchange_scratch(n),
        compiler_params=pltpu.CompilerParams(has_side_effects=True),
    )(*arrays)


def _call(body, *, name, grid, in_specs, out_specs, out_shape, scratch_shapes, sem, args, carry=None):
    if carry is None:
        outs = pl.pallas_call(body, name=name, grid=grid, in_specs=in_specs, out_specs=out_specs, out_shape=out_shape,
                              scratch_shapes=scratch_shapes, compiler_params=_cparams(*sem))(*args)
        return list(outs), []
    arrays, scatter = carry
    nc, n_in, n_out, n_scr = len(arrays), len(in_specs), len(out_specs), len(scratch_shapes)

    def wrapped(*refs):
        ins, cins = refs[:n_in], refs[n_in:n_in + nc]
        outs, couts = refs[n_in + nc:n_in + nc + n_out], refs[n_in + nc + n_out:n_in + 2 * nc + n_out]
        scr = refs[n_in + 2 * nc + n_out:n_in + 2 * nc + n_out + n_scr]
        ids = [pl.program_id(k) for k in range(len(grid))]
        first = functools.reduce(jnp.logical_and, [i == 0 for i in ids])
        last = functools.reduce(jnp.logical_and, [i == g - 1 for i, g in zip(ids, grid)])
        starts, waits = _exchange_copies(cins, couts, refs[-3:], scatter)

        @pl.when(first)
        def _():
            for f in starts:
                f()

        body(*ins, *outs, *scr)

        @pl.when(last)
        def _():
            for f in waits:
                f()

    hbm = pl.BlockSpec(memory_space=pltpu.HBM)
    res = pl.pallas_call(
        wrapped, name=name, grid=grid,
        in_specs=list(in_specs) + [hbm] * nc, out_specs=list(out_specs) + [hbm] * nc,
        out_shape=list(out_shape) + _exchange_shapes(arrays, scatter),
        scratch_shapes=list(scratch_shapes) + _exchange_scratch(nc),
        compiler_params=_cparams(*(("arbitrary",) * len(grid))),
    )(*args, *arrays)
    return list(res[:n_out]), list(res[n_out:])


def _relu2_fwd(acc):
    r = jnp.maximum(acc, 0.0)
    return acc, r * r


def _relu2_bwd(acc, z):
    return (acc * 2.0 * jnp.maximum(z.astype(F32), 0.0),)


def _add_scaled(scale):
    def epi(acc, t):
        return (acc + scale * t,)
    return epi


def _local_step(x, target, P, plan):
    S, D = x.shape
    H = (D // 2) // HEAD_DIM
    W = H * HEAD_DIM
    alpha = (2 * 2) ** 0.25
    g = {}

    def mm(a, b, *, name, **kw):
        res, carried = _mm(a, b, name=name, carry=plan.carry(name, g), **kw)
        plan.arrived(name, carried, P, g)
        return res

    u0 = mm(x, P["ev_w_in"], mode="nn", name="ev_in", b_blocked=True)
    (a_out, cv), carried = _conv_fwd(u0, P["ev_conv_w"], P["ev_conv_b"], P["ev_conv_ln_g"], P["ev_conv_ln_b"], "conv_fwd",
                                     carry=plan.carry("conv_fwd", g))
    plan.arrived("conv_fwd", carried, P, g)
    b_out, h_raw, h_states = _hgrn_fwd(u0, P["hgrn_lb_logits"], P["ev_gnorm_g"], "hgrn_fwd")
    cat0 = (a_out, b_out)
    mix0 = mm(cat0, P["ev_w_out"], mode="nn", name="ev_out")
    x1, x1b, xh1, rs1 = _ln_fwd(x, mix0, P["ln_mix_g"][0:1], P["ln_mix_b"][0:1], alpha, "ln_mix0")
    z0, hh0 = mm(x1b, P["mlp_w1_0"], mode="nn", name="mlp_up0", out_dtypes=(BF16, BF16), epi=_relu2_fwd, b_blocked=True)
    m0 = mm(hh0, P["mlp_w2_0"], mode="nn", name="mlp_down0")
    x2, x2b, xh2, rs2 = _ln_fwd(x1, m0, P["ln_mlp_g"][0:1], P["ln_mlp_b"][0:1], alpha, "ln_mlp0")

    uc = mm(x2b, P["od_w_c"], mode="nn", name="od_in_c")
    ud = mm(x2b, P["od_w_d"], mode="nn", name="od_in_d")
    f_logit, F = _fgate_fwd(x2b, P["od_w_f_t"], P["fox_b_f"].reshape(H, 1), "fgate_fwd")
    F3 = F.reshape(H, 1, S)
    c_out, c_out_b, c_lse, carried = _fox_fwd(uc, F3, H, "fox_fwd", carry=plan.carry("fox_fwd", g))
    plan.arrived("fox_fwd", carried, P, g)
    bias = _relbias_fwd(P["rel_bias"], "relbias_fwd")
    d_out, d_out_b, d_lse = _ca_fwd(ud, bias, H, "ca_fwd")
    cat1 = (c_out_b, d_out_b)
    mix1 = mm(cat1, P["od_w_out"], mode="nn", name="od_out")
    x3, x3b, xh3, rs3 = _ln_fwd(x2, mix1, P["ln_mix_g"][1:2], P["ln_mix_b"][1:2], alpha, "ln_mix1")
    z1, hh1 = mm(x3b, P["mlp_w1_1"], mode="nn", name="mlp_up1", out_dtypes=(BF16, BF16), epi=_relu2_fwd, b_blocked=True)
    m1 = mm(hh1, P["mlp_w2_1"], mode="nn", name="mlp_down1")
    dy, xh4, rs4, loss = _ln_loss(x3, m1, P["ln_mlp_g"][1:2], P["ln_mlp_b"][1:2], alpha, target, "ln_mlp1_loss")

    dzm1, dzm1b, dg_, db_ = _ln_bwd(dy, xh4, rs4, P["ln_mlp_g"][1:2], "ln_mlp1_bwd")
    g["ln_mlp_g1"], g["ln_mlp_b1"] = dg_, db_
    g["mlp_w2_1"] = mm(hh1, dzm1b, mode="tn", name="mlp_down1_dw", out_dtypes=(BF16,))
    dz1 = mm(dzm1b, P["mlp_w2_1"], mode="nt", name="mlp_down1_dx", out_dtypes=(BF16,), extras=(z1,), epi=_relu2_bwd)
    g["mlp_w1_1"] = mm(x3b, dz1, mode="tn", name="mlp_up1_dw", out_dtypes=(BF16,), out_blocked=True)
    dx3 = mm(dz1, P["mlp_w1_1"], mode="nt", name="mlp_up1_dx", extras=(dzm1,), epi=_add_scaled(alpha), b_blocked=True)
    dzx1, dzx1b, dg_, db_ = _ln_bwd(dx3, xh3, rs3, P["ln_mix_g"][1:2], "ln_mix1_bwd")
    g["ln_mix_g1"], g["ln_mix_b1"] = dg_, db_
    g["od_w_out"] = mm(cat1, dzx1b, mode="tn", name="od_out_dw", out_dtypes=(BF16,))
    dcat1 = mm(dzx1b, P["od_w_out"], mode="nt", name="od_out_dx")
    dq_d, dk_d, dv_d, dbias = _ca_bwd(ud, bias, d_out, dcat1, H, d_lse, H, "ca_bwd")
    g["rel_bias"] = _relbias_bwd(dbias, "relbias_bwd")
    dud = jnp.concatenate([dq_d, dk_d, dv_d], axis=1)
    (dq_c, dk_c, dv_c, dF3), carried = _fox_bwd(uc, F3, c_out, dcat1, 0, c_lse, H, "fox_bwd", carry=plan.carry("fox_bwd", g))
    plan.arrived("fox_bwd", carried, P, g)
    duc = jnp.concatenate([dq_c, dk_c, dv_c], axis=1)
    dfl, dbf = _fgate_bwd(dF3.reshape(H, S), f_logit, "fgate_bwd")
    g["fox_b_f"] = dbf.reshape(1, H)
    g["od_w_c"] = mm(x2b, duc, mode="tn", name="od_in_c_dw", out_dtypes=(BF16,))
    g["od_w_d"] = mm(x2b, dud, mode="tn", name="od_in_d_dw", out_dtypes=(BF16,))
    g["od_w_f_t"] = mm(dfl, x2b, mode="nn", name="od_in_f_dw", out_dtypes=(BF16,), exact_products=True)
    dx2 = mm(duc, P["od_w_c"], mode="nt", name="od_in_c_dx", extras=(dzx1,), epi=_add_scaled(alpha))
    dx2 = mm(dud, P["od_w_d"], mode="nt", name="od_in_d_dx", extras=(dx2,), epi=_add_scaled(1.0))
    dx2 = mm(dfl, P["od_w_f_t"], mode="tn", name="od_in_f_dx", extras=(dx2,), epi=_add_scaled(1.0), exact_products=True)

    dzm0, dzm0b, dg_, db_ = _ln_bwd(dx2, xh2, rs2, P["ln_mlp_g"][0:1], "ln_mlp0_bwd")
    g["ln_mlp_g0"], g["ln_mlp_b0"] = dg_, db_
    g["mlp_w2_0"] = mm(hh0, dzm0b, mode="tn", name="mlp_down0_dw", out_dtypes=(BF16,))
    dz0 = mm(dzm0b, P["mlp_w2_0"], mode="nt", name="mlp_down0_dx", out_dtypes=(BF16,), extras=(z0,), epi=_relu2_bwd)
    g["mlp_w1_0"] = mm(x1b, dz0, mode="tn", name="mlp_up0_dw", out_dtypes=(BF16,), out_blocked=True)
    dx1 = mm(dz0, P["mlp_w1_0"], mode="nt", name="mlp_up0_dx", extras=(dzm0,), epi=_add_scaled(alpha), b_blocked=True)
    dzx0, dzx0b, dg_, db_ = _ln_bwd(dx1, xh1, rs1, P["ln_mix_g"][0:1], "ln_mix0_bwd")
    g["ln_mix_g0"], g["ln_mix_b0"] = dg_, db_
    g["ev_w_out"] = mm(cat0, dzx0b, mode="tn", name="ev_out_dw", out_dtypes=(BF16,))
    dcat0 = mm(dzx0b, P["ev_w_out"], mode="nt", name="ev_out_dx")
    dcv, g["ev_conv_ln_g"], g["ev_conv_ln_b"], g["ev_conv_b"] = _conv_bwd_norm(dcat0, 0, cv, P["ev_conv_ln_g"], P["ev_conv_ln_b"], "conv_bwd_norm")
    dag, g["ev_conv_w"] = _conv_bwd_taps(dcv, u0, P["ev_conv_w"], "conv_bwd_taps")
    du0, g["ev_gnorm_g"], g["hgrn_lb_logits"] = _hgrn_bwd(dcat0, 1, dag, u0, h_raw, h_states, P["hgrn_lb_logits"], P["ev_gnorm_g"], "hgrn_bwd")
    g["ev_w_in"] = mm(x, du0, mode="tn", name="ev_in_dw", out_dtypes=(BF16,), out_blocked=True)
    dx0 = mm(du0, P["ev_w_in"], mode="nt", name="ev_in_dx", extras=(dzx0,), epi=_add_scaled(alpha), b_blocked=True)
    return loss, dx0, g


_NAMES = ['ev_w_in', 'ev_conv_w', 'ev_conv_b', 'ev_conv_ln_g', 'ev_conv_ln_b', 'hgrn_lb_logits', 'ev_gnorm_g', 'ev_w_out',
          'od_w_in', 'fox_b_f', 'rel_bias', 'od_w_out', 'ln_mix_g', 'ln_mix_b', 'mlp_w1', 'mlp_w2', 'ln_mlp_g', 'ln_mlp_b']
_SMALL = ['ev_conv_b', 'ev_conv_ln_g', 'ev_conv_ln_b', 'hgrn_lb_logits', 'ev_gnorm_g', 'fox_b_f', 'ln_mix_g', 'ln_mix_b',
          'ln_mlp_g', 'ln_mlp_b', 'ev_conv_w', 'rel_bias']
_PACK_COLS = 2048


def _cols_to_full(gathered):
    nd, K, n = gathered.shape
    return jnp.transpose(gathered, (1, 0, 2)).reshape(K, nd * n)


def _full_to_cols(full):
    K, N = full.shape
    return jnp.transpose(full.reshape(K, N_DEV, N // N_DEV), (1, 0, 2))


_ROW_SHARDED = ("ev_w_out", "od_w_out", "mlp_w2_0", "mlp_w2_1")
_BLOCKED = ("ev_w_in", "mlp_w1_0", "mlp_w1_1")


def _full_weights(name, gathered, heads):
    if name in _BLOCKED:
        return {name: gathered}
    if name in _ROW_SHARDED:
        return {name: gathered.reshape(-1, gathered.shape[-1])}
    full = _cols_to_full(gathered)
    if name != "od_w_in":
        return {name: full}
    w = heads * HEAD_DIM
    return {"od_w_c": full[:, :3 * w], "od_w_f_t": jnp.transpose(full[:, 3 * w:3 * w + heads]), "od_w_d": full[:, 3 * w + heads:]}


def _grad_blocks(name, g):
    if name in _BLOCKED:
        return g[name]
    if name in _ROW_SHARDED:
        return g[name].reshape(N_DEV, -1, g[name].shape[-1])
    if name == "od_w_in":
        return _full_to_cols(jnp.concatenate([g["od_w_c"], jnp.transpose(g["od_w_f_t"]), g["od_w_d"]], axis=1))
    return _full_to_cols(g[name])


class _Plan:
    GATHER = {"ev_in": ("mlp_w1_0",), "conv_fwd": ("ev_w_out",), "mlp_up0": ("mlp_w2_0",), "mlp_down0": ("od_w_in",), "od_in_c": ("od_w_out",),
              "fox_fwd": ("mlp_w1_1", "mlp_w2_1")}
    SCATTER = {"fox_bwd": ("mlp_w2_1", "mlp_w1_1", "od_w_out"), "mlp_down0_dw": ("od_w_in",),
               "mlp_down0_dx": ("mlp_w2_0",), "mlp_up0_dx": ("mlp_w1_0",), "ev_out_dx": ("ev_w_out",), "ev_in_dx": ("ev_w_in",)}

    def __init__(self, shards, heads):
        self.shards, self.heads = shards, heads

    def carry(self, call, g):
        if call in self.GATHER:
            return [self.shards[n] for n in self.GATHER[call]], False
        if call in self.SCATTER:
            return [_grad_blocks(n, g) for n in self.SCATTER[call]], True
        return None

    def arrived(self, call, carried, P, g):
        if call in self.GATHER:
            for n, gathered in zip(self.GATHER[call], carried):
                P.update(_full_weights(n, gathered, self.heads))
        elif call in self.SCATTER:
            for n, received in zip(self.SCATTER[call], carried):
                g[n] = received


def _pack(parts):
    flat = jnp.concatenate([p.reshape(-1).astype(F32) for p in parts])
    rows = -(-flat.shape[0] // (_PACK_COLS * 8)) * 8
    return jnp.pad(flat, (0, rows * _PACK_COLS - flat.shape[0])).reshape(rows, _PACK_COLS)


def _unpack(packed, shapes):
    flat = packed.reshape(-1)
    out, off = [], 0
    for s in shapes:
        n = math.prod(s)
        out.append(flat[off:off + n].reshape(s))
        off += n
    return out


def kernel(x, ev_w_in, ev_conv_w, ev_conv_b, ev_conv_ln_g, ev_conv_ln_b, hgrn_lb_logits, ev_gnorm_g, ev_w_out, od_w_in, fox_b_f, rel_bias, od_w_out, ln_mix_g, ln_mix_b, mlp_w1, mlp_w2, ln_mlp_g, ln_mlp_b, loss_target, m_ev_w_in, m_ev_conv_w, m_ev_conv_b, m_ev_conv_ln_g, m_ev_conv_ln_b, m_hgrn_lb_logits, m_ev_gnorm_g, m_ev_w_out, m_od_w_in, m_fox_b_f, m_rel_bias, m_od_w_out, m_ln_mix_g, m_ln_mix_b, m_mlp_w1, m_mlp_w2, m_ln_mlp_g, m_ln_mlp_b, v_ev_w_in, v_ev_conv_w, v_ev_conv_b, v_ev_conv_ln_g, v_ev_conv_ln_b, v_hgrn_lb_logits, v_ev_gnorm_g, v_ev_w_out, v_od_w_in, v_fox_b_f, v_rel_bias, v_od_w_out, v_ln_mix_g, v_ln_mix_b, v_mlp_w1, v_mlp_w2, v_ln_mlp_g, v_ln_mlp_b):
    args = locals()
    w = {n: args[n] for n in _NAMES}
    m = {n: args["m_" + n] for n in _NAMES}
    v = {n: args["v_" + n] for n in _NAMES}
    me = 4 * lax.axis_index("x") + 2 * lax.axis_index("y") + lax.axis_index("c")
    S, D = x.shape[1], x.shape[2]
    H = (D // 2) // HEAD_DIM
    W = H * HEAD_DIM
    n_layers = mlp_w1.shape[0]
    assert n_layers == 2 and ev_w_in.shape[0] == 1 and od_w_in.shape[0] == 1

    shards = {"ev_w_in": ev_w_in[0].astype(BF16), "ev_w_out": ev_w_out[0].astype(BF16),
              "od_w_in": od_w_in[0].astype(BF16), "od_w_out": od_w_out[0].astype(BF16)}
    for l in range(n_layers):
        shards["mlp_w1_%d" % l] = mlp_w1[l].astype(BF16)
        shards["mlp_w2_%d" % l] = mlp_w2[l].astype(BF16)
    first = ["ev_w_in", "ev_conv_w", "rel_bias"]
    G = _exchange([shards["ev_w_in"], ev_conv_w[0], rel_bias[0]], False, "gather_first")
    P = {
        "ev_conv_b": ev_conv_b, "ev_conv_ln_g": ev_conv_ln_g, "ev_conv_ln_b": ev_conv_ln_b,
        "hgrn_lb_logits": hgrn_lb_logits, "ev_gnorm_g": ev_gnorm_g, "fox_b_f": fox_b_f,
        "ln_mix_g": ln_mix_g, "ln_mix_b": ln_mix_b, "ln_mlp_g": ln_mlp_g, "ln_mlp_b": ln_mlp_b,
    }
    for name, gathered in zip(first, G):
        P.update(_full_weights(name, gathered, H))

    loss, grad_x, g = _local_step(x[0], loss_target[0], P, _Plan(shards, H))
    recv = [g["ev_w_in"], g["ev_w_out"], g["od_w_in"], g["od_w_out"]]
    recv += [g["mlp_w1_%d" % l] for l in range(n_layers)] + [g["mlp_w2_%d" % l] for l in range(n_layers)]

    small = {
        "ev_conv_b": g["ev_conv_b"], "ev_conv_ln_g": g["ev_conv_ln_g"], "ev_conv_ln_b": g["ev_conv_ln_b"],
        "hgrn_lb_logits": g["hgrn_lb_logits"], "ev_gnorm_g": g["ev_gnorm_g"], "fox_b_f": g["fox_b_f"],
        "ln_mix_g": jnp.concatenate([g["ln_mix_g0"], g["ln_mix_g1"]]), "ln_mix_b": jnp.concatenate([g["ln_mix_b0"], g["ln_mix_b1"]]),
        "ln_mlp_g": jnp.concatenate([g["ln_mlp_g0"], g["ln_mlp_g1"]]), "ln_mlp_b": jnp.concatenate([g["ln_mlp_b0"], g["ln_mlp_b1"]]),
        "ev_conv_w": g["ev_conv_w"], "rel_bias": g["rel_bias"],
    }
    full_shapes = [small[n].shape for n in _SMALL]
    small_all = _exchange([_pack([small[n] for n in _SMALL])], False, "gather_small_grads")[0]
    small_sum = _unpack(_sum_parts(small_all, "sum_small_grads"), full_shapes)
    small_g = dict(zip(_SMALL, small_sum))
    cw = small_g["ev_conv_w"]
    small_g["ev_conv_w"] = lax.dynamic_slice_in_dim(cw, me * (cw.shape[1] // N_DEV), cw.shape[1] // N_DEV, axis=1)
    rb = small_g["rel_bias"]
    small_g["rel_bias"] = lax.dynamic_slice_in_dim(rb, me * (rb.shape[1] // N_DEV), rb.shape[1] // N_DEV, axis=1)

    out_g, out_d, out_m, out_v = {}, {}, {}, {}
    big = [("ev_w_in", recv[0:1]), ("ev_w_out", recv[1:2]), ("od_w_in", recv[2:3]), ("od_w_out", recv[3:4]),
           ("mlp_w1", recv[4:4 + n_layers]), ("mlp_w2", recv[4 + n_layers:4 + 2 * n_layers])]
    for name, parts in big:
        out_g[name], out_d[name], out_m[name], out_v[name] = _adamw(parts, w[name], m[name], v[name], "adamw_" + name)
    shapes = [w[n].shape for n in _SMALL]
    packed = _adamw([_pack([small_g[n] for n in _SMALL])[None]], _pack([w[n] for n in _SMALL])[None], _pack([m[n] for n in _SMALL])[None],
                    _pack([v[n] for n in _SMALL])[None], "adamw_small")
    for k, dst in enumerate((out_g, out_d, out_m, out_v)):
        for n, a in zip(_SMALL, _unpack(packed[k], shapes)):
            dst[n] = a

    loss = lax.psum(loss[0, 0], ("x", "y", "c"))
    return (loss, grad_x[None], *[out_g[n] for n in _NAMES], *[out_d[n] for n in _NAMES],
            *[out_m[n] for n in _NAMES], *[out_v[n] for n in _NAMES])
```

```python
import functools
import math

import jax
import jax.numpy as jnp
from jax import lax
from jax.experimental import pallas as pl
from jax.experimental.pallas import tpu as pltpu

F32 = jnp.float32
BF16 = jnp.bfloat16
HI = lax.Precision.HIGHEST
MESH_ID = pl.DeviceIdType.MESH

N_DEV = 8
LN_EPS = 1e-5
CHUNK = 64
HEAD_DIM = 128
CONV_WIDTH = 31
CONV_HALO = 32
CA_LEFT_CHUNKS = 8
CA_TILE = 256
CA_WIN = CA_TILE + CA_LEFT_CHUNKS * CHUNK
CA_SKEW = 1024
REL_CLIP = 256
REL_TABLE = (CHUNK - 1) + REL_CLIP + 1
NEG = -1e30
ADAM_LR = 0.001
ADAM_B1 = 0.9
ADAM_B2 = 0.999
ADAM_EPS = 1e-08
ADAM_WD = 0.01
ADAM_STEP = 10
VMEM_LIMIT_V7X = 56 * 1024 * 1024


def _cparams(*sem):
    return pltpu.CompilerParams(dimension_semantics=sem, vmem_limit_bytes=VMEM_LIMIT_V7X)


def _tile(n, t):
    if n <= t:
        return n
    for c in range(t - t % 128, 0, -128):
        if n % c == 0:
            return c
    return n


def _sigmoid(x):
    return 1.0 / (1.0 + jnp.exp(-x))


def _dot(a, b, dims, precision=None):
    return lax.dot_general(a, b, (dims, ((), ())), preferred_element_type=F32, precision=precision)


def _round_bf16(x):
    return x.astype(BF16).astype(F32)


NN = ((1,), (0,))
NT = ((1,), (1,))
TN = ((0,), (0,))


def _select(idx, loads):
    if len(loads) == 1:
        return loads[0]()
    mid = len(loads) // 2
    return lax.cond(idx < mid, lambda: _select(idx, loads[:mid]), lambda: _select(idx - mid, loads[mid:]))


def _mm(a, b, *, mode, name, out_dtypes=(F32,), extras=(), epi=None, exact_products=False, carry=None,
        b_blocked=False, out_blocked=False, tm=1024, tn=1024, tk=2048):
    dims = {"nn": NN, "nt": NT, "tn": TN}[mode]
    a_parts = a if isinstance(a, tuple) else (a,)
    n_a = len(a_parts)
    a_split_k = n_a > 1 and mode != "tn"
    a_split_m = n_a > 1 and mode == "tn"
    if mode == "tn":
        K, M = a_parts[0].shape[0], a_parts[0].shape[1] * n_a
    else:
        M, K = a_parts[0].shape[0], a_parts[0].shape[1] * n_a
    if b_blocked:
        nb, _, shard = b.shape
        N = b.shape[1] if mode == "nt" else nb * shard
    else:
        N = b.shape[0] if mode == "nt" else b.shape[1]
    tm, tn, tk = _tile(M, tm), _tile(N, tn), _tile(K, tk)
    if a_split_k:
        tk = K // n_a
    if a_split_m:
        tm = M // n_a
    per_step = 1
    if b_blocked and mode == "nt":
        per_step = max(s for s in range(1, nb + 1) if nb % s == 0 and s * shard <= max(tk, shard))
        tk = per_step * shard
    if b_blocked and mode == "nn":
        tn = shard
    if out_blocked:
        tn = N // N_DEV
    nk = K // tk
    n_ex, n_out = len(extras), len(out_dtypes)

    def body(*refs):
        a_refs, b_ref = refs[:n_a], refs[n_a]
        ex_refs = refs[n_a + 1:n_a + 1 + n_ex]
        o_refs = refs[n_a + 1 + n_ex:n_a + 1 + n_ex + n_out]
        cast = _round_bf16 if exact_products else (lambda t: t.astype(BF16))
        part = pl.program_id(0) if a_split_m else pl.program_id(2)
        av = _select(part, [functools.partial(lambda r: cast(r[...]), r) for r in a_refs])
        precision = HI if exact_products else None
        if per_step > 1:
            d = sum(_dot(av[:, q * shard:(q + 1) * shard], cast(b_ref[q]), dims, precision) for q in range(per_step))
        else:
            d = _dot(av, cast(b_ref[...]), dims, precision)

        def finish(acc):
            outs = (acc,) if epi is None else epi(acc, *[r[...] for r in ex_refs])
            for o_ref, o in zip(o_refs, outs):
                o_ref[...] = o.astype(o_ref.dtype)

        if nk == 1:
            finish(d)
        else:
            acc_ref = refs[-1]
            k = pl.program_id(2)

            @pl.when(k == 0)
            def _():
                acc_ref[...] = d

            @pl.when(k > 0)
            def _():
                acc_ref[...] += d

            @pl.when(k == nk - 1)
            def _():
                finish(acc_ref[...])

    if a_split_k:
        a_specs = [pl.BlockSpec((tm, tk), lambda i, j, k: (i, 0))] * n_a
    elif a_split_m:
        a_specs = [pl.BlockSpec((tk, tm), functools.partial(lambda p, i, j, k: (jnp.where(i == p, k, 0), 0), p)) for p in range(n_a)]
    elif mode == "tn":
        a_specs = [pl.BlockSpec((tk, tm), lambda i, j, k: (k, i))]
    else:
        a_specs = [pl.BlockSpec((tm, tk), lambda i, j, k: (i, k))]
    if b_blocked and mode == "nt" and per_step > 1:
        b_spec = pl.BlockSpec((per_step, tn, shard), lambda i, j, k: (k, j, 0))
    elif b_blocked:
        b_spec = pl.BlockSpec((None, tn, tk), lambda i, j, k: (k, j, 0)) if mode == "nt" else pl.BlockSpec((None, tk, tn), lambda i, j, k: (j, k, 0))
    else:
        b_spec = pl.BlockSpec((tn, tk), lambda i, j, k: (j, k)) if mode == "nt" else pl.BlockSpec((tk, tn), lambda i, j, k: (k, j))
    mn_spec = pl.BlockSpec((tm, tn), lambda i, j, k: (i, j))
    if out_blocked:
        out_specs = [pl.BlockSpec((None, tm, tn), lambda i, j, k: (j, i, 0))] * n_out
        out_shape = [jax.ShapeDtypeStruct((N_DEV, M, tn), dt) for dt in out_dtypes]
    else:
        out_specs = [mn_spec] * n_out
        out_shape = [jax.ShapeDtypeStruct((M, N), dt) for dt in out_dtypes]
    outs, carried = _call(
        body, name=name, grid=(M // tm, N // tn, nk),
        in_specs=a_specs + [b_spec] + [mn_spec] * n_ex,
        out_specs=out_specs, out_shape=out_shape,
        scratch_shapes=[pltpu.VMEM((tm, tn), F32)] if nk > 1 else [],
        sem=("parallel", "parallel", "arbitrary"), args=(*a_parts, b, *extras), carry=carry)
    return (outs[0] if n_out == 1 else outs), carried


def _ln_fwd(x, r, g, b, alpha, name, tr=256):
    S, D = x.shape
    tr = _tile(S, tr)

    def body(x_ref, r_ref, g_ref, b_ref, y_ref, yb_ref, xh_ref, rs_ref):
        z = alpha * x_ref[...] + r_ref[...]
        zc = z - jnp.mean(z, axis=-1, keepdims=True)
        rs = lax.rsqrt(jnp.mean(zc * zc, axis=-1, keepdims=True) + LN_EPS)
        xh = zc * rs
        xh_ref[...] = xh
        rs_ref[...] = rs
        y = xh * g_ref[...] + b_ref[...]
        y_ref[...] = y
        yb_ref[...] = y.astype(BF16)

    row = pl.BlockSpec((tr, D), lambda i: (i, 0))
    par = pl.BlockSpec((1, D), lambda i: (0, 0))
    return pl.pallas_call(
        body, name=name, grid=(S // tr,),
        in_specs=[row, row, par, par],
        out_specs=[row, row, row, pl.BlockSpec((tr, 1), lambda i: (i, 0))],
        out_shape=[jax.ShapeDtypeStruct((S, D), F32), jax.ShapeDtypeStruct((S, D), BF16), jax.ShapeDtypeStruct((S, D), F32),
                   jax.ShapeDtypeStruct((S, 1), F32)],
        compiler_params=_cparams("parallel"),
    )(x, r, g, b)


def _ln_loss(x, r, g, b, alpha, target, name, tr=256):
    S, D = x.shape
    tr = _tile(S, tr)

    def body(x_ref, r_ref, g_ref, b_ref, t_ref, dy_ref, xh_ref, rs_ref, loss_ref):
        z = alpha * x_ref[...] + r_ref[...]
        zc = z - jnp.mean(z, axis=-1, keepdims=True)
        rs = lax.rsqrt(jnp.mean(zc * zc, axis=-1, keepdims=True) + LN_EPS)
        xh = zc * rs
        xh_ref[...] = xh
        rs_ref[...] = rs
        e = (xh * g_ref[...] + b_ref[...]) - t_ref[...]
        dy_ref[...] = e * (1.0 / D)

        @pl.when(pl.program_id(0) == 0)
        def _():
            loss_ref[...] = jnp.zeros_like(loss_ref)

        loss_ref[...] += jnp.sum(jnp.sum(e * e, axis=-1, keepdims=True), axis=0, keepdims=True) * (0.5 / D)

    row = pl.BlockSpec((tr, D), lambda i: (i, 0))
    par = pl.BlockSpec((1, D), lambda i: (0, 0))
    return pl.pallas_call(
        body, name=name, grid=(S // tr,),
        in_specs=[row, row, par, par, row],
        out_specs=[row, row, pl.BlockSpec((tr, 1), lambda i: (i, 0)), pl.BlockSpec((1, 1), lambda i: (0, 0))],
        out_shape=[jax.ShapeDtypeStruct((S, D), F32), jax.ShapeDtypeStruct((S, D), F32), jax.ShapeDtypeStruct((S, 1), F32),
                   jax.ShapeDtypeStruct((1, 1), F32)],
        compiler_params=_cparams("arbitrary"),
    )(x, r, g, b, target)


def _ln_bwd(dy, xh, rs, g, name, tr=256):
    S, D = dy.shape
    tr = _tile(S, tr)

    def body(dy_ref, xh_ref, rs_ref, g_ref, dz_ref, dzb_ref, dg_ref, db_ref):
        dyt = dy_ref[...]
        xh = xh_ref[...]
        dxh = dyt * g_ref[...]
        m1 = jnp.mean(dxh, axis=-1, keepdims=True)
        m2 = jnp.mean(dxh * xh, axis=-1, keepdims=True)
        dz = rs_ref[...] * (dxh - m1 - xh * m2)
        dz_ref[...] = dz
        dzb_ref[...] = dz.astype(BF16)

        @pl.when(pl.program_id(0) == 0)
        def _():
            dg_ref[...] = jnp.zeros_like(dg_ref)
            db_ref[...] = jnp.zeros_like(db_ref)

        dg_ref[...] += jnp.sum(dyt * xh, axis=0, keepdims=True)
        db_ref[...] += jnp.sum(dyt, axis=0, keepdims=True)

    row = pl.BlockSpec((tr, D), lambda i: (i, 0))
    par = pl.BlockSpec((1, D), lambda i: (0, 0))
    return pl.pallas_call(
        body, name=name, grid=(S // tr,),
        in_specs=[row, row, pl.BlockSpec((tr, 1), lambda i: (i, 0)), par],
        out_specs=[row, row, par, par],
        out_shape=[jax.ShapeDtypeStruct((S, D), F32), jax.ShapeDtypeStruct((S, D), BF16),
                   jax.ShapeDtypeStruct((1, D), F32), jax.ShapeDtypeStruct((1, D), F32)],
        compiler_params=_cparams("arbitrary"),
    )(dy, xh, rs, g)


SUBLANES = 8


def _tap_reads(ext, shifted, tt, offset_of_tap):
    for r in range(SUBLANES):
        taps = [k for k in range(CONV_WIDTH) if offset_of_tap(k) % SUBLANES == r]
        if r == 0:
            src = ext
        else:
            shifted[...] = ext[pl.ds(r, tt + CONV_HALO - SUBLANES), :]
            src = shifted
        for k in taps:
            yield src, k, offset_of_tap(k) - r


def _conv_fwd(u, w, cb, lg, lb, name, carry=None, tt=512):
    S = u.shape[0]
    C = w.shape[1]
    tt = _tile(S, tt)
    hpt = tt // CONV_HALO

    def body(a_ref, g_ref, ap_ref, gp_ref, w_ref, cb_ref, lg_ref, lb_ref, out_ref, cv_ref, hext, shifted):
        i = pl.program_id(0)
        hext[pl.ds(CONV_HALO, tt), :] = a_ref[...] * _sigmoid(g_ref[...])
        hp = ap_ref[...] * _sigmoid(gp_ref[...])
        hext[pl.ds(0, CONV_HALO), :] = jnp.where(i > 0, hp, 0.0)
        acc = jnp.zeros((tt, C), F32)
        for src, k, row0 in _tap_reads(hext, shifted, tt, lambda k: CONV_HALO - (CONV_WIDTH - 1) + k):
            acc = acc + w_ref[pl.ds(k, 1), :] * src[pl.ds(row0, tt), :]
        cv = acc + cb_ref[...]
        cv_ref[...] = cv
        zc = cv - jnp.mean(cv, axis=-1, keepdims=True)
        n = zc * lax.rsqrt(jnp.mean(zc * zc, axis=-1, keepdims=True) + LN_EPS) * lg_ref[...] + lb_ref[...]
        out_ref[...] = (n * _sigmoid(n)).astype(BF16)

    cur = lambda cb_: pl.BlockSpec((tt, C), lambda i: (i, cb_))
    prev = lambda cb_: pl.BlockSpec((CONV_HALO, C), lambda i: (jnp.maximum(i * hpt - 1, 0), cb_))
    par = pl.BlockSpec((1, C), lambda i: (0, 0))
    row = pl.BlockSpec((tt, C), lambda i: (i, 0))
    return _call(
        body, name=name, grid=(S // tt,),
        in_specs=[cur(0), cur(1), prev(0), prev(1), pl.BlockSpec((CONV_WIDTH, C), lambda i: (0, 0)), par, par, par],
        out_specs=[row, row],
        out_shape=[jax.ShapeDtypeStruct((S, C), BF16), jax.ShapeDtypeStruct((S, C), F32)],
        scratch_shapes=[pltpu.VMEM((tt + CONV_HALO, C), F32), pltpu.VMEM((tt + CONV_HALO - SUBLANES, C), F32)],
        sem=("parallel",), args=(u, u, u, u, w, cb, lg, lb), carry=carry)


def _conv_bwd_norm(da, da_col, cv, lg, lb, name, tt=512):
    S, C = cv.shape
    tt = _tile(S, tt)

    def body(da_ref, cv_ref, lg_ref, lb_ref, dcv_ref, dlg_ref, dlb_ref, dcb_ref):
        cv = cv_ref[...]
        zc = cv - jnp.mean(cv, axis=-1, keepdims=True)
        rs = lax.rsqrt(jnp.mean(zc * zc, axis=-1, keepdims=True) + LN_EPS)
        xh = zc * rs
        n = xh * lg_ref[...] + lb_ref[...]
        sg = _sigmoid(n)
        dn = da_ref[...] * sg * (1.0 + n * (1.0 - sg))
        dxh = dn * lg_ref[...]
        m1 = jnp.mean(dxh, axis=-1, keepdims=True)
        m2 = jnp.mean(dxh * xh, axis=-1, keepdims=True)
        dcv = rs * (dxh - m1 - xh * m2)
        dcv_ref[...] = dcv

        @pl.when(pl.program_id(0) == 0)
        def _():
            dlg_ref[...] = jnp.zeros_like(dlg_ref)
            dlb_ref[...] = jnp.zeros_like(dlb_ref)
            dcb_ref[...] = jnp.zeros_like(dcb_ref)

        dlg_ref[...] += jnp.sum(dn * xh, axis=0, keepdims=True)
        dlb_ref[...] += jnp.sum(dn, axis=0, keepdims=True)
        dcb_ref[...] += jnp.sum(dcv, axis=0, keepdims=True)

    row = pl.BlockSpec((tt, C), lambda i: (i, 0))
    par = pl.BlockSpec((1, C), lambda i: (0, 0))
    return pl.pallas_call(
        body, name=name, grid=(S // tt,),
        in_specs=[pl.BlockSpec((tt, C), lambda i: (i, da_col)), row, par, par],
        out_specs=[row, par, par, par],
        out_shape=[jax.ShapeDtypeStruct((S, C), F32)] + [jax.ShapeDtypeStruct((1, C), F32)] * 3,
        compiler_params=_cparams("arbitrary"),
    )(da, cv, lg, lb)


def _conv_bwd_taps(dcv, u, w, name, tt=512):
    S, C = dcv.shape
    tt = _tile(S, tt)
    hpt = tt // CONV_HALO
    nt = S // tt
    WPAD = 32

    def body(dc_ref, dn_ref, a_ref, g_ref, ap_ref, gp_ref, w_ref, dag_ref, dw_ref, hext, dext, shifted):
        i = pl.program_id(0)
        a = a_ref[...]
        sg = _sigmoid(g_ref[...])
        hext[pl.ds(CONV_HALO, tt), :] = a * sg
        hp = ap_ref[...] * _sigmoid(gp_ref[...])
        hext[pl.ds(0, CONV_HALO), :] = jnp.where(i > 0, hp, 0.0)
        dc = dc_ref[...]
        dext[pl.ds(0, tt), :] = dc
        dext[pl.ds(tt, CONV_HALO), :] = jnp.where(i < nt - 1, dn_ref[...], 0.0)

        @pl.when(i == 0)
        def _():
            dw_ref[...] = jnp.zeros_like(dw_ref)

        dh = jnp.zeros((tt, C), F32)
        for src, k, row0 in _tap_reads(dext, shifted, tt, lambda k: CONV_WIDTH - 1 - k):
            dh = dh + w_ref[pl.ds(k, 1), :] * src[pl.ds(row0, tt), :]
        for src, k, row0 in _tap_reads(hext, shifted, tt, lambda k: CONV_HALO - (CONV_WIDTH - 1) + k):
            dw_ref[pl.ds(k, 1), :] += jnp.sum(dc * src[pl.ds(row0, tt), :], axis=0, keepdims=True)
        dag_ref[:, :C] = dh * sg
        dag_ref[:, C:] = dh * a * sg * (1.0 - sg)

    row = pl.BlockSpec((tt, C), lambda i: (i, 0))
    nxt = pl.BlockSpec((CONV_HALO, C), lambda i: (jnp.minimum((i + 1) * hpt, S // CONV_HALO - 1), 0))
    cur = lambda cb_: pl.BlockSpec((tt, C), lambda i: (i, cb_))
    prev = lambda cb_: pl.BlockSpec((CONV_HALO, C), lambda i: (jnp.maximum(i * hpt - 1, 0), cb_))
    dag, dw = pl.pallas_call(
        body, name=name, grid=(nt,),
        in_specs=[row, nxt, cur(0), cur(1), prev(0), prev(1), pl.BlockSpec((CONV_WIDTH, C), lambda i: (0, 0))],
        out_specs=[pl.BlockSpec((tt, 2 * C), lambda i: (i, 0)), pl.BlockSpec((WPAD, C), lambda i: (0, 0))],
        out_shape=[jax.ShapeDtypeStruct((S, 2 * C), F32), jax.ShapeDtypeStruct((WPAD, C), F32)],
        scratch_shapes=[pltpu.VMEM((tt + CONV_HALO, C), F32)] * 2 + [pltpu.VMEM((tt + CONV_HALO - SUBLANES, C), F32)],
        compiler_params=_cparams("arbitrary"),
    )(dcv, dcv, u, u, u, u, w)
    return dag, dw[:CONV_WIDTH]


def _lower_bound(logits):
    e = jnp.exp(logits - jnp.max(logits, axis=0, keepdims=True))
    p = e / jnp.sum(e, axis=0, keepdims=True)
    return p[0:1, :], p


def _tri(n, lower):
    r = lax.broadcasted_iota(jnp.int32, (n, n), 0)
    c = lax.broadcasted_iota(jnp.int32, (n, n), 1)
    return ((c <= r) if lower else (c >= r)).astype(F32)


def _hgrn_gates(xq, xf, lb):
    sq = _sigmoid(xq)
    q = xq * sq
    sf = _sigmoid(xf)
    f = lb + (1.0 - lb) * sf
    logf = jnp.log(f)
    L = _dot(_tri(CHUNK, True), logf, NN, HI)
    Lend = L[CHUNK - 1:CHUNK, :]
    eL = jnp.exp(L)
    enL = jnp.exp(-L)
    eLe = jnp.exp(Lend - L)
    kk = 1.0 - f
    return dict(sq=sq, q=q, sf=sf, f=f, L=L, Lend=Lend, eL=eL, enL=enL, eLe=eLe, kk=kk,
                qe=q * eL, ke=kk * enL, kd=kk * eLe)


def _hgrn_fwd(u, lbl, gn, name, tb=256):
    S = u.shape[0]
    W = gn.shape[1]
    H = W // HEAD_DIM
    tb = _tile(S, tb)
    cpb = tb // CHUNK
    nc = S // CHUNK
    R = lbl.shape[0]

    def body(q_ref, f_ref, i_ref, g_ref, lbl_ref, gn_ref, out_ref, raw_ref, st_ref, state):
        @pl.when(pl.program_id(0) == 0)
        def _():
            state[...] = jnp.zeros_like(state)

        lb, _ = _lower_bound(lbl_ref[...])
        tril = _tri(CHUNK, True) > 0.5

        def chunk(c, carry):
            rows = pl.ds(pl.multiple_of(c * CHUNK, CHUNK), CHUNK)
            G = _hgrn_gates(q_ref[rows, :], f_ref[rows, :], lb)
            v = i_ref[rows, :]
            xg = g_ref[rows, :]
            outs = []
            for h in range(H):
                ln = slice(h * HEAD_DIM, (h + 1) * HEAD_DIM)
                qe, ke, kd, vh = G["qe"][:, ln].astype(BF16), G["ke"][:, ln].astype(BF16), G["kd"][:, ln].astype(BF16), v[:, ln].astype(BF16)
                st = state[h]
                st_ref[c, h] = st
                A = jnp.where(tril, _dot(qe, ke, NT), 0.0)
                o = _dot(A.astype(BF16), vh, NN) + _dot(qe, st.astype(BF16), NT)
                state[h] = jnp.exp(G["Lend"][:, ln]) * st + _dot(vh, kd, TN)
                outs.append(o)
            o = jnp.concatenate(outs, axis=1)
            raw_ref[rows, :] = o
            ns = []
            for h in range(H):
                oh = outs[h]
                ns.append(oh * lax.rsqrt(jnp.mean(oh * oh, axis=-1, keepdims=True) + LN_EPS))
            n = jnp.concatenate(ns, axis=1)
            out_ref[rows, :] = (n * gn_ref[...] * (xg * _sigmoid(xg))).astype(BF16)
            return carry

        lax.fori_loop(0, cpb, chunk, 0)

    col = lambda cb_: pl.BlockSpec((tb, W), lambda i: (i, cb_))
    row = pl.BlockSpec((tb, W), lambda i: (i, 0))
    return pl.pallas_call(
        body, name=name, grid=(S // tb,),
        in_specs=[col(2), col(3), col(4), col(5), pl.BlockSpec((R, W), lambda i: (0, 0)), pl.BlockSpec((1, W), lambda i: (0, 0))],
        out_specs=[row, row, pl.BlockSpec((cpb, H, HEAD_DIM, HEAD_DIM), lambda i: (i, 0, 0, 0))],
        out_shape=[jax.ShapeDtypeStruct((S, W), BF16), jax.ShapeDtypeStruct((S, W), F32),
                   jax.ShapeDtypeStruct((nc, H, HEAD_DIM, HEAD_DIM), F32)],
        scratch_shapes=[pltpu.VMEM((H, HEAD_DIM, HEAD_DIM), F32)],
        compiler_params=_cparams("arbitrary"),
    )(u, u, u, u, lbl, gn)


def _hgrn_bwd(dout, dout_col, dag, u, raw, states, lbl, gn, name, tb=256):
    S = u.shape[0]
    W = gn.shape[1]
    H = W // HEAD_DIM
    tb = _tile(S, tb)
    cpb = tb // CHUNK
    nb = S // tb
    R = lbl.shape[0]

    def body(do_ref, dag_ref, q_ref, f_ref, i_ref, g_ref, raw_ref, st_ref, lbl_ref, gn_ref,
             du_ref, dgn_ref, dlbl_ref, dstate, dlb_acc):
        @pl.when(pl.program_id(0) == 0)
        def _():
            dstate[...] = jnp.zeros_like(dstate)
            dlb_acc[...] = jnp.zeros_like(dlb_acc)
            dgn_ref[...] = jnp.zeros_like(dgn_ref)

        du_ref[:, pl.ds(0, 2 * W)] = dag_ref[...]
        dq_cols, df_cols, di_cols, dg_cols = (pl.ds(c * W, W) for c in (2, 3, 4, 5))

        lb, p = _lower_bound(lbl_ref[...])
        tril = _tri(CHUNK, True) > 0.5
        triu = _tri(CHUNK, False)
        gn_row = gn_ref[...]

        def chunk(cc, carry):
            c = cpb - 1 - cc
            rows = pl.ds(pl.multiple_of(c * CHUNK, CHUNK), CHUNK)
            xq, xf = q_ref[rows, :], f_ref[rows, :]
            G = _hgrn_gates(xq, xf, lb)
            v = i_ref[rows, :]
            xg = g_ref[rows, :]
            dy = do_ref[rows, :]
            o = raw_ref[rows, :]
            sgg = _sigmoid(xg)
            silu_g = xg * sgg
            do_parts, n_parts = [], []
            for h in range(H):
                ln = slice(h * HEAD_DIM, (h + 1) * HEAD_DIM)
                oh = o[:, ln]
                r = lax.rsqrt(jnp.mean(oh * oh, axis=-1, keepdims=True) + LN_EPS)
                nh = oh * r
                dn = dy[:, ln] * gn_row[:, ln] * silu_g[:, ln]
                do_parts.append(r * (dn - nh * jnp.mean(dn * nh, axis=-1, keepdims=True)))
                n_parts.append(nh)
            n = jnp.concatenate(n_parts, axis=1)
            dgn_ref[...] += jnp.sum(dy * n * silu_g, axis=0, keepdims=True)
            du_ref[rows, dg_cols] = dy * n * gn_row * sgg * (1.0 + xg * (1.0 - sgg))
            dqe_p, dke_p, dkd_p, dv_p, dle_p = [], [], [], [], []
            for h in range(H):
                ln = slice(h * HEAD_DIM, (h + 1) * HEAD_DIM)
                qe, ke, kd, vh = G["qe"][:, ln].astype(BF16), G["ke"][:, ln].astype(BF16), G["kd"][:, ln].astype(BF16), v[:, ln].astype(BF16)
                doh = do_parts[h].astype(BF16)
                st = st_ref[c, h]
                dst = dstate[h]
                dec = jnp.exp(G["Lend"][:, ln])
                A = jnp.where(tril, _dot(qe, ke, NT), 0.0).astype(BF16)
                dA = jnp.where(tril, _dot(doh, vh, NT), 0.0).astype(BF16)
                dv_p.append(_dot(A, doh, TN) + _dot(kd, dst.astype(BF16), NT))
                dqe_p.append(_dot(dA, ke, NN) + _dot(doh, st.astype(BF16), NN))
                dke_p.append(_dot(dA, qe, TN))
                dkd_p.append(_dot(vh, dst.astype(BF16), NN))
                dle_p.append(jnp.sum(dst * dec * st, axis=0, keepdims=True))
                dstate[h] = dec * dst + _dot(doh, qe, TN)
            dqe = jnp.concatenate(dqe_p, axis=1)
            dke = jnp.concatenate(dke_p, axis=1)
            dkd = jnp.concatenate(dkd_p, axis=1)
            qe_r, ke_r, kd_r = _round_bf16(G["qe"]), _round_bf16(G["ke"]), _round_bf16(G["kd"])
            dLend = jnp.concatenate(dle_p, axis=1) + jnp.sum(dkd * kd_r, axis=0, keepdims=True)
            dL = dqe * qe_r - dke * ke_r - dkd * kd_r
            dlogf = _dot(triu, dL, NN, HI) + dLend
            dkk = dke * G["enL"] + dkd * G["eLe"]
            dfv = dlogf / G["f"] - dkk
            sf = G["sf"]
            du_ref[rows, df_cols] = dfv * (1.0 - lb) * sf * (1.0 - sf)
            dlb_acc[...] += jnp.sum(dfv * (1.0 - sf), axis=0, keepdims=True)
            sq = G["sq"]
            du_ref[rows, dq_cols] = dqe * G["eL"] * sq * (1.0 + xq * (1.0 - sq))
            du_ref[rows, di_cols] = jnp.concatenate(dv_p, axis=1)
            return carry

        lax.fori_loop(0, cpb, chunk, 0)

        onehot0 = (lax.broadcasted_iota(jnp.int32, (R, W), 0) == 0).astype(F32)
        dlbl_ref[...] = p * (onehot0 - p[0:1, :]) * dlb_acc[...]

    rev = lambda i: nb - 1 - i
    col = lambda cb_: pl.BlockSpec((tb, W), lambda i: (rev(i), cb_))
    row = pl.BlockSpec((tb, W), lambda i: (rev(i), 0))
    par = pl.BlockSpec((1, W), lambda i: (0, 0))
    parR = pl.BlockSpec((R, W), lambda i: (0, 0))
    return pl.pallas_call(
        body, name=name, grid=(nb,),
        in_specs=[pl.BlockSpec((tb, W), lambda i: (rev(i), dout_col)), pl.BlockSpec((tb, 2 * W), lambda i: (rev(i), 0)),
                  col(2), col(3), col(4), col(5), row,
                  pl.BlockSpec((cpb, H, HEAD_DIM, HEAD_DIM), lambda i: (rev(i), 0, 0, 0)), parR, par],
        out_specs=[pl.BlockSpec((tb, 6 * W), lambda i: (rev(i), 0)), par, parR],
        out_shape=[jax.ShapeDtypeStruct((S, 6 * W), F32), jax.ShapeDtypeStruct((1, W), F32), jax.ShapeDtypeStruct((R, W), F32)],
        scratch_shapes=[pltpu.VMEM((H, HEAD_DIM, HEAD_DIM), F32), pltpu.VMEM((1, W), F32)],
        compiler_params=_cparams("arbitrary"),
    )(dout, dag, u, u, u, u, raw, states, lbl, gn)


def _fgate_fwd(xb, wft, bf, name, ts=512):
    S, D = xb.shape
    H = wft.shape[0]
    ts = _tile(S, ts)

    def body(x_ref, w_ref, b_ref, lg_ref, F_ref, carry):
        @pl.when(pl.program_id(0) == 0)
        def _():
            carry[...] = jnp.zeros_like(carry)

        lg = _dot(_round_bf16(w_ref[...]), _round_bf16(x_ref[...]), NT, HI) + b_ref[...]
        lg_ref[...] = lg
        ls = jnp.minimum(lg, 0.0) - jnp.log(1.0 + jnp.exp(-jnp.abs(lg)))
        F = _dot(ls, _tri(ts, False), NN, HI) + carry[...]
        F_ref[...] = F
        carry[...] = F[:, ts - 1:ts]

    return pl.pallas_call(
        body, name=name, grid=(S // ts,),
        in_specs=[pl.BlockSpec((ts, D), lambda i: (i, 0)), pl.BlockSpec((H, D), lambda i: (0, 0)), pl.BlockSpec((H, 1), lambda i: (0, 0))],
        out_specs=[pl.BlockSpec((H, ts), lambda i: (0, i))] * 2,
        out_shape=[jax.ShapeDtypeStruct((H, S), F32)] * 2,
        scratch_shapes=[pltpu.VMEM((H, 1), F32)],
        compiler_params=_cparams("arbitrary"),
    )(xb, wft, bf)


def _fgate_bwd(dF, lg, name, ts=512):
    H, S = dF.shape
    ts = _tile(S, ts)
    nb = S // ts

    def body(dF_ref, lg_ref, dl_ref, db_ref, carry):
        @pl.when(pl.program_id(0) == 0)
        def _():
            carry[...] = jnp.zeros_like(carry)
            db_ref[...] = jnp.zeros_like(db_ref)

        dls = _dot(dF_ref[...], _tri(ts, True), NN, HI) + carry[...]
        carry[...] = dls[:, 0:1]
        dl = dls * _sigmoid(-lg_ref[...])
        dl_ref[...] = dl
        db_ref[...] += jnp.sum(dl, axis=1, keepdims=True)

    blk = pl.BlockSpec((H, ts), lambda i: (0, nb - 1 - i))
    return pl.pallas_call(
        body, name=name, grid=(nb,),
        in_specs=[blk, blk],
        out_specs=[blk, pl.BlockSpec((H, 1), lambda i: (0, 0))],
        out_shape=[jax.ShapeDtypeStruct((H, S), F32), jax.ShapeDtypeStruct((H, 1), F32)],
        scratch_shapes=[pltpu.VMEM((H, 1), F32)],
        compiler_params=_cparams("arbitrary"),
    )(dF, lg)


def _causal_keep(i, j, tq, tk):
    rows = lax.broadcasted_iota(jnp.int32, (tq, tk), 0)
    cols = lax.broadcasted_iota(jnp.int32, (tq, tk), 1)
    return jnp.logical_or(j < i, cols <= rows)


def _causal_grid(S, t):
    n = S // t
    assert n % 2 == 0, (S, t)

    def by_query(r, c):
        second = c > r
        return jnp.where(second, n - 1 - r, r), jnp.where(second, c - r - 1, c)

    def by_key(r, c):
        second = c >= n - r
        return jnp.where(second, c - 1, r + c), jnp.where(second, n - 1 - r, r)

    return n, (n // 2, n + 1), by_query, by_key


def _fox_scores(q_ref, k_ref, fk_ref, i, j, t, scale):
    s = _dot(q_ref[...].astype(BF16), k_ref[...].astype(BF16), NT) * scale - fk_ref[...]
    return lax.cond(i == j, lambda x: jnp.where(_causal_keep(i, j, t, t), x, NEG), lambda x: x, s)


def _softmax_bwd(p, pb, dp, delta):
    return pb.astype(F32) * dp - p * delta


FOX_TILE = 1024


def _fox_tile(S):
    return _tile(S, min(FOX_TILE, S // 2))


def _fox_fwd(u, F3, H, name, carry=None):
    S = u.shape[0]
    W = H * HEAD_DIM
    t = _fox_tile(S)
    n, tri, ij, _ = _causal_grid(S, t)
    scale = HEAD_DIM ** -0.5

    def body(q_ref, k_ref, v_ref, fk_ref, o_ref, ob_ref, lse_ref, m_s, l_s, acc_s):
        i, j = ij(pl.program_id(1), pl.program_id(2))

        @pl.when(j == 0)
        def _():
            m_s[...] = jnp.full_like(m_s, NEG)
            l_s[...] = jnp.zeros_like(l_s)
            acc_s[...] = jnp.zeros_like(acc_s)

        s = _fox_scores(q_ref, k_ref, fk_ref, i, j, t, scale)
        m_new = jnp.maximum(m_s[...], jnp.max(s, axis=-1, keepdims=True))
        a = jnp.exp(m_s[...] - m_new)
        p = jnp.exp(s - m_new)
        hi = p.astype(BF16)
        lo = (p - hi.astype(F32)).astype(BF16)
        v = v_ref[...].astype(BF16)
        l_s[...] = a * l_s[...] + jnp.sum(p, axis=-1, keepdims=True)
        acc_s[...] = a * acc_s[...] + (_dot(hi, v, NN) + _dot(lo, v, NN))
        m_s[...] = m_new

        @pl.when(j == i)
        def _():
            o = acc_s[...] / l_s[...]
            o_ref[...] = o
            ob_ref[...] = o.astype(BF16)
            lse_ref[...] = m_s[...] + jnp.log(l_s[...])

    qblk = pl.BlockSpec((t, HEAD_DIM), lambda h, r, c: (ij(r, c)[0], h))
    kv = lambda off: pl.BlockSpec((t, HEAD_DIM), lambda h, r, c: (ij(r, c)[1], off * H + h))
    outs, carried = _call(
        body, name=name, grid=(H,) + tri,
        in_specs=[qblk, kv(1), kv(2), pl.BlockSpec((None, 1, t), lambda h, r, c: (h, 0, ij(r, c)[1]))],
        out_specs=[qblk, qblk, pl.BlockSpec((None, t, 1), lambda h, r, c: (h, ij(r, c)[0], 0))],
        out_shape=[jax.ShapeDtypeStruct((S, W), F32), jax.ShapeDtypeStruct((S, W), BF16), jax.ShapeDtypeStruct((H, S, 1), F32)],
        scratch_shapes=[pltpu.VMEM((t, 1), F32), pltpu.VMEM((t, 1), F32), pltpu.VMEM((t, HEAD_DIM), F32)],
        sem=("parallel", "parallel", "arbitrary"), args=(u, u, u, F3), carry=carry)
    return outs[0], outs[1], outs[2], carried


def _fox_bwd(u, F3, o, do, do_off, lse, H, name, carry=None):
    S = u.shape[0]
    W = H * HEAD_DIM
    t = _fox_tile(S)
    n, tri, _, ij = _causal_grid(S, t)
    scale = HEAD_DIM ** -0.5

    def body(q_ref, k_ref, v_ref, fk_ref, o_ref, do_ref, lse_ref, dq_ref, dk_ref, dv_ref, dF_ref, dk_s, dv_s, dF_s):
        r, c = pl.program_id(1), pl.program_id(2)
        i, j = ij(r, c)

        @pl.when(jnp.logical_and(r == 0, c == 0))
        def _():
            dq_ref[...] = jnp.zeros_like(dq_ref)

        @pl.when(i == j)
        def _():
            dk_s[...] = jnp.zeros_like(dk_s)
            dv_s[...] = jnp.zeros_like(dv_s)
            dF_s[...] = jnp.zeros_like(dF_s)

        q = q_ref[...].astype(BF16)
        k = k_ref[...].astype(BF16)
        dob = do_ref[...].astype(BF16)
        p = jnp.exp(_fox_scores(q_ref, k_ref, fk_ref, i, j, t, scale) - lse_ref[...])
        dp = _dot(dob, v_ref[...].astype(BF16), NT)
        delta = jnp.sum(dob.astype(F32) * o_ref[...], axis=-1, keepdims=True)
        ds = p * (dp - delta)
        dsb = ds.astype(BF16)
        dv_s[...] += _dot(p.astype(BF16), dob, TN)
        dk_s[...] += _dot(dsb, q, TN)
        dF_s[...] -= jnp.sum(ds, axis=0, keepdims=True)
        rows = pl.ds(pl.multiple_of(i * t, t), t)
        dq_ref[rows, :] += _dot(dsb, k, NN) * scale

        @pl.when(i == n - 1)
        def _():
            dk_ref[...] = dk_s[...] * scale
            dv_ref[...] = dv_s[...]
            dF_ref[...] = dF_s[...]

    qblk = pl.BlockSpec((t, HEAD_DIM), lambda h, r, c: (ij(r, c)[0], h))
    kv = lambda off: pl.BlockSpec((t, HEAD_DIM), lambda h, r, c: (ij(r, c)[1], off * H + h))
    oblk = pl.BlockSpec((t, HEAD_DIM), lambda h, r, c: (ij(r, c)[1], h))
    fblk = pl.BlockSpec((None, 1, t), lambda h, r, c: (h, 0, ij(r, c)[1]))
    return _call(
        body, name=name, grid=(H,) + tri,
        in_specs=[qblk, kv(1), kv(2), fblk, qblk, pl.BlockSpec((t, HEAD_DIM), lambda h, r, c: (ij(r, c)[0], do_off + h)),
                  pl.BlockSpec((None, t, 1), lambda h, r, c: (h, ij(r, c)[0], 0))],
        out_specs=[pl.BlockSpec((S, HEAD_DIM), lambda h, r, c: (0, h)), oblk, oblk, fblk],
        out_shape=[jax.ShapeDtypeStruct((S, W), F32)] * 3 + [jax.ShapeDtypeStruct((H, 1, S), F32)],
        scratch_shapes=[pltpu.VMEM((t, HEAD_DIM), F32), pltpu.VMEM((t, HEAD_DIM), F32), pltpu.VMEM((1, t), F32)],
        sem=("parallel", "arbitrary", "arbitrary"), args=(u, u, u, F3, o, do, lse), carry=carry)


def _rel_index_matrix():
    a = lax.broadcasted_iota(jnp.int32, (REL_TABLE, CA_SKEW), 0)
    j = lax.broadcasted_iota(jnp.int32, (REL_TABLE, CA_SKEW), 1)
    rel = jnp.where(j < CA_WIN, CA_LEFT_CHUNKS * CHUNK - j, REL_CLIP)
    idx = jnp.clip(jnp.minimum(rel, REL_CLIP) + (CHUNK - 1), 0, REL_TABLE - 1)
    return (a == idx).astype(F32)


def _skew(x, sign):
    r = lax.broadcasted_iota(jnp.int32, x.shape, 0)
    for b in range(int(math.log2(CA_TILE))):
        sh = (1 << b) if sign > 0 else CA_SKEW - (1 << b)
        x = jnp.where((r >> b) & 1 == 1, pltpu.roll(x, sh, 1), x)
    return x


def _band_valid():
    shift = int(math.log2(CHUNK))
    r = lax.broadcasted_iota(jnp.int32, (CA_TILE, CA_WIN), 0) >> shift
    m = lax.broadcasted_iota(jnp.int32, (CA_TILE, CA_WIN), 1) >> shift
    return jnp.logical_and(m >= r, m <= r + CA_LEFT_CHUNKS)


def _relbias_fwd(table, name):
    H = table.shape[0]

    def body(t_ref, b_ref):
        rowv = _dot(t_ref[...], _rel_index_matrix(), NN, HI)
        valid = _band_valid()
        for h in range(H):
            x = _skew(jnp.broadcast_to(rowv[h:h + 1, :], (CA_TILE, CA_SKEW)), +1)
            b_ref[h] = jnp.where(valid, x[:, :CA_WIN], NEG)

    return pl.pallas_call(
        body, name=name,
        out_shape=jax.ShapeDtypeStruct((H, CA_TILE, CA_WIN), F32),
        compiler_params=pltpu.CompilerParams(vmem_limit_bytes=VMEM_LIMIT_V7X),
    )(table)


def _relbias_bwd(dB, name):
    H = dB.shape[0]
    HP = -(-H // 8) * 8

    def body(d_ref, dt_ref, rows):
        rows[...] = jnp.zeros_like(rows)
        for h in range(H):
            x = jnp.concatenate([d_ref[h], jnp.zeros((CA_TILE, CA_SKEW - CA_WIN), F32)], axis=1)
            rows[pl.ds(h, 1), :] = jnp.sum(_skew(x, -1), axis=0, keepdims=True)
        dt_ref[...] = _dot(rows[...], _rel_index_matrix(), NT, HI)[:H]

    return pl.pallas_call(
        body, name=name,
        out_shape=jax.ShapeDtypeStruct((H, REL_TABLE), F32),
        scratch_shapes=[pltpu.VMEM((HP, CA_SKEW), F32)],
        compiler_params=pltpu.CompilerParams(vmem_limit_bytes=VMEM_LIMIT_V7X),
    )(dB)


CA_PIECES = CA_WIN // CA_TILE


def _ca_fwd(u, bias, H, name):
    S = u.shape[0]
    W = H * HEAD_DIM
    T = CA_TILE
    n = S // T
    scale = HEAD_DIM ** -0.5

    def body(q_ref, k0, k1, k2, v0, v1, v2, b_ref, o_ref, ob_ref, lse_ref):
        i = pl.program_id(1)
        q = q_ref[...].astype(BF16)
        ss = []
        for pce, k_ref in enumerate((k0, k1, k2)):
            s = _dot(q, k_ref[...].astype(BF16), NT) * scale + b_ref[:, pce * T:(pce + 1) * T]
            ss.append(jnp.where(i + pce >= CA_PIECES - 1, s, NEG))
        m = jnp.maximum(jnp.maximum(jnp.max(ss[0], -1, keepdims=True), jnp.max(ss[1], -1, keepdims=True)), jnp.max(ss[2], -1, keepdims=True))
        ps = [jnp.exp(s - m) for s in ss]
        l = sum(jnp.sum(p, -1, keepdims=True) for p in ps)
        inv = 1.0 / l
        o = sum(_dot((p * inv).astype(BF16), v_ref[...].astype(BF16), NN) for p, v_ref in zip(ps, (v0, v1, v2)))
        o_ref[...] = o
        ob_ref[...] = o.astype(BF16)
        lse_ref[...] = m + jnp.log(l)

    qblk = pl.BlockSpec((T, HEAD_DIM), lambda h, i: (i, h))
    kv = lambda off, back: pl.BlockSpec((T, HEAD_DIM), lambda h, i: (jnp.maximum(i - back, 0), off * H + h))
    return pl.pallas_call(
        body, name=name, grid=(H, n),
        in_specs=[qblk, kv(1, 2), kv(1, 1), kv(1, 0), kv(2, 2), kv(2, 1), kv(2, 0),
                  pl.BlockSpec((None, T, CA_WIN), lambda h, i: (h, 0, 0))],
        out_specs=[qblk, qblk, pl.BlockSpec((None, T, 1), lambda h, i: (h, i, 0))],
        out_shape=[jax.ShapeDtypeStruct((S, W), F32), jax.ShapeDtypeStruct((S, W), BF16), jax.ShapeDtypeStruct((H, S, 1), F32)],
        compiler_params=_cparams("parallel", "arbitrary"),
    )(u, u, u, u, u, u, u, bias)


def _ca_bwd(u, bias, o, do, do_off, lse, H, name):
    S = u.shape[0]
    W = H * HEAD_DIM
    T = CA_TILE
    n = S // T
    scale = HEAD_DIM ** -0.5

    def body(q_ref, k0, k1, k2, v0, v1, v2, b_ref, o_ref, do_ref, lse_ref, dq_ref, dk_ref, dv_ref, db_ref):
        i = pl.program_id(1)

        @pl.when(i == 0)
        def _():
            db_ref[...] = jnp.zeros_like(db_ref)
            dk_ref[...] = jnp.zeros_like(dk_ref)
            dv_ref[...] = jnp.zeros_like(dv_ref)

        q = q_ref[...].astype(BF16)
        dob = do_ref[...].astype(BF16)
        delta = jnp.sum(dob.astype(F32) * o_ref[...], axis=-1, keepdims=True)
        dq = jnp.zeros((T, HEAD_DIM), F32)
        for pce, (k_ref, v_ref) in enumerate(((k0, v0), (k1, v1), (k2, v2))):
            k = k_ref[...].astype(BF16)
            s = _dot(q, k, NT) * scale + b_ref[:, pce * T:(pce + 1) * T]
            p = jnp.where(i + pce >= CA_PIECES - 1, jnp.exp(s - lse_ref[...]), 0.0)
            pb = p.astype(BF16)
            ds = _softmax_bwd(p, pb, _dot(dob, v_ref[...].astype(BF16), NT), delta)
            dsb = ds.astype(BF16)
            db_ref[:, pce * T:(pce + 1) * T] += ds
            dq = dq + _dot(dsb, k, NN)
            rows = pl.ds(pl.multiple_of(jnp.maximum(i - (CA_PIECES - 1) + pce, 0) * T, T), T)
            dk_ref[rows, :] += _dot(dsb, q, TN) * scale
            dv_ref[rows, :] += _dot(pb, dob, TN)
        dq_ref[...] = dq * scale

    qblk = pl.BlockSpec((T, HEAD_DIM), lambda h, i: (i, h))
    kv = lambda off, back: pl.BlockSpec((T, HEAD_DIM), lambda h, i: (jnp.maximum(i - back, 0), off * H + h))
    bblk = pl.BlockSpec((None, T, CA_WIN), lambda h, i: (h, 0, 0))
    head = pl.BlockSpec((S, HEAD_DIM), lambda h, i: (0, h))
    return pl.pallas_call(
        body, name=name, grid=(H, n),
        in_specs=[qblk, kv(1, 2), kv(1, 1), kv(1, 0), kv(2, 2), kv(2, 1), kv(2, 0), bblk,
                  qblk, pl.BlockSpec((T, HEAD_DIM), lambda h, i: (i, do_off + h)), pl.BlockSpec((None, T, 1), lambda h, i: (h, i, 0))],
        out_specs=[qblk, head, head, bblk],
        out_shape=[jax.ShapeDtypeStruct((S, W), F32)] * 3 + [jax.ShapeDtypeStruct((H, T, CA_WIN), F32)],
        compiler_params=_cparams("parallel", "arbitrary"),
    )(u, u, u, u, u, u, u, bias, o, do, lse)


def _sum_parts(parts, name):
    _, R, C = parts.shape

    def body(p_ref, o_ref):
        acc = p_ref[0].astype(F32)
        for d in range(1, N_DEV):
            acc = acc + p_ref[d].astype(F32)
        o_ref[...] = acc

    return pl.pallas_call(
        body, name=name, out_shape=jax.ShapeDtypeStruct((R, C), F32),
        compiler_params=pltpu.CompilerParams(vmem_limit_bytes=VMEM_LIMIT_V7X),
    )(parts)


def _adamw(parts, w, m, v, name, tr=128):
    L, R, C = w.shape
    P = parts[0].shape[0]
    tr = _tile(R, tr)
    c1 = 1.0 / (1.0 - ADAM_B1 ** ADAM_STEP)
    c2 = 1.0 / (1.0 - ADAM_B2 ** ADAM_STEP)

    def body(*refs):
        p_refs = refs[:L]
        w_ref, m_ref, v_ref, g_ref, d_ref, nm_ref, nv_ref = refs[L:]

        def total(p_ref):
            g = p_ref[0].astype(F32)
            for d in range(1, P):
                g = g + p_ref[d].astype(F32)
            return g

        g = _select(pl.program_id(0), [functools.partial(total, r) for r in p_refs])
        nm = ADAM_B1 * m_ref[...] + (1.0 - ADAM_B1) * g
        nv = ADAM_B2 * v_ref[...] + (1.0 - ADAM_B2) * (g * g)
        g_ref[...] = g
        nm_ref[...] = nm
        nv_ref[...] = nv
        d_ref[...] = -ADAM_LR * ((nm * c1) / (jnp.sqrt(nv * c2) + ADAM_EPS) + ADAM_WD * w_ref[...])

    blk = pl.BlockSpec((None, tr, C), lambda l, i: (l, i, 0))
    p_specs = [pl.BlockSpec((P, tr, C), functools.partial(lambda p, l, i: (0, jnp.where(l == p, i, 0), 0), p)) for p in range(L)]
    return pl.pallas_call(
        body, name=name, grid=(L, R // tr),
        in_specs=p_specs + [blk, blk, blk],
        out_specs=[blk] * 4,
        out_shape=[jax.ShapeDtypeStruct((L, R, C), F32)] * 4,
        compiler_params=_cparams("arbitrary", "arbitrary"),
    )(*parts, w, m, v)


def _peer(d):
    x, y, c = lax.axis_index("x"), lax.axis_index("y"), lax.axis_index("c")
    px = (1 - x) if d & 4 else x
    py = (1 - y) if d & 2 else y
    pc = (1 - c) if d & 1 else c
    return (px, py, pc), 4 * px + 2 * py + pc


N_PEER = N_DEV - 1


def _exchange_copies(ins, outs, sems, scatter):
    send_sems, recv_sems, local_sems = sems
    me = 4 * lax.axis_index("x") + 2 * lax.axis_index("y") + lax.axis_index("c")
    starts, waits = [], []
    for t in range(len(ins)):
        loc = pltpu.make_async_copy(ins[t].at[me] if scatter else ins[t], outs[t].at[me], local_sems.at[t])
        starts.append(loc.start)
        waits.append(loc.wait)
        for d in range(1, N_DEV):
            peer, pidx = _peer(d)
            src = ins[t].at[pidx] if scatter else ins[t]
            k = t * N_PEER + d - 1
            common = dict(src_ref=src, send_sem=send_sems.at[k], recv_sem=recv_sems.at[k], device_id=peer, device_id_type=MESH_ID)
            starts.append(pltpu.make_async_remote_copy(dst_ref=outs[t].at[me], **common).start)
            waits.append(pltpu.make_async_remote_copy(dst_ref=outs[t].at[pidx], **common).wait)
    return starts, waits


def _exchange_scratch(n):
    return [pltpu.SemaphoreType.DMA((n * N_PEER,)), pltpu.SemaphoreType.DMA((n * N_PEER,)), pltpu.SemaphoreType.DMA((n,))]


def _exchange_shapes(arrays, scatter):
    return [jax.ShapeDtypeStruct(a.shape if scatter else (N_DEV,) + a.shape, a.dtype) for a in arrays]


def _exchange(arrays, scatter, name):
    n = len(arrays)

    def body(*refs):
        starts, waits = _exchange_copies(refs[:n], refs[n:2 * n], refs[2 * n:], scatter)
        for f in starts:
            f()
        for f in waits:
            f()

    hbm = pl.BlockSpec(memory_space=pltpu.HBM)
    return pl.pallas_call(
        body, name=name,
        in_specs=[hbm] * n, out_specs=[hbm] * n,
        out_shape=_exchange_shapes(arrays, scatter),
        scratch_shapes=_exchange_scratch(n),
        compiler_params=pltpu.CompilerParams(has_side_effects=True),
    )(*arrays)


def _gather_two_level(arrays, name):
    n = len(arrays)

    def body(*refs):
        ins, outs = refs[:n], refs[n:2 * n]
        send_sems, recv_sems, local_sems = refs[2 * n:]
        x, y, c = lax.axis_index("x"), lax.axis_index("y"), lax.axis_index("c")
        me, sibling = (x, y, c), (x, y, 1 - c)
        chips = [(1 - x, y), (x, 1 - y), (1 - x, 1 - y)]

        def copy(t, k, block, to, src=None):
            slot = outs[t].at[4 * block[0] + 2 * block[1] + block[2]]
            return pltpu.make_async_remote_copy(
                src_ref=slot if src is None else src, dst_ref=slot, send_sem=send_sems.at[t * N_PEER + k],
                recv_sem=recv_sems.at[t * N_PEER + k], device_id=to, device_id_type=MESH_ID)

        local, sent = [], []
        for t in range(n):
            loc = pltpu.make_async_copy(ins[t], outs[t].at[4 * x + 2 * y + c], local_sems.at[t])
            loc.start()
            local.append(loc)
            first = [copy(t, 0, me, sibling, src=ins[t])]
            first += [copy(t, 1 + j, me, (*chip, c), src=ins[t]) for j, chip in enumerate(chips)]
            for cp in first:
                cp.start()
            sent += first
        for t in range(n):
            for j, chip in enumerate(chips):
                copy(t, 1 + j, (*chip, c), me).wait_recv()
                passed = copy(t, 4 + j, (*chip, c), sibling)
                passed.start()
                sent.append(passed)
        for t in range(n):
            copy(t, 0, sibling, me).wait_recv()
            for j, chip in enumerate(chips):
                copy(t, 4 + j, (*chip, 1 - c), me).wait_recv()
        for cp in sent:
            cp.wait_send()
        for loc in local:
            loc.wait()

    hbm = pl.BlockSpec(memory_space=pltpu.HBM)
    return pl.pallas_call(
        body, name=name,
        in_specs=[hbm] * n, out_specs=[hbm] * n,
        out_shape=_exchange_shapes(arrays, False),
        scratch_shapes=_exchange_scratch(n),
        compiler_params=pltpu.CompilerParams(has_side_effects=True),
    )(*arrays)


def _call(body, *, name, grid, in_specs, out_specs, out_shape, scratch_shapes, sem, args, carry=None):
    if carry is None:
        outs = pl.pallas_call(body, name=name, grid=grid, in_specs=in_specs, out_specs=out_specs, out_shape=out_shape,
                              scratch_shapes=scratch_shapes, compiler_params=_cparams(*sem))(*args)
        return list(outs), []
    arrays, scatter = carry
    nc, n_in, n_out, n_scr = len(arrays), len(in_specs), len(out_specs), len(scratch_shapes)

    def wrapped(*refs):
        ins, cins = refs[:n_in], refs[n_in:n_in + nc]
        outs, couts = refs[n_in + nc:n_in + nc + n_out], refs[n_in + nc + n_out:n_in + 2 * nc + n_out]
        scr = refs[n_in + 2 * nc + n_out:n_in + 2 * nc + n_out + n_scr]
        ids = [pl.program_id(k) for k in range(len(grid))]
        first = functools.reduce(jnp.logical_and, [i == 0 for i in ids])
        last = functools.reduce(jnp.logical_and, [i == g - 1 for i, g in zip(ids, grid)])
        starts, waits = _exchange_copies(cins, couts, refs[-3:], scatter)

        @pl.when(first)
        def _():
            for f in starts:
                f()

        body(*ins, *outs, *scr)

        @pl.when(last)
        def _():
            for f in waits:
                f()

    hbm = pl.BlockSpec(memory_space=pltpu.HBM)
    res = pl.pallas_call(
        wrapped, name=name, grid=grid,
        in_specs=list(in_specs) + [hbm] * nc, out_specs=list(out_specs) + [hbm] * nc,
        out_shape=list(out_shape) + _exchange_shapes(arrays, scatter),
        scratch_shapes=list(scratch_shapes) + _exchange_scratch(nc),
        compiler_params=_cparams(*(("arbitrary",) * len(grid))),
    )(*args, *arrays)
    return list(res[:n_out]), list(res[n_out:])


def _relu2_fwd(acc):
    r = jnp.maximum(acc, 0.0)
    return acc, r * r


def _relu2_bwd(acc, z):
    return (acc * 2.0 * jnp.maximum(z.astype(F32), 0.0),)


def _add_scaled(scale):
    def epi(acc, t):
        return (acc + scale * t,)
    return epi


def _local_step(x, target, P, plan):
    S, D = x.shape
    H = (D // 2) // HEAD_DIM
    W = H * HEAD_DIM
    alpha = (2 * 2) ** 0.25
    g = {}

    def mm(a, b, *, name, **kw):
        res, carried = _mm(a, b, name=name, carry=plan.carry(name, g), **kw)
        plan.arrived(name, carried, P, g)
        return res

    u0 = mm(x, P["ev_w_in"], mode="nn", name="ev_in", b_blocked=True)
    (a_out, cv), carried = _conv_fwd(u0, P["ev_conv_w"], P["ev_conv_b"], P["ev_conv_ln_g"], P["ev_conv_ln_b"], "conv_fwd",
                                     carry=plan.carry("conv_fwd", g))
    plan.arrived("conv_fwd", carried, P, g)
    b_out, h_raw, h_states = _hgrn_fwd(u0, P["hgrn_lb_logits"], P["ev_gnorm_g"], "hgrn_fwd")
    cat0 = (a_out, b_out)
    mix0 = mm(cat0, P["ev_w_out"], mode="nn", name="ev_out")
    x1, x1b, xh1, rs1 = _ln_fwd(x, mix0, P["ln_mix_g"][0:1], P["ln_mix_b"][0:1], alpha, "ln_mix0")
    z0, hh0 = mm(x1b, P["mlp_w1_0"], mode="nn", name="mlp_up0", out_dtypes=(BF16, BF16), epi=_relu2_fwd, b_blocked=True)
    m0 = mm(hh0, P["mlp_w2_0"], mode="nn", name="mlp_down0")
    x2, x2b, xh2, rs2 = _ln_fwd(x1, m0, P["ln_mlp_g"][0:1], P["ln_mlp_b"][0:1], alpha, "ln_mlp0")

    uc = mm(x2b, P["od_w_c"], mode="nn", name="od_in_c")
    ud = mm(x2b, P["od_w_d"], mode="nn", name="od_in_d")
    f_logit, F = _fgate_fwd(x2b, P["od_w_f_t"], P["fox_b_f"].reshape(H, 1), "fgate_fwd")
    F3 = F.reshape(H, 1, S)
    c_out, c_out_b, c_lse, carried = _fox_fwd(uc, F3, H, "fox_fwd", carry=plan.carry("fox_fwd", g))
    plan.arrived("fox_fwd", carried, P, g)
    bias = _relbias_fwd(P["rel_bias"], "relbias_fwd")
    d_out, d_out_b, d_lse = _ca_fwd(ud, bias, H, "ca_fwd")
    cat1 = (c_out_b, d_out_b)
    mix1 = mm(cat1, P["od_w_out"], mode="nn", name="od_out")
    x3, x3b, xh3, rs3 = _ln_fwd(x2, mix1, P["ln_mix_g"][1:2], P["ln_mix_b"][1:2], alpha, "ln_mix1")
    z1, hh1 = mm(x3b, P["mlp_w1_1"], mode="nn", name="mlp_up1", out_dtypes=(BF16, BF16), epi=_relu2_fwd, b_blocked=True)
    m1 = mm(hh1, P["mlp_w2_1"], mode="nn", name="mlp_down1")
    dy, xh4, rs4, loss = _ln_loss(x3, m1, P["ln_mlp_g"][1:2], P["ln_mlp_b"][1:2], alpha, target, "ln_mlp1_loss")

    dzm1, dzm1b, dg_, db_ = _ln_bwd(dy, xh4, rs4, P["ln_mlp_g"][1:2], "ln_mlp1_bwd")
    g["ln_mlp_g1"], g["ln_mlp_b1"] = dg_, db_
    g["mlp_w2_1"] = mm(hh1, dzm1b, mode="tn", name="mlp_down1_dw", out_dtypes=(BF16,))
    dz1 = mm(dzm1b, P["mlp_w2_1"], mode="nt", name="mlp_down1_dx", out_dtypes=(BF16,), extras=(z1,), epi=_relu2_bwd)
    g["mlp_w1_1"] = mm(x3b, dz1, mode="tn", name="mlp_up1_dw", out_dtypes=(BF16,), out_blocked=True)
    dx3 = mm(dz1, P["mlp_w1_1"], mode="nt", name="mlp_up1_dx", extras=(dzm1,), epi=_add_scaled(alpha), b_blocked=True)
    dzx1, dzx1b, dg_, db_ = _ln_bwd(dx3, xh3, rs3, P["ln_mix_g"][1:2], "ln_mix1_bwd")
    g["ln_mix_g1"], g["ln_mix_b1"] = dg_, db_
    g["od_w_out"] = mm(cat1, dzx1b, mode="tn", name="od_out_dw", out_dtypes=(BF16,))
    dcat1 = mm(dzx1b, P["od_w_out"], mode="nt", name="od_out_dx")
    dq_d, dk_d, dv_d, dbias = _ca_bwd(ud, bias, d_out, dcat1, H, d_lse, H, "ca_bwd")
    g["rel_bias"] = _relbias_bwd(dbias, "relbias_bwd")
    dud = jnp.concatenate([dq_d, dk_d, dv_d], axis=1)
    (dq_c, dk_c, dv_c, dF3), carried = _fox_bwd(uc, F3, c_out, dcat1, 0, c_lse, H, "fox_bwd", carry=plan.carry("fox_bwd", g))
    plan.arrived("fox_bwd", carried, P, g)
    duc = jnp.concatenate([dq_c, dk_c, dv_c], axis=1)
    dfl, dbf = _fgate_bwd(dF3.reshape(H, S), f_logit, "fgate_bwd")
    g["fox_b_f"] = dbf.reshape(1, H)
    g["od_w_c"] = mm(x2b, duc, mode="tn", name="od_in_c_dw", out_dtypes=(BF16,))
    g["od_w_d"] = mm(x2b, dud, mode="tn", name="od_in_d_dw", out_dtypes=(BF16,))
    g["od_w_f_t"] = mm(dfl, x2b, mode="nn", name="od_in_f_dw", out_dtypes=(BF16,), exact_products=True)
    dx2 = mm(duc, P["od_w_c"], mode="nt", name="od_in_c_dx", extras=(dzx1,), epi=_add_scaled(alpha))
    dx2 = mm(dud, P["od_w_d"], mode="nt", name="od_in_d_dx", extras=(dx2,), epi=_add_scaled(1.0))
    dx2 = mm(dfl, P["od_w_f_t"], mode="tn", name="od_in_f_dx", extras=(dx2,), epi=_add_scaled(1.0), exact_products=True)

    dzm0, dzm0b, dg_, db_ = _ln_bwd(dx2, xh2, rs2, P["ln_mlp_g"][0:1], "ln_mlp0_bwd")
    g["ln_mlp_g0"], g["ln_mlp_b0"] = dg_, db_
    g["mlp_w2_0"] = mm(hh0, dzm0b, mode="tn", name="mlp_down0_dw", out_dtypes=(BF16,))
    dz0 = mm(dzm0b, P["mlp_w2_0"], mode="nt", name="mlp_down0_dx", out_dtypes=(BF16,), extras=(z0,), epi=_relu2_bwd)
    g["mlp_w1_0"] = mm(x1b, dz0, mode="tn", name="mlp_up0_dw", out_dtypes=(BF16,), out_blocked=True)
    dx1 = mm(dz0, P["mlp_w1_0"], mode="nt", name="mlp_up0_dx", extras=(dzm0,), epi=_add_scaled(alpha), b_blocked=True)
    dzx0, dzx0b, dg_, db_ = _ln_bwd(dx1, xh1, rs1, P["ln_mix_g"][0:1], "ln_mix0_bwd")
    g["ln_mix_g0"], g["ln_mix_b0"] = dg_, db_
    g["ev_w_out"] = mm(cat0, dzx0b, mode="tn", name="ev_out_dw", out_dtypes=(BF16,))
    dcat0 = mm(dzx0b, P["ev_w_out"], mode="nt", name="ev_out_dx")
    dcv, g["ev_conv_ln_g"], g["ev_conv_ln_b"], g["ev_conv_b"] = _conv_bwd_norm(dcat0, 0, cv, P["ev_conv_ln_g"], P["ev_conv_ln_b"], "conv_bwd_norm")
    dag, g["ev_conv_w"] = _conv_bwd_taps(dcv, u0, P["ev_conv_w"], "conv_bwd_taps")
    du0, g["ev_gnorm_g"], g["hgrn_lb_logits"] = _hgrn_bwd(dcat0, 1, dag, u0, h_raw, h_states, P["hgrn_lb_logits"], P["ev_gnorm_g"], "hgrn_bwd")
    g["ev_w_in"] = mm(x, du0, mode="tn", name="ev_in_dw", out_dtypes=(BF16,), out_blocked=True)
    dx0 = mm(du0, P["ev_w_in"], mode="nt", name="ev_in_dx", extras=(dzx0,), epi=_add_scaled(alpha), b_blocked=True)
    return loss, dx0, g


_NAMES = ['ev_w_in', 'ev_conv_w', 'ev_conv_b', 'ev_conv_ln_g', 'ev_conv_ln_b', 'hgrn_lb_logits', 'ev_gnorm_g', 'ev_w_out',
          'od_w_in', 'fox_b_f', 'rel_bias', 'od_w_out', 'ln_mix_g', 'ln_mix_b', 'mlp_w1', 'mlp_w2', 'ln_mlp_g', 'ln_mlp_b']
_SMALL = ['ev_conv_b', 'ev_conv_ln_g', 'ev_conv_ln_b', 'hgrn_lb_logits', 'ev_gnorm_g', 'fox_b_f', 'ln_mix_g', 'ln_mix_b',
          'ln_mlp_g', 'ln_mlp_b', 'ev_conv_w', 'rel_bias']
_PACK_COLS = 2048


def _cols_to_full(gathered):
    nd, K, n = gathered.shape
    return jnp.transpose(gathered, (1, 0, 2)).reshape(K, nd * n)


def _full_to_cols(full):
    K, N = full.shape
    return jnp.transpose(full.reshape(K, N_DEV, N // N_DEV), (1, 0, 2))


_ROW_SHARDED = ("ev_w_out", "od_w_out", "mlp_w2_0", "mlp_w2_1")
_BLOCKED = ("ev_w_in", "mlp_w1_0", "mlp_w1_1")


def _full_weights(name, gathered, heads):
    if name in _BLOCKED:
        return {name: gathered}
    if name in _ROW_SHARDED:
        return {name: gathered.reshape(-1, gathered.shape[-1])}
    full = _cols_to_full(gathered)
    if name != "od_w_in":
        return {name: full}
    w = heads * HEAD_DIM
    return {"od_w_c": full[:, :3 * w], "od_w_f_t": jnp.transpose(full[:, 3 * w:3 * w + heads]), "od_w_d": full[:, 3 * w + heads:]}


def _grad_blocks(name, g):
    if name in _BLOCKED:
        return g[name]
    if name in _ROW_SHARDED:
        return g[name].reshape(N_DEV, -1, g[name].shape[-1])
    if name == "od_w_in":
        return _full_to_cols(jnp.concatenate([g["od_w_c"], jnp.transpose(g["od_w_f_t"]), g["od_w_d"]], axis=1))
    return _full_to_cols(g[name])


class _Plan:
    GATHER = {"ev_in": ("mlp_w1_0",), "conv_fwd": ("ev_w_out",), "mlp_up0": ("mlp_w2_0",), "mlp_down0": ("od_w_in",), "od_in_c": ("od_w_out",),
              "fox_fwd": ("mlp_w1_1", "mlp_w2_1")}
    SCATTER = {"fox_bwd": ("mlp_w2_1", "mlp_w1_1", "od_w_out"), "mlp_down0_dw": ("od_w_in",),
               "mlp_down0_dx": ("mlp_w2_0",), "mlp_up0_dx": ("mlp_w1_0",), "ev_out_dx": ("ev_w_out",), "ev_in_dx": ("ev_w_in",)}

    def __init__(self, shards, heads):
        self.shards, self.heads = shards, heads

    def carry(self, call, g):
        if call in self.GATHER:
            return [self.shards[n] for n in self.GATHER[call]], False
        if call in self.SCATTER:
            return [_grad_blocks(n, g) for n in self.SCATTER[call]], True
        return None

    def arrived(self, call, carried, P, g):
        if call in self.GATHER:
            for n, gathered in zip(self.GATHER[call], carried):
                P.update(_full_weights(n, gathered, self.heads))
        elif call in self.SCATTER:
            for n, received in zip(self.SCATTER[call], carried):
                g[n] = received


def _pack(parts):
    flat = jnp.concatenate([p.reshape(-1).astype(F32) for p in parts])
    rows = -(-flat.shape[0] // (_PACK_COLS * 8)) * 8
    return jnp.pad(flat, (0, rows * _PACK_COLS - flat.shape[0])).reshape(rows, _PACK_COLS)


def _unpack(packed, shapes):
    flat = packed.reshape(-1)
    out, off = [], 0
    for s in shapes:
        n = math.prod(s)
        out.append(flat[off:off + n].reshape(s))
        off += n
    return out


def kernel(x, ev_w_in, ev_conv_w, ev_conv_b, ev_conv_ln_g, ev_conv_ln_b, hgrn_lb_logits, ev_gnorm_g, ev_w_out, od_w_in, fox_b_f, rel_bias, od_w_out, ln_mix_g, ln_mix_b, mlp_w1, mlp_w2, ln_mlp_g, ln_mlp_b, loss_target, m_ev_w_in, m_ev_conv_w, m_ev_conv_b, m_ev_conv_ln_g, m_ev_conv_ln_b, m_hgrn_lb_logits, m_ev_gnorm_g, m_ev_w_out, m_od_w_in, m_fox_b_f, m_rel_bias, m_od_w_out, m_ln_mix_g, m_ln_mix_b, m_mlp_w1, m_mlp_w2, m_ln_mlp_g, m_ln_mlp_b, v_ev_w_in, v_ev_conv_w, v_ev_conv_b, v_ev_conv_ln_g, v_ev_conv_ln_b, v_hgrn_lb_logits, v_ev_gnorm_g, v_ev_w_out, v_od_w_in, v_fox_b_f, v_rel_bias, v_od_w_out, v_ln_mix_g, v_ln_mix_b, v_mlp_w1, v_mlp_w2, v_ln_mlp_g, v_ln_mlp_b):
    args = locals()
    w = {n: args[n] for n in _NAMES}
    m = {n: args["m_" + n] for n in _NAMES}
    v = {n: args["v_" + n] for n in _NAMES}
    me = 4 * lax.axis_index("x") + 2 * lax.axis_index("y") + lax.axis_index("c")
    S, D = x.shape[1], x.shape[2]
    H = (D // 2) // HEAD_DIM
    W = H * HEAD_DIM
    n_layers = mlp_w1.shape[0]
    assert n_layers == 2 and ev_w_in.shape[0] == 1 and od_w_in.shape[0] == 1

    shards = {"ev_w_in": ev_w_in[0].astype(BF16), "ev_w_out": ev_w_out[0].astype(BF16),
              "od_w_in": od_w_in[0].astype(BF16), "od_w_out": od_w_out[0].astype(BF16)}
    for l in range(n_layers):
        shards["mlp_w1_%d" % l] = mlp_w1[l].astype(BF16)
        shards["mlp_w2_%d" % l] = mlp_w2[l].astype(BF16)
    first = ["ev_w_in", "ev_conv_w", "rel_bias"]
    G = _gather_two_level([shards["ev_w_in"], ev_conv_w[0], rel_bias[0]], "gather_first")
    P = {
        "ev_conv_b": ev_conv_b, "ev_conv_ln_g": ev_conv_ln_g, "ev_conv_ln_b": ev_conv_ln_b,
        "hgrn_lb_logits": hgrn_lb_logits, "ev_gnorm_g": ev_gnorm_g, "fox_b_f": fox_b_f,
        "ln_mix_g": ln_mix_g, "ln_mix_b": ln_mix_b, "ln_mlp_g": ln_mlp_g, "ln_mlp_b": ln_mlp_b,
    }
    for name, gathered in zip(first, G):
        P.update(_full_weights(name, gathered, H))

    loss, grad_x, g = _local_step(x[0], loss_target[0], P, _Plan(shards, H))
    recv = [g["ev_w_in"], g["ev_w_out"], g["od_w_in"], g["od_w_out"]]
    recv += [g["mlp_w1_%d" % l] for l in range(n_layers)] + [g["mlp_w2_%d" % l] for l in range(n_layers)]

    small = {
        "ev_conv_b": g["ev_conv_b"], "ev_conv_ln_g": g["ev_conv_ln_g"], "ev_conv_ln_b": g["ev_conv_ln_b"],
        "hgrn_lb_logits": g["hgrn_lb_logits"], "ev_gnorm_g": g["ev_gnorm_g"], "fox_b_f": g["fox_b_f"],
        "ln_mix_g": jnp.concatenate([g["ln_mix_g0"], g["ln_mix_g1"]]), "ln_mix_b": jnp.concatenate([g["ln_mix_b0"], g["ln_mix_b1"]]),
        "ln_mlp_g": jnp.concatenate([g["ln_mlp_g0"], g["ln_mlp_g1"]]), "ln_mlp_b": jnp.concatenate([g["ln_mlp_b0"], g["ln_mlp_b1"]]),
        "ev_conv_w": g["ev_conv_w"], "rel_bias": g["rel_bias"],
    }
    full_shapes = [small[n].shape for n in _SMALL]
    small_all = _exchange([_pack([small[n] for n in _SMALL])], False, "gather_small_grads")[0]
    small_sum = _unpack(_sum_parts(small_all, "sum_small_grads"), full_shapes)
    small_g = dict(zip(_SMALL, small_sum))
    cw = small_g["ev_conv_w"]
    small_g["ev_conv_w"] = lax.dynamic_slice_in_dim(cw, me * (cw.shape[1] // N_DEV), cw.shape[1] // N_DEV, axis=1)
    rb = small_g["rel_bias"]
    small_g["rel_bias"] = lax.dynamic_slice_in_dim(rb, me * (rb.shape[1] // N_DEV), rb.shape[1] // N_DEV, axis=1)

    out_g, out_d, out_m, out_v = {}, {}, {}, {}
    big = [("ev_w_in", recv[0:1]), ("ev_w_out", recv[1:2]), ("od_w_in", recv[2:3]), ("od_w_out", recv[3:4]),
           ("mlp_w1", recv[4:4 + n_layers]), ("mlp_w2", recv[4 + n_layers:4 + 2 * n_layers])]
    for name, parts in big:
        out_g[name], out_d[name], out_m[name], out_v[name] = _adamw(parts, w[name], m[name], v[name], "adamw_" + name)
    shapes = [w[n].shape for n in _SMALL]
    packed = _adamw([_pack([small_g[n] for n in _SMALL])[None]], _pack([w[n] for n in _SMALL])[None], _pack([m[n] for n in _SMALL])[None],
                    _pack([v[n] for n in _SMALL])[None], "adamw_small")
    for k, dst in enumerate((out_g, out_d, out_m, out_v)):
        for n, a in zip(_SMALL, _unpack(packed[k], shapes)):
            dst[n] = a

    loss = lax.psum(loss[0, 0], ("x", "y", "c"))
    return (loss, grad_x[None], *[out_g[n] for n in _NAMES], *[out_d[n] for n in _NAMES],
            *[out_m[n] for n in _NAMES], *[out_v[n] for n in _NAMES])
```
